```python
import math
import jax, jax.numpy as jnp
from jax import lax
import numpy as np

D_MODEL = 2048
BATCH = 16
SEQ = 2048
DEPTH = 1

P_DIM = 256
GRID_W = 64
MIX_WIDTH = D_MODEL
MLSTM_WIDTH = MIX_WIDTH // 2
ATTN_WIDTH = MIX_WIDTH - MLSTM_WIDTH
MLSTM_HEADS = 8
MLSTM_DV = MLSTM_WIDTH // MLSTM_HEADS
MLSTM_DQK = MLSTM_DV // 2
MLSTM_CHUNK = 64
CONV_W = 5
ATTN_HEAD_DIM = 128
ATTN_Q_HEADS = ATTN_WIDTH // ATTN_HEAD_DIM
ATTN_KV_HEADS = 2
ROPE_FREQS = ATTN_HEAD_DIM // 4
ROPE_THETA = 10000.0
Q_BLOCK = 128
N_EXPERTS = 16
EXPERT_FF = D_MODEL // 2
CAPACITY_FACTOR = 2
NORM_EPS = 1e-6
ALPHA = (2.0 * DEPTH) ** 0.25
BETA = (8.0 * DEPTH) ** -0.25

MQ_COLS = MLSTM_HEADS * MLSTM_DQK
MK_COLS = MLSTM_HEADS * MLSTM_DQK
MV_COLS = MLSTM_WIDTH
MO_COLS = MLSTM_WIDTH
MG_COLS = 2 * 2 * MLSTM_HEADS
AQ_COLS = ATTN_Q_HEADS * ATTN_HEAD_DIM
AK_COLS = ATTN_KV_HEADS * ATTN_HEAD_DIM
AV_COLS = ATTN_KV_HEADS * ATTN_HEAD_DIM
IN_SIZES = (MQ_COLS, MK_COLS, MV_COLS, MO_COLS, MG_COLS, AQ_COLS, AK_COLS, AV_COLS)
IN_COLS = MQ_COLS + MK_COLS + MV_COLS + MO_COLS + MG_COLS + AQ_COLS + AK_COLS + AV_COLS

kernel_name = "hybrid_mlstm_axialgqa_ecmoe_deepnorm"


def _split_points():
    pts, acc = [], 0
    for s in IN_SIZES[:-1]:
        acc += s
        pts.append(acc)
    return pts


def _layer_norm(x, g, b):
    xf = x.astype(jnp.float32)
    mu = jnp.mean(xf, axis=-1, keepdims=True)
    var = jnp.mean(jnp.square(xf - mu), axis=-1, keepdims=True)
    return ((xf - mu) * lax.rsqrt(var + NORM_EPS) * g + b).astype(x.dtype)


def _rms_norm(x, g):
    xf = x.astype(jnp.float32)
    return (xf * lax.rsqrt(jnp.mean(xf * xf, axis=-1, keepdims=True) + NORM_EPS) * g).astype(x.dtype)


def _centred_depthwise_conv(x, w):
    c = x.shape[-1]
    return lax.conv_general_dilated(
        x, w[:, None, :], window_strides=(1,), padding=[(CONV_W // 2, CONV_W // 2)],
        dimension_numbers=("NWC", "WIO", "NWC"), feature_group_count=c)


def _mlstm_chunkwise(q, k, v, log_i, log_f):
    B, H, S, _ = q.shape
    dqk, dv = q.shape[-1], v.shape[-1]
    L = MLSTM_CHUNK
    nc = S // L

    def to_chunks(a):
        return jnp.moveaxis(a.reshape(B, H, nc, L, *a.shape[3:]), 2, 0)

    xs = (to_chunks(q), to_chunks(k), to_chunks(v), to_chunks(log_i), to_chunks(log_f))
    lower = jnp.tril(jnp.ones((L, L), dtype=bool))

    def step(carry, inp):
        C, n, m = carry
        qb, kb, vb, li, lf = inp
        b = jnp.cumsum(lf, axis=-1)
        d = jnp.where(lower, b[..., :, None] - b[..., None, :] + li[..., None, :], -jnp.inf)
        inter = b + m[..., None]
        m_t = jnp.maximum(inter, jnp.max(d, axis=-1))
        w_intra = jnp.exp(d - m_t[..., None])
        w_inter = jnp.exp(inter - m_t)
        s = jnp.einsum("bhtd,bhsd->bhts", qb, kb) * w_intra
        num = (jnp.einsum("bhts,bhsv->bhtv", s, vb)
               + w_inter[..., None] * jnp.einsum("bhtd,bhdv->bhtv", qb, C))
        den = jnp.sum(s, axis=-1) + w_inter * jnp.einsum("bhtd,bhd->bht", qb, n)
        h = num / jnp.maximum(jnp.abs(den), jnp.exp(-m_t))[..., None]
        b_last = b[..., -1]
        g = b_last[..., None] - b + li
        m_new = jnp.maximum(b_last + m, jnp.max(g, axis=-1))
        decay = jnp.exp(b_last + m - m_new)
        wk = jnp.exp(g - m_new[..., None])
        C_new = decay[..., None, None] * C + jnp.einsum("bhs,bhsd,bhsv->bhdv", wk, kb, vb)
        n_new = decay[..., None] * n + jnp.einsum("bhs,bhsd->bhd", wk, kb)
        return (C_new, n_new, m_new), h

    init = (jnp.zeros((B, H, dqk, dv), jnp.float32), jnp.zeros((B, H, dqk), jnp.float32),
            jnp.zeros((B, H), jnp.float32))
    _, hs = lax.scan(step, init, xs)
    return jnp.moveaxis(hs, 0, 2).reshape(B, H, S, dv)


def _mlstm_mixer(q, k, v, o, gates, conv_w, b_i, b_f, g_head):
    B, S, _ = q.shape
    qk = jax.nn.silu(_centred_depthwise_conv(jnp.concatenate([q, k], axis=-1), conv_w))
    q, k = jnp.split(qk, 2, axis=-1)

    def heads(a, d):
        return a.reshape(B, S, MLSTM_HEADS, d).transpose(0, 2, 1, 3).astype(jnp.float32)

    qh = heads(q, MLSTM_DQK) * (MLSTM_DQK ** -0.5)
    kh = heads(k, MLSTM_DQK)
    vh = heads(v, MLSTM_DV)
    g = gates.astype(jnp.float32).reshape(B, S, 2, 2, MLSTM_HEADS)
    log_i = (g[:, :, :, 0] + b_i).transpose(2, 0, 3, 1)
    log_f = jax.nn.log_sigmoid(g[:, :, :, 1] + b_f).transpose(2, 0, 3, 1)
    h_fwd = _mlstm_chunkwise(qh, kh, vh, log_i[0], log_f[0])
    rev = lambda a: jnp.flip(a, axis=2)
    h_bwd = rev(_mlstm_chunkwise(rev(qh), rev(kh), rev(vh), rev(log_i[1]), rev(log_f[1])))
    h = _rms_norm((h_fwd + h_bwd).transpose(0, 2, 1, 3), g_head)
    return (jax.nn.sigmoid(o.astype(jnp.float32)) * h.reshape(B, S, MLSTM_WIDTH)).astype(o.dtype)


def _axial_rope_tables(S):
    rows = S // GRID_W
    row_idx = jnp.broadcast_to(jnp.arange(rows, dtype=jnp.float32)[:, None], (rows, GRID_W)).reshape(-1)
    col_idx = jnp.broadcast_to(jnp.arange(GRID_W, dtype=jnp.float32)[None, :], (rows, GRID_W)).reshape(-1)
    inv_freq = ROPE_THETA ** (-jnp.arange(ROPE_FREQS, dtype=jnp.float32) / ROPE_FREQS)
    ang = jnp.stack([row_idx[:, None] * inv_freq, col_idx[:, None] * inv_freq], axis=1)
    return jnp.cos(ang), jnp.sin(ang)


def _apply_axial_rope(x, cos, sin):
    xr = x.astype(jnp.float32).reshape(*x.shape[:-1], 2, 2, ROPE_FREQS)
    x1, x2 = xr[..., 0, :], xr[..., 1, :]
    out = jnp.stack([x1 * cos - x2 * sin, x2 * cos + x1 * sin], axis=-2)
    return out.reshape(x.shape)


def _axial_gqa(q, k, v, g_q, g_k):
    B, S, _ = q.shape
    G = ATTN_Q_HEADS // ATTN_KV_HEADS
    d = ATTN_HEAD_DIM
    qh = q.reshape(B, S, ATTN_KV_HEADS, G, d).transpose(0, 2, 3, 1, 4)
    kh = k.reshape(B, S, ATTN_KV_HEADS, d).transpose(0, 2, 1, 3)
    vh = v.reshape(B, S, ATTN_KV_HEADS, d).transpose(0, 2, 1, 3)
    cos, sin = _axial_rope_tables(S)
    qh = _apply_axial_rope(_rms_norm(qh, g_q), cos, sin) * (d ** -0.5)
    kh = _apply_axial_rope(_rms_norm(kh, g_k), cos, sin)
    nb = S // Q_BLOCK
    qb = qh.reshape(B, ATTN_KV_HEADS, G, nb, Q_BLOCK, d).transpose(3, 0, 1, 2, 4, 5)

    def block(q_blk):
        s = jnp.einsum("bkgqd,bksd->bkgqs", q_blk, kh)
        pr = jax.nn.softmax(s.astype(jnp.float32), axis=-1)
        return jnp.einsum("bkgqs,bksd->bkgqd", pr.astype(vh.dtype), vh)

    ob = lax.map(block, qb)
    return ob.transpose(1, 0, 4, 2, 3, 5).reshape(B, S, ATTN_WIDTH).astype(v.dtype)


def _expert_choice_moe(x, w_router, w_gate, w_up, w_down):
    B, S, D = x.shape
    cap = CAPACITY_FACTOR * S // N_EXPERTS
    affinity = jax.nn.softmax(jnp.einsum("bsd,de->bse", x, w_router).astype(jnp.float32), axis=-1)
    gate, idx = lax.top_k(affinity.transpose(0, 2, 1), cap)
    xs = jax.vmap(lambda xb, ib: xb[ib])(x, idx)
    hid = jax.nn.silu(jnp.einsum("becd,edf->becf", xs, w_gate)) * jnp.einsum("becd,edf->becf", xs, w_up)
    y = jnp.einsum("becf,efd->becd", hid, w_down) * gate[..., None].astype(x.dtype)
    return jax.vmap(lambda yb, ib: jnp.zeros((S, D), yb.dtype).at[ib.reshape(-1)].add(yb.reshape(-1, D)))(y, idx)


def setup_inputs(seed: int = 0) -> dict:
    key = jax.random.key(seed)
    ks = jax.random.split(key, 24)
    f32 = jnp.float32
    nrm = lambda k, shape, s: jax.random.normal(k, shape, f32) * s
    v_scale = jnp.concatenate([
        jnp.ones((MQ_COLS + MK_COLS,), f32), jnp.full((MV_COLS,), BETA, f32),
        jnp.ones((MO_COLS + MG_COLS + AQ_COLS + AK_COLS,), f32), jnp.full((AV_COLS,), BETA, f32)])
    return {
        "x": jax.random.normal(ks[0], (BATCH, SEQ, D_MODEL), f32),
        "p": jax.random.normal(ks[1], (DEPTH, BATCH, SEQ, P_DIM), f32),
        "w_in": nrm(ks[2], (DEPTH, D_MODEL, IN_COLS), D_MODEL ** -0.5) * v_scale,
        "conv_w": nrm(ks[3], (DEPTH, CONV_W, MQ_COLS + MK_COLS), CONV_W ** -0.5),
        "b_igate": nrm(ks[4], (DEPTH, 2, MLSTM_HEADS), 0.1),
        "b_fgate": jnp.broadcast_to(jnp.linspace(3.0, 6.0, MLSTM_HEADS, dtype=f32), (DEPTH, 2, MLSTM_HEADS))
                   + nrm(ks[5], (DEPTH, 2, MLSTM_HEADS), 0.1),
        "g_mlstm": 1.0 + nrm(ks[6], (DEPTH, MLSTM_HEADS, MLSTM_DV), 0.05),
        "g_q": 1.0 + nrm(ks[7], (DEPTH, ATTN_HEAD_DIM), 0.05),
        "g_k": 1.0 + nrm(ks[8], (DEPTH, ATTN_HEAD_DIM), 0.05),
        "w_out": nrm(ks[9], (DEPTH, MIX_WIDTH, D_MODEL), BETA * MIX_WIDTH ** -0.5),
        "ln1_g": 1.0 + nrm(ks[10], (DEPTH, D_MODEL), 0.05),
        "ln1_b": nrm(ks[11], (DEPTH, D_MODEL), 0.02),
        "w_router": nrm(ks[12], (DEPTH, D_MODEL, N_EXPERTS), D_MODEL ** -0.5),
        "w_gate": nrm(ks[13], (DEPTH, N_EXPERTS, D_MODEL, EXPERT_FF), D_MODEL ** -0.5),
        "w_up": nrm(ks[14], (DEPTH, N_EXPERTS, D_MODEL, EXPERT_FF), D_MODEL ** -0.5),
        "w_down": nrm(ks[15], (DEPTH, N_EXPERTS, EXPERT_FF, D_MODEL), BETA * EXPERT_FF ** -0.5),
        "w_pl_proj": nrm(ks[16], (DEPTH, P_DIM, D_MODEL), BETA * P_DIM ** -0.5),
        "w_pl_gate": nrm(ks[17], (DEPTH, D_MODEL, D_MODEL), D_MODEL ** -0.5),
        "b_pl_gate": nrm(ks[18], (DEPTH, D_MODEL), 0.02),
        "ln2_g": 1.0 + nrm(ks[19], (DEPTH, D_MODEL), 0.05),
        "ln2_b": nrm(ks[20], (DEPTH, D_MODEL), 0.02),
    }


def reference(x, p, w_in, conv_w, b_igate, b_fgate, g_mlstm, g_q, g_k, w_out, ln1_g, ln1_b,
              w_router, w_gate, w_up, w_down, w_pl_proj, w_pl_gate, b_pl_gate, ln2_g, ln2_b):
    splits = _split_points()
    for i in range(DEPTH):
        proj = jnp.einsum("bsd,dc->bsc", x, w_in[i])
        mq, mk, mv, mo, mg, aq, ak, av = jnp.split(proj, splits, axis=-1)
        h_mlstm = _mlstm_mixer(mq, mk, mv, mo, mg, conv_w[i], b_igate[i], b_fgate[i], g_mlstm[i])
        h_attn = _axial_gqa(aq, ak, av, g_q[i], g_k[i])
        mix = jnp.einsum("bsc,cd->bsd", jnp.concatenate([h_mlstm, h_attn], axis=-1), w_out[i])
        x = _layer_norm(ALPHA * x + mix, ln1_g[i], ln1_b[i])
        moe = _expert_choice_moe(x, w_router[i], w_gate[i], w_up[i], w_down[i])
        pl_gate = jax.nn.sigmoid(jnp.einsum("bsd,de->bse", x, w_pl_gate[i]) + b_pl_gate[i])
        pl = pl_gate * jnp.einsum("bsp,pd->bsd", p[i], w_pl_proj[i])
        x = _layer_norm(ALPHA * x + moe + pl, ln2_g[i], ln2_b[i])
    return x
```

```python
import functools

import jax
import jax.numpy as jnp
from jax import lax
from jax.experimental import pallas as pl
from jax.experimental.pallas import tpu as pltpu

F32 = jnp.float32
BF16 = jnp.bfloat16

D_MODEL = 2048
P_DIM = 256
GRID_W = 64
MLSTM_WIDTH = D_MODEL // 2
ATTN_WIDTH = D_MODEL - MLSTM_WIDTH
MLSTM_HEADS = 8
MLSTM_DV = MLSTM_WIDTH // MLSTM_HEADS
MLSTM_DQK = MLSTM_DV // 2
CONV_W = 5
ATTN_HEAD_DIM = 128
ATTN_Q_HEADS = ATTN_WIDTH // ATTN_HEAD_DIM
ATTN_KV_HEADS = 2
ATTN_GROUP = ATTN_Q_HEADS // ATTN_KV_HEADS
ROPE_FREQS = ATTN_HEAD_DIM // 4
ROPE_THETA = 10000.0
N_EXPERTS = 16
EXPERT_FF = D_MODEL // 2
CAPACITY_FACTOR = 2
NORM_EPS = 1e-6
DEPTH = 1
ALPHA = (2.0 * DEPTH) ** 0.25
LOG2E = 1.4426950408889634

MQ_COLS = MLSTM_HEADS * MLSTM_DQK
MV_COLS = MLSTM_WIDTH
MG_COLS = 2 * 2 * MLSTM_HEADS
AQ_COLS = ATTN_WIDTH
AKV_COLS = ATTN_KV_HEADS * ATTN_HEAD_DIM
PROJ_COLS = 2 * MQ_COLS + 2 * MV_COLS + AQ_COLS + 2 * AKV_COLS
OFF_MQ, OFF_MK, OFF_MV, OFF_MO = 0, MQ_COLS, 2 * MQ_COLS, 2 * MQ_COLS + MV_COLS
OFF_AQ = OFF_MO + MV_COLS
OFF_AK = OFF_AQ + AQ_COLS
OFF_AV = OFF_AK + AKV_COLS

HEAD_PAIRS = MLSTM_HEADS // 2
GATES_PER_PAIR = 8
CHAINS = 4
MLSTM_LC = 128
LANES = 128
SUBLANES = 8
V7X_VMEM_LIMIT = 56 * 1024 * 1024
TOPK_REFINE_STEPS = 24


def _cparams(sem, vmem=V7X_VMEM_LIMIT):
    return pltpu.CompilerParams(dimension_semantics=sem, vmem_limit_bytes=vmem)


def _resident(shape, index_map):
    return pl.BlockSpec(shape, index_map, pipeline_mode=pl.Buffered(1))


def _sigmoid(x):
    return 0.5 * jnp.tanh(0.5 * x) + 0.5


def _log_sigmoid(x):
    return jnp.minimum(x, 0.0) - jnp.log1p(jnp.exp(-jnp.abs(x)))


def _proj_kernel(x_ref, w_ref, wgt_ref, o_ref, gr_ref, xb_ref):
    @pl.when(pl.program_id(1) == 0)
    def _():
        xb = x_ref[...].astype(BF16)
        xb_ref[...] = xb
        gr = lax.dot_general(wgt_ref[...], xb, (((1,), (1,)), ((), ())),
                             preferred_element_type=F32)
        for p in range(HEAD_PAIRS):
            gr_ref[p] = gr[GATES_PER_PAIR * p:GATES_PER_PAIR * (p + 1), :]

    o_ref[...] = jnp.dot(xb_ref[...], w_ref[...], preferred_element_type=F32).astype(o_ref.dtype)


def _proj(x2, w, wgt, tm=1024, tn=1536):
    T, D = x2.shape
    N = w.shape[1]
    return pl.pallas_call(
        _proj_kernel,
        grid=(T // tm, N // tn),
        in_specs=[
            pl.BlockSpec((tm, D), lambda i, j: (i, 0)),
            pl.BlockSpec((D, tn), lambda i, j: (0, j)),
            _resident((MG_COLS, D), lambda i, j: (0, 0)),
        ],
        out_specs=[
            pl.BlockSpec((tm, tn), lambda i, j: (i, j)),
            pl.BlockSpec((HEAD_PAIRS, GATES_PER_PAIR, tm), lambda i, j: (0, 0, i)),
        ],
        out_shape=[
            jax.ShapeDtypeStruct((T, N), BF16),
            jax.ShapeDtypeStruct((HEAD_PAIRS, GATES_PER_PAIR, T), F32),
        ],
        scratch_shapes=[pltpu.VMEM((tm, D), BF16)],
        compiler_params=_cparams(("parallel", "arbitrary")),
        name="proj",
    )(x2, w, wgt)


def _conv_silu(x, w, pad_s):
    S, C = x.shape
    half = CONV_W // 2
    halo = jnp.zeros((SUBLANES, C), F32)
    pad_s[0:SUBLANES, :] = halo
    pad_s[SUBLANES + S:2 * SUBLANES + S, :] = halo
    pad_s[SUBLANES:SUBLANES + S, :] = x
    acc = x * w[half:half + 1, :]
    for j in range(CONV_W):
        if j != half:
            acc = acc + pad_s[SUBLANES + j - half:SUBLANES + j - half + S, :] * w[j:j + 1, :]
    return acc * _sigmoid(acc)


def _mlstm_kernel(q_ref, k_ref, v_ref, o_ref, gr_ref, cwq_ref, cwk_ref, br_ref, gh_ref, out_ref,
                  q0_s, q1_s, kt_s, col_s, rrow_s, wl_s, dec_s, sc_s, cst_s, call_s, pad_s):
    S = q_ref.shape[0]
    L = MLSTM_LC
    NC = S // L
    DQ = MLSTM_DQK
    DV = MLSTM_DV
    hi = lax.Precision.HIGHEST
    neg = -jnp.inf

    g8 = gr_ref[...] + br_ref[...]
    li8 = g8 * LOG2E
    lf8 = pltpu.roll(_log_sigmoid(g8) * LOG2E, CHAINS, 0)
    row = lax.broadcasted_iota(jnp.int32, (GATES_PER_PAIR, L), 0)
    lane = lax.broadcasted_iota(jnp.int32, (GATES_PER_PAIR, L), 1)
    fwd = (row % CHAINS) < 2
    fwd1 = fwd[:, :1]
    si = lax.broadcasted_iota(jnp.int32, (L, 2 * L), 0)
    ti = lax.broadcasted_iota(jnp.int32, (L, 2 * L), 1)
    tri = jnp.where(ti < L, jnp.where(si <= ti, 1.0, 0.0), jnp.where(si >= ti - L, 1.0, 0.0))
    tot, mloc, b_l, cm_l = [], [], [], []
    for c in range(NC):
        pr = jnp.dot(lf8[:, c * L:(c + 1) * L], tri, precision=hi, preferred_element_type=F32)
        b_c = jnp.where(fwd, pr[:, :L], pr[:, L:])
        tot_c = pr[:, L - 1:L]
        li_c = li8[:, c * L:(c + 1) * L]
        r_c = li_c - b_c
        cm = r_c
        k = 1
        while k < L:
            pre = jnp.where(lane >= k, pltpu.roll(cm, k, 1), neg)
            suf = jnp.where(lane < L - k, pltpu.roll(cm, L - k, 1), neg)
            cm = jnp.maximum(cm, jnp.where(fwd, pre, suf))
            k *= 2
        g_c = tot_c - b_c + li_c
        mloc_c = jnp.max(g_c, axis=1, keepdims=True)
        wl_s[c] = jnp.exp2(g_c - mloc_c)
        rrow_s[c] = r_c
        tot.append(tot_c)
        mloc.append(mloc_c)
        b_l.append(b_c)
        cm_l.append(cm)

    def scan(order):
        m = jnp.zeros((GATES_PER_PAIR, 1), F32)
        m_in, dec, sc = [None] * NC, [None] * NC, [None] * NC
        for c in order:
            m_new = jnp.maximum(tot[c] + m, mloc[c])
            m_in[c] = m
            dec[c] = jnp.exp2(tot[c] + m - m_new)
            sc[c] = jnp.exp2(mloc[c] - m_new)
            m = m_new
        return m_in, dec, sc

    mf, df, sf = scan(range(NC))
    mb, db, sb = scan(range(NC - 1, -1, -1))
    pad = jnp.zeros((LANES - 3 * GATES_PER_PAIR, L), F32)
    for c in range(NC):
        m_in = jnp.where(fwd1, mf[c], mb[c])
        dec_s[c] = jnp.broadcast_to(jnp.where(fwd1, df[c], db[c]), (GATES_PER_PAIR, L))
        sc_s[c] = jnp.broadcast_to(jnp.where(fwd1, sf[c], sb[c]), (GATES_PER_PAIR, L))
        a_c = jnp.maximum(m_in, cm_l[c])
        per_t = jnp.concatenate([a_c, jnp.exp2(m_in - a_c), jnp.exp2(-(b_l[c] + a_c)), pad], axis=0)
        col_s[c * L:(c + 1) * L, :] = per_t.T

    qs = (_conv_silu(q_ref[...].astype(F32), cwq_ref[...], pad_s) * (DQ ** -0.5)).astype(BF16)
    q0_s[...] = qs[:, :DQ]
    q1_s[...] = qs[:, DQ:]
    kt = _conv_silu(k_ref[...].astype(F32), cwk_ref[...], pad_s).T
    for c in range(NC):
        kt_s[c] = kt[:, c * L:(c + 1) * L]

    ones_blk = jnp.ones((L, DV), BF16)

    def vext_of(sl, hh):
        return jnp.concatenate([v_ref[sl, DV * hh:DV * (hh + 1)], ones_blk], axis=1)

    cst_s[...] = jnp.zeros(cst_s.shape, F32)

    def state_body(c, carry):
        for d in range(2):
            ch = c if d == 0 else NC - 1 - c
            sl = pl.ds(pl.multiple_of(ch * L, L), L)
            wl = wl_s[ch]
            ktc = kt_s[ch]
            decs = dec_s[ch]
            scs = sc_s[ch]
            for hh in range(2):
                ci = d * 2 + hh
                cx = cst_s[ci]
                call_s[ch, ci] = cx.astype(BF16)
                kw = (ktc[DQ * hh:DQ * (hh + 1), :] * wl[ci:ci + 1, :]).astype(BF16)
                cst_s[ci] = (decs[ci:ci + 1, 0:1] * cx
                             + scs[ci:ci + 1, 0:1] * jnp.dot(kw, vext_of(sl, hh), preferred_element_type=F32))
        return carry

    lax.fori_loop(0, NC, state_body, 0)

    tt = lax.broadcasted_iota(jnp.int32, (L, L), 0)
    ss = lax.broadcasted_iota(jnp.int32, (L, L), 1)
    masks = (ss <= tt, ss >= tt)

    def chunk_out(ch):
        sl = pl.ds(pl.multiple_of(ch * L, L), L)
        cols = col_s[sl, :]
        rrow = rrow_s[ch]
        ktc = kt_s[ch]
        og = _sigmoid(o_ref[sl, :].astype(F32))
        for hh in range(2):
            qc = (q0_s if hh == 0 else q1_s)[sl, :]
            kth = ktc[DQ * hh:DQ * (hh + 1), :].astype(BF16)
            qk = jnp.dot(qc, kth, preferred_element_type=F32)
            vext = vext_of(sl, hh)
            hsum = None
            for d in range(2):
                ci = d * 2 + hh
                a_t = cols[:, ci:ci + 1]
                w_inter = cols[:, GATES_PER_PAIR + ci:GATES_PER_PAIR + ci + 1]
                emt = cols[:, 2 * GATES_PER_PAIR + ci:2 * GATES_PER_PAIR + ci + 1]
                w_intra = jnp.exp2(jnp.where(masks[d], rrow[ci:ci + 1, :] - a_t, neg))
                qcx = jnp.dot(qc, call_s[ch, ci], preferred_element_type=F32)
                nd = (jnp.dot((qk * w_intra).astype(BF16), vext, preferred_element_type=F32)
                      + w_inter * qcx)
                h = nd[:, :DV] / jnp.maximum(jnp.abs(nd[:, DV:]), emt)
                hsum = h if hsum is None else hsum + h
            y = hsum * lax.rsqrt(jnp.mean(hsum * hsum, axis=1, keepdims=True) + NORM_EPS) * gh_ref[hh:hh + 1, :]
            out_ref[sl, DV * hh:DV * (hh + 1)] = (og[:, DV * hh:DV * (hh + 1)] * y).astype(out_ref.dtype)

    def out_body(c, carry):
        chunk_out(2 * c)
        chunk_out(2 * c + 1)
        return carry

    lax.fori_loop(0, NC // 2, out_body, 0)


def _mlstm(proj, grow, conv_w, bias_r, g_head, B, S):
    T = B * S
    NC = S // MLSTM_LC
    pw = 2 * MLSTM_DQK
    vw = 2 * MLSTM_DV
    return pl.pallas_call(
        _mlstm_kernel,
        grid=(B, HEAD_PAIRS),
        in_specs=[
            pl.BlockSpec((S, pw), lambda b, p: (b, OFF_MQ // pw + p)),
            pl.BlockSpec((S, pw), lambda b, p: (b, OFF_MK // pw + p)),
            pl.BlockSpec((S, vw), lambda b, p: (b, OFF_MV // vw + p)),
            pl.BlockSpec((S, vw), lambda b, p: (b, OFF_MO // vw + p)),
            pl.BlockSpec((None, GATES_PER_PAIR, S), lambda b, p: (p, 0, b)),
            pl.BlockSpec((CONV_W, pw), lambda b, p: (0, p)),
            pl.BlockSpec((CONV_W, pw), lambda b, p: (0, MQ_COLS // pw + p)),
            pl.BlockSpec((None, GATES_PER_PAIR, 1), lambda b, p: (p, 0, 0)),
            pl.BlockSpec((None, 2, MLSTM_DV), lambda b, p: (p, 0, 0)),
        ],
        out_specs=pl.BlockSpec((S, vw), lambda b, p: (b, p)),
        out_shape=jax.ShapeDtypeStruct((T, MLSTM_WIDTH), BF16),
        scratch_shapes=[
            pltpu.VMEM((S, MLSTM_DQK), BF16),
            pltpu.VMEM((S, MLSTM_DQK), BF16),
            pltpu.VMEM((NC, pw, MLSTM_LC), F32),
            pltpu.VMEM((S, LANES), F32),
            pltpu.VMEM((NC, GATES_PER_PAIR, MLSTM_LC), F32),
            pltpu.VMEM((NC, GATES_PER_PAIR, MLSTM_LC), F32),
            pltpu.VMEM((NC, GATES_PER_PAIR, MLSTM_LC), F32),
            pltpu.VMEM((NC, GATES_PER_PAIR, MLSTM_LC), F32),
            pltpu.VMEM((CHAINS, MLSTM_DQK, 2 * MLSTM_DV), F32),
            pltpu.VMEM((NC, CHAINS, MLSTM_DQK, 2 * MLSTM_DV), BF16),
            pltpu.VMEM((S + 2 * SUBLANES, pw), F32),
        ],
        compiler_params=_cparams(("parallel", "parallel")),
        name="mlstm",
    )(proj, proj, proj, proj, grow, conv_w, conv_w, bias_r, g_head)


def _norm_rope(x, g, cos, sin_signed):
    xn = x * lax.rsqrt(jnp.mean(x * x, axis=1, keepdims=True) + NORM_EPS) * g
    lane = lax.broadcasted_iota(jnp.int32, x.shape, 1)
    first_half = (lane % (2 * ROPE_FREQS)) < ROPE_FREQS
    partner = jnp.where(first_half,
                        pltpu.roll(xn, LANES - ROPE_FREQS, 1),
                        pltpu.roll(xn, ROPE_FREQS, 1))
    return xn * cos + partner * sin_signed


def _attn_kernel(q_ref, k_ref, v_ref, cq_ref, sq_ref, ck_ref, sk_ref, gq_ref, gk_ref, o_ref, kr_s, vx_s):
    d = ATTN_HEAD_DIM

    @pl.when(pl.program_id(2) == 0)
    def _():
        kr_s[...] = _norm_rope(k_ref[...].astype(F32), gk_ref[...], ck_ref[...], sk_ref[...]).astype(BF16)
        vx_s[...] = jnp.concatenate([v_ref[...], jnp.ones(v_ref.shape, BF16)], axis=1)

    cq, sq, gq = cq_ref[...], sq_ref[...], gq_ref[...]
    kr = kr_s[...]
    vx = vx_s[...]
    for g in range(ATTN_GROUP):
        qg = (_norm_rope(q_ref[:, d * g:d * (g + 1)].astype(F32), gq, cq, sq) * (d ** -0.5 * LOG2E)).astype(BF16)
        s = lax.dot_general(qg, kr, (((1,), (1,)), ((), ())), preferred_element_type=F32)
        p = jnp.exp2(s - jnp.max(s, axis=1, keepdims=True))
        ov = jnp.dot(p.astype(BF16), vx, preferred_element_type=F32)
        o_ref[:, d * g:d * (g + 1)] = (ov[:, :d] / ov[:, d:]).astype(o_ref.dtype)


def _attn(proj, cos_t, sin_t, g_q, g_k, B, S, tq=256):
    T = B * S
    d = ATTN_HEAD_DIM
    gw = ATTN_GROUP * d
    nq = S // tq
    return pl.pallas_call(
        _attn_kernel,
        grid=(B, ATTN_KV_HEADS, nq),
        in_specs=[
            pl.BlockSpec((tq, gw), lambda b, kv, qi: (b * nq + qi, OFF_AQ // gw + kv)),
            pl.BlockSpec((S, d), lambda b, kv, qi: (b, OFF_AK // d + kv)),
            pl.BlockSpec((S, d), lambda b, kv, qi: (b, OFF_AV // d + kv)),
            pl.BlockSpec((tq, d), lambda b, kv, qi: (qi, 0)),
            pl.BlockSpec((tq, d), lambda b, kv, qi: (qi, 0)),
            _resident((S, d), lambda b, kv, qi: (0, 0)),
            _resident((S, d), lambda b, kv, qi: (0, 0)),
            _resident((1, d), lambda b, kv, qi: (0, 0)),
            _resident((1, d), lambda b, kv, qi: (0, 0)),
        ],
        out_specs=pl.BlockSpec((tq, gw), lambda b, kv, qi: (b * nq + qi, kv)),
        out_shape=jax.ShapeDtypeStruct((T, ATTN_WIDTH), BF16),
        scratch_shapes=[pltpu.VMEM((S, d), BF16), pltpu.VMEM((S, 2 * d), BF16)],
        compiler_params=_cparams(("parallel", "parallel", "arbitrary")),
        name="attn",
    )(proj, proj, proj, cos_t, sin_t, cos_t, sin_t, g_q, g_k)


def _layer_norm(y, g, b):
    mu = jnp.mean(y, axis=1, keepdims=True)
    yc = y - mu
    var = jnp.mean(yc * yc, axis=1, keepdims=True)
    return yc * lax.rsqrt(var + NORM_EPS) * g + b


def _split_bf16(x):
    hi = x.astype(BF16)
    return hi, (x - hi.astype(F32)).astype(BF16)


def _outproj_kernel(hm_ref, ha_ref, x_ref, wt_ref, wb_ref, g_ref, b_ref, wr_ref, x1_ref, x1b_ref, aff_ref):
    y = (ALPHA * x_ref[...]
         + jnp.dot(hm_ref[...], wt_ref[...], preferred_element_type=F32)
         + jnp.dot(ha_ref[...], wb_ref[...], preferred_element_type=F32))
    x1 = _layer_norm(y, g_ref[...], b_ref[...])
    x1_ref[...] = x1
    xh, xl = _split_bf16(x1)
    wh, wl = _split_bf16(wr_ref[...])
    x1b_ref[...] = xh
    nt = (((1,), (1,)), ((), ()))
    logits = (lax.dot_general(wh, xh, nt, preferred_element_type=F32)
              + lax.dot_general(wh, xl, nt, preferred_element_type=F32)
              + lax.dot_general(wl, xh, nt, preferred_element_type=F32))
    e = jnp.exp(logits - jnp.max(logits, axis=0, keepdims=True))
    aff_ref[...] = e / jnp.sum(e, axis=0, keepdims=True)


def _outproj(hm, ha, x2, wt, wb, g, b, wr, tm=512):
    T, D = x2.shape
    return pl.pallas_call(
        _outproj_kernel,
        grid=(T // tm,),
        in_specs=[
            pl.BlockSpec((tm, MLSTM_WIDTH), lambda i: (i, 0)),
            pl.BlockSpec((tm, ATTN_WIDTH), lambda i: (i, 0)),
            pl.BlockSpec((tm, D), lambda i: (i, 0)),
            _resident((MLSTM_WIDTH, D), lambda i: (0, 0)),
            _resident((ATTN_WIDTH, D), lambda i: (0, 0)),
            _resident((1, D), lambda i: (0, 0)),
            _resident((1, D), lambda i: (0, 0)),
            _resident((N_EXPERTS, D), lambda i: (0, 0)),
        ],
        out_specs=[
            pl.BlockSpec((tm, D), lambda i: (i, 0)),
            pl.BlockSpec((tm, D), lambda i: (i, 0)),
            pl.BlockSpec((N_EXPERTS, tm), lambda i: (0, i)),
        ],
        out_shape=[
            jax.ShapeDtypeStruct((T, D), F32),
            jax.ShapeDtypeStruct((T, D), BF16),
            jax.ShapeDtypeStruct((N_EXPERTS, T), F32),
        ],
        compiler_params=_cparams(("parallel",)),
        name="outproj",
    )(hm, ha, x2, wt, wb, g, b, wr)


def _topk_kernel(aff_ref, pos_ref, tri_s, *, cap):
    S = aff_ref.shape[1]

    @pl.when(pl.program_id(0) == 0)
    def _():
        r = lax.broadcasted_iota(jnp.int32, (S, S), 0)
        c = lax.broadcasted_iota(jnp.int32, (S, S), 1)
        tri_s[...] = jnp.where(r < c, 1.0, 0.0).astype(BF16)

    a = aff_ref[...]

    def count_ge(v):
        return jnp.sum(jnp.where(a >= v, 1.0, 0.0), axis=1, keepdims=True)

    thr = jnp.zeros((a.shape[0], 1), jnp.int32)
    for bit in range(30, -1, -1):
        cand = thr | (1 << bit)
        thr = jnp.where(count_ge(pltpu.bitcast(cand, F32)) >= cap, cand, thr)
    lo = pltpu.bitcast(thr, F32)
    hi = pltpu.bitcast(thr + 1, F32)
    for _ in range(TOPK_REFINE_STEPS):
        mid = 0.5 * (lo + hi)
        ok = count_ge(mid) >= cap
        lo = jnp.where(ok, mid, lo)
        hi = jnp.where(ok, hi, mid)
    gt = a > lo
    eq = a == lo
    need = cap - jnp.sum(jnp.where(gt, 1.0, 0.0), axis=1, keepdims=True)
    tri = tri_s[...]
    eq_rank = jnp.dot(jnp.where(eq, 1.0, 0.0).astype(BF16), tri, preferred_element_type=F32)
    sel = jnp.logical_or(gt, jnp.logical_and(eq, eq_rank < need))
    pos = jnp.dot(jnp.where(sel, 1.0, 0.0).astype(BF16), tri, preferred_element_type=F32)
    pos_ref[...] = jnp.where(sel, pos, -1.0)


def _topk(aff, B, S, cap):
    E = aff.shape[0]
    return pl.pallas_call(
        functools.partial(_topk_kernel, cap=cap),
        grid=(B,),
        in_specs=[pl.BlockSpec((E, S), lambda b: (0, b))],
        out_specs=pl.BlockSpec((E, S), lambda b: (0, b)),
        out_shape=jax.ShapeDtypeStruct((E, B * S), F32),
        scratch_shapes=[pltpu.VMEM((S, S), BF16)],
        compiler_params=_cparams(("arbitrary",)),
        name="topk",
    )(aff)


def _moe_ffn_kernel(x_ref, pos_ref, aff_ref, wg_ref, wu_ref, wd_ref, y_ref, *, cap):
    S = x_ref.shape[0]
    pos = pos_ref[...]
    slot = lax.broadcasted_iota(jnp.int32, (cap, S), 0).astype(F32)
    hit = slot == pos
    xs = jnp.dot(jnp.where(hit, 1.0, 0.0).astype(BF16), x_ref[...],
                 preferred_element_type=F32).astype(BF16)
    gate = jnp.sum(jnp.where(hit, aff_ref[...], 0.0), axis=1, keepdims=True)
    hg = jnp.dot(xs, wg_ref[...], preferred_element_type=F32)
    hu = jnp.dot(xs, wu_ref[...], preferred_element_type=F32)
    hid = (hg * _sigmoid(hg) * hu).astype(BF16)
    y_ref[...] = (jnp.dot(hid, wd_ref[...], preferred_element_type=F32) * gate).astype(y_ref.dtype)


def _moe_ffn(x1b, pos4, aff4, wg, wu, wd, B, S, cap):
    E, D, F = wg.shape
    return pl.pallas_call(
        functools.partial(_moe_ffn_kernel, cap=cap),
        grid=(E, B),
        in_specs=[
            pl.BlockSpec((S, D), lambda e, b: (b, 0)),
            pl.BlockSpec((None, None, 1, S), lambda e, b: (e, b, 0, 0)),
            pl.BlockSpec((None, None, 1, S), lambda e, b: (e, b, 0, 0)),
            pl.BlockSpec((None, D, F), lambda e, b: (e, 0, 0)),
            pl.BlockSpec((None, D, F), lambda e, b: (e, 0, 0)),
            pl.BlockSpec((None, F, D), lambda e, b: (e, 0, 0)),
        ],
        out_specs=pl.BlockSpec((None, None, cap, D), lambda e, b: (b, e, 0, 0)),
        out_shape=jax.ShapeDtypeStruct((B, E, cap, D), BF16),
        compiler_params=_cparams(("arbitrary", "arbitrary")),
        name="moe_ffn",
    )(x1b, pos4, aff4, wg, wu, wd)


def _moe_comb_kernel(pos_ref, y_ref, o_ref, *, cap):
    ts = pos_ref.shape[1]

    @pl.when(pl.program_id(2) == 0)
    def _():
        o_ref[...] = jnp.zeros(o_ref.shape, F32)

    eg = pos_ref.shape[0]
    slot = lax.broadcasted_iota(jnp.int32, (cap, ts), 0).astype(F32)
    onehot = jnp.concatenate([jnp.where(slot == pos_ref[i], 1.0, 0.0).astype(BF16) for i in range(eg)],
                             axis=0)
    ys = y_ref[...].reshape(eg * cap, y_ref.shape[2])
    o_ref[...] += lax.dot_general(onehot, ys, (((0,), (0,)), ((), ())), preferred_element_type=F32)


def _moe_comb(pos4, y, B, S, cap, ts=512, eg=4):
    E, D = y.shape[1], y.shape[3]
    nt = S // ts
    return pl.pallas_call(
        functools.partial(_moe_comb_kernel, cap=cap),
        grid=(B, nt, E // eg),
        in_specs=[
            pl.BlockSpec((eg, None, 1, ts), lambda b, t, e: (e, b, 0, t)),
            pl.BlockSpec((None, eg, cap, D), lambda b, t, e: (b, e, 0, 0)),
        ],
        out_specs=pl.BlockSpec((ts, D), lambda b, t, e: (b * nt + t, 0)),
        out_shape=jax.ShapeDtypeStruct((B * S, D), F32),
        compiler_params=_cparams(("parallel", "parallel", "arbitrary")),
        name="moe_comb",
    )(pos4, y)


def _final_kernel(x1_ref, x1b_ref, moe_ref, p_ref, wpg_ref, bpg_ref, wpp_ref, g_ref, b_ref, o_ref):
    gate = _sigmoid(jnp.dot(x1b_ref[...], wpg_ref[...], preferred_element_type=F32) + bpg_ref[...])
    plv = gate * jnp.dot(p_ref[...].astype(BF16), wpp_ref[...], preferred_element_type=F32)
    o_ref[...] = _layer_norm(ALPHA * x1_ref[...] + moe_ref[...] + plv, g_ref[...], b_ref[...])


def _final(x1, x1b, moe, p2, wpg, bpg, wpp, g, b, tm=512):
    T, D = x1.shape
    return pl.pallas_call(
        _final_kernel,
        grid=(T // tm,),
        in_specs=[
            pl.BlockSpec((tm, D), lambda i: (i, 0)),
            pl.BlockSpec((tm, D), lambda i: (i, 0)),
            pl.BlockSpec((tm, D), lambda i: (i, 0)),
            pl.BlockSpec((tm, P_DIM), lambda i: (i, 0)),
            _resident((D, D), lambda i: (0, 0)),
            _resident((1, D), lambda i: (0, 0)),
            _resident((P_DIM, D), lambda i: (0, 0)),
            _resident((1, D), lambda i: (0, 0)),
            _resident((1, D), lambda i: (0, 0)),
        ],
        out_specs=pl.BlockSpec((tm, D), lambda i: (i, 0)),
        out_shape=jax.ShapeDtypeStruct((T, D), F32),
        compiler_params=_cparams(("parallel",)),
        name="final",
    )(x1, x1b, moe, p2, wpg, bpg, wpp, g, b)


def _rope_tables(S):
    rows = S // GRID_W
    row_idx = jnp.broadcast_to(jnp.arange(rows, dtype=F32)[:, None], (rows, GRID_W)).reshape(-1)
    col_idx = jnp.broadcast_to(jnp.arange(GRID_W, dtype=F32)[None, :], (rows, GRID_W)).reshape(-1)
    inv_freq = ROPE_THETA ** (-jnp.arange(ROPE_FREQS, dtype=F32) / ROPE_FREQS)
    ar = row_idx[:, None] * inv_freq
    ac = col_idx[:, None] * inv_freq
    cos_t = jnp.concatenate([jnp.cos(ar), jnp.cos(ar), jnp.cos(ac), jnp.cos(ac)], axis=1)
    sin_t = jnp.concatenate([-jnp.sin(ar), jnp.sin(ar), -jnp.sin(ac), jnp.sin(ac)], axis=1)
    return cos_t, sin_t


def _pair_major(a):
    return a.reshape(2, HEAD_PAIRS, 2).transpose(1, 0, 2)


def _layer(x2, p2, B, S, w_in, conv_w, b_i, b_f, g_mlstm, g_q, g_k, w_out, ln1_g, ln1_b, w_router, w_gate, w_up,
           w_down, w_pl_proj, w_pl_gate, b_pl_gate, ln2_g, ln2_b):
    D = D_MODEL
    cap = CAPACITY_FACTOR * S // N_EXPERTS
    o_mg = 2 * MQ_COLS + 2 * MV_COLS
    w_main = jnp.concatenate([w_in[:, :o_mg], w_in[:, o_mg + MG_COLS:]], axis=1).astype(BF16)
    w_g = w_in[:, o_mg:o_mg + MG_COLS].reshape(D, 2, 2, HEAD_PAIRS, 2).transpose(0, 3, 2, 1, 4).reshape(D, MG_COLS)
    bias = jnp.stack([_pair_major(b_i), _pair_major(b_f)], axis=1).reshape(HEAD_PAIRS, GATES_PER_PAIR)

    proj, grow = _proj(x2, w_main, w_g.astype(BF16).T)
    h_m = _mlstm(proj, grow, conv_w, bias[:, :, None], g_mlstm.reshape(HEAD_PAIRS, 2, MLSTM_DV), B, S)
    cos_t, sin_t = _rope_tables(S)
    h_a = _attn(proj, cos_t, sin_t, g_q[None, :], g_k[None, :], B, S)
    w_o = w_out.astype(BF16)
    x1, x1b, aff = _outproj(h_m, h_a, x2, w_o[:MLSTM_WIDTH], w_o[MLSTM_WIDTH:], ln1_g[None, :], ln1_b[None, :],
                            w_router.T)
    pos = _topk(aff, B, S, cap)
    pos4 = pos.reshape(N_EXPERTS, B, 1, S)
    aff4 = aff.reshape(N_EXPERTS, B, 1, S)
    y = _moe_ffn(x1b, pos4, aff4, w_gate.astype(BF16), w_up.astype(BF16), w_down.astype(BF16), B, S, cap)
    moe = _moe_comb(pos4, y, B, S, cap)
    return _final(x1, x1b, moe, p2, w_pl_gate.astype(BF16), b_pl_gate[None, :], w_pl_proj.astype(BF16),
                  ln2_g[None, :], ln2_b[None, :])


def kernel(x, p, w_in, conv_w, b_igate, b_fgate, g_mlstm, g_q, g_k, w_out, ln1_g, ln1_b, w_router, w_gate, w_up,
           w_down, w_pl_proj, w_pl_gate, b_pl_gate, ln2_g, ln2_b):
    B, S, D = x.shape
    x2 = x.reshape(B * S, D)
    for i in range(DEPTH):
        x2 = _layer(x2, p[i].reshape(B * S, P_DIM), B, S, w_in[i], conv_w[i], b_igate[i], b_fgate[i], g_mlstm[i],
                    g_q[i], g_k[i], w_out[i], ln1_g[i], ln1_b[i], w_router[i], w_gate[i], w_up[i], w_down[i],
                    w_pl_proj[i], w_pl_gate[i], b_pl_gate[i], ln2_g[i], ln2_b[i])
    return x2.reshape(B, S, D)
```

```python
import functools

import jax
import jax.numpy as jnp
from jax import lax
from jax.experimental import pallas as pl
from jax.experimental.pallas import tpu as pltpu

F32 = jnp.float32
BF16 = jnp.bfloat16

D_MODEL = 2048
P_DIM = 256
GRID_W = 64
MLSTM_WIDTH = D_MODEL // 2
ATTN_WIDTH = D_MODEL - MLSTM_WIDTH
MLSTM_HEADS = 8
MLSTM_DV = MLSTM_WIDTH // MLSTM_HEADS
MLSTM_DQK = MLSTM_DV // 2
CONV_W = 5
ATTN_HEAD_DIM = 128
ATTN_Q_HEADS = ATTN_WIDTH // ATTN_HEAD_DIM
ATTN_KV_HEADS = 2
ATTN_GROUP = ATTN_Q_HEADS // ATTN_KV_HEADS
ROPE_FREQS = ATTN_HEAD_DIM // 4
ROPE_THETA = 10000.0
N_EXPERTS = 16
EXPERT_FF = D_MODEL // 2
CAPACITY_FACTOR = 2
NORM_EPS = 1e-6
DEPTH = 1
ALPHA = (2.0 * DEPTH) ** 0.25
LOG2E = 1.4426950408889634

MQ_COLS = MLSTM_HEADS * MLSTM_DQK
MV_COLS = MLSTM_WIDTH
MG_COLS = 2 * 2 * MLSTM_HEADS
AQ_COLS = ATTN_WIDTH
AKV_COLS = ATTN_KV_HEADS * ATTN_HEAD_DIM
PROJ_COLS = 2 * MQ_COLS + 2 * MV_COLS + AQ_COLS + 2 * AKV_COLS
OFF_MQ, OFF_MK, OFF_MV, OFF_MO = 0, MQ_COLS, 2 * MQ_COLS, 2 * MQ_COLS + MV_COLS
OFF_AQ = OFF_MO + MV_COLS
OFF_AK = OFF_AQ + AQ_COLS
OFF_AV = OFF_AK + AKV_COLS

HEAD_PAIRS = MLSTM_HEADS // 2
GATES_PER_PAIR = 8
CHAINS = 4
MLSTM_LC = 128
LANES = 128
SUBLANES = 8
V7X_VMEM_LIMIT = 56 * 1024 * 1024
TOPK_REFINE_STEPS = 24


def _cparams(sem, vmem=V7X_VMEM_LIMIT):
    return pltpu.CompilerParams(dimension_semantics=sem, vmem_limit_bytes=vmem)


def _resident(shape, index_map):
    return pl.BlockSpec(shape, index_map, pipeline_mode=pl.Buffered(1))


def _sigmoid(x):
    return 0.5 * jnp.tanh(0.5 * x) + 0.5


def _log_sigmoid(x):
    return jnp.minimum(x, 0.0) - jnp.log1p(jnp.exp(-jnp.abs(x)))


def _proj_kernel(x_ref, w_ref, wgt_ref, o_ref, gr_ref, xb_ref):
    @pl.when(pl.program_id(1) == 0)
    def _():
        xb = x_ref[...].astype(BF16)
        xb_ref[...] = xb
        gr = lax.dot_general(wgt_ref[...], xb, (((1,), (1,)), ((), ())),
                             preferred_element_type=F32)
        for p in range(HEAD_PAIRS):
            gr_ref[p] = gr[GATES_PER_PAIR * p:GATES_PER_PAIR * (p + 1), :]

    o_ref[...] = jnp.dot(xb_ref[...], w_ref[...], preferred_element_type=F32).astype(o_ref.dtype)


def _proj(x2, w, wgt, tm=1024, tn=1536):
    T, D = x2.shape
    N = w.shape[1]
    return pl.pallas_call(
        _proj_kernel,
        grid=(T // tm, N // tn),
        in_specs=[
            pl.BlockSpec((tm, D), lambda i, j: (i, 0)),
            pl.BlockSpec((D, tn), lambda i, j: (0, j)),
            _resident((MG_COLS, D), lambda i, j: (0, 0)),
        ],
        out_specs=[
            pl.BlockSpec((tm, tn), lambda i, j: (i, j)),
            pl.BlockSpec((HEAD_PAIRS, GATES_PER_PAIR, tm), lambda i, j: (0, 0, i)),
        ],
        out_shape=[
            jax.ShapeDtypeStruct((T, N), BF16),
            jax.ShapeDtypeStruct((HEAD_PAIRS, GATES_PER_PAIR, T), F32),
        ],
        scratch_shapes=[pltpu.VMEM((tm, D), BF16)],
        compiler_params=_cparams(("parallel", "arbitrary")),
        name="proj",
    )(x2, w, wgt)


def _conv_silu(x, w, pad_s):
    S, C = x.shape
    half = CONV_W // 2
    halo = jnp.zeros((SUBLANES, C), F32)
    pad_s[0:SUBLANES, :] = halo
    pad_s[SUBLANES + S:2 * SUBLANES + S, :] = halo
    pad_s[SUBLANES:SUBLANES + S, :] = x
    acc = x * w[half:half + 1, :]
    for j in range(CONV_W):
        if j != half:
            acc = acc + pad_s[SUBLANES + j - half:SUBLANES + j - half + S, :] * w[j:j + 1, :]
    return acc * _sigmoid(acc)


def _mlstm_kernel(q_ref, k_ref, v_ref, o_ref, gr_ref, cwq_ref, cwk_ref, br_ref, gh_ref, out_ref,
                  q0_s, q1_s, kt_s, col_s, rrow_s, wl_s, dec_s, sc_s, cst_s, call_s, pad_s):
    S = q_ref.shape[0]
    L = MLSTM_LC
    NC = S // L
    DQ = MLSTM_DQK
    DV = MLSTM_DV
    hi = lax.Precision.HIGHEST
    neg = -jnp.inf

    g8 = gr_ref[...] + br_ref[...]
    li8 = g8 * LOG2E
    lf8 = pltpu.roll(_log_sigmoid(g8) * LOG2E, CHAINS, 0)
    row = lax.broadcasted_iota(jnp.int32, (GATES_PER_PAIR, L), 0)
    lane = lax.broadcasted_iota(jnp.int32, (GATES_PER_PAIR, L), 1)
    fwd = (row % CHAINS) < 2
    fwd1 = fwd[:, :1]
    si = lax.broadcasted_iota(jnp.int32, (L, 2 * L), 0)
    ti = lax.broadcasted_iota(jnp.int32, (L, 2 * L), 1)
    tri = jnp.where(ti < L, jnp.where(si <= ti, 1.0, 0.0), jnp.where(si >= ti - L, 1.0, 0.0))
    tot, mloc, b_l, cm_l = [], [], [], []
    for c in range(NC):
        pr = jnp.dot(lf8[:, c * L:(c + 1) * L], tri, precision=hi, preferred_element_type=F32)
        b_c = jnp.where(fwd, pr[:, :L], pr[:, L:])
        tot_c = pr[:, L - 1:L]
        li_c = li8[:, c * L:(c + 1) * L]
        r_c = li_c - b_c
        cm = r_c
        k = 1
        while k < L:
            pre = jnp.where(lane >= k, pltpu.roll(cm, k, 1), neg)
            suf = jnp.where(lane < L - k, pltpu.roll(cm, L - k, 1), neg)
            cm = jnp.maximum(cm, jnp.where(fwd, pre, suf))
            k *= 2
        g_c = tot_c - b_c + li_c
        mloc_c = jnp.max(g_c, axis=1, keepdims=True)
        wl_s[c] = jnp.exp2(g_c - mloc_c)
        rrow_s[c] = r_c
        tot.append(tot_c)
        mloc.append(mloc_c)
        b_l.append(b_c)
        cm_l.append(cm)

    def scan(order):
        m = jnp.zeros((GATES_PER_PAIR, 1), F32)
        m_in, dec, sc = [None] * NC, [None] * NC, [None] * NC
        for c in order:
            m_new = jnp.maximum(tot[c] + m, mloc[c])
            m_in[c] = m
            dec[c] = jnp.exp2(tot[c] + m - m_new)
            sc[c] = jnp.exp2(mloc[c] - m_new)
            m = m_new
        return m_in, dec, sc

    mf, df, sf = scan(range(NC))
    mb, db, sb = scan(range(NC - 1, -1, -1))
    pad = jnp.zeros((LANES - 3 * GATES_PER_PAIR, L), F32)
    for c in range(NC):
        m_in = jnp.where(fwd1, mf[c], mb[c])
        dec_s[c] = jnp.broadcast_to(jnp.where(fwd1, df[c], db[c]), (GATES_PER_PAIR, L))
        sc_s[c] = jnp.broadcast_to(jnp.where(fwd1, sf[c], sb[c]), (GATES_PER_PAIR, L))
        a_c = jnp.maximum(m_in, cm_l[c])
        per_t = jnp.concatenate([a_c, jnp.exp2(m_in - a_c), jnp.exp2(-(b_l[c] + a_c)), pad], axis=0)
        col_s[c * L:(c + 1) * L, :] = per_t.T

    qs = (_conv_silu(q_ref[...].astype(F32), cwq_ref[...], pad_s) * (DQ ** -0.5)).astype(BF16)
    q0_s[...] = qs[:, :DQ]
    q1_s[...] = qs[:, DQ:]
    kt = _conv_silu(k_ref[...].astype(F32), cwk_ref[...], pad_s).T
    for c in range(NC):
        kt_s[c] = kt[:, c * L:(c + 1) * L]

    ones_blk = jnp.ones((L, DV), BF16)

    def vext_of(sl, hh):
        return jnp.concatenate([v_ref[sl, DV * hh:DV * (hh + 1)], ones_blk], axis=1)

    cst_s[...] = jnp.zeros(cst_s.shape, F32)

    def state_body(c, carry):
        for d in range(2):
            ch = c if d == 0 else NC - 1 - c
            sl = pl.ds(pl.multiple_of(ch * L, L), L)
            wl = wl_s[ch]
            ktc = kt_s[ch]
            decs = dec_s[ch]
            scs = sc_s[ch]
            for hh in range(2):
                ci = d * 2 + hh
                cx = cst_s[ci]
                call_s[ch, ci] = cx.astype(BF16)
                kw = (ktc[DQ * hh:DQ * (hh + 1), :] * wl[ci:ci + 1, :]).astype(BF16)
                cst_s[ci] = (decs[ci:ci + 1, 0:1] * cx
                             + scs[ci:ci + 1, 0:1] * jnp.dot(kw, vext_of(sl, hh), preferred_element_type=F32))
        return carry

    lax.fori_loop(0, NC, state_body, 0)

    tt = lax.broadcasted_iota(jnp.int32, (L, L), 0)
    ss = lax.broadcasted_iota(jnp.int32, (L, L), 1)
    masks = (ss <= tt, ss >= tt)

    def chunk_out(ch):
        sl = pl.ds(pl.multiple_of(ch * L, L), L)
        cols = col_s[sl, :]
        rrow = rrow_s[ch]
        ktc = kt_s[ch]
        og = _sigmoid(o_ref[sl, :].astype(F32))
        for hh in range(2):
            qc = (q0_s if hh == 0 else q1_s)[sl, :]
            kth = ktc[DQ * hh:DQ * (hh + 1), :].astype(BF16)
            qk = jnp.dot(qc, kth, preferred_element_type=F32)
            vext = vext_of(sl, hh)
            hsum = None
            for d in range(2):
                ci = d * 2 + hh
                a_t = cols[:, ci:ci + 1]
                w_inter = cols[:, GATES_PER_PAIR + ci:GATES_PER_PAIR + ci + 1]
                emt = cols[:, 2 * GATES_PER_PAIR + ci:2 * GATES_PER_PAIR + ci + 1]
                w_intra = jnp.exp2(jnp.where(masks[d], rrow[ci:ci + 1, :] - a_t, neg))
                qcx = jnp.dot(qc, call_s[ch, ci], preferred_element_type=F32)
                nd = (jnp.dot((qk * w_intra).astype(BF16), vext, preferred_element_type=F32)
                      + w_inter * qcx)
                h = nd[:, :DV] / jnp.maximum(jnp.abs(nd[:, DV:]), emt)
                hsum = h if hsum is None else hsum + h
            y = hsum * lax.rsqrt(jnp.mean(hsum * hsum, axis=1, keepdims=True) + NORM_EPS) * gh_ref[hh:hh + 1, :]
            out_ref[sl, DV * hh:DV * (hh + 1)] = (og[:, DV * hh:DV * (hh + 1)] * y).astype(out_ref.dtype)

    def out_body(c, carry):
        chunk_out(2 * c)
        chunk_out(2 * c + 1)
        return carry

    lax.fori_loop(0, NC // 2, out_body, 0)


def _mlstm(proj, grow, conv_w, bias_r, g_head, B, S):
    T = B * S
    NC = S // MLSTM_LC
    pw = 2 * MLSTM_DQK
    vw = 2 * MLSTM_DV
    return pl.pallas_call(
        _mlstm_kernel,
        grid=(B, HEAD_PAIRS),
        in_specs=[
            pl.BlockSpec((S, pw), lambda b, p: (b, OFF_MQ // pw + p)),
            pl.BlockSpec((S, pw), lambda b, p: (b, OFF_MK // pw + p)),
            pl.BlockSpec((S, vw), lambda b, p: (b, OFF_MV // vw + p)),
            pl.BlockSpec((S, vw), lambda b, p: (b, OFF_MO // vw + p)),
            pl.BlockSpec((None, GATES_PER_PAIR, S), lambda b, p: (p, 0, b)),
            pl.BlockSpec((CONV_W, pw), lambda b, p: (0, p)),
            pl.BlockSpec((CONV_W, pw), lambda b, p: (0, MQ_COLS // pw + p)),
            pl.BlockSpec((None, GATES_PER_PAIR, 1), lambda b, p: (p, 0, 0)),
            pl.BlockSpec((None, 2, MLSTM_DV), lambda b, p: (p, 0, 0)),
        ],
        out_specs=pl.BlockSpec((S, vw), lambda b, p: (b, p)),
        out_shape=jax.ShapeDtypeStruct((T, MLSTM_WIDTH), BF16),
        scratch_shapes=[
            pltpu.VMEM((S, MLSTM_DQK), BF16),
            pltpu.VMEM((S, MLSTM_DQK), BF16),
            pltpu.VMEM((NC, pw, MLSTM_LC), F32),
            pltpu.VMEM((S, LANES), F32),
            pltpu.VMEM((NC, GATES_PER_PAIR, MLSTM_LC), F32),
            pltpu.VMEM((NC, GATES_PER_PAIR, MLSTM_LC), F32),
            pltpu.VMEM((NC, GATES_PER_PAIR, MLSTM_LC), F32),
            pltpu.VMEM((NC, GATES_PER_PAIR, MLSTM_LC), F32),
            pltpu.VMEM((CHAINS, MLSTM_DQK, 2 * MLSTM_DV), F32),
            pltpu.VMEM((NC, CHAINS, MLSTM_DQK, 2 * MLSTM_DV), BF16),
            pltpu.VMEM((S + 2 * SUBLANES, pw), F32),
        ],
        compiler_params=_cparams(("parallel", "parallel")),
        name="mlstm",
    )(proj, proj, proj, proj, grow, conv_w, conv_w, bias_r, g_head)


def _norm_rope(x, g, cos, sin_signed):
    xn = x * lax.rsqrt(jnp.mean(x * x, axis=1, keepdims=True) + NORM_EPS) * g
    lane = lax.broadcasted_iota(jnp.int32, x.shape, 1)
    first_half = (lane % (2 * ROPE_FREQS)) < ROPE_FREQS
    partner = jnp.where(first_half,
                        pltpu.roll(xn, LANES - ROPE_FREQS, 1),
                        pltpu.roll(xn, ROPE_FREQS, 1))
    return xn * cos + partner * sin_signed


def _attn_kernel(q_ref, k_ref, v_ref, cq_ref, sq_ref, ck_ref, sk_ref, gq_ref, gk_ref, o_ref, kr_s, vx_s):
    d = ATTN_HEAD_DIM

    @pl.when(pl.program_id(2) == 0)
    def _():
        kr_s[...] = _norm_rope(k_ref[...].astype(F32), gk_ref[...], ck_ref[...], sk_ref[...]).astype(BF16)
        vx_s[...] = jnp.concatenate([v_ref[...], jnp.ones(v_ref.shape, BF16)], axis=1)

    cq, sq, gq = cq_ref[...], sq_ref[...], gq_ref[...]
    kr = kr_s[...]
    vx = vx_s[...]
    for g in range(ATTN_GROUP):
        qg = (_norm_rope(q_ref[:, d * g:d * (g + 1)].astype(F32), gq, cq, sq) * (d ** -0.5 * LOG2E)).astype(BF16)
        s = lax.dot_general(qg, kr, (((1,), (1,)), ((), ())), preferred_element_type=F32)
        p = jnp.exp2(s - jnp.max(s, axis=1, keepdims=True))
        ov = jnp.dot(p.astype(BF16), vx, preferred_element_type=F32)
        o_ref[:, d * g:d * (g + 1)] = (ov[:, :d] / ov[:, d:]).astype(o_ref.dtype)


def _attn(proj, cos_t, sin_t, g_q, g_k, B, S, tq=256):
    T = B * S
    d = ATTN_HEAD_DIM
    gw = ATTN_GROUP * d
    nq = S // tq
    return pl.pallas_call(
        _attn_kernel,
        grid=(B, ATTN_KV_HEADS, nq),
        in_specs=[
            pl.BlockSpec((tq, gw), lambda b, kv, qi: (b * nq + qi, OFF_AQ // gw + kv)),
            pl.BlockSpec((S, d), lambda b, kv, qi: (b, OFF_AK // d + kv)),
            pl.BlockSpec((S, d), lambda b, kv, qi: (b, OFF_AV // d + kv)),
            pl.BlockSpec((tq, d), lambda b, kv, qi: (qi, 0)),
            pl.BlockSpec((tq, d), lambda b, kv, qi: (qi, 0)),
            _resident((S, d), lambda b, kv, qi: (0, 0)),
            _resident((S, d), lambda b, kv, qi: (0, 0)),
            _resident((1, d), lambda b, kv, qi: (0, 0)),
            _resident((1, d), lambda b, kv, qi: (0, 0)),
        ],
        out_specs=pl.BlockSpec((tq, gw), lambda b, kv, qi: (b * nq + qi, kv)),
        out_shape=jax.ShapeDtypeStruct((T, ATTN_WIDTH), BF16),
        scratch_shapes=[pltpu.VMEM((S, d), BF16), pltpu.VMEM((S, 2 * d), BF16)],
        compiler_params=_cparams(("parallel", "parallel", "arbitrary")),
        name="attn",
    )(proj, proj, proj, cos_t, sin_t, cos_t, sin_t, g_q, g_k)


def _layer_norm(y, g, b):
    mu = jnp.mean(y, axis=1, keepdims=True)
    yc = y - mu
    var = jnp.mean(yc * yc, axis=1, keepdims=True)
    return yc * lax.rsqrt(var + NORM_EPS) * g + b


def _split_bf16(x):
    hi = x.astype(BF16)
    return hi, (x - hi.astype(F32)).astype(BF16)


def _outproj_kernel(hm_ref, ha_ref, x_ref, wt_ref, wb_ref, g_ref, b_ref, wr_ref, x1_ref, x1b_ref, aff_ref):
    y = (ALPHA * x_ref[...]
         + jnp.dot(hm_ref[...], wt_ref[...], preferred_element_type=F32)
         + jnp.dot(ha_ref[...], wb_ref[...], preferred_element_type=F32))
    x1 = _layer_norm(y, g_ref[...], b_ref[...])
    x1_ref[...] = x1
    xh, xl = _split_bf16(x1)
    wh, wl = _split_bf16(wr_ref[...])
    x1b_ref[...] = xh
    nt = (((1,), (1,)), ((), ()))
    logits = (lax.dot_general(wh, xh, nt, preferred_element_type=F32)
              + lax.dot_general(wh, xl, nt, preferred_element_type=F32)
              + lax.dot_general(wl, xh, nt, preferred_element_type=F32))
    e = jnp.exp(logits - jnp.max(logits, axis=0, keepdims=True))
    aff_ref[...] = e / jnp.sum(e, axis=0, keepdims=True)


def _outproj(hm, ha, x2, wt, wb, g, b, wr, tm=512):
    T, D = x2.shape
    return pl.pallas_call(
        _outproj_kernel,
        grid=(T // tm,),
        in_specs=[
            pl.BlockSpec((tm, MLSTM_WIDTH), lambda i: (i, 0)),
            pl.BlockSpec((tm, ATTN_WIDTH), lambda i: (i, 0)),
            pl.BlockSpec((tm, D), lambda i: (i, 0)),
            _resident((MLSTM_WIDTH, D), lambda i: (0, 0)),
            _resident((ATTN_WIDTH, D), lambda i: (0, 0)),
            _resident((1, D), lambda i: (0, 0)),
            _resident((1, D), lambda i: (0, 0)),
            _resident((N_EXPERTS, D), lambda i: (0, 0)),
        ],
        out_specs=[
            pl.BlockSpec((tm, D), lambda i: (i, 0)),
            pl.BlockSpec((tm, D), lambda i: (i, 0)),
            pl.BlockSpec((N_EXPERTS, tm), lambda i: (0, i)),
        ],
        out_shape=[
            jax.ShapeDtypeStruct((T, D), F32),
            jax.ShapeDtypeStruct((T, D), BF16),
            jax.ShapeDtypeStruct((N_EXPERTS, T), F32),
        ],
        compiler_params=_cparams(("parallel",)),
        name="outproj",
    )(hm, ha, x2, wt, wb, g, b, wr)


def _topk_kernel(aff_ref, pos_ref, tri_s, *, cap):
    S = aff_ref.shape[1]

    @pl.when(pl.program_id(0) == 0)
    def _():
        r = lax.broadcasted_iota(jnp.int32, (S, S), 0)
        c = lax.broadcasted_iota(jnp.int32, (S, S), 1)
        tri_s[...] = jnp.where(r < c, 1.0, 0.0).astype(BF16)

    a = aff_ref[...]

    def count_ge(v):
        return jnp.sum(jnp.where(a >= v, 1.0, 0.0), axis=1, keepdims=True)

    thr = jnp.zeros((a.shape[0], 1), jnp.int32)
    for bit in range(30, -1, -1):
        cand = thr | (1 << bit)
        thr = jnp.where(count_ge(pltpu.bitcast(cand, F32)) >= cap, cand, thr)
    lo = pltpu.bitcast(thr, F32)
    hi = pltpu.bitcast(thr + 1, F32)
    for _ in range(TOPK_REFINE_STEPS):
        mid = 0.5 * (lo + hi)
        ok = count_ge(mid) >= cap
        lo = jnp.where(ok, mid, lo)
        hi = jnp.where(ok, hi, mid)
    gt = a > lo
    eq = a == lo
    need = cap - jnp.sum(jnp.where(gt, 1.0, 0.0), axis=1, keepdims=True)
    tri = tri_s[...]
    eq_rank = jnp.dot(jnp.where(eq, 1.0, 0.0).astype(BF16), tri, preferred_element_type=F32)
    sel = jnp.logical_or(gt, jnp.logical_and(eq, eq_rank < need))
    pos = jnp.dot(jnp.where(sel, 1.0, 0.0).astype(BF16), tri, preferred_element_type=F32)
    pos_ref[...] = jnp.where(sel, pos, -1.0)


def _topk(aff, B, S, cap):
    E = aff.shape[0]
    return pl.pallas_call(
        functools.partial(_topk_kernel, cap=cap),
        grid=(B,),
        in_specs=[pl.BlockSpec((E, S), lambda b: (0, b))],
        out_specs=pl.BlockSpec((E, S), lambda b: (0, b)),
        out_shape=jax.ShapeDtypeStruct((E, B * S), F32),
        scratch_shapes=[pltpu.VMEM((S, S), BF16)],
        compiler_params=_cparams(("arbitrary",)),
        name="topk",
    )(aff)


def _moe_ffn_kernel(x_ref, pos_ref, aff_ref, wg_ref, wu_ref, wd_ref, y_ref, *, cap):
    S = x_ref.shape[0]
    pos = pos_ref[...]
    slot = lax.broadcasted_iota(jnp.int32, (cap, S), 0).astype(F32)
    hit = slot == pos
    xs = jnp.dot(jnp.where(hit, 1.0, 0.0).astype(BF16), x_ref[...],
                 preferred_element_type=F32).astype(BF16)
    gate = jnp.sum(jnp.where(hit, aff_ref[...], 0.0), axis=1, keepdims=True)
    hg = jnp.dot(xs, wg_ref[...], preferred_element_type=F32)
    hu = jnp.dot(xs, wu_ref[...], preferred_element_type=F32)
    hid = (hg * _sigmoid(hg) * hu).astype(BF16)
    y_ref[...] = (jnp.dot(hid, wd_ref[...], preferred_element_type=F32) * gate).astype(y_ref.dtype)


def _moe_ffn(x1b, pos4, aff4, wg, wu, wd, B, S, cap):
    E, D, F = wg.shape
    return pl.pallas_call(
        functools.partial(_moe_ffn_kernel, cap=cap),
        grid=(E, B),
        in_specs=[
            pl.BlockSpec((S, D), lambda e, b: (b, 0)),
            pl.BlockSpec((None, None, 1, S), lambda e, b: (e, b, 0, 0)),
            pl.BlockSpec((None, None, 1, S), lambda e, b: (e, b, 0, 0)),
            pl.BlockSpec((None, D, F), lambda e, b: (e, 0, 0)),
            pl.BlockSpec((None, D, F), lambda e, b: (e, 0, 0)),
            pl.BlockSpec((None, F, D), lambda e, b: (e, 0, 0)),
        ],
        out_specs=pl.BlockSpec((None, None, cap, D), lambda e, b: (b, e, 0, 0)),
        out_shape=jax.ShapeDtypeStruct((B, E, cap, D), BF16),
        compiler_params=_cparams(("arbitrary", "arbitrary")),
        name="moe_ffn",
    )(x1b, pos4, aff4, wg, wu, wd)


def _final_kernel(pos_ref, y_ref, x1_ref, x1b_ref, p_ref, wpg_ref, bpg_ref, wpp_ref, g_ref, b_ref, o_ref, *, cap):
    eg, _, ts = pos_ref.shape
    j = pl.program_id(2)
    last = pl.num_programs(2) - 1
    slot = lax.broadcasted_iota(jnp.int32, (cap, ts), 0).astype(F32)
    onehot = jnp.concatenate([jnp.where(slot == pos_ref[i], 1.0, 0.0).astype(BF16) for i in range(eg)],
                             axis=0)
    ys = y_ref[...].reshape(eg * cap, y_ref.shape[2])
    part = lax.dot_general(onehot, ys, (((0,), (0,)), ((), ())), preferred_element_type=F32)

    @pl.when(j == 0)
    def _():
        o_ref[...] = part

    @pl.when(jnp.logical_and(j > 0, j < last))
    def _():
        o_ref[...] += part

    @pl.when(j == last)
    def _():
        gate = _sigmoid(jnp.dot(x1b_ref[...], wpg_ref[...], preferred_element_type=F32) + bpg_ref[...])
        plv = gate * jnp.dot(p_ref[...].astype(BF16), wpp_ref[...], preferred_element_type=F32)
        o_ref[...] = _layer_norm(ALPHA * x1_ref[...] + (o_ref[...] + part) + plv, g_ref[...], b_ref[...])


def _final(pos4, y, x1, x1b, p2, wpg, bpg, wpp, g, b, B, S, cap, ts=512, eg=4):
    T, D = x1.shape
    E = y.shape[1]
    nt = S // ts
    tok = lambda b_, t, e: (b_ * nt + t, 0)
    const = lambda b_, t, e: (0, 0)
    return pl.pallas_call(
        functools.partial(_final_kernel, cap=cap),
        grid=(B, nt, E // eg),
        in_specs=[
            pl.BlockSpec((eg, None, 1, ts), lambda b_, t, e: (e, b_, 0, t)),
            pl.BlockSpec((None, eg, cap, D), lambda b_, t, e: (b_, e, 0, 0)),
            pl.BlockSpec((ts, D), tok),
            pl.BlockSpec((ts, D), tok),
            pl.BlockSpec((ts, P_DIM), tok),
            _resident((D, D), const),
            _resident((1, D), const),
            _resident((P_DIM, D), const),
            _resident((1, D), const),
            _resident((1, D), const),
        ],
        out_specs=pl.BlockSpec((ts, D), tok),
        out_shape=jax.ShapeDtypeStruct((T, D), F32),
        compiler_params=_cparams(("parallel", "parallel", "arbitrary")),
        name="final",
    )(pos4, y, x1, x1b, p2, wpg, bpg, wpp, g, b)


def _rope_tables(S):
    rows = S // GRID_W
    row_idx = jnp.broadcast_to(jnp.arange(rows, dtype=F32)[:, None], (rows, GRID_W)).reshape(-1)
    col_idx = jnp.broadcast_to(jnp.arange(GRID_W, dtype=F32)[None, :], (rows, GRID_W)).reshape(-1)
    inv_freq = ROPE_THETA ** (-jnp.arange(ROPE_FREQS, dtype=F32) / ROPE_FREQS)
    ar = row_idx[:, None] * inv_freq
    ac = col_idx[:, None] * inv_freq
    cos_t = jnp.concatenate([jnp.cos(ar), jnp.cos(ar), jnp.cos(ac), jnp.cos(ac)], axis=1)
    sin_t = jnp.concatenate([-jnp.sin(ar), jnp.sin(ar), -jnp.sin(ac), jnp.sin(ac)], axis=1)
    return cos_t, sin_t


def _pair_major(a):
    return a.reshape(2, HEAD_PAIRS, 2).transpose(1, 0, 2)


def _layer(x2, p2, B, S, w_in, conv_w, b_i, b_f, g_mlstm, g_q, g_k, w_out, ln1_g, ln1_b, w_router, w_gate, w_up,
           w_down, w_pl_proj, w_pl_gate, b_pl_gate, ln2_g, ln2_b):
    D = D_MODEL
    cap = CAPACITY_FACTOR * S // N_EXPERTS
    o_mg = 2 * MQ_COLS + 2 * MV_COLS
    w_main = jnp.concatenate([w_in[:, :o_mg], w_in[:, o_mg + MG_COLS:]], axis=1).astype(BF16)
    w_g = w_in[:, o_mg:o_mg + MG_COLS].reshape(D, 2, 2, HEAD_PAIRS, 2).transpose(0, 3, 2, 1, 4).reshape(D, MG_COLS)
    bias = jnp.stack([_pair_major(b_i), _pair_major(b_f)], axis=1).reshape(HEAD_PAIRS, GATES_PER_PAIR)

    proj, grow = _proj(x2, w_main, w_g.astype(BF16).T)
    h_m = _mlstm(proj, grow, conv_w, bias[:, :, None], g_mlstm.reshape(HEAD_PAIRS, 2, MLSTM_DV), B, S)
    cos_t, sin_t = _rope_tables(S)
    h_a = _attn(proj, cos_t, sin_t, g_q[None, :], g_k[None, :], B, S)
    w_o = w_out.astype(BF16)
    x1, x1b, aff = _outproj(h_m, h_a, x2, w_o[:MLSTM_WIDTH], w_o[MLSTM_WIDTH:], ln1_g[None, :], ln1_b[None, :],
                            w_router.T)
    pos = _topk(aff, B, S, cap)
    pos4 = pos.reshape(N_EXPERTS, B, 1, S)
    aff4 = aff.reshape(N_EXPERTS, B, 1, S)
    y = _moe_ffn(x1b, pos4, aff4, w_gate.astype(BF16), w_up.astype(BF16), w_down.astype(BF16), B, S, cap)
    return _final(pos4, y, x1, x1b, p2, w_pl_gate.astype(BF16), b_pl_gate[None, :], w_pl_proj.astype(BF16),
                  ln2_g[None, :], ln2_b[None, :], B, S, cap)


def kernel(x, p, w_in, conv_w, b_igate, b_fgate, g_mlstm, g_q, g_k, w_out, ln1_g, ln1_b, w_router, w_gate, w_up,
           w_down, w_pl_proj, w_pl_gate, b_pl_gate, ln2_g, ln2_b):
    B, S, D = x.shape
    x2 = x.reshape(B * S, D)
    for i in range(DEPTH):
        x2 = _layer(x2, p[i].reshape(B * S, P_DIM), B, S, w_in[i], conv_w[i], b_igate[i], b_fgate[i], g_mlstm[i],
                    g_q[i], g_k[i], w_out[i], ln1_g[i], ln1_b[i], w_router[i], w_gate[i], w_up[i], w_down[i],
                    w_pl_proj[i], w_pl_gate[i], b_pl_gate[i], ln2_g[i], ln2_b[i])
    return x2.reshape(B, S, D)
```

```python
import functools

import jax
import jax.numpy as jnp
from jax import lax
from jax.experimental import pallas as pl
from jax.experimental.pallas import tpu as pltpu

F32 = jnp.float32
BF16 = jnp.bfloat16

D_MODEL = 2048
P_DIM = 256
GRID_W = 64
MLSTM_WIDTH = D_MODEL // 2
ATTN_WIDTH = D_MODEL - MLSTM_WIDTH
MLSTM_HEADS = 8
MLSTM_DV = MLSTM_WIDTH // MLSTM_HEADS
MLSTM_DQK = MLSTM_DV // 2
CONV_W = 5
ATTN_HEAD_DIM = 128
ATTN_Q_HEADS = ATTN_WIDTH // ATTN_HEAD_DIM
ATTN_KV_HEADS = 2
ATTN_GROUP = ATTN_Q_HEADS // ATTN_KV_HEADS
ROPE_FREQS = ATTN_HEAD_DIM // 4
ROPE_THETA = 10000.0
N_EXPERTS = 16
EXPERT_FF = D_MODEL // 2
CAPACITY_FACTOR = 2
NORM_EPS = 1e-6
DEPTH = 1
ALPHA = (2.0 * DEPTH) ** 0.25
LOG2E = 1.4426950408889634

MQ_COLS = MLSTM_HEADS * MLSTM_DQK
MV_COLS = MLSTM_WIDTH
MG_COLS = 2 * 2 * MLSTM_HEADS
AQ_COLS = ATTN_WIDTH
AKV_COLS = ATTN_KV_HEADS * ATTN_HEAD_DIM
PROJ_COLS = 2 * MQ_COLS + 2 * MV_COLS + AQ_COLS + 2 * AKV_COLS
OFF_MQ, OFF_MK, OFF_MV, OFF_MO = 0, MQ_COLS, 2 * MQ_COLS, 2 * MQ_COLS + MV_COLS
OFF_AQ = OFF_MO + MV_COLS
OFF_AK = OFF_AQ + AQ_COLS
OFF_AV = OFF_AK + AKV_COLS

HEAD_PAIRS = MLSTM_HEADS // 2
GATES_PER_PAIR = 8
CHAINS = 4
MLSTM_LC = 128
ATTN_CHAIN_ROWS = 256
ROW_CHAIN = 256
LANES = 128
SUBLANES = 8
TOK_TILE = 16
COL_BLOCKS = D_MODEL // LANES
V7X_VMEM_LIMIT = 56 * 1024 * 1024
TOPK_REFINE_STEPS = 24


def _cparams(sem, vmem=V7X_VMEM_LIMIT):
    return pltpu.CompilerParams(dimension_semantics=sem, vmem_limit_bytes=vmem)


def _resident(shape, index_map):
    return pl.BlockSpec(shape, index_map, pipeline_mode=pl.Buffered(1))


def _sigmoid(x):
    return 0.5 * jnp.tanh(0.5 * x) + 0.5


def _log_sigmoid(x):
    return jnp.minimum(x, 0.0) - jnp.log1p(jnp.exp(-jnp.abs(x)))


def _proj_kernel(x_ref, w_ref, wgt_ref, o_ref, gr_ref, xb_ref):
    @pl.when(pl.program_id(1) == 0)
    def _():
        xb = x_ref[...].astype(BF16)
        xb_ref[...] = xb
        gr = lax.dot_general(wgt_ref[...], xb, (((1,), (1,)), ((), ())),
                             preferred_element_type=F32)
        for p in range(HEAD_PAIRS):
            gr_ref[p] = gr[GATES_PER_PAIR * p:GATES_PER_PAIR * (p + 1), :]

    o_ref[...] = jnp.dot(xb_ref[...], w_ref[...], preferred_element_type=F32).astype(o_ref.dtype)


def _proj(x2, w, wgt, tm=1024, tn=1536):
    T, D = x2.shape
    N = w.shape[1]
    return pl.pallas_call(
        _proj_kernel,
        grid=(T // tm, N // tn),
        in_specs=[
            pl.BlockSpec((tm, D), lambda i, j: (i, 0)),
            pl.BlockSpec((D, tn), lambda i, j: (0, j)),
            _resident((MG_COLS, D), lambda i, j: (0, 0)),
        ],
        out_specs=[
            pl.BlockSpec((tm, tn), lambda i, j: (i, j)),
            pl.BlockSpec((HEAD_PAIRS, GATES_PER_PAIR, tm), lambda i, j: (0, 0, i)),
        ],
        out_shape=[
            jax.ShapeDtypeStruct((T, N), BF16),
            jax.ShapeDtypeStruct((HEAD_PAIRS, GATES_PER_PAIR, T), F32),
        ],
        scratch_shapes=[pltpu.VMEM((tm, D), BF16)],
        compiler_params=_cparams(("parallel", "arbitrary")),
        name="proj",
    )(x2, w, wgt)


def _conv_silu(x, w, pad_s):
    S, C = x.shape
    half = CONV_W // 2
    halo = jnp.zeros((SUBLANES, C), F32)
    pad_s[0:SUBLANES, :] = halo
    pad_s[SUBLANES + S:2 * SUBLANES + S, :] = halo
    pad_s[SUBLANES:SUBLANES + S, :] = x
    acc = x * w[half:half + 1, :]
    for j in range(CONV_W):
        if j != half:
            acc = acc + pad_s[SUBLANES + j - half:SUBLANES + j - half + S, :] * w[j:j + 1, :]
    return acc * _sigmoid(acc)


def _mlstm_kernel(q_ref, k_ref, v_ref, o_ref, gr_ref, cwq_ref, cwk_ref, br_ref, gh_ref, out_ref,
                  q0_s, q1_s, kt_s, col_s, rrow_s, wl_s, dec_s, sc_s, cst_s, call_s, pad_s):
    S = q_ref.shape[0]
    L = MLSTM_LC
    NC = S // L
    DQ = MLSTM_DQK
    DV = MLSTM_DV
    hi = lax.Precision.HIGHEST
    neg = -jnp.inf

    g8 = gr_ref[...] + br_ref[...]
    li8 = g8 * LOG2E
    lf8 = pltpu.roll(_log_sigmoid(g8) * LOG2E, CHAINS, 0)
    row = lax.broadcasted_iota(jnp.int32, (GATES_PER_PAIR, L), 0)
    lane = lax.broadcasted_iota(jnp.int32, (GATES_PER_PAIR, L), 1)
    fwd = (row % CHAINS) < 2
    fwd1 = fwd[:, :1]
    si = lax.broadcasted_iota(jnp.int32, (L, 2 * L), 0)
    ti = lax.broadcasted_iota(jnp.int32, (L, 2 * L), 1)
    tri = jnp.where(ti < L, jnp.where(si <= ti, 1.0, 0.0), jnp.where(si >= ti - L, 1.0, 0.0))
    tot, mloc, b_l, cm_l = [], [], [], []
    for c in range(NC):
        pr = jnp.dot(lf8[:, c * L:(c + 1) * L], tri, precision=hi, preferred_element_type=F32)
        b_c = jnp.where(fwd, pr[:, :L], pr[:, L:])
        tot_c = pr[:, L - 1:L]
        li_c = li8[:, c * L:(c + 1) * L]
        r_c = li_c - b_c
        cm = r_c
        k = 1
        while k < L:
            pre = jnp.where(lane >= k, pltpu.roll(cm, k, 1), neg)
            suf = jnp.where(lane < L - k, pltpu.roll(cm, L - k, 1), neg)
            cm = jnp.maximum(cm, jnp.where(fwd, pre, suf))
            k *= 2
        g_c = tot_c - b_c + li_c
        mloc_c = jnp.max(g_c, axis=1, keepdims=True)
        wl_s[c] = jnp.exp2(g_c - mloc_c)
        rrow_s[c] = r_c
        tot.append(tot_c)
        mloc.append(mloc_c)
        b_l.append(b_c)
        cm_l.append(cm)

    def scan(order):
        m = jnp.zeros((GATES_PER_PAIR, 1), F32)
        m_in, dec, sc = [None] * NC, [None] * NC, [None] * NC
        for c in order:
            m_new = jnp.maximum(tot[c] + m, mloc[c])
            m_in[c] = m
            dec[c] = jnp.exp2(tot[c] + m - m_new)
            sc[c] = jnp.exp2(mloc[c] - m_new)
            m = m_new
        return m_in, dec, sc

    mf, df, sf = scan(range(NC))
    mb, db, sb = scan(range(NC - 1, -1, -1))
    pad = jnp.zeros((LANES - 3 * GATES_PER_PAIR, L), F32)
    for c in range(NC):
        m_in = jnp.where(fwd1, mf[c], mb[c])
        dec_s[c] = jnp.broadcast_to(jnp.where(fwd1, df[c], db[c]), (GATES_PER_PAIR, L))
        sc_s[c] = jnp.broadcast_to(jnp.where(fwd1, sf[c], sb[c]), (GATES_PER_PAIR, L))
        a_c = jnp.maximum(m_in, cm_l[c])
        per_t = jnp.concatenate([a_c, jnp.exp2(m_in - a_c), jnp.exp2(-(b_l[c] + a_c)), pad], axis=0)
        col_s[c * L:(c + 1) * L, :] = per_t.T

    qs = (_conv_silu(q_ref[...].astype(F32), cwq_ref[...], pad_s) * (DQ ** -0.5)).astype(BF16)
    q0_s[...] = qs[:, :DQ]
    q1_s[...] = qs[:, DQ:]
    kt = _conv_silu(k_ref[...].astype(F32), cwk_ref[...], pad_s).T
    for c in range(NC):
        kt_s[c] = kt[:, c * L:(c + 1) * L]

    ones_blk = jnp.ones((L, DV), BF16)

    def vext_of(sl, hh):
        return jnp.concatenate([v_ref[sl, DV * hh:DV * (hh + 1)], ones_blk], axis=1)

    cst_s[...] = jnp.zeros(cst_s.shape, F32)

    def state_body(c, carry):
        for d in range(2):
            ch = c if d == 0 else NC - 1 - c
            sl = pl.ds(pl.multiple_of(ch * L, L), L)
            wl = wl_s[ch]
            ktc = kt_s[ch]
            decs = dec_s[ch]
            scs = sc_s[ch]
            for hh in range(2):
                ci = d * 2 + hh
                cx = cst_s[ci]
                call_s[ch, ci] = cx.astype(BF16)
                kw = (ktc[DQ * hh:DQ * (hh + 1), :] * wl[ci:ci + 1, :]).astype(BF16)
                cst_s[ci] = (decs[ci:ci + 1, 0:1] * cx
                             + scs[ci:ci + 1, 0:1] * jnp.dot(kw, vext_of(sl, hh), preferred_element_type=F32))
        return carry

    lax.fori_loop(0, NC, state_body, 0)

    tt = lax.broadcasted_iota(jnp.int32, (L, L), 0)
    ss = lax.broadcasted_iota(jnp.int32, (L, L), 1)
    masks = (ss <= tt, ss >= tt)

    def chunk_out(ch):
        sl = pl.ds(pl.multiple_of(ch * L, L), L)
        cols = col_s[sl, :]
        rrow = rrow_s[ch]
        ktc = kt_s[ch]
        og = _sigmoid(o_ref[sl, :].astype(F32))
        for hh in range(2):
            qc = (q0_s if hh == 0 else q1_s)[sl, :]
            kth = ktc[DQ * hh:DQ * (hh + 1), :].astype(BF16)
            qk = jnp.dot(qc, kth, preferred_element_type=F32)
            vext = vext_of(sl, hh)
            hsum = None
            for d in range(2):
                ci = d * 2 + hh
                a_t = cols[:, ci:ci + 1]
                w_inter = cols[:, GATES_PER_PAIR + ci:GATES_PER_PAIR + ci + 1]
                emt = cols[:, 2 * GATES_PER_PAIR + ci:2 * GATES_PER_PAIR + ci + 1]
                w_intra = jnp.exp2(jnp.where(masks[d], rrow[ci:ci + 1, :] - a_t, neg))
                qcx = jnp.dot(qc, call_s[ch, ci], preferred_element_type=F32)
                nd = (jnp.dot((qk * w_intra).astype(BF16), vext, preferred_element_type=F32)
                      + w_inter * qcx)
                h = nd[:, :DV] / jnp.maximum(jnp.abs(nd[:, DV:]), emt)
                hsum = h if hsum is None else hsum + h
            y = hsum * lax.rsqrt(jnp.mean(hsum * hsum, axis=1, keepdims=True) + NORM_EPS) * gh_ref[hh:hh + 1, :]
            out_ref[sl, DV * hh:DV * (hh + 1)] = (og[:, DV * hh:DV * (hh + 1)] * y).astype(out_ref.dtype)

    def out_body(c, carry):
        chunk_out(2 * c)
        chunk_out(2 * c + 1)
        return carry

    lax.fori_loop(0, NC // 2, out_body, 0)


def _mlstm(proj, grow, conv_w, bias_r, g_head, B, S):
    T = B * S
    NC = S // MLSTM_LC
    pw = 2 * MLSTM_DQK
    vw = 2 * MLSTM_DV
    return pl.pallas_call(
        _mlstm_kernel,
        grid=(B, HEAD_PAIRS),
        in_specs=[
            pl.BlockSpec((S, pw), lambda b, p: (b, OFF_MQ // pw + p)),
            pl.BlockSpec((S, pw), lambda b, p: (b, OFF_MK // pw + p)),
            pl.BlockSpec((S, vw), lambda b, p: (b, OFF_MV // vw + p)),
            pl.BlockSpec((S, vw), lambda b, p: (b, OFF_MO // vw + p)),
            pl.BlockSpec((None, GATES_PER_PAIR, S), lambda b, p: (p, 0, b)),
            pl.BlockSpec((CONV_W, pw), lambda b, p: (0, p)),
            pl.BlockSpec((CONV_W, pw), lambda b, p: (0, MQ_COLS // pw + p)),
            pl.BlockSpec((None, GATES_PER_PAIR, 1), lambda b, p: (p, 0, 0)),
            pl.BlockSpec((None, 2, MLSTM_DV), lambda b, p: (p, 0, 0)),
        ],
        out_specs=pl.BlockSpec((S, vw), lambda b, p: (b, p)),
        out_shape=jax.ShapeDtypeStruct((T, MLSTM_WIDTH), BF16),
        scratch_shapes=[
            pltpu.VMEM((S, MLSTM_DQK), BF16),
            pltpu.VMEM((S, MLSTM_DQK), BF16),
            pltpu.VMEM((NC, pw, MLSTM_LC), F32),
            pltpu.VMEM((S, LANES), F32),
            pltpu.VMEM((NC, GATES_PER_PAIR, MLSTM_LC), F32),
            pltpu.VMEM((NC, GATES_PER_PAIR, MLSTM_LC), F32),
            pltpu.VMEM((NC, GATES_PER_PAIR, MLSTM_LC), F32),
            pltpu.VMEM((NC, GATES_PER_PAIR, MLSTM_LC), F32),
            pltpu.VMEM((CHAINS, MLSTM_DQK, 2 * MLSTM_DV), F32),
            pltpu.VMEM((NC, CHAINS, MLSTM_DQK, 2 * MLSTM_DV), BF16),
            pltpu.VMEM((S + 2 * SUBLANES, pw), F32),
        ],
        compiler_params=_cparams(("parallel", "parallel")),
        name="mlstm",
    )(proj, proj, proj, proj, grow, conv_w, conv_w, bias_r, g_head)


def _norm_rope(x, g, cos, sin_signed):
    xn = x * lax.rsqrt(jnp.mean(x * x, axis=1, keepdims=True) + NORM_EPS) * g
    lane = lax.broadcasted_iota(jnp.int32, x.shape, 1)
    first_half = (lane % (2 * ROPE_FREQS)) < ROPE_FREQS
    partner = jnp.where(first_half,
                        pltpu.roll(xn, LANES - ROPE_FREQS, 1),
                        pltpu.roll(xn, ROPE_FREQS, 1))
    return xn * cos + partner * sin_signed


def _attn_kernel(q_ref, k_ref, v_ref, cq_ref, sq_ref, ck_ref, sk_ref, gq_ref, gk_ref, o_ref, kr_s, vx_s):
    d = ATTN_HEAD_DIM

    @pl.when(pl.program_id(2) == 0)
    def _():
        kr_s[...] = _norm_rope(k_ref[...].astype(F32), gk_ref[...], ck_ref[...], sk_ref[...]).astype(BF16)
        vx_s[...] = jnp.concatenate([v_ref[...], jnp.ones(v_ref.shape, BF16)], axis=1)

    gq = gq_ref[...]
    kr = kr_s[...]
    vx = vx_s[...]
    for r0 in range(0, q_ref.shape[0], ATTN_CHAIN_ROWS):
        rows = slice(r0, r0 + ATTN_CHAIN_ROWS)
        cq, sq = cq_ref[rows, :], sq_ref[rows, :]
        for g in range(ATTN_GROUP):
            cols = slice(d * g, d * (g + 1))
            qg = (_norm_rope(q_ref[rows, cols].astype(F32), gq, cq, sq) * (d ** -0.5 * LOG2E)).astype(BF16)
            s = lax.dot_general(qg, kr, (((1,), (1,)), ((), ())), preferred_element_type=F32)
            p = jnp.exp2(s - jnp.max(s, axis=1, keepdims=True))
            ov = jnp.dot(p.astype(BF16), vx, preferred_element_type=F32)
            o_ref[rows, cols] = (ov[:, :d] / ov[:, d:]).astype(o_ref.dtype)


def _attn(proj, cos_t, sin_t, g_q, g_k, B, S, tq=512):
    T = B * S
    d = ATTN_HEAD_DIM
    gw = ATTN_GROUP * d
    nq = S // tq
    return pl.pallas_call(
        _attn_kernel,
        grid=(B, ATTN_KV_HEADS, nq),
        in_specs=[
            pl.BlockSpec((tq, gw), lambda b, kv, qi: (b * nq + qi, OFF_AQ // gw + kv)),
            pl.BlockSpec((S, d), lambda b, kv, qi: (b, OFF_AK // d + kv)),
            pl.BlockSpec((S, d), lambda b, kv, qi: (b, OFF_AV // d + kv)),
            pl.BlockSpec((tq, d), lambda b, kv, qi: (qi, 0)),
            pl.BlockSpec((tq, d), lambda b, kv, qi: (qi, 0)),
            _resident((S, d), lambda b, kv, qi: (0, 0)),
            _resident((S, d), lambda b, kv, qi: (0, 0)),
            _resident((1, d), lambda b, kv, qi: (0, 0)),
            _resident((1, d), lambda b, kv, qi: (0, 0)),
        ],
        out_specs=pl.BlockSpec((tq, gw), lambda b, kv, qi: (b * nq + qi, kv)),
        out_shape=jax.ShapeDtypeStruct((T, ATTN_WIDTH), BF16),
        scratch_shapes=[pltpu.VMEM((S, d), BF16), pltpu.VMEM((S, 2 * d), BF16)],
        compiler_params=_cparams(("parallel", "parallel", "arbitrary")),
        name="attn",
    )(proj, proj, proj, cos_t, sin_t, cos_t, sin_t, g_q, g_k)


def _layer_norm(y, g, b):
    mu = jnp.mean(y, axis=1, keepdims=True)
    yc = y - mu
    var = jnp.mean(yc * yc, axis=1, keepdims=True)
    return yc * lax.rsqrt(var + NORM_EPS) * g + b


def _split_bf16(x):
    hi = x.astype(BF16)
    return hi, (x - hi.astype(F32)).astype(BF16)


def _tile_perm():
    n = TOK_TILE * COL_BLOCKS
    a = lax.broadcasted_iota(jnp.int32, (n, n), 0)
    b = lax.broadcasted_iota(jnp.int32, (n, n), 1)
    return jnp.where((a >> 4) == (b & 15), jnp.where((a & 15) == (b >> 4), 1.0, 0.0), 0.0).astype(BF16)


def _token_major(xh, perm):
    out = []
    for r0 in range(0, xh.shape[0], 2 * TOK_TILE):
        halves = [jnp.concatenate([xh[r0 + h * TOK_TILE:r0 + (h + 1) * TOK_TILE, LANES * j:LANES * (j + 1)]
                                   for j in range(COL_BLOCKS)], axis=0) for h in range(2)]
        o2 = jnp.dot(perm, jnp.concatenate(halves, axis=1), preferred_element_type=F32).astype(BF16)
        out += [o2[:, :LANES], o2[:, LANES:]]
    return jnp.concatenate(out, axis=0)


def _row_major(g_ref, perm):
    n = TOK_TILE * COL_BLOCKS
    cols = [[] for _ in range(COL_BLOCKS)]
    for r0 in range(0, g_ref.shape[0], 2 * n):
        m2 = jnp.concatenate([g_ref[r0:r0 + n, :], g_ref[r0 + n:r0 + 2 * n, :]], axis=1)
        o2 = jnp.dot(perm, m2, preferred_element_type=F32).astype(BF16)
        for j in range(COL_BLOCKS):
            blk = o2[TOK_TILE * j:TOK_TILE * (j + 1), :]
            cols[j] += [blk[:, :LANES], blk[:, LANES:]]
    return jnp.concatenate([jnp.concatenate(cj, axis=0) for cj in cols], axis=1)


def _outproj_kernel(hm_ref, ha_ref, x_ref, wt_ref, wb_ref, g_ref, b_ref, wr_ref, x1_ref, x1b_ref, xg_ref, aff_ref):
    y = (ALPHA * x_ref[...]
         + jnp.dot(hm_ref[...], wt_ref[...], preferred_element_type=F32)
         + jnp.dot(ha_ref[...], wb_ref[...], preferred_element_type=F32))
    x1 = _layer_norm(y, g_ref[...], b_ref[...])
    x1_ref[...] = x1
    xh, xl = _split_bf16(x1)
    wh, wl = _split_bf16(wr_ref[...])
    x1b_ref[...] = xh
    xg_ref[...] = _token_major(xh, _tile_perm())
    nt = (((1,), (1,)), ((), ()))
    logits = (lax.dot_general(wh, xh, nt, preferred_element_type=F32)
              + lax.dot_general(wh, xl, nt, preferred_element_type=F32)
              + lax.dot_general(wl, xh, nt, preferred_element_type=F32))
    e = jnp.exp(logits - jnp.max(logits, axis=0, keepdims=True))
    aff_ref[...] = e / jnp.sum(e, axis=0, keepdims=True)


def _outproj(hm, ha, x2, wt, wb, g, b, wr, tm=512):
    T, D = x2.shape
    return pl.pallas_call(
        _outproj_kernel,
        grid=(T // tm,),
        in_specs=[
            pl.BlockSpec((tm, MLSTM_WIDTH), lambda i: (i, 0)),
            pl.BlockSpec((tm, ATTN_WIDTH), lambda i: (i, 0)),
            pl.BlockSpec((tm, D), lambda i: (i, 0)),
            _resident((MLSTM_WIDTH, D), lambda i: (0, 0)),
            _resident((ATTN_WIDTH, D), lambda i: (0, 0)),
            _resident((1, D), lambda i: (0, 0)),
            _resident((1, D), lambda i: (0, 0)),
            _resident((N_EXPERTS, D), lambda i: (0, 0)),
        ],
        out_specs=[
            pl.BlockSpec((tm, D), lambda i: (i, 0)),
            pl.BlockSpec((tm, D), lambda i: (i, 0)),
            pl.BlockSpec((tm * TOK_TILE, LANES), lambda i: (i, 0)),
            pl.BlockSpec((N_EXPERTS, tm), lambda i: (0, i)),
        ],
        out_shape=[
            jax.ShapeDtypeStruct((T, D), F32),
            jax.ShapeDtypeStruct((T, D), BF16),
            jax.ShapeDtypeStruct((T * TOK_TILE, LANES), BF16),
            jax.ShapeDtypeStruct((N_EXPERTS, T), F32),
        ],
        compiler_params=_cparams(("parallel",)),
        name="outproj",
    )(hm, ha, x2, wt, wb, g, b, wr)


def _topk_kernel(aff_ref, pos_ref, idx_ref, tri_s, *, cap):
    S = aff_ref.shape[1]

    @pl.when(pl.program_id(0) == 0)
    def _():
        r = lax.broadcasted_iota(jnp.int32, (S, S), 0)
        c = lax.broadcasted_iota(jnp.int32, (S, S), 1)
        tri_s[...] = jnp.where(r < c, 1.0, 0.0).astype(BF16)

    a = aff_ref[...]

    def count_ge(v):
        return jnp.sum(jnp.where(a >= v, 1.0, 0.0), axis=1, keepdims=True)

    thr = jnp.zeros((a.shape[0], 1), jnp.int32)
    for bit in range(30, -1, -1):
        cand = thr | (1 << bit)
        thr = jnp.where(count_ge(pltpu.bitcast(cand, F32)) >= cap, cand, thr)
    lo = pltpu.bitcast(thr, F32)
    hi = pltpu.bitcast(thr + 1, F32)
    for _ in range(TOPK_REFINE_STEPS):
        mid = 0.5 * (lo + hi)
        ok = count_ge(mid) >= cap
        lo = jnp.where(ok, mid, lo)
        hi = jnp.where(ok, hi, mid)
    gt = a > lo
    eq = a == lo
    need = cap - jnp.sum(jnp.where(gt, 1.0, 0.0), axis=1, keepdims=True)
    tri = tri_s[...]
    eq_rank = jnp.dot(jnp.where(eq, 1.0, 0.0).astype(BF16), tri, preferred_element_type=F32)
    sel = jnp.logical_or(gt, jnp.logical_and(eq, eq_rank < need))
    pos = jnp.dot(jnp.where(sel, 1.0, 0.0).astype(BF16), tri, preferred_element_type=F32)
    posm = jnp.where(sel, pos, -1.0)
    pos_ref[...] = posm
    digit_row = lax.broadcasted_iota(jnp.int32, (2 * SUBLANES, S), 0)
    tok = lax.broadcasted_iota(jnp.int32, (2 * SUBLANES, S), 1)
    digits = jnp.where(digit_row == 0, tok >> 4, jnp.where(digit_row == 1, tok & 15, 0)).astype(F32).astype(BF16)
    slot = lax.broadcasted_iota(jnp.int32, (cap, S), 0).astype(F32)
    for e in range(a.shape[0]):
        hit = jnp.where(slot == posm[e:e + 1, :], 1.0, 0.0).astype(BF16)
        idx_ref[e] = lax.dot_general(digits, hit, (((1,), (1,)), ((), ())),
                                     preferred_element_type=F32)[:SUBLANES]


def _topk(aff, B, S, cap):
    E = aff.shape[0]
    return pl.pallas_call(
        functools.partial(_topk_kernel, cap=cap),
        grid=(B,),
        in_specs=[pl.BlockSpec((E, S), lambda b: (0, b))],
        out_specs=[pl.BlockSpec((E, S), lambda b: (0, b)),
                   pl.BlockSpec((E, None, SUBLANES, cap), lambda b: (0, b, 0, 0))],
        out_shape=[jax.ShapeDtypeStruct((E, B * S), F32),
                   jax.ShapeDtypeStruct((E, B, SUBLANES, cap), F32)],
        scratch_shapes=[pltpu.VMEM((S, S), BF16)],
        compiler_params=_cparams(("arbitrary",)),
        name="topk",
    )(aff)


def _moe_ffn_kernel(idx_ref, xg_ref, pos_ref, aff_ref, wg_ref, wu_ref, wd_ref, y_ref, gbuf_s, *, cap):
    S = pos_ref.shape[1]
    base = (pl.program_id(0) * pl.num_programs(1) + pl.program_id(1)) * cap
    for c in range(cap):
        row = pl.multiple_of(idx_ref[base + c] * TOK_TILE, TOK_TILE)
        gbuf_s[TOK_TILE * c:TOK_TILE * (c + 1), :] = xg_ref[pl.ds(row, TOK_TILE), :]
    xs = _row_major(gbuf_s, _tile_perm())
    slot = lax.broadcasted_iota(jnp.int32, (cap, S), 0).astype(F32)
    gate = jnp.sum(jnp.where(slot == pos_ref[...], aff_ref[...], 0.0), axis=1, keepdims=True)
    hg = jnp.dot(xs, wg_ref[...], preferred_element_type=F32)
    hu = jnp.dot(xs, wu_ref[...], preferred_element_type=F32)
    hid = (hg * _sigmoid(hg) * hu).astype(BF16)
    y_ref[...] = (jnp.dot(hid, wd_ref[...], preferred_element_type=F32) * gate).astype(y_ref.dtype)


def _moe_ffn(idx, xg, pos4, aff4, wg, wu, wd, B, S, cap):
    E, D, F = wg.shape
    grid_spec = pltpu.PrefetchScalarGridSpec(
        num_scalar_prefetch=1,
        grid=(E, B),
        in_specs=[
            pl.BlockSpec((S * TOK_TILE, LANES), lambda e, b, idx_ref: (b, 0)),
            pl.BlockSpec((None, None, 1, S), lambda e, b, idx_ref: (e, b, 0, 0)),
            pl.BlockSpec((None, None, 1, S), lambda e, b, idx_ref: (e, b, 0, 0)),
            pl.BlockSpec((None, D, F), lambda e, b, idx_ref: (e, 0, 0)),
            pl.BlockSpec((None, D, F), lambda e, b, idx_ref: (e, 0, 0)),
            pl.BlockSpec((None, F, D), lambda e, b, idx_ref: (e, 0, 0)),
        ],
        out_specs=pl.BlockSpec((None, None, cap, D), lambda e, b, idx_ref: (b, e, 0, 0)),
        scratch_shapes=[pltpu.VMEM((cap * TOK_TILE, LANES), BF16)],
    )
    return pl.pallas_call(
        functools.partial(_moe_ffn_kernel, cap=cap),
        grid_spec=grid_spec,
        out_shape=jax.ShapeDtypeStruct((B, E, cap, D), BF16),
        compiler_params=_cparams(("arbitrary", "arbitrary")),
        name="moe_ffn",
    )(idx, xg, pos4, aff4, wg, wu, wd)


def _moe_comb_kernel(pos_ref, y_ref, o_ref, *, cap):
    ts = pos_ref.shape[1]

    @pl.when(pl.program_id(2) == 0)
    def _():
        o_ref[...] = jnp.zeros(o_ref.shape, F32)

    eg = pos_ref.shape[0]
    slot = lax.broadcasted_iota(jnp.int32, (cap, ts), 0).astype(F32)
    onehot = jnp.concatenate([jnp.where(slot == pos_ref[i], 1.0, 0.0).astype(BF16) for i in range(eg)],
                             axis=0)
    ys = y_ref[...].reshape(eg * cap, y_ref.shape[2])
    o_ref[...] += lax.dot_general(onehot, ys, (((0,), (0,)), ((), ())), preferred_element_type=F32)


def _moe_comb(pos4, y, B, S, cap, ts=512, eg=4):
    E, D = y.shape[1], y.shape[3]
    nt = S // ts
    return pl.pallas_call(
        functools.partial(_moe_comb_kernel, cap=cap),
        grid=(B, nt, E // eg),
        in_specs=[
            pl.BlockSpec((eg, None, 1, ts), lambda b, t, e: (e, b, 0, t)),
            pl.BlockSpec((None, eg, cap, D), lambda b, t, e: (b, e, 0, 0)),
        ],
        out_specs=pl.BlockSpec((ts, D), lambda b, t, e: (b * nt + t, 0)),
        out_shape=jax.ShapeDtypeStruct((B * S, D), F32),
        compiler_params=_cparams(("parallel", "parallel", "arbitrary")),
        name="moe_comb",
    )(pos4, y)


def _final_kernel(x1_ref, x1b_ref, moe_ref, p_ref, wpg_ref, bpg_ref, wpp_ref, g_ref, b_ref, o_ref):
    for r0 in range(0, x1_ref.shape[0], ROW_CHAIN):
        rows = slice(r0, r0 + ROW_CHAIN)
        gate = _sigmoid(jnp.dot(x1b_ref[rows, :], wpg_ref[...], preferred_element_type=F32) + bpg_ref[...])
        plv = gate * jnp.dot(p_ref[rows, :].astype(BF16), wpp_ref[...], preferred_element_type=F32)
        o_ref[rows, :] = _layer_norm(ALPHA * x1_ref[rows, :] + moe_ref[rows, :] + plv, g_ref[...], b_ref[...])


def _final(x1, x1b, moe, p2, wpg, bpg, wpp, g, b, tm=512):
    T, D = x1.shape
    return pl.pallas_call(
        _final_kernel,
        grid=(T // tm,),
        in_specs=[
            pl.BlockSpec((tm, D), lambda i: (i, 0)),
            pl.BlockSpec((tm, D), lambda i: (i, 0)),
            pl.BlockSpec((tm, D), lambda i: (i, 0)),
            pl.BlockSpec((tm, P_DIM), lambda i: (i, 0)),
            _resident((D, D), lambda i: (0, 0)),
            _resident((1, D), lambda i: (0, 0)),
            _resident((P_DIM, D), lambda i: (0, 0)),
            _resident((1, D), lambda i: (0, 0)),
            _resident((1, D), lambda i: (0, 0)),
        ],
        out_specs=pl.BlockSpec((tm, D), lambda i: (i, 0)),
        out_shape=jax.ShapeDtypeStruct((T, D), F32),
        compiler_params=_cparams(("parallel",)),
        name="final",
    )(x1, x1b, moe, p2, wpg, bpg, wpp, g, b)


def _rope_tables(S):
    rows = S // GRID_W
    row_idx = jnp.broadcast_to(jnp.arange(rows, dtype=F32)[:, None], (rows, GRID_W)).reshape(-1)
    col_idx = jnp.broadcast_to(jnp.arange(GRID_W, dtype=F32)[None, :], (rows, GRID_W)).reshape(-1)
    inv_freq = ROPE_THETA ** (-jnp.arange(ROPE_FREQS, dtype=F32) / ROPE_FREQS)
    ar = row_idx[:, None] * inv_freq
    ac = col_idx[:, None] * inv_freq
    cos_t = jnp.concatenate([jnp.cos(ar), jnp.cos(ar), jnp.cos(ac), jnp.cos(ac)], axis=1)
    sin_t = jnp.concatenate([-jnp.sin(ar), jnp.sin(ar), -jnp.sin(ac), jnp.sin(ac)], axis=1)
    return cos_t, sin_t


def _pair_major(a):
    return a.reshape(2, HEAD_PAIRS, 2).transpose(1, 0, 2)


def _layer(x2, p2, B, S, w_in, conv_w, b_i, b_f, g_mlstm, g_q, g_k, w_out, ln1_g, ln1_b, w_router, w_gate, w_up,
           w_down, w_pl_proj, w_pl_gate, b_pl_gate, ln2_g, ln2_b):
    D = D_MODEL
    cap = CAPACITY_FACTOR * S // N_EXPERTS
    o_mg = 2 * MQ_COLS + 2 * MV_COLS
    w_main = jnp.concatenate([w_in[:, :o_mg], w_in[:, o_mg + MG_COLS:]], axis=1).astype(BF16)
    w_g = w_in[:, o_mg:o_mg + MG_COLS].reshape(D, 2, 2, HEAD_PAIRS, 2).transpose(0, 3, 2, 1, 4).reshape(D, MG_COLS)
    bias = jnp.stack([_pair_major(b_i), _pair_major(b_f)], axis=1).reshape(HEAD_PAIRS, GATES_PER_PAIR)

    proj, grow = _proj(x2, w_main, w_g.astype(BF16).T)
    h_m = _mlstm(proj, grow, conv_w, bias[:, :, None], g_mlstm.reshape(HEAD_PAIRS, 2, MLSTM_DV), B, S)
    cos_t, sin_t = _rope_tables(S)
    h_a = _attn(proj, cos_t, sin_t, g_q[None, :], g_k[None, :], B, S)
    w_o = w_out.astype(BF16)
    x1, x1b, xg, aff = _outproj(h_m, h_a, x2, w_o[:MLSTM_WIDTH], w_o[MLSTM_WIDTH:], ln1_g[None, :], ln1_b[None, :],
                                w_router.T)
    pos, idx_digits = _topk(aff, B, S, cap)
    pos4 = pos.reshape(N_EXPERTS, B, 1, S)
    aff4 = aff.reshape(N_EXPERTS, B, 1, S)
    idx = (idx_digits[:, :, 0, :] * TOK_TILE + idx_digits[:, :, 1, :]).astype(jnp.int32).reshape(-1)
    y = _moe_ffn(idx, xg, pos4, aff4, w_gate.astype(BF16), w_up.astype(BF16), w_down.astype(BF16), B, S, cap)
    moe = _moe_comb(pos4, y, B, S, cap)
    return _final(x1, x1b, moe, p2, w_pl_gate.astype(BF16), b_pl_gate[None, :], w_pl_proj.astype(BF16),
                  ln2_g[None, :], ln2_b[None, :])


def kernel(x, p, w_in, conv_w, b_igate, b_fgate, g_mlstm, g_q, g_k, w_out, ln1_g, ln1_b, w_router, w_gate, w_up,
           w_down, w_pl_proj, w_pl_gate, b_pl_gate, ln2_g, ln2_b):
    B, S, D = x.shape
    x2 = x.reshape(B * S, D)
    for i in range(DEPTH):
        x2 = _layer(x2, p[i].reshape(B * S, P_DIM), B, S, w_in[i], conv_w[i], b_igate[i], b_fgate[i], g_mlstm[i],
                    g_q[i], g_k[i], w_out[i], ln1_g[i], ln1_b[i], w_router[i], w_gate[i], w_up[i], w_down[i],
                    w_pl_proj[i], w_pl_gate[i], b_pl_gate[i], ln2_g[i], ln2_b[i])
    return x2.reshape(B, S, D)
```

```python
import functools

import jax
import jax.numpy as jnp
from jax import lax
from jax.experimental import pallas as pl
from jax.experimental.pallas import tpu as pltpu

F32 = jnp.float32
BF16 = jnp.bfloat16

D_MODEL = 2048
P_DIM = 256
GRID_W = 64
MLSTM_WIDTH = D_MODEL // 2
ATTN_WIDTH = D_MODEL - MLSTM_WIDTH
MLSTM_HEADS = 8
MLSTM_DV = MLSTM_WIDTH // MLSTM_HEADS
MLSTM_DQK = MLSTM_DV // 2
CONV_W = 5
ATTN_HEAD_DIM = 128
ATTN_Q_HEADS = ATTN_WIDTH // ATTN_HEAD_DIM
ATTN_KV_HEADS = 2
ATTN_GROUP = ATTN_Q_HEADS // ATTN_KV_HEADS
ROPE_FREQS = ATTN_HEAD_DIM // 4
ROPE_THETA = 10000.0
N_EXPERTS = 16
EXPERT_FF = D_MODEL // 2
CAPACITY_FACTOR = 2
NORM_EPS = 1e-6
DEPTH = 1
ALPHA = (2.0 * DEPTH) ** 0.25
LOG2E = 1.4426950408889634

MQ_COLS = MLSTM_HEADS * MLSTM_DQK
MV_COLS = MLSTM_WIDTH
MG_COLS = 2 * 2 * MLSTM_HEADS
AQ_COLS = ATTN_WIDTH
AKV_COLS = ATTN_KV_HEADS * ATTN_HEAD_DIM
PROJ_COLS = 2 * MQ_COLS + 2 * MV_COLS + AQ_COLS + 2 * AKV_COLS
OFF_MQ, OFF_MK, OFF_MV, OFF_MO = 0, MQ_COLS, 2 * MQ_COLS, 2 * MQ_COLS + MV_COLS
OFF_AQ = OFF_MO + MV_COLS
OFF_AK = OFF_AQ + AQ_COLS
OFF_AV = OFF_AK + AKV_COLS

HEAD_PAIRS = MLSTM_HEADS // 2
GATES_PER_PAIR = 8
CHAINS = 4
MLSTM_LC = 128
ATTN_CHAIN_ROWS = 256
ROW_CHAIN = 256
LANES = 128
SUBLANES = 8
TOK_TILE = 16
COL_BLOCKS = D_MODEL // LANES
SCATTER_BATCH = 8
V7X_VMEM_LIMIT = 56 * 1024 * 1024
TOPK_REFINE_STEPS = 24


def _cparams(sem, vmem=V7X_VMEM_LIMIT):
    return pltpu.CompilerParams(dimension_semantics=sem, vmem_limit_bytes=vmem)


def _resident(shape, index_map):
    return pl.BlockSpec(shape, index_map, pipeline_mode=pl.Buffered(1))


def _sigmoid(x):
    return 0.5 * jnp.tanh(0.5 * x) + 0.5


def _log_sigmoid(x):
    return jnp.minimum(x, 0.0) - jnp.log1p(jnp.exp(-jnp.abs(x)))


def _proj_kernel(x_ref, w_ref, wgt_ref, o_ref, gr_ref, xb_ref):
    @pl.when(pl.program_id(1) == 0)
    def _():
        xb = x_ref[...].astype(BF16)
        xb_ref[...] = xb
        gr = lax.dot_general(wgt_ref[...], xb, (((1,), (1,)), ((), ())),
                             preferred_element_type=F32)
        for p in range(HEAD_PAIRS):
            gr_ref[p] = gr[GATES_PER_PAIR * p:GATES_PER_PAIR * (p + 1), :]

    o_ref[...] = jnp.dot(xb_ref[...], w_ref[...], preferred_element_type=F32).astype(o_ref.dtype)


def _proj(x2, w, wgt, tm=1024, tn=1536):
    T, D = x2.shape
    N = w.shape[1]
    return pl.pallas_call(
        _proj_kernel,
        grid=(T // tm, N // tn),
        in_specs=[
            pl.BlockSpec((tm, D), lambda i, j: (i, 0)),
            pl.BlockSpec((D, tn), lambda i, j: (0, j)),
            _resident((MG_COLS, D), lambda i, j: (0, 0)),
        ],
        out_specs=[
            pl.BlockSpec((tm, tn), lambda i, j: (i, j)),
            pl.BlockSpec((HEAD_PAIRS, GATES_PER_PAIR, tm), lambda i, j: (0, 0, i)),
        ],
        out_shape=[
            jax.ShapeDtypeStruct((T, N), BF16),
            jax.ShapeDtypeStruct((HEAD_PAIRS, GATES_PER_PAIR, T), F32),
        ],
        scratch_shapes=[pltpu.VMEM((tm, D), BF16)],
        compiler_params=_cparams(("parallel", "arbitrary")),
        name="proj",
    )(x2, w, wgt)


def _conv_silu(x, w, pad_s):
    S, C = x.shape
    half = CONV_W // 2
    halo = jnp.zeros((SUBLANES, C), F32)
    pad_s[0:SUBLANES, :] = halo
    pad_s[SUBLANES + S:2 * SUBLANES + S, :] = halo
    pad_s[SUBLANES:SUBLANES + S, :] = x
    acc = x * w[half:half + 1, :]
    for j in range(CONV_W):
        if j != half:
            acc = acc + pad_s[SUBLANES + j - half:SUBLANES + j - half + S, :] * w[j:j + 1, :]
    return acc * _sigmoid(acc)


def _mlstm_kernel(q_ref, k_ref, v_ref, o_ref, gr_ref, cwq_ref, cwk_ref, br_ref, gh_ref, out_ref,
                  q0_s, q1_s, kt_s, col_s, rrow_s, wl_s, dec_s, sc_s, cst_s, call_s, pad_s):
    S = q_ref.shape[0]
    L = MLSTM_LC
    NC = S // L
    DQ = MLSTM_DQK
    DV = MLSTM_DV
    hi = lax.Precision.HIGHEST
    neg = -jnp.inf

    g8 = gr_ref[...] + br_ref[...]
    li8 = g8 * LOG2E
    lf8 = pltpu.roll(_log_sigmoid(g8) * LOG2E, CHAINS, 0)
    row = lax.broadcasted_iota(jnp.int32, (GATES_PER_PAIR, L), 0)
    lane = lax.broadcasted_iota(jnp.int32, (GATES_PER_PAIR, L), 1)
    fwd = (row % CHAINS) < 2
    fwd1 = fwd[:, :1]
    si = lax.broadcasted_iota(jnp.int32, (L, 2 * L), 0)
    ti = lax.broadcasted_iota(jnp.int32, (L, 2 * L), 1)
    tri = jnp.where(ti < L, jnp.where(si <= ti, 1.0, 0.0), jnp.where(si >= ti - L, 1.0, 0.0))
    tot, mloc, b_l, cm_l = [], [], [], []
    for c in range(NC):
        pr = jnp.dot(lf8[:, c * L:(c + 1) * L], tri, precision=hi, preferred_element_type=F32)
        b_c = jnp.where(fwd, pr[:, :L], pr[:, L:])
        tot_c = pr[:, L - 1:L]
        li_c = li8[:, c * L:(c + 1) * L]
        r_c = li_c - b_c
        cm = r_c
        k = 1
        while k < L:
            pre = jnp.where(lane >= k, pltpu.roll(cm, k, 1), neg)
            suf = jnp.where(lane < L - k, pltpu.roll(cm, L - k, 1), neg)
            cm = jnp.maximum(cm, jnp.where(fwd, pre, suf))
            k *= 2
        g_c = tot_c - b_c + li_c
        mloc_c = jnp.max(g_c, axis=1, keepdims=True)
        wl_s[c] = jnp.exp2(g_c - mloc_c)
        rrow_s[c] = r_c
        tot.append(tot_c)
        mloc.append(mloc_c)
        b_l.append(b_c)
        cm_l.append(cm)

    def scan(order):
        m = jnp.zeros((GATES_PER_PAIR, 1), F32)
        m_in, dec, sc = [None] * NC, [None] * NC, [None] * NC
        for c in order:
            m_new = jnp.maximum(tot[c] + m, mloc[c])
            m_in[c] = m
            dec[c] = jnp.exp2(tot[c] + m - m_new)
            sc[c] = jnp.exp2(mloc[c] - m_new)
            m = m_new
        return m_in, dec, sc

    mf, df, sf = scan(range(NC))
    mb, db, sb = scan(range(NC - 1, -1, -1))
    pad = jnp.zeros((LANES - 3 * GATES_PER_PAIR, L), F32)
    for c in range(NC):
        m_in = jnp.where(fwd1, mf[c], mb[c])
        dec_s[c] = jnp.broadcast_to(jnp.where(fwd1, df[c], db[c]), (GATES_PER_PAIR, L))
        sc_s[c] = jnp.broadcast_to(jnp.where(fwd1, sf[c], sb[c]), (GATES_PER_PAIR, L))
        a_c = jnp.maximum(m_in, cm_l[c])
        per_t = jnp.concatenate([a_c, jnp.exp2(m_in - a_c), jnp.exp2(-(b_l[c] + a_c)), pad], axis=0)
        col_s[c * L:(c + 1) * L, :] = per_t.T

    qs = (_conv_silu(q_ref[...].astype(F32), cwq_ref[...], pad_s) * (DQ ** -0.5)).astype(BF16)
    q0_s[...] = qs[:, :DQ]
    q1_s[...] = qs[:, DQ:]
    kt = _conv_silu(k_ref[...].astype(F32), cwk_ref[...], pad_s).T
    for c in range(NC):
        kt_s[c] = kt[:, c * L:(c + 1) * L]

    ones_blk = jnp.ones((L, DV), BF16)

    def vext_of(sl, hh):
        return jnp.concatenate([v_ref[sl, DV * hh:DV * (hh + 1)], ones_blk], axis=1)

    cst_s[...] = jnp.zeros(cst_s.shape, F32)

    def state_body(c, carry):
        for d in range(2):
            ch = c if d == 0 else NC - 1 - c
            sl = pl.ds(pl.multiple_of(ch * L, L), L)
            wl = wl_s[ch]
            ktc = kt_s[ch]
            decs = dec_s[ch]
            scs = sc_s[ch]
            for hh in range(2):
                ci = d * 2 + hh
                cx = cst_s[ci]
                call_s[ch, ci] = cx.astype(BF16)
                kw = (ktc[DQ * hh:DQ * (hh + 1), :] * wl[ci:ci + 1, :]).astype(BF16)
                cst_s[ci] = (decs[ci:ci + 1, 0:1] * cx
                             + scs[ci:ci + 1, 0:1] * jnp.dot(kw, vext_of(sl, hh), preferred_element_type=F32))
        return carry

    lax.fori_loop(0, NC, state_body, 0)

    tt = lax.broadcasted_iota(jnp.int32, (L, L), 0)
    ss = lax.broadcasted_iota(jnp.int32, (L, L), 1)
    masks = (ss <= tt, ss >= tt)

    def chunk_out(ch):
        sl = pl.ds(pl.multiple_of(ch * L, L), L)
        cols = col_s[sl, :]
        rrow = rrow_s[ch]
        ktc = kt_s[ch]
        og = _sigmoid(o_ref[sl, :].astype(F32))
        for hh in range(2):
            qc = (q0_s if hh == 0 else q1_s)[sl, :]
            kth = ktc[DQ * hh:DQ * (hh + 1), :].astype(BF16)
            qk = jnp.dot(qc, kth, preferred_element_type=F32)
            vext = vext_of(sl, hh)
            hsum = None
            for d in range(2):
                ci = d * 2 + hh
                a_t = cols[:, ci:ci + 1]
                w_inter = cols[:, GATES_PER_PAIR + ci:GATES_PER_PAIR + ci + 1]
                emt = cols[:, 2 * GATES_PER_PAIR + ci:2 * GATES_PER_PAIR + ci + 1]
                w_intra = jnp.exp2(jnp.where(masks[d], rrow[ci:ci + 1, :] - a_t, neg))
                qcx = jnp.dot(qc, call_s[ch, ci], preferred_element_type=F32)
                nd = (jnp.dot((qk * w_intra).astype(BF16), vext, preferred_element_type=F32)
                      + w_inter * qcx)
                h = nd[:, :DV] / jnp.maximum(jnp.abs(nd[:, DV:]), emt)
                hsum = h if hsum is None else hsum + h
            y = hsum * lax.rsqrt(jnp.mean(hsum * hsum, axis=1, keepdims=True) + NORM_EPS) * gh_ref[hh:hh + 1, :]
            out_ref[sl, DV * hh:DV * (hh + 1)] = (og[:, DV * hh:DV * (hh + 1)] * y).astype(out_ref.dtype)

    def out_body(c, carry):
        chunk_out(2 * c)
        chunk_out(2 * c + 1)
        return carry

    lax.fori_loop(0, NC // 2, out_body, 0)


def _mlstm(proj, grow, conv_w, bias_r, g_head, B, S):
    T = B * S
    NC = S // MLSTM_LC
    pw = 2 * MLSTM_DQK
    vw = 2 * MLSTM_DV
    return pl.pallas_call(
        _mlstm_kernel,
        grid=(B, HEAD_PAIRS),
        in_specs=[
            pl.BlockSpec((S, pw), lambda b, p: (b, OFF_MQ // pw + p)),
            pl.BlockSpec((S, pw), lambda b, p: (b, OFF_MK // pw + p)),
            pl.BlockSpec((S, vw), lambda b, p: (b, OFF_MV // vw + p)),
            pl.BlockSpec((S, vw), lambda b, p: (b, OFF_MO // vw + p)),
            pl.BlockSpec((None, GATES_PER_PAIR, S), lambda b, p: (p, 0, b)),
            pl.BlockSpec((CONV_W, pw), lambda b, p: (0, p)),
            pl.BlockSpec((CONV_W, pw), lambda b, p: (0, MQ_COLS // pw + p)),
            pl.BlockSpec((None, GATES_PER_PAIR, 1), lambda b, p: (p, 0, 0)),
            pl.BlockSpec((None, 2, MLSTM_DV), lambda b, p: (p, 0, 0)),
        ],
        out_specs=pl.BlockSpec((S, vw), lambda b, p: (b, p)),
        out_shape=jax.ShapeDtypeStruct((T, MLSTM_WIDTH), BF16),
        scratch_shapes=[
            pltpu.VMEM((S, MLSTM_DQK), BF16),
            pltpu.VMEM((S, MLSTM_DQK), BF16),
            pltpu.VMEM((NC, pw, MLSTM_LC), F32),
            pltpu.VMEM((S, LANES), F32),
            pltpu.VMEM((NC, GATES_PER_PAIR, MLSTM_LC), F32),
            pltpu.VMEM((NC, GATES_PER_PAIR, MLSTM_LC), F32),
            pltpu.VMEM((NC, GATES_PER_PAIR, MLSTM_LC), F32),
            pltpu.VMEM((NC, GATES_PER_PAIR, MLSTM_LC), F32),
            pltpu.VMEM((CHAINS, MLSTM_DQK, 2 * MLSTM_DV), F32),
            pltpu.VMEM((NC, CHAINS, MLSTM_DQK, 2 * MLSTM_DV), BF16),
            pltpu.VMEM((S + 2 * SUBLANES, pw), F32),
        ],
        compiler_params=_cparams(("parallel", "parallel")),
        name="mlstm",
    )(proj, proj, proj, proj, grow, conv_w, conv_w, bias_r, g_head)


def _norm_rope(x, g, cos, sin_signed):
    xn = x * lax.rsqrt(jnp.mean(x * x, axis=1, keepdims=True) + NORM_EPS) * g
    lane = lax.broadcasted_iota(jnp.int32, x.shape, 1)
    first_half = (lane % (2 * ROPE_FREQS)) < ROPE_FREQS
    partner = jnp.where(first_half,
                        pltpu.roll(xn, LANES - ROPE_FREQS, 1),
                        pltpu.roll(xn, ROPE_FREQS, 1))
    return xn * cos + partner * sin_signed


def _attn_kernel(q_ref, k_ref, v_ref, cq_ref, sq_ref, ck_ref, sk_ref, gq_ref, gk_ref, o_ref, kr_s, vx_s):
    d = ATTN_HEAD_DIM

    @pl.when(pl.program_id(2) == 0)
    def _():
        kr_s[...] = _norm_rope(k_ref[...].astype(F32), gk_ref[...], ck_ref[...], sk_ref[...]).astype(BF16)
        vx_s[...] = jnp.concatenate([v_ref[...], jnp.ones(v_ref.shape, BF16)], axis=1)

    gq = gq_ref[...]
    kr = kr_s[...]
    vx = vx_s[...]
    for r0 in range(0, q_ref.shape[0], ATTN_CHAIN_ROWS):
        rows = slice(r0, r0 + ATTN_CHAIN_ROWS)
        cq, sq = cq_ref[rows, :], sq_ref[rows, :]
        for g in range(ATTN_GROUP):
            cols = slice(d * g, d * (g + 1))
            qg = (_norm_rope(q_ref[rows, cols].astype(F32), gq, cq, sq) * (d ** -0.5 * LOG2E)).astype(BF16)
            s = lax.dot_general(qg, kr, (((1,), (1,)), ((), ())), preferred_element_type=F32)
            p = jnp.exp2(s - jnp.max(s, axis=1, keepdims=True))
            ov = jnp.dot(p.astype(BF16), vx, preferred_element_type=F32)
            o_ref[rows, cols] = (ov[:, :d] / ov[:, d:]).astype(o_ref.dtype)


def _attn(proj, cos_t, sin_t, g_q, g_k, B, S, tq=512):
    T = B * S
    d = ATTN_HEAD_DIM
    gw = ATTN_GROUP * d
    nq = S // tq
    return pl.pallas_call(
        _attn_kernel,
        grid=(B, ATTN_KV_HEADS, nq),
        in_specs=[
            pl.BlockSpec((tq, gw), lambda b, kv, qi: (b * nq + qi, OFF_AQ // gw + kv)),
            pl.BlockSpec((S, d), lambda b, kv, qi: (b, OFF_AK // d + kv)),
            pl.BlockSpec((S, d), lambda b, kv, qi: (b, OFF_AV // d + kv)),
            pl.BlockSpec((tq, d), lambda b, kv, qi: (qi, 0)),
            pl.BlockSpec((tq, d), lambda b, kv, qi: (qi, 0)),
            _resident((S, d), lambda b, kv, qi: (0, 0)),
            _resident((S, d), lambda b, kv, qi: (0, 0)),
            _resident((1, d), lambda b, kv, qi: (0, 0)),
            _resident((1, d), lambda b, kv, qi: (0, 0)),
        ],
        out_specs=pl.BlockSpec((tq, gw), lambda b, kv, qi: (b * nq + qi, kv)),
        out_shape=jax.ShapeDtypeStruct((T, ATTN_WIDTH), BF16),
        scratch_shapes=[pltpu.VMEM((S, d), BF16), pltpu.VMEM((S, 2 * d), BF16)],
        compiler_params=_cparams(("parallel", "parallel", "arbitrary")),
        name="attn",
    )(proj, proj, proj, cos_t, sin_t, cos_t, sin_t, g_q, g_k)


def _layer_norm(y, g, b):
    mu = jnp.mean(y, axis=1, keepdims=True)
    yc = y - mu
    var = jnp.mean(yc * yc, axis=1, keepdims=True)
    return yc * lax.rsqrt(var + NORM_EPS) * g + b


def _split_bf16(x):
    hi = x.astype(BF16)
    return hi, (x - hi.astype(F32)).astype(BF16)


def _tile_perm():
    n = TOK_TILE * COL_BLOCKS
    a = lax.broadcasted_iota(jnp.int32, (n, n), 0)
    b = lax.broadcasted_iota(jnp.int32, (n, n), 1)
    return jnp.where((a >> 4) == (b & 15), jnp.where((a & 15) == (b >> 4), 1.0, 0.0), 0.0).astype(BF16)


def _token_major(xh, perm):
    out = []
    for r0 in range(0, xh.shape[0], 2 * TOK_TILE):
        halves = [jnp.concatenate([xh[r0 + h * TOK_TILE:r0 + (h + 1) * TOK_TILE, LANES * j:LANES * (j + 1)]
                                   for j in range(COL_BLOCKS)], axis=0) for h in range(2)]
        o2 = jnp.dot(perm, jnp.concatenate(halves, axis=1), preferred_element_type=F32).astype(BF16)
        out += [o2[:, :LANES], o2[:, LANES:]]
    return jnp.concatenate(out, axis=0)


def _row_major(g_ref, perm):
    n = TOK_TILE * COL_BLOCKS
    cols = [[] for _ in range(COL_BLOCKS)]
    for r0 in range(0, g_ref.shape[0], 2 * n):
        m2 = jnp.concatenate([g_ref[r0:r0 + n, :], g_ref[r0 + n:r0 + 2 * n, :]], axis=1)
        o2 = jnp.dot(perm, m2, preferred_element_type=F32).astype(BF16)
        for j in range(COL_BLOCKS):
            blk = o2[TOK_TILE * j:TOK_TILE * (j + 1), :]
            cols[j] += [blk[:, :LANES], blk[:, LANES:]]
    return jnp.concatenate([jnp.concatenate(cj, axis=0) for cj in cols], axis=1)


def _outproj_kernel(hm_ref, ha_ref, x_ref, wt_ref, wb_ref, g_ref, b_ref, wr_ref, x1_ref, x1b_ref, xg_ref, aff_ref):
    y = (ALPHA * x_ref[...]
         + jnp.dot(hm_ref[...], wt_ref[...], preferred_element_type=F32)
         + jnp.dot(ha_ref[...], wb_ref[...], preferred_element_type=F32))
    x1 = _layer_norm(y, g_ref[...], b_ref[...])
    x1_ref[...] = x1
    xh, xl = _split_bf16(x1)
    wh, wl = _split_bf16(wr_ref[...])
    x1b_ref[...] = xh
    xg_ref[...] = _token_major(xh, _tile_perm())
    nt = (((1,), (1,)), ((), ()))
    logits = (lax.dot_general(wh, xh, nt, preferred_element_type=F32)
              + lax.dot_general(wh, xl, nt, preferred_element_type=F32)
              + lax.dot_general(wl, xh, nt, preferred_element_type=F32))
    e = jnp.exp(logits - jnp.max(logits, axis=0, keepdims=True))
    aff_ref[...] = e / jnp.sum(e, axis=0, keepdims=True)


def _outproj(hm, ha, x2, wt, wb, g, b, wr, tm=512):
    T, D = x2.shape
    return pl.pallas_call(
        _outproj_kernel,
        grid=(T // tm,),
        in_specs=[
            pl.BlockSpec((tm, MLSTM_WIDTH), lambda i: (i, 0)),
            pl.BlockSpec((tm, ATTN_WIDTH), lambda i: (i, 0)),
            pl.BlockSpec((tm, D), lambda i: (i, 0)),
            _resident((MLSTM_WIDTH, D), lambda i: (0, 0)),
            _resident((ATTN_WIDTH, D), lambda i: (0, 0)),
            _resident((1, D), lambda i: (0, 0)),
            _resident((1, D), lambda i: (0, 0)),
            _resident((N_EXPERTS, D), lambda i: (0, 0)),
        ],
        out_specs=[
            pl.BlockSpec((tm, D), lambda i: (i, 0)),
            pl.BlockSpec((tm, D), lambda i: (i, 0)),
            pl.BlockSpec((tm * TOK_TILE, LANES), lambda i: (i, 0)),
            pl.BlockSpec((N_EXPERTS, tm), lambda i: (0, i)),
        ],
        out_shape=[
            jax.ShapeDtypeStruct((T, D), F32),
            jax.ShapeDtypeStruct((T, D), BF16),
            jax.ShapeDtypeStruct((T * TOK_TILE, LANES), BF16),
            jax.ShapeDtypeStruct((N_EXPERTS, T), F32),
        ],
        compiler_params=_cparams(("parallel",)),
        name="outproj",
    )(hm, ha, x2, wt, wb, g, b, wr)


def _topk_kernel(aff_ref, pos_ref, idx_ref, tri_s, *, cap):
    S = aff_ref.shape[1]

    @pl.when(pl.program_id(0) == 0)
    def _():
        r = lax.broadcasted_iota(jnp.int32, (S, S), 0)
        c = lax.broadcasted_iota(jnp.int32, (S, S), 1)
        tri_s[...] = jnp.where(r < c, 1.0, 0.0).astype(BF16)

    a = aff_ref[...]

    def count_ge(v):
        return jnp.sum(jnp.where(a >= v, 1.0, 0.0), axis=1, keepdims=True)

    thr = jnp.zeros((a.shape[0], 1), jnp.int32)
    for bit in range(30, -1, -1):
        cand = thr | (1 << bit)
        thr = jnp.where(count_ge(pltpu.bitcast(cand, F32)) >= cap, cand, thr)
    lo = pltpu.bitcast(thr, F32)
    hi = pltpu.bitcast(thr + 1, F32)
    for _ in range(TOPK_REFINE_STEPS):
        mid = 0.5 * (lo + hi)
        ok = count_ge(mid) >= cap
        lo = jnp.where(ok, mid, lo)
        hi = jnp.where(ok, hi, mid)
    gt = a > lo
    eq = a == lo
    need = cap - jnp.sum(jnp.where(gt, 1.0, 0.0), axis=1, keepdims=True)
    tri = tri_s[...]
    eq_rank = jnp.dot(jnp.where(eq, 1.0, 0.0).astype(BF16), tri, preferred_element_type=F32)
    sel = jnp.logical_or(gt, jnp.logical_and(eq, eq_rank < need))
    pos = jnp.dot(jnp.where(sel, 1.0, 0.0).astype(BF16), tri, preferred_element_type=F32)
    posm = jnp.where(sel, pos, -1.0)
    pos_ref[...] = posm
    digit_row = lax.broadcasted_iota(jnp.int32, (2 * SUBLANES, S), 0)
    tok = lax.broadcasted_iota(jnp.int32, (2 * SUBLANES, S), 1)
    digits = jnp.where(digit_row == 0, tok >> 4, jnp.where(digit_row == 1, tok & 15, 0)).astype(F32).astype(BF16)
    slot = lax.broadcasted_iota(jnp.int32, (cap, S), 0).astype(F32)
    for e in range(a.shape[0]):
        hit = jnp.where(slot == posm[e:e + 1, :], 1.0, 0.0).astype(BF16)
        idx_ref[e] = lax.dot_general(digits, hit, (((1,), (1,)), ((), ())),
                                     preferred_element_type=F32)[:SUBLANES]


def _topk(aff, B, S, cap):
    E = aff.shape[0]
    return pl.pallas_call(
        functools.partial(_topk_kernel, cap=cap),
        grid=(B,),
        in_specs=[pl.BlockSpec((E, S), lambda b: (0, b))],
        out_specs=[pl.BlockSpec((E, S), lambda b: (0, b)),
                   pl.BlockSpec((E, None, SUBLANES, cap), lambda b: (0, b, 0, 0))],
        out_shape=[jax.ShapeDtypeStruct((E, B * S), F32),
                   jax.ShapeDtypeStruct((E, B, SUBLANES, cap), F32)],
        scratch_shapes=[pltpu.VMEM((S, S), BF16)],
        compiler_params=_cparams(("arbitrary",)),
        name="topk",
    )(aff)


def _moe_ffn_kernel(idx_ref, xg_ref, pos_ref, aff_ref, wg_ref, wu_ref, wd_ref, y_ref, gbuf_s, *, cap):
    S = pos_ref.shape[1]
    base = (pl.program_id(0) * pl.num_programs(1) + pl.program_id(1)) * cap
    for c in range(cap):
        row = pl.multiple_of(idx_ref[base + c] * TOK_TILE, TOK_TILE)
        gbuf_s[TOK_TILE * c:TOK_TILE * (c + 1), :] = xg_ref[pl.ds(row, TOK_TILE), :]
    perm = _tile_perm()
    xs = _row_major(gbuf_s, perm)
    slot = lax.broadcasted_iota(jnp.int32, (cap, S), 0).astype(F32)
    gate = jnp.sum(jnp.where(slot == pos_ref[...], aff_ref[...], 0.0), axis=1, keepdims=True)
    hg = jnp.dot(xs, wg_ref[...], preferred_element_type=F32)
    hu = jnp.dot(xs, wu_ref[...], preferred_element_type=F32)
    hid = (hg * _sigmoid(hg) * hu).astype(BF16)
    y = (jnp.dot(hid, wd_ref[...], preferred_element_type=F32) * gate).astype(BF16)
    y_ref[...] = _token_major(y, perm)


def _moe_ffn(idx, xg, pos4, aff4, wg, wu, wd, B, S, cap):
    E, D, F = wg.shape
    grid_spec = pltpu.PrefetchScalarGridSpec(
        num_scalar_prefetch=1,
        grid=(E, B),
        in_specs=[
            pl.BlockSpec((S * TOK_TILE, LANES), lambda e, b, idx_ref: (b, 0)),
            pl.BlockSpec((None, None, 1, S), lambda e, b, idx_ref: (e, b, 0, 0)),
            pl.BlockSpec((None, None, 1, S), lambda e, b, idx_ref: (e, b, 0, 0)),
            pl.BlockSpec((None, D, F), lambda e, b, idx_ref: (e, 0, 0)),
            pl.BlockSpec((None, D, F), lambda e, b, idx_ref: (e, 0, 0)),
            pl.BlockSpec((None, F, D), lambda e, b, idx_ref: (e, 0, 0)),
        ],
        out_specs=pl.BlockSpec((None, None, cap * TOK_TILE, LANES), lambda e, b, idx_ref: (b, e, 0, 0)),
        scratch_shapes=[pltpu.VMEM((cap * TOK_TILE, LANES), BF16)],
    )
    return pl.pallas_call(
        functools.partial(_moe_ffn_kernel, cap=cap),
        grid_spec=grid_spec,
        out_shape=jax.ShapeDtypeStruct((B, E, cap * TOK_TILE, LANES), BF16),
        compiler_params=_cparams(("arbitrary", "arbitrary")),
        name="moe_ffn",
    )(idx, xg, pos4, aff4, wg, wu, wd)


def _row_major_f32(g_ref, r0, n_tok, perm):
    n = TOK_TILE * COL_BLOCKS
    cols = [[] for _ in range(COL_BLOCKS)]
    for k in range(0, n_tok * TOK_TILE, 2 * n):
        m2 = jnp.concatenate([g_ref[pl.ds(r0 + k, n), :], g_ref[pl.ds(r0 + k + n, n), :]], axis=1)
        hi = m2.astype(BF16)
        r1 = m2 - hi.astype(F32)
        mid = r1.astype(BF16)
        lo = (r1 - mid.astype(F32)).astype(BF16)
        o2 = (jnp.dot(perm, hi, preferred_element_type=F32) + jnp.dot(perm, mid, preferred_element_type=F32)
              + jnp.dot(perm, lo, preferred_element_type=F32))
        for j in range(COL_BLOCKS):
            blk = o2[TOK_TILE * j:TOK_TILE * (j + 1), :]
            cols[j] += [blk[:, :LANES], blk[:, LANES:]]
    return jnp.concatenate([jnp.concatenate(cj, axis=0) for cj in cols], axis=1)


def _moe_comb_kernel(idx_ref, y_ref, o_ref, acc_s, *, cap, n_exp):
    b = pl.program_id(0)
    j = pl.program_id(1)

    @pl.when(j == 0)
    def _():
        acc_s[...] = jnp.zeros(acc_s.shape, F32)

    @pl.when(j < n_exp)
    def _():
        base = (j * pl.num_programs(0) + b) * cap
        for c0 in range(0, cap, SCATTER_BATCH):
            rows = [pl.multiple_of(idx_ref[base + c0 + u] * TOK_TILE, TOK_TILE) for u in range(SCATTER_BATCH)]
            new = [acc_s[pl.ds(rows[u], TOK_TILE), :]
                   + y_ref[TOK_TILE * (c0 + u):TOK_TILE * (c0 + u + 1), :].astype(F32) for u in range(SCATTER_BATCH)]
            for u in range(SCATTER_BATCH):
                acc_s[pl.ds(rows[u], TOK_TILE), :] = new[u]

    @pl.when(j >= n_exp)
    def _():
        n_tok = o_ref.shape[0]
        r0 = pl.multiple_of((j - n_exp) * (n_tok * TOK_TILE), n_tok * TOK_TILE)
        o_ref[...] = _row_major_f32(acc_s, r0, n_tok, _tile_perm())


def _moe_comb(idx, y, B, S, cap, chunk=512):
    E = y.shape[1]
    D = D_MODEL
    nch = S // chunk
    grid_spec = pltpu.PrefetchScalarGridSpec(
        num_scalar_prefetch=1,
        grid=(B, E + nch),
        in_specs=[pl.BlockSpec((None, None, cap * TOK_TILE, LANES),
                               lambda b, j, idx_ref: (b, jnp.minimum(j, E - 1), 0, 0))],
        out_specs=pl.BlockSpec((chunk, D), lambda b, j, idx_ref: (b * nch + jnp.maximum(j - E, 0), 0)),
        scratch_shapes=[pltpu.VMEM((S * TOK_TILE, LANES), F32)],
    )
    return pl.pallas_call(
        functools.partial(_moe_comb_kernel, cap=cap, n_exp=E),
        grid_spec=grid_spec,
        out_shape=jax.ShapeDtypeStruct((B * S, D), F32),
        compiler_params=_cparams(("arbitrary", "arbitrary")),
        name="moe_comb",
    )(idx, y)


def _final_kernel(x1_ref, x1b_ref, moe_ref, p_ref, wpg_ref, bpg_ref, wpp_ref, g_ref, b_ref, o_ref):
    for r0 in range(0, x1_ref.shape[0], ROW_CHAIN):
        rows = slice(r0, r0 + ROW_CHAIN)
        gate = _sigmoid(jnp.dot(x1b_ref[rows, :], wpg_ref[...], preferred_element_type=F32) + bpg_ref[...])
        plv = gate * jnp.dot(p_ref[rows, :].astype(BF16), wpp_ref[...], preferred_element_type=F32)
        o_ref[rows, :] = _layer_norm(ALPHA * x1_ref[rows, :] + moe_ref[rows, :] + plv, g_ref[...], b_ref[...])


def _final(x1, x1b, moe, p2, wpg, bpg, wpp, g, b, tm=512):
    T, D = x1.shape
    return pl.pallas_call(
        _final_kernel,
        grid=(T // tm,),
        in_specs=[
            pl.BlockSpec((tm, D), lambda i: (i, 0)),
            pl.BlockSpec((tm, D), lambda i: (i, 0)),
            pl.BlockSpec((tm, D), lambda i: (i, 0)),
            pl.BlockSpec((tm, P_DIM), lambda i: (i, 0)),
            _resident((D, D), lambda i: (0, 0)),
            _resident((1, D), lambda i: (0, 0)),
            _resident((P_DIM, D), lambda i: (0, 0)),
            _resident((1, D), lambda i: (0, 0)),
            _resident((1, D), lambda i: (0, 0)),
        ],
        out_specs=pl.BlockSpec((tm, D), lambda i: (i, 0)),
        out_shape=jax.ShapeDtypeStruct((T, D), F32),
        compiler_params=_cparams(("parallel",)),
        name="final",
    )(x1, x1b, moe, p2, wpg, bpg, wpp, g, b)


def _rope_tables(S):
    rows = S // GRID_W
    row_idx = jnp.broadcast_to(jnp.arange(rows, dtype=F32)[:, None], (rows, GRID_W)).reshape(-1)
    col_idx = jnp.broadcast_to(jnp.arange(GRID_W, dtype=F32)[None, :], (rows, GRID_W)).reshape(-1)
    inv_freq = ROPE_THETA ** (-jnp.arange(ROPE_FREQS, dtype=F32) / ROPE_FREQS)
    ar = row_idx[:, None] * inv_freq
    ac = col_idx[:, None] * inv_freq
    cos_t = jnp.concatenate([jnp.cos(ar), jnp.cos(ar), jnp.cos(ac), jnp.cos(ac)], axis=1)
    sin_t = jnp.concatenate([-jnp.sin(ar), jnp.sin(ar), -jnp.sin(ac), jnp.sin(ac)], axis=1)
    return cos_t, sin_t


def _pair_major(a):
    return a.reshape(2, HEAD_PAIRS, 2).transpose(1, 0, 2)


def _layer(x2, p2, B, S, w_in, conv_w, b_i, b_f, g_mlstm, g_q, g_k, w_out, ln1_g, ln1_b, w_router, w_gate, w_up,
           w_down, w_pl_proj, w_pl_gate, b_pl_gate, ln2_g, ln2_b):
    D = D_MODEL
    cap = CAPACITY_FACTOR * S // N_EXPERTS
    o_mg = 2 * MQ_COLS + 2 * MV_COLS
    w_main = jnp.concatenate([w_in[:, :o_mg], w_in[:, o_mg + MG_COLS:]], axis=1).astype(BF16)
    w_g = w_in[:, o_mg:o_mg + MG_COLS].reshape(D, 2, 2, HEAD_PAIRS, 2).transpose(0, 3, 2, 1, 4).reshape(D, MG_COLS)
    bias = jnp.stack([_pair_major(b_i), _pair_major(b_f)], axis=1).reshape(HEAD_PAIRS, GATES_PER_PAIR)

    proj, grow = _proj(x2, w_main, w_g.astype(BF16).T)
    h_m = _mlstm(proj, grow, conv_w, bias[:, :, None], g_mlstm.reshape(HEAD_PAIRS, 2, MLSTM_DV), B, S)
    cos_t, sin_t = _rope_tables(S)
    h_a = _attn(proj, cos_t, sin_t, g_q[None, :], g_k[None, :], B, S)
    w_o = w_out.astype(BF16)
    x1, x1b, xg, aff = _outproj(h_m, h_a, x2, w_o[:MLSTM_WIDTH], w_o[MLSTM_WIDTH:], ln1_g[None, :], ln1_b[None, :],
                                w_router.T)
    pos, idx_digits = _topk(aff, B, S, cap)
    pos4 = pos.reshape(N_EXPERTS, B, 1, S)
    aff4 = aff.reshape(N_EXPERTS, B, 1, S)
    idx = (idx_digits[:, :, 0, :] * TOK_TILE + idx_digits[:, :, 1, :]).astype(jnp.int32).reshape(-1)
    y = _moe_ffn(idx, xg, pos4, aff4, w_gate.astype(BF16), w_up.astype(BF16), w_down.astype(BF16), B, S, cap)
    moe = _moe_comb(idx, y, B, S, cap)
    return _final(x1, x1b, moe, p2, w_pl_gate.astype(BF16), b_pl_gate[None, :], w_pl_proj.astype(BF16),
                  ln2_g[None, :], ln2_b[None, :])


def kernel(x, p, w_in, conv_w, b_igate, b_fgate, g_mlstm, g_q, g_k, w_out, ln1_g, ln1_b, w_router, w_gate, w_up,
           w_down, w_pl_proj, w_pl_gate, b_pl_gate, ln2_g, ln2_b):
    B, S, D = x.shape
    x2 = x.reshape(B * S, D)
    for i in range(DEPTH):
        x2 = _layer(x2, p[i].reshape(B * S, P_DIM), B, S, w_in[i], conv_w[i], b_igate[i], b_fgate[i], g_mlstm[i],
                    g_q[i], g_k[i], w_out[i], ln1_g[i], ln1_b[i], w_router[i], w_gate[i], w_up[i], w_down[i],
                    w_pl_proj[i], w_pl_gate[i], b_pl_gate[i], ln2_g[i], ln2_b[i])
    return x2.reshape(B, S, D)
```

```python
import functools

import jax
import jax.numpy as jnp
from jax import lax
from jax.experimental import pallas as pl
from jax.experimental.pallas import tpu as pltpu

F32 = jnp.float32
BF16 = jnp.bfloat16

D_MODEL = 2048
P_DIM = 256
GRID_W = 64
MLSTM_WIDTH = D_MODEL // 2
ATTN_WIDTH = D_MODEL - MLSTM_WIDTH
MLSTM_HEADS = 8
MLSTM_DV = MLSTM_WIDTH // MLSTM_HEADS
MLSTM_DQK = MLSTM_DV // 2
CONV_W = 5
ATTN_HEAD_DIM = 128
ATTN_Q_HEADS = ATTN_WIDTH // ATTN_HEAD_DIM
ATTN_KV_HEADS = 2
ATTN_GROUP = ATTN_Q_HEADS // ATTN_KV_HEADS
ROPE_FREQS = ATTN_HEAD_DIM // 4
ROPE_THETA = 10000.0
N_EXPERTS = 16
EXPERT_FF = D_MODEL // 2
CAPACITY_FACTOR = 2
NORM_EPS = 1e-6
DEPTH = 1
ALPHA = (2.0 * DEPTH) ** 0.25
LOG2E = 1.4426950408889634

MQ_COLS = MLSTM_HEADS * MLSTM_DQK
MV_COLS = MLSTM_WIDTH
MG_COLS = 2 * 2 * MLSTM_HEADS
AQ_COLS = ATTN_WIDTH
AKV_COLS = ATTN_KV_HEADS * ATTN_HEAD_DIM
PROJ_COLS = 2 * MQ_COLS + 2 * MV_COLS + AQ_COLS + 2 * AKV_COLS
OFF_MQ, OFF_MK, OFF_MV, OFF_MO = 0, MQ_COLS, 2 * MQ_COLS, 2 * MQ_COLS + MV_COLS
OFF_AQ = OFF_MO + MV_COLS
OFF_AK = OFF_AQ + AQ_COLS
OFF_AV = OFF_AK + AKV_COLS

HEAD_PAIRS = MLSTM_HEADS // 2
GATES_PER_PAIR = 8
CHAINS = 4
MLSTM_LC = 128
ATTN_CHAIN_ROWS = 256
ROW_CHAIN = 256
LANES = 128
SUBLANES = 8
TOK_TILE = 16
COL_BLOCKS = D_MODEL // LANES
SCATTER_BATCH = 8
V7X_VMEM_LIMIT = 56 * 1024 * 1024
TOPK_REFINE_STEPS = 24


def _cparams(sem, vmem=V7X_VMEM_LIMIT):
    return pltpu.CompilerParams(dimension_semantics=sem, vmem_limit_bytes=vmem)


def _resident(shape, index_map):
    return pl.BlockSpec(shape, index_map, pipeline_mode=pl.Buffered(1))


def _sigmoid(x):
    return 0.5 * jnp.tanh(0.5 * x) + 0.5


def _log_sigmoid(x):
    return jnp.minimum(x, 0.0) - jnp.log1p(jnp.exp(-jnp.abs(x)))


def _proj_kernel(x_ref, w_ref, wgt_ref, o_ref, gr_ref, xb_ref):
    @pl.when(pl.program_id(1) == 0)
    def _():
        xb = x_ref[...].astype(BF16)
        xb_ref[...] = xb
        gr = lax.dot_general(wgt_ref[...], xb, (((1,), (1,)), ((), ())),
                             preferred_element_type=F32)
        for p in range(HEAD_PAIRS):
            gr_ref[p] = gr[GATES_PER_PAIR * p:GATES_PER_PAIR * (p + 1), :]

    o_ref[...] = jnp.dot(xb_ref[...], w_ref[...], preferred_element_type=F32).astype(o_ref.dtype)


def _proj(x2, w, wgt, tm=1024, tn=1536):
    T, D = x2.shape
    N = w.shape[1]
    return pl.pallas_call(
        _proj_kernel,
        grid=(T // tm, N // tn),
        in_specs=[
            pl.BlockSpec((tm, D), lambda i, j: (i, 0)),
            pl.BlockSpec((D, tn), lambda i, j: (0, j)),
            _resident((MG_COLS, D), lambda i, j: (0, 0)),
        ],
        out_specs=[
            pl.BlockSpec((tm, tn), lambda i, j: (i, j)),
            pl.BlockSpec((HEAD_PAIRS, GATES_PER_PAIR, tm), lambda i, j: (0, 0, i)),
        ],
        out_shape=[
            jax.ShapeDtypeStruct((T, N), BF16),
            jax.ShapeDtypeStruct((HEAD_PAIRS, GATES_PER_PAIR, T), F32),
        ],
        scratch_shapes=[pltpu.VMEM((tm, D), BF16)],
        compiler_params=_cparams(("parallel", "arbitrary")),
        name="proj",
    )(x2, w, wgt)


def _conv_silu(x, w, pad_s):
    S, C = x.shape
    half = CONV_W // 2
    halo = jnp.zeros((SUBLANES, C), F32)
    pad_s[0:SUBLANES, :] = halo
    pad_s[SUBLANES + S:2 * SUBLANES + S, :] = halo
    pad_s[SUBLANES:SUBLANES + S, :] = x
    acc = x * w[half:half + 1, :]
    for j in range(CONV_W):
        if j != half:
            acc = acc + pad_s[SUBLANES + j - half:SUBLANES + j - half + S, :] * w[j:j + 1, :]
    return acc * _sigmoid(acc)


def _mlstm_kernel(q_ref, k_ref, v_ref, o_ref, gr_ref, cwq_ref, cwk_ref, br_ref, gh_ref, out_ref,
                  q0_s, q1_s, kt_s, col_s, rrow_s, wl_s, dec_s, sc_s, cst_s, call_s, pad_s):
    S = q_ref.shape[0]
    L = MLSTM_LC
    NC = S // L
    DQ = MLSTM_DQK
    DV = MLSTM_DV
    hi = lax.Precision.HIGHEST
    neg = -jnp.inf

    g8 = gr_ref[...] + br_ref[...]
    li8 = g8 * LOG2E
    lf8 = pltpu.roll(_log_sigmoid(g8) * LOG2E, CHAINS, 0)
    row = lax.broadcasted_iota(jnp.int32, (GATES_PER_PAIR, L), 0)
    lane = lax.broadcasted_iota(jnp.int32, (GATES_PER_PAIR, L), 1)
    fwd = (row % CHAINS) < 2
    fwd1 = fwd[:, :1]
    si = lax.broadcasted_iota(jnp.int32, (L, 2 * L), 0)
    ti = lax.broadcasted_iota(jnp.int32, (L, 2 * L), 1)
    tri = jnp.where(ti < L, jnp.where(si <= ti, 1.0, 0.0), jnp.where(si >= ti - L, 1.0, 0.0))
    tot, mloc, b_l, cm_l = [], [], [], []
    for c in range(NC):
        pr = jnp.dot(lf8[:, c * L:(c + 1) * L], tri, precision=hi, preferred_element_type=F32)
        b_c = jnp.where(fwd, pr[:, :L], pr[:, L:])
        tot_c = pr[:, L - 1:L]
        li_c = li8[:, c * L:(c + 1) * L]
        r_c = li_c - b_c
        cm = r_c
        k = 1
        while k < L:
            pre = jnp.where(lane >= k, pltpu.roll(cm, k, 1), neg)
            suf = jnp.where(lane < L - k, pltpu.roll(cm, L - k, 1), neg)
            cm = jnp.maximum(cm, jnp.where(fwd, pre, suf))
            k *= 2
        g_c = tot_c - b_c + li_c
        mloc_c = jnp.max(g_c, axis=1, keepdims=True)
        wl_s[c] = jnp.exp2(g_c - mloc_c)
        rrow_s[c] = r_c
        tot.append(tot_c)
        mloc.append(mloc_c)
        b_l.append(b_c)
        cm_l.append(cm)

    def scan(order):
        m = jnp.zeros((GATES_PER_PAIR, 1), F32)
        m_in, dec, sc = [None] * NC, [None] * NC, [None] * NC
        for c in order:
            m_new = jnp.maximum(tot[c] + m, mloc[c])
            m_in[c] = m
            dec[c] = jnp.exp2(tot[c] + m - m_new)
            sc[c] = jnp.exp2(mloc[c] - m_new)
            m = m_new
        return m_in, dec, sc

    mf, df, sf = scan(range(NC))
    mb, db, sb = scan(range(NC - 1, -1, -1))
    pad = jnp.zeros((LANES - 3 * GATES_PER_PAIR, L), F32)
    for c in range(NC):
        m_in = jnp.where(fwd1, mf[c], mb[c])
        dec_s[c] = jnp.broadcast_to(jnp.where(fwd1, df[c], db[c]), (GATES_PER_PAIR, L))
        sc_s[c] = jnp.broadcast_to(jnp.where(fwd1, sf[c], sb[c]), (GATES_PER_PAIR, L))
        a_c = jnp.maximum(m_in, cm_l[c])
        per_t = jnp.concatenate([a_c, jnp.exp2(m_in - a_c), jnp.exp2(-(b_l[c] + a_c)), pad], axis=0)
        col_s[c * L:(c + 1) * L, :] = per_t.T

    qs = (_conv_silu(q_ref[...].astype(F32), cwq_ref[...], pad_s) * (DQ ** -0.5)).astype(BF16)
    q0_s[...] = qs[:, :DQ]
    q1_s[...] = qs[:, DQ:]
    kt = _conv_silu(k_ref[...].astype(F32), cwk_ref[...], pad_s).T
    for c in range(NC):
        kt_s[c] = kt[:, c * L:(c + 1) * L]

    ones_blk = jnp.ones((L, DV), BF16)

    def vext_of(sl, hh):
        return jnp.concatenate([v_ref[sl, DV * hh:DV * (hh + 1)], ones_blk], axis=1)

    cst_s[...] = jnp.zeros(cst_s.shape, F32)

    def state_body(c, carry):
        for d in range(2):
            ch = c if d == 0 else NC - 1 - c
            sl = pl.ds(pl.multiple_of(ch * L, L), L)
            wl = wl_s[ch]
            ktc = kt_s[ch]
            decs = dec_s[ch]
            scs = sc_s[ch]
            for hh in range(2):
                ci = d * 2 + hh
                cx = cst_s[ci]
                call_s[ch, ci] = cx.astype(BF16)
                kw = (ktc[DQ * hh:DQ * (hh + 1), :] * wl[ci:ci + 1, :]).astype(BF16)
                cst_s[ci] = (decs[ci:ci + 1, 0:1] * cx
                             + scs[ci:ci + 1, 0:1] * jnp.dot(kw, vext_of(sl, hh), preferred_element_type=F32))
        return carry

    lax.fori_loop(0, NC, state_body, 0)

    tt = lax.broadcasted_iota(jnp.int32, (L, L), 0)
    ss = lax.broadcasted_iota(jnp.int32, (L, L), 1)
    masks = (ss <= tt, ss >= tt)

    def chunk_out(ch):
        sl = pl.ds(pl.multiple_of(ch * L, L), L)
        cols = col_s[sl, :]
        rrow = rrow_s[ch]
        ktc = kt_s[ch]
        og = _sigmoid(o_ref[sl, :].astype(F32))
        for hh in range(2):
            qc = (q0_s if hh == 0 else q1_s)[sl, :]
            kth = ktc[DQ * hh:DQ * (hh + 1), :].astype(BF16)
            qk = jnp.dot(qc, kth, preferred_element_type=F32)
            vext = vext_of(sl, hh)
            hsum = None
            for d in range(2):
                ci = d * 2 + hh
                a_t = cols[:, ci:ci + 1]
                w_inter = cols[:, GATES_PER_PAIR + ci:GATES_PER_PAIR + ci + 1]
                emt = cols[:, 2 * GATES_PER_PAIR + ci:2 * GATES_PER_PAIR + ci + 1]
                w_intra = jnp.exp2(jnp.where(masks[d], rrow[ci:ci + 1, :] - a_t, neg))
                qcx = jnp.dot(qc, call_s[ch, ci], preferred_element_type=F32)
                nd = (jnp.dot((qk * w_intra).astype(BF16), vext, preferred_element_type=F32)
                      + w_inter * qcx)
                h = nd[:, :DV] / jnp.maximum(jnp.abs(nd[:, DV:]), emt)
                hsum = h if hsum is None else hsum + h
            y = hsum * lax.rsqrt(jnp.mean(hsum * hsum, axis=1, keepdims=True) + NORM_EPS) * gh_ref[hh:hh + 1, :]
            out_ref[sl, DV * hh:DV * (hh + 1)] = (og[:, DV * hh:DV * (hh + 1)] * y).astype(out_ref.dtype)

    def out_body(c, carry):
        chunk_out(2 * c)
        chunk_out(2 * c + 1)
        return carry

    lax.fori_loop(0, NC // 2, out_body, 0)


def _mlstm(proj, grow, conv_w, bias_r, g_head, B, S):
    T = B * S
    NC = S // MLSTM_LC
    pw = 2 * MLSTM_DQK
    vw = 2 * MLSTM_DV
    return pl.pallas_call(
        _mlstm_kernel,
        grid=(B, HEAD_PAIRS),
        in_specs=[
            pl.BlockSpec((S, pw), lambda b, p: (b, OFF_MQ // pw + p)),
            pl.BlockSpec((S, pw), lambda b, p: (b, OFF_MK // pw + p)),
            pl.BlockSpec((S, vw), lambda b, p: (b, OFF_MV // vw + p)),
            pl.BlockSpec((S, vw), lambda b, p: (b, OFF_MO // vw + p)),
            pl.BlockSpec((None, GATES_PER_PAIR, S), lambda b, p: (p, 0, b)),
            pl.BlockSpec((CONV_W, pw), lambda b, p: (0, p)),
            pl.BlockSpec((CONV_W, pw), lambda b, p: (0, MQ_COLS // pw + p)),
            pl.BlockSpec((None, GATES_PER_PAIR, 1), lambda b, p: (p, 0, 0)),
            pl.BlockSpec((None, 2, MLSTM_DV), lambda b, p: (p, 0, 0)),
        ],
        out_specs=pl.BlockSpec((S, vw), lambda b, p: (b, p)),
        out_shape=jax.ShapeDtypeStruct((T, MLSTM_WIDTH), BF16),
        scratch_shapes=[
            pltpu.VMEM((S, MLSTM_DQK), BF16),
            pltpu.VMEM((S, MLSTM_DQK), BF16),
            pltpu.VMEM((NC, pw, MLSTM_LC), F32),
            pltpu.VMEM((S, LANES), F32),
            pltpu.VMEM((NC, GATES_PER_PAIR, MLSTM_LC), F32),
            pltpu.VMEM((NC, GATES_PER_PAIR, MLSTM_LC), F32),
            pltpu.VMEM((NC, GATES_PER_PAIR, MLSTM_LC), F32),
            pltpu.VMEM((NC, GATES_PER_PAIR, MLSTM_LC), F32),
            pltpu.VMEM((CHAINS, MLSTM_DQK, 2 * MLSTM_DV), F32),
            pltpu.VMEM((NC, CHAINS, MLSTM_DQK, 2 * MLSTM_DV), BF16),
            pltpu.VMEM((S + 2 * SUBLANES, pw), F32),
        ],
        compiler_params=_cparams(("parallel", "parallel")),
        name="mlstm",
    )(proj, proj, proj, proj, grow, conv_w, conv_w, bias_r, g_head)


def _norm_rope(x, g, cos, sin_signed):
    xn = x * lax.rsqrt(jnp.mean(x * x, axis=1, keepdims=True) + NORM_EPS) * g
    lane = lax.broadcasted_iota(jnp.int32, x.shape, 1)
    first_half = (lane % (2 * ROPE_FREQS)) < ROPE_FREQS
    partner = jnp.where(first_half,
                        pltpu.roll(xn, LANES - ROPE_FREQS, 1),
                        pltpu.roll(xn, ROPE_FREQS, 1))
    return xn * cos + partner * sin_signed


def _attn_kernel(q_ref, k_ref, v_ref, cq_ref, sq_ref, ck_ref, sk_ref, gq_ref, gk_ref, o_ref, kr_s, vx_s):
    d = ATTN_HEAD_DIM

    @pl.when(pl.program_id(2) == 0)
    def _():
        kr_s[...] = _norm_rope(k_ref[...].astype(F32), gk_ref[...], ck_ref[...], sk_ref[...]).astype(BF16)
        vx_s[...] = jnp.concatenate([v_ref[...], jnp.ones(v_ref.shape, BF16)], axis=1)

    gq = gq_ref[...]
    kr = kr_s[...]
    vx = vx_s[...]
    for r0 in range(0, q_ref.shape[0], ATTN_CHAIN_ROWS):
        rows = slice(r0, r0 + ATTN_CHAIN_ROWS)
        cq, sq = cq_ref[rows, :], sq_ref[rows, :]
        for g in range(ATTN_GROUP):
            cols = slice(d * g, d * (g + 1))
            qg = (_norm_rope(q_ref[rows, cols].astype(F32), gq, cq, sq) * (d ** -0.5 * LOG2E)).astype(BF16)
            s = lax.dot_general(qg, kr, (((1,), (1,)), ((), ())), preferred_element_type=F32)
            p = jnp.exp2(s - jnp.max(s, axis=1, keepdims=True))
            ov = jnp.dot(p.astype(BF16), vx, preferred_element_type=F32)
            o_ref[rows, cols] = (ov[:, :d] / ov[:, d:]).astype(o_ref.dtype)


def _attn(proj, cos_t, sin_t, g_q, g_k, B, S, tq=512):
    T = B * S
    d = ATTN_HEAD_DIM
    gw = ATTN_GROUP * d
    nq = S // tq
    return pl.pallas_call(
        _attn_kernel,
        grid=(B, ATTN_KV_HEADS, nq),
        in_specs=[
            pl.BlockSpec((tq, gw), lambda b, kv, qi: (b * nq + qi, OFF_AQ // gw + kv)),
            pl.BlockSpec((S, d), lambda b, kv, qi: (b, OFF_AK // d + kv)),
            pl.BlockSpec((S, d), lambda b, kv, qi: (b, OFF_AV // d + kv)),
            pl.BlockSpec((tq, d), lambda b, kv, qi: (qi, 0)),
            pl.BlockSpec((tq, d), lambda b, kv, qi: (qi, 0)),
            _resident((S, d), lambda b, kv, qi: (0, 0)),
            _resident((S, d), lambda b, kv, qi: (0, 0)),
            _resident((1, d), lambda b, kv, qi: (0, 0)),
            _resident((1, d), lambda b, kv, qi: (0, 0)),
        ],
        out_specs=pl.BlockSpec((tq, gw), lambda b, kv, qi: (b * nq + qi, kv)),
        out_shape=jax.ShapeDtypeStruct((T, ATTN_WIDTH), BF16),
        scratch_shapes=[pltpu.VMEM((S, d), BF16), pltpu.VMEM((S, 2 * d), BF16)],
        compiler_params=_cparams(("parallel", "parallel", "arbitrary")),
        name="attn",
    )(proj, proj, proj, cos_t, sin_t, cos_t, sin_t, g_q, g_k)


def _layer_norm(y, g, b):
    mu = jnp.mean(y, axis=1, keepdims=True)
    yc = y - mu
    var = jnp.mean(yc * yc, axis=1, keepdims=True)
    return yc * lax.rsqrt(var + NORM_EPS) * g + b


def _split_bf16(x):
    hi = x.astype(BF16)
    return hi, (x - hi.astype(F32)).astype(BF16)


def _tile_perm():
    n = TOK_TILE * COL_BLOCKS
    a = lax.broadcasted_iota(jnp.int32, (n, n), 0)
    b = lax.broadcasted_iota(jnp.int32, (n, n), 1)
    return jnp.where((a >> 4) == (b & 15), jnp.where((a & 15) == (b >> 4), 1.0, 0.0), 0.0).astype(BF16)


def _token_major(xh, perm):
    out = []
    for r0 in range(0, xh.shape[0], 2 * TOK_TILE):
        halves = [jnp.concatenate([xh[r0 + h * TOK_TILE:r0 + (h + 1) * TOK_TILE, LANES * j:LANES * (j + 1)]
                                   for j in range(COL_BLOCKS)], axis=0) for h in range(2)]
        o2 = jnp.dot(perm, jnp.concatenate(halves, axis=1), preferred_element_type=F32).astype(BF16)
        out += [o2[:, :LANES], o2[:, LANES:]]
    return jnp.concatenate(out, axis=0)


def _row_major(g_ref, perm):
    n = TOK_TILE * COL_BLOCKS
    cols = [[] for _ in range(COL_BLOCKS)]
    for r0 in range(0, g_ref.shape[0], 2 * n):
        m2 = jnp.concatenate([g_ref[r0:r0 + n, :], g_ref[r0 + n:r0 + 2 * n, :]], axis=1)
        o2 = jnp.dot(perm, m2, preferred_element_type=F32).astype(BF16)
        for j in range(COL_BLOCKS):
            blk = o2[TOK_TILE * j:TOK_TILE * (j + 1), :]
            cols[j] += [blk[:, :LANES], blk[:, LANES:]]
    return jnp.concatenate([jnp.concatenate(cj, axis=0) for cj in cols], axis=1)


def _outproj_kernel(hm_ref, ha_ref, x_ref, wt_ref, wb_ref, g_ref, b_ref, wr_ref, x1_ref, x1b_ref, xg_ref, aff_ref):
    y = (ALPHA * x_ref[...]
         + jnp.dot(hm_ref[...], wt_ref[...], preferred_element_type=F32)
         + jnp.dot(ha_ref[...], wb_ref[...], preferred_element_type=F32))
    x1 = _layer_norm(y, g_ref[...], b_ref[...])
    x1_ref[...] = x1
    xh, xl = _split_bf16(x1)
    wh, wl = _split_bf16(wr_ref[...])
    x1b_ref[...] = xh
    xg_ref[...] = _token_major(xh, _tile_perm())
    nt = (((1,), (1,)), ((), ()))
    logits = (lax.dot_general(wh, xh, nt, preferred_element_type=F32)
              + lax.dot_general(wh, xl, nt, preferred_element_type=F32)
              + lax.dot_general(wl, xh, nt, preferred_element_type=F32))
    e = jnp.exp(logits - jnp.max(logits, axis=0, keepdims=True))
    aff_ref[...] = e / jnp.sum(e, axis=0, keepdims=True)


def _outproj(hm, ha, x2, wt, wb, g, b, wr, tm=512):
    T, D = x2.shape
    return pl.pallas_call(
        _outproj_kernel,
        grid=(T // tm,),
        in_specs=[
            pl.BlockSpec((tm, MLSTM_WIDTH), lambda i: (i, 0)),
            pl.BlockSpec((tm, ATTN_WIDTH), lambda i: (i, 0)),
            pl.BlockSpec((tm, D), lambda i: (i, 0)),
            _resident((MLSTM_WIDTH, D), lambda i: (0, 0)),
            _resident((ATTN_WIDTH, D), lambda i: (0, 0)),
            _resident((1, D), lambda i: (0, 0)),
            _resident((1, D), lambda i: (0, 0)),
            _resident((N_EXPERTS, D), lambda i: (0, 0)),
        ],
        out_specs=[
            pl.BlockSpec((tm, D), lambda i: (i, 0)),
            pl.BlockSpec((tm, D), lambda i: (i, 0)),
            pl.BlockSpec((tm * TOK_TILE, LANES), lambda i: (i, 0)),
            pl.BlockSpec((N_EXPERTS, tm), lambda i: (0, i)),
        ],
        out_shape=[
            jax.ShapeDtypeStruct((T, D), F32),
            jax.ShapeDtypeStruct((T, D), BF16),
            jax.ShapeDtypeStruct((T * TOK_TILE, LANES), BF16),
            jax.ShapeDtypeStruct((N_EXPERTS, T), F32),
        ],
        compiler_params=_cparams(("parallel",)),
        name="outproj",
    )(hm, ha, x2, wt, wb, g, b, wr)


def _topk_kernel(aff_ref, pos_ref, idx_ref, tri_s, *, cap):
    S = aff_ref.shape[1]

    @pl.when(pl.program_id(0) == 0)
    def _():
        r = lax.broadcasted_iota(jnp.int32, (S, S), 0)
        c = lax.broadcasted_iota(jnp.int32, (S, S), 1)
        tri_s[...] = jnp.where(r < c, 1.0, 0.0).astype(BF16)

    a = aff_ref[...]

    def count_ge(v):
        return jnp.sum(jnp.where(a >= v, 1.0, 0.0), axis=1, keepdims=True)

    thr = jnp.zeros((a.shape[0], 1), jnp.int32)
    for bit in range(30, -1, -1):
        cand = thr | (1 << bit)
        thr = jnp.where(count_ge(pltpu.bitcast(cand, F32)) >= cap, cand, thr)
    lo = pltpu.bitcast(thr, F32)
    hi = pltpu.bitcast(thr + 1, F32)
    for _ in range(TOPK_REFINE_STEPS):
        mid = 0.5 * (lo + hi)
        ok = count_ge(mid) >= cap
        lo = jnp.where(ok, mid, lo)
        hi = jnp.where(ok, hi, mid)
    gt = a > lo
    eq = a == lo
    need = cap - jnp.sum(jnp.where(gt, 1.0, 0.0), axis=1, keepdims=True)
    tri = tri_s[...]
    eq_rank = jnp.dot(jnp.where(eq, 1.0, 0.0).astype(BF16), tri, preferred_element_type=F32)
    sel = jnp.logical_or(gt, jnp.logical_and(eq, eq_rank < need))
    pos = jnp.dot(jnp.where(sel, 1.0, 0.0).astype(BF16), tri, preferred_element_type=F32)
    posm = jnp.where(sel, pos, -1.0)
    pos_ref[...] = posm
    digit_row = lax.broadcasted_iota(jnp.int32, (2 * SUBLANES, S), 0)
    tok = lax.broadcasted_iota(jnp.int32, (2 * SUBLANES, S), 1)
    digits = jnp.where(digit_row == 0, tok >> 4, jnp.where(digit_row == 1, tok & 15, 0)).astype(F32).astype(BF16)
    slot = lax.broadcasted_iota(jnp.int32, (cap, S), 0).astype(F32)
    for e in range(a.shape[0]):
        hit = jnp.where(slot == posm[e:e + 1, :], 1.0, 0.0).astype(BF16)
        idx_ref[e] = lax.dot_general(digits, hit, (((1,), (1,)), ((), ())),
                                     preferred_element_type=F32)[:SUBLANES]


def _topk(aff, B, S, cap):
    E = aff.shape[0]
    return pl.pallas_call(
        functools.partial(_topk_kernel, cap=cap),
        grid=(B,),
        in_specs=[pl.BlockSpec((E, S), lambda b: (0, b))],
        out_specs=[pl.BlockSpec((E, S), lambda b: (0, b)),
                   pl.BlockSpec((E, None, SUBLANES, cap), lambda b: (0, b, 0, 0))],
        out_shape=[jax.ShapeDtypeStruct((E, B * S), F32),
                   jax.ShapeDtypeStruct((E, B, SUBLANES, cap), F32)],
        scratch_shapes=[pltpu.VMEM((S, S), BF16)],
        compiler_params=_cparams(("arbitrary",)),
        name="topk",
    )(aff)


def _moe_ffn_kernel(idx_ref, xg_ref, pos_ref, aff_ref, wg_ref, wu_ref, wd_ref, y_ref, gbuf_s, *, cap):
    S = pos_ref.shape[1]
    for c in range(cap):
        row = pl.multiple_of(idx_ref[0, c] * TOK_TILE, TOK_TILE)
        gbuf_s[TOK_TILE * c:TOK_TILE * (c + 1), :] = xg_ref[pl.ds(row, TOK_TILE), :]
    perm = _tile_perm()
    xs = _row_major(gbuf_s, perm)
    slot = lax.broadcasted_iota(jnp.int32, (cap, S), 0).astype(F32)
    gate = jnp.sum(jnp.where(slot == pos_ref[...], aff_ref[...], 0.0), axis=1, keepdims=True)
    hg = jnp.dot(xs, wg_ref[...], preferred_element_type=F32)
    hu = jnp.dot(xs, wu_ref[...], preferred_element_type=F32)
    hid = (hg * _sigmoid(hg) * hu).astype(BF16)
    y = (jnp.dot(hid, wd_ref[...], preferred_element_type=F32) * gate).astype(BF16)
    y_ref[...] = _token_major(y, perm)


def _moe_ffn(idx, xg, pos4, aff4, wg, wu, wd, B, S, cap):
    E, D, F = wg.shape
    return pl.pallas_call(
        functools.partial(_moe_ffn_kernel, cap=cap),
        grid=(E, B),
        in_specs=[
            pl.BlockSpec((None, None, 1, cap), lambda e, b: (e, b, 0, 0), memory_space=pltpu.SMEM),
            pl.BlockSpec((S * TOK_TILE, LANES), lambda e, b: (b, 0)),
            pl.BlockSpec((None, None, 1, S), lambda e, b: (e, b, 0, 0)),
            pl.BlockSpec((None, None, 1, S), lambda e, b: (e, b, 0, 0)),
            pl.BlockSpec((None, D, F), lambda e, b: (e, 0, 0)),
            pl.BlockSpec((None, D, F), lambda e, b: (e, 0, 0)),
            pl.BlockSpec((None, F, D), lambda e, b: (e, 0, 0)),
        ],
        out_specs=pl.BlockSpec((None, None, cap * TOK_TILE, LANES), lambda e, b: (b, e, 0, 0)),
        out_shape=jax.ShapeDtypeStruct((B, E, cap * TOK_TILE, LANES), BF16),
        scratch_shapes=[pltpu.VMEM((cap * TOK_TILE, LANES), BF16)],
        compiler_params=_cparams(("arbitrary", "arbitrary")),
        name="moe_ffn",
    )(idx, xg, pos4, aff4, wg, wu, wd)


def _row_major_f32(g_ref, r0, n_tok, perm):
    n = TOK_TILE * COL_BLOCKS
    cols = [[] for _ in range(COL_BLOCKS)]
    for k in range(0, n_tok * TOK_TILE, 2 * n):
        m2 = jnp.concatenate([g_ref[pl.ds(r0 + k, n), :], g_ref[pl.ds(r0 + k + n, n), :]], axis=1)
        hi = m2.astype(BF16)
        r1 = m2 - hi.astype(F32)
        mid = r1.astype(BF16)
        lo = (r1 - mid.astype(F32)).astype(BF16)
        o2 = (jnp.dot(perm, hi, preferred_element_type=F32) + jnp.dot(perm, mid, preferred_element_type=F32)
              + jnp.dot(perm, lo, preferred_element_type=F32))
        for j in range(COL_BLOCKS):
            blk = o2[TOK_TILE * j:TOK_TILE * (j + 1), :]
            cols[j] += [blk[:, :LANES], blk[:, LANES:]]
    return jnp.concatenate([jnp.concatenate(cj, axis=0) for cj in cols], axis=1)


def _moe_comb_kernel(idx_ref, y_ref, o_ref, acc_s, *, cap, n_exp):
    j = pl.program_id(1)

    @pl.when(j == 0)
    def _():
        acc_s[...] = jnp.zeros(acc_s.shape, F32)

    @pl.when(j < n_exp)
    def _():
        for c0 in range(0, cap, SCATTER_BATCH):
            rows = [pl.multiple_of(idx_ref[0, c0 + u] * TOK_TILE, TOK_TILE) for u in range(SCATTER_BATCH)]
            new = [acc_s[pl.ds(rows[u], TOK_TILE), :]
                   + y_ref[TOK_TILE * (c0 + u):TOK_TILE * (c0 + u + 1), :].astype(F32) for u in range(SCATTER_BATCH)]
            for u in range(SCATTER_BATCH):
                acc_s[pl.ds(rows[u], TOK_TILE), :] = new[u]

    @pl.when(j >= n_exp)
    def _():
        n_tok = o_ref.shape[0]
        r0 = pl.multiple_of((j - n_exp) * (n_tok * TOK_TILE), n_tok * TOK_TILE)
        o_ref[...] = _row_major_f32(acc_s, r0, n_tok, _tile_perm())


def _moe_comb(idx, y, B, S, cap, chunk=512):
    E = y.shape[1]
    D = D_MODEL
    nch = S // chunk
    return pl.pallas_call(
        functools.partial(_moe_comb_kernel, cap=cap, n_exp=E),
        grid=(B, E + nch),
        in_specs=[
            pl.BlockSpec((None, None, 1, cap), lambda b, j: (jnp.minimum(j, E - 1), b, 0, 0),
                         memory_space=pltpu.SMEM),
            pl.BlockSpec((None, None, cap * TOK_TILE, LANES), lambda b, j: (b, jnp.minimum(j, E - 1), 0, 0)),
        ],
        out_specs=pl.BlockSpec((chunk, D), lambda b, j: (b * nch + jnp.maximum(j - E, 0), 0)),
        out_shape=jax.ShapeDtypeStruct((B * S, D), F32),
        scratch_shapes=[pltpu.VMEM((S * TOK_TILE, LANES), F32)],
        compiler_params=_cparams(("arbitrary", "arbitrary")),
        name="moe_comb",
    )(idx, y)


def _final_kernel(x1_ref, x1b_ref, moe_ref, p_ref, wpg_ref, bpg_ref, wpp_ref, g_ref, b_ref, o_ref):
    for r0 in range(0, x1_ref.shape[0], ROW_CHAIN):
        rows = slice(r0, r0 + ROW_CHAIN)
        gate = _sigmoid(jnp.dot(x1b_ref[rows, :], wpg_ref[...], preferred_element_type=F32) + bpg_ref[...])
        plv = gate * jnp.dot(p_ref[rows, :].astype(BF16), wpp_ref[...], preferred_element_type=F32)
        o_ref[rows, :] = _layer_norm(ALPHA * x1_ref[rows, :] + moe_ref[rows, :] + plv, g_ref[...], b_ref[...])


def _final(x1, x1b, moe, p2, wpg, bpg, wpp, g, b, tm=512):
    T, D = x1.shape
    return pl.pallas_call(
        _final_kernel,
        grid=(T // tm,),
        in_specs=[
            pl.BlockSpec((tm, D), lambda i: (i, 0)),
            pl.BlockSpec((tm, D), lambda i: (i, 0)),
            pl.BlockSpec((tm, D), lambda i: (i, 0)),
            pl.BlockSpec((tm, P_DIM), lambda i: (i, 0)),
            _resident((D, D), lambda i: (0, 0)),
            _resident((1, D), lambda i: (0, 0)),
            _resident((P_DIM, D), lambda i: (0, 0)),
            _resident((1, D), lambda i: (0, 0)),
            _resident((1, D), lambda i: (0, 0)),
        ],
        out_specs=pl.BlockSpec((tm, D), lambda i: (i, 0)),
        out_shape=jax.ShapeDtypeStruct((T, D), F32),
        compiler_params=_cparams(("parallel",)),
        name="final",
    )(x1, x1b, moe, p2, wpg, bpg, wpp, g, b)


def _rope_tables(S):
    rows = S // GRID_W
    row_idx = jnp.broadcast_to(jnp.arange(rows, dtype=F32)[:, None], (rows, GRID_W)).reshape(-1)
    col_idx = jnp.broadcast_to(jnp.arange(GRID_W, dtype=F32)[None, :], (rows, GRID_W)).reshape(-1)
    inv_freq = ROPE_THETA ** (-jnp.arange(ROPE_FREQS, dtype=F32) / ROPE_FREQS)
    ar = row_idx[:, None] * inv_freq
    ac = col_idx[:, None] * inv_freq
    cos_t = jnp.concatenate([jnp.cos(ar), jnp.cos(ar), jnp.cos(ac), jnp.cos(ac)], axis=1)
    sin_t = jnp.concatenate([-jnp.sin(ar), jnp.sin(ar), -jnp.sin(ac), jnp.sin(ac)], axis=1)
    return cos_t, sin_t


def _pair_major(a):
    return a.reshape(2, HEAD_PAIRS, 2).transpose(1, 0, 2)


def _layer(x2, p2, B, S, w_in, conv_w, b_i, b_f, g_mlstm, g_q, g_k, w_out, ln1_g, ln1_b, w_router, w_gate, w_up,
           w_down, w_pl_proj, w_pl_gate, b_pl_gate, ln2_g, ln2_b):
    D = D_MODEL
    cap = CAPACITY_FACTOR * S // N_EXPERTS
    o_mg = 2 * MQ_COLS + 2 * MV_COLS
    w_main = jnp.concatenate([w_in[:, :o_mg], w_in[:, o_mg + MG_COLS:]], axis=1).astype(BF16)
    w_g = w_in[:, o_mg:o_mg + MG_COLS].reshape(D, 2, 2, HEAD_PAIRS, 2).transpose(0, 3, 2, 1, 4).reshape(D, MG_COLS)
    bias = jnp.stack([_pair_major(b_i), _pair_major(b_f)], axis=1).reshape(HEAD_PAIRS, GATES_PER_PAIR)

    proj, grow = _proj(x2, w_main, w_g.astype(BF16).T)
    h_m = _mlstm(proj, grow, conv_w, bias[:, :, None], g_mlstm.reshape(HEAD_PAIRS, 2, MLSTM_DV), B, S)
    cos_t, sin_t = _rope_tables(S)
    h_a = _attn(proj, cos_t, sin_t, g_q[None, :], g_k[None, :], B, S)
    w_o = w_out.astype(BF16)
    x1, x1b, xg, aff = _outproj(h_m, h_a, x2, w_o[:MLSTM_WIDTH], w_o[MLSTM_WIDTH:], ln1_g[None, :], ln1_b[None, :],
                                w_router.T)
    pos, idx_digits = _topk(aff, B, S, cap)
    pos4 = pos.reshape(N_EXPERTS, B, 1, S)
    aff4 = aff.reshape(N_EXPERTS, B, 1, S)
    idx = (idx_digits[:, :, 0:1, :] * TOK_TILE + idx_digits[:, :, 1:2, :]).astype(jnp.int32)
    y = _moe_ffn(idx, xg, pos4, aff4, w_gate.astype(BF16), w_up.astype(BF16), w_down.astype(BF16), B, S, cap)
    moe = _moe_comb(idx, y, B, S, cap)
    return _final(x1, x1b, moe, p2, w_pl_gate.astype(BF16), b_pl_gate[None, :], w_pl_proj.astype(BF16),
                  ln2_g[None, :], ln2_b[None, :])


def kernel(x, p, w_in, conv_w, b_igate, b_fgate, g_mlstm, g_q, g_k, w_out, ln1_g, ln1_b, w_router, w_gate, w_up,
           w_down, w_pl_proj, w_pl_gate, b_pl_gate, ln2_g, ln2_b):
    B, S, D = x.shape
    x2 = x.reshape(B * S, D)
    for i in range(DEPTH):
        x2 = _layer(x2, p[i].reshape(B * S, P_DIM), B, S, w_in[i], conv_w[i], b_igate[i], b_fgate[i], g_mlstm[i],
                    g_q[i], g_k[i], w_out[i], ln1_g[i], ln1_b[i], w_router[i], w_gate[i], w_up[i], w_down[i],
                    w_pl_proj[i], w_pl_gate[i], b_pl_gate[i], ln2_g[i], ln2_b[i])
    return x2.reshape(B, S, D)
```

```python
import functools

import jax
import jax.numpy as jnp
from jax import lax
from jax.experimental import pallas as pl
from jax.experimental.pallas import tpu as pltpu

F32 = jnp.float32
BF16 = jnp.bfloat16

D_MODEL = 2048
P_DIM = 256
GRID_W = 64
MLSTM_WIDTH = D_MODEL // 2
ATTN_WIDTH = D_MODEL - MLSTM_WIDTH
MLSTM_HEADS = 8
MLSTM_DV = MLSTM_WIDTH // MLSTM_HEADS
MLSTM_DQK = MLSTM_DV // 2
CONV_W = 5
ATTN_HEAD_DIM = 128
ATTN_Q_HEADS = ATTN_WIDTH // ATTN_HEAD_DIM
ATTN_KV_HEADS = 2
ATTN_GROUP = ATTN_Q_HEADS // ATTN_KV_HEADS
ROPE_FREQS = ATTN_HEAD_DIM // 4
ROPE_THETA = 10000.0
N_EXPERTS = 16
EXPERT_FF = D_MODEL // 2
CAPACITY_FACTOR = 2
NORM_EPS = 1e-6
DEPTH = 1
ALPHA = (2.0 * DEPTH) ** 0.25
LOG2E = 1.4426950408889634

MQ_COLS = MLSTM_HEADS * MLSTM_DQK
MV_COLS = MLSTM_WIDTH
MG_COLS = 2 * 2 * MLSTM_HEADS
AQ_COLS = ATTN_WIDTH
AKV_COLS = ATTN_KV_HEADS * ATTN_HEAD_DIM
PROJ_COLS = 2 * MQ_COLS + 2 * MV_COLS + AQ_COLS + 2 * AKV_COLS
OFF_MQ, OFF_MK, OFF_MV, OFF_MO = 0, MQ_COLS, 2 * MQ_COLS, 2 * MQ_COLS + MV_COLS
OFF_AQ = OFF_MO + MV_COLS
OFF_AK = OFF_AQ + AQ_COLS
OFF_AV = OFF_AK + AKV_COLS

HEAD_PAIRS = MLSTM_HEADS // 2
GATES_PER_PAIR = 8
CHAINS = 4
MLSTM_LC = 128
STATE_UNROLL = 4
ATTN_CHAIN_ROWS = 256
ROW_CHAIN = 256
LANES = 128
SUBLANES = 8
TOK_TILE = 16
COL_BLOCKS = D_MODEL // LANES
SCATTER_BATCH = 8
V7X_VMEM_LIMIT = 56 * 1024 * 1024
TOPK_REFINE_STEPS = 24


def _cparams(sem, vmem=V7X_VMEM_LIMIT):
    return pltpu.CompilerParams(dimension_semantics=sem, vmem_limit_bytes=vmem)


def _resident(shape, index_map):
    return pl.BlockSpec(shape, index_map, pipeline_mode=pl.Buffered(1))


def _sigmoid(x):
    return 0.5 * jnp.tanh(0.5 * x) + 0.5


def _log_sigmoid(x):
    return jnp.minimum(x, 0.0) - jnp.log1p(jnp.exp(-jnp.abs(x)))


def _proj_kernel(x_ref, w_ref, wgt_ref, o_ref, gr_ref, xb_ref):
    @pl.when(pl.program_id(1) == 0)
    def _():
        xb = x_ref[...].astype(BF16)
        xb_ref[...] = xb
        gr = lax.dot_general(wgt_ref[...], xb, (((1,), (1,)), ((), ())),
                             preferred_element_type=F32)
        for p in range(HEAD_PAIRS):
            gr_ref[p] = gr[GATES_PER_PAIR * p:GATES_PER_PAIR * (p + 1), :]

    o_ref[...] = jnp.dot(xb_ref[...], w_ref[...], preferred_element_type=F32).astype(o_ref.dtype)


def _proj(x2, w, wgt, tm=1024, tn=1536):
    T, D = x2.shape
    N = w.shape[1]
    return pl.pallas_call(
        _proj_kernel,
        grid=(T // tm, N // tn),
        in_specs=[
            pl.BlockSpec((tm, D), lambda i, j: (i, 0)),
            pl.BlockSpec((D, tn), lambda i, j: (0, j)),
            _resident((MG_COLS, D), lambda i, j: (0, 0)),
        ],
        out_specs=[
            pl.BlockSpec((tm, tn), lambda i, j: (i, j)),
            pl.BlockSpec((HEAD_PAIRS, GATES_PER_PAIR, tm), lambda i, j: (0, 0, i)),
        ],
        out_shape=[
            jax.ShapeDtypeStruct((T, N), BF16),
            jax.ShapeDtypeStruct((HEAD_PAIRS, GATES_PER_PAIR, T), F32),
        ],
        scratch_shapes=[pltpu.VMEM((tm, D), BF16)],
        compiler_params=_cparams(("parallel", "arbitrary")),
        name="proj",
    )(x2, w, wgt)


def _conv_silu(x, w, pad_s):
    S, C = x.shape
    half = CONV_W // 2
    halo = jnp.zeros((SUBLANES, C), F32)
    pad_s[0:SUBLANES, :] = halo
    pad_s[SUBLANES + S:2 * SUBLANES + S, :] = halo
    pad_s[SUBLANES:SUBLANES + S, :] = x
    acc = x * w[half:half + 1, :]
    for j in range(CONV_W):
        if j != half:
            acc = acc + pad_s[SUBLANES + j - half:SUBLANES + j - half + S, :] * w[j:j + 1, :]
    return acc * _sigmoid(acc)


def _mlstm_kernel(q_ref, k_ref, v_ref, o_ref, gr_ref, cwq_ref, cwk_ref, br_ref, gh_ref, out_ref,
                  q0_s, q1_s, kt_s, col_s, rrow_s, wl_s, dec_s, sc_s, cst_s, call_s, pad_s):
    S = q_ref.shape[0]
    L = MLSTM_LC
    NC = S // L
    DQ = MLSTM_DQK
    DV = MLSTM_DV
    hi = lax.Precision.HIGHEST
    neg = -jnp.inf

    g8 = gr_ref[...] + br_ref[...]
    li8 = g8 * LOG2E
    lf8 = pltpu.roll(_log_sigmoid(g8) * LOG2E, CHAINS, 0)
    row = lax.broadcasted_iota(jnp.int32, (GATES_PER_PAIR, L), 0)
    lane = lax.broadcasted_iota(jnp.int32, (GATES_PER_PAIR, L), 1)
    fwd = (row % CHAINS) < 2
    fwd1 = fwd[:, :1]
    si = lax.broadcasted_iota(jnp.int32, (L, 2 * L), 0)
    ti = lax.broadcasted_iota(jnp.int32, (L, 2 * L), 1)
    tri = jnp.where(ti < L, jnp.where(si <= ti, 1.0, 0.0), jnp.where(si >= ti - L, 1.0, 0.0))
    tot, mloc, b_l, cm_l = [], [], [], []
    for c in range(NC):
        pr = jnp.dot(lf8[:, c * L:(c + 1) * L], tri, precision=hi, preferred_element_type=F32)
        b_c = jnp.where(fwd, pr[:, :L], pr[:, L:])
        tot_c = pr[:, L - 1:L]
        li_c = li8[:, c * L:(c + 1) * L]
        r_c = li_c - b_c
        cm = r_c
        k = 1
        while k < L:
            pre = jnp.where(lane >= k, pltpu.roll(cm, k, 1), neg)
            suf = jnp.where(lane < L - k, pltpu.roll(cm, L - k, 1), neg)
            cm = jnp.maximum(cm, jnp.where(fwd, pre, suf))
            k *= 2
        g_c = tot_c - b_c + li_c
        mloc_c = jnp.max(g_c, axis=1, keepdims=True)
        wl_s[c] = jnp.exp2(g_c - mloc_c)
        rrow_s[c] = r_c
        tot.append(tot_c)
        mloc.append(mloc_c)
        b_l.append(b_c)
        cm_l.append(cm)

    def scan(order):
        m = jnp.zeros((GATES_PER_PAIR, 1), F32)
        m_in, dec, sc = [None] * NC, [None] * NC, [None] * NC
        for c in order:
            m_new = jnp.maximum(tot[c] + m, mloc[c])
            m_in[c] = m
            dec[c] = jnp.exp2(tot[c] + m - m_new)
            sc[c] = jnp.exp2(mloc[c] - m_new)
            m = m_new
        return m_in, dec, sc

    mf, df, sf = scan(range(NC))
    mb, db, sb = scan(range(NC - 1, -1, -1))
    pad = jnp.zeros((LANES - 3 * GATES_PER_PAIR, L), F32)
    for c in range(NC):
        m_in = jnp.where(fwd1, mf[c], mb[c])
        dec_s[c] = jnp.broadcast_to(jnp.where(fwd1, df[c], db[c]), (GATES_PER_PAIR, L))
        sc_s[c] = jnp.broadcast_to(jnp.where(fwd1, sf[c], sb[c]), (GATES_PER_PAIR, L))
        a_c = jnp.maximum(m_in, cm_l[c])
        per_t = jnp.concatenate([a_c, jnp.exp2(m_in - a_c), jnp.exp2(-(b_l[c] + a_c)), pad], axis=0)
        col_s[c * L:(c + 1) * L, :] = per_t.T

    qs = (_conv_silu(q_ref[...].astype(F32), cwq_ref[...], pad_s) * (DQ ** -0.5)).astype(BF16)
    q0_s[...] = qs[:, :DQ]
    q1_s[...] = qs[:, DQ:]
    kt = _conv_silu(k_ref[...].astype(F32), cwk_ref[...], pad_s).T
    for c in range(NC):
        kt_s[c] = kt[:, c * L:(c + 1) * L]

    ones_blk = jnp.ones((L, DV), BF16)

    def vext_of(sl, hh):
        return jnp.concatenate([v_ref[sl, DV * hh:DV * (hh + 1)], ones_blk], axis=1)

    cst_s[...] = jnp.zeros(cst_s.shape, F32)

    def state_step(c):
        for d in range(2):
            ch = c if d == 0 else NC - 1 - c
            sl = pl.ds(pl.multiple_of(ch * L, L), L)
            wl = wl_s[ch]
            ktc = kt_s[ch]
            decs = dec_s[ch]
            scs = sc_s[ch]
            for hh in range(2):
                ci = d * 2 + hh
                cx = cst_s[ci]
                call_s[ch, ci] = cx.astype(BF16)
                kw = (ktc[DQ * hh:DQ * (hh + 1), :] * wl[ci:ci + 1, :]).astype(BF16)
                cst_s[ci] = (decs[ci:ci + 1, 0:1] * cx
                             + scs[ci:ci + 1, 0:1] * jnp.dot(kw, vext_of(sl, hh), preferred_element_type=F32))

    def state_body(c, carry):
        for u in range(STATE_UNROLL):
            state_step(STATE_UNROLL * c + u)
        return carry

    lax.fori_loop(0, NC // STATE_UNROLL, state_body, 0)

    tt = lax.broadcasted_iota(jnp.int32, (L, L), 0)
    ss = lax.broadcasted_iota(jnp.int32, (L, L), 1)
    masks = (ss <= tt, ss >= tt)

    def chunk_out(ch, hh):
        sl = pl.ds(pl.multiple_of(ch * L, L), L)
        cols = col_s[sl, :]
        rrow = rrow_s[ch]
        og = _sigmoid(o_ref[sl, DV * hh:DV * (hh + 1)].astype(F32))
        qc = (q0_s if hh == 0 else q1_s)[sl, :]
        kth = kt_s[ch, DQ * hh:DQ * (hh + 1), :].astype(BF16)
        qk = jnp.dot(qc, kth, preferred_element_type=F32)
        vext = vext_of(sl, hh)
        hsum = None
        for d in range(2):
            ci = d * 2 + hh
            a_t = cols[:, ci:ci + 1]
            w_inter = cols[:, GATES_PER_PAIR + ci:GATES_PER_PAIR + ci + 1]
            emt = cols[:, 2 * GATES_PER_PAIR + ci:2 * GATES_PER_PAIR + ci + 1]
            w_intra = jnp.exp2(jnp.where(masks[d], rrow[ci:ci + 1, :] - a_t, neg))
            qcx = jnp.dot(qc, call_s[ch, ci], preferred_element_type=F32)
            nd = (jnp.dot((qk * w_intra).astype(BF16), vext, preferred_element_type=F32)
                  + w_inter * qcx)
            h = nd[:, :DV] / jnp.maximum(jnp.abs(nd[:, DV:]), emt)
            hsum = h if hsum is None else hsum + h
        y = hsum * lax.rsqrt(jnp.mean(hsum * hsum, axis=1, keepdims=True) + NORM_EPS) * gh_ref[hh:hh + 1, :]
        out_ref[sl, DV * hh:DV * (hh + 1)] = (og * y).astype(out_ref.dtype)

    def out_body(c, carry):
        chunk_out(c, 0)
        chunk_out(c, 1)
        return carry

    lax.fori_loop(0, NC, out_body, 0)


def _mlstm(proj, grow, conv_w, bias_r, g_head, B, S):
    T = B * S
    NC = S // MLSTM_LC
    pw = 2 * MLSTM_DQK
    vw = 2 * MLSTM_DV
    return pl.pallas_call(
        _mlstm_kernel,
        grid=(B, HEAD_PAIRS),
        in_specs=[
            pl.BlockSpec((S, pw), lambda b, p: (b, OFF_MQ // pw + p)),
            pl.BlockSpec((S, pw), lambda b, p: (b, OFF_MK // pw + p)),
            pl.BlockSpec((S, vw), lambda b, p: (b, OFF_MV // vw + p)),
            pl.BlockSpec((S, vw), lambda b, p: (b, OFF_MO // vw + p)),
            pl.BlockSpec((None, GATES_PER_PAIR, S), lambda b, p: (p, 0, b)),
            pl.BlockSpec((CONV_W, pw), lambda b, p: (0, p)),
            pl.BlockSpec((CONV_W, pw), lambda b, p: (0, MQ_COLS // pw + p)),
            pl.BlockSpec((None, GATES_PER_PAIR, 1), lambda b, p: (p, 0, 0)),
            pl.BlockSpec((None, 2, MLSTM_DV), lambda b, p: (p, 0, 0)),
        ],
        out_specs=pl.BlockSpec((S, vw), lambda b, p: (b, p)),
        out_shape=jax.ShapeDtypeStruct((T, MLSTM_WIDTH), BF16),
        scratch_shapes=[
            pltpu.VMEM((S, MLSTM_DQK), BF16),
            pltpu.VMEM((S, MLSTM_DQK), BF16),
            pltpu.VMEM((NC, pw, MLSTM_LC), F32),
            pltpu.VMEM((S, LANES), F32),
            pltpu.VMEM((NC, GATES_PER_PAIR, MLSTM_LC), F32),
            pltpu.VMEM((NC, GATES_PER_PAIR, MLSTM_LC), F32),
            pltpu.VMEM((NC, GATES_PER_PAIR, MLSTM_LC), F32),
            pltpu.VMEM((NC, GATES_PER_PAIR, MLSTM_LC), F32),
            pltpu.VMEM((CHAINS, MLSTM_DQK, 2 * MLSTM_DV), F32),
            pltpu.VMEM((NC, CHAINS, MLSTM_DQK, 2 * MLSTM_DV), BF16),
            pltpu.VMEM((S + 2 * SUBLANES, pw), F32),
        ],
        compiler_params=_cparams(("parallel", "parallel")),
        name="mlstm",
    )(proj, proj, proj, proj, grow, conv_w, conv_w, bias_r, g_head)


def _norm_rope(x, g, cos, sin_signed):
    xn = x * lax.rsqrt(jnp.mean(x * x, axis=1, keepdims=True) + NORM_EPS) * g
    lane = lax.broadcasted_iota(jnp.int32, x.shape, 1)
    first_half = (lane % (2 * ROPE_FREQS)) < ROPE_FREQS
    partner = jnp.where(first_half,
                        pltpu.roll(xn, LANES - ROPE_FREQS, 1),
                        pltpu.roll(xn, ROPE_FREQS, 1))
    return xn * cos + partner * sin_signed


def _attn_kernel(q_ref, k_ref, v_ref, cq_ref, sq_ref, ck_ref, sk_ref, gq_ref, gk_ref, o_ref, kr_s, vx_s):
    d = ATTN_HEAD_DIM

    @pl.when(pl.program_id(2) == 0)
    def _():
        kr_s[...] = _norm_rope(k_ref[...].astype(F32), gk_ref[...], ck_ref[...], sk_ref[...]).astype(BF16)
        vx_s[...] = jnp.concatenate([v_ref[...], jnp.ones(v_ref.shape, BF16)], axis=1)

    gq = gq_ref[...]
    kr = kr_s[...]
    vx = vx_s[...]
    for r0 in range(0, q_ref.shape[0], ATTN_CHAIN_ROWS):
        rows = slice(r0, r0 + ATTN_CHAIN_ROWS)
        cq, sq = cq_ref[rows, :], sq_ref[rows, :]
        for g in range(ATTN_GROUP):
            cols = slice(d * g, d * (g + 1))
            qg = (_norm_rope(q_ref[rows, cols].astype(F32), gq, cq, sq) * (d ** -0.5 * LOG2E)).astype(BF16)
            s = lax.dot_general(qg, kr, (((1,), (1,)), ((), ())), preferred_element_type=F32)
            p = jnp.exp2(s - jnp.max(s, axis=1, keepdims=True))
            ov = jnp.dot(p.astype(BF16), vx, preferred_element_type=F32)
            o_ref[rows, cols] = (ov[:, :d] / ov[:, d:]).astype(o_ref.dtype)


def _attn(proj, cos_t, sin_t, g_q, g_k, B, S, tq=512):
    T = B * S
    d = ATTN_HEAD_DIM
    gw = ATTN_GROUP * d
    nq = S // tq
    return pl.pallas_call(
        _attn_kernel,
        grid=(B, ATTN_KV_HEADS, nq),
        in_specs=[
            pl.BlockSpec((tq, gw), lambda b, kv, qi: (b * nq + qi, OFF_AQ // gw + kv)),
            pl.BlockSpec((S, d), lambda b, kv, qi: (b, OFF_AK // d + kv)),
            pl.BlockSpec((S, d), lambda b, kv, qi: (b, OFF_AV // d + kv)),
            pl.BlockSpec((tq, d), lambda b, kv, qi: (qi, 0)),
            pl.BlockSpec((tq, d), lambda b, kv, qi: (qi, 0)),
            _resident((S, d), lambda b, kv, qi: (0, 0)),
            _resident((S, d), lambda b, kv, qi: (0, 0)),
            _resident((1, d), lambda b, kv, qi: (0, 0)),
            _resident((1, d), lambda b, kv, qi: (0, 0)),
        ],
        out_specs=pl.BlockSpec((tq, gw), lambda b, kv, qi: (b * nq + qi, kv)),
        out_shape=jax.ShapeDtypeStruct((T, ATTN_WIDTH), BF16),
        scratch_shapes=[pltpu.VMEM((S, d), BF16), pltpu.VMEM((S, 2 * d), BF16)],
        compiler_params=_cparams(("parallel", "parallel", "arbitrary")),
        name="attn",
    )(proj, proj, proj, cos_t, sin_t, cos_t, sin_t, g_q, g_k)


def _layer_norm(y, g, b):
    mu = jnp.mean(y, axis=1, keepdims=True)
    yc = y - mu
    var = jnp.mean(yc * yc, axis=1, keepdims=True)
    return yc * lax.rsqrt(var + NORM_EPS) * g + b


def _split_bf16(x):
    hi = x.astype(BF16)
    return hi, (x - hi.astype(F32)).astype(BF16)


def _tile_perm():
    n = TOK_TILE * COL_BLOCKS
    a = lax.broadcasted_iota(jnp.int32, (n, n), 0)
    b = lax.broadcasted_iota(jnp.int32, (n, n), 1)
    return jnp.where((a >> 4) == (b & 15), jnp.where((a & 15) == (b >> 4), 1.0, 0.0), 0.0).astype(BF16)


def _token_major(xh, perm):
    out = []
    for r0 in range(0, xh.shape[0], 2 * TOK_TILE):
        halves = [jnp.concatenate([xh[r0 + h * TOK_TILE:r0 + (h + 1) * TOK_TILE, LANES * j:LANES * (j + 1)]
                                   for j in range(COL_BLOCKS)], axis=0) for h in range(2)]
        o2 = jnp.dot(perm, jnp.concatenate(halves, axis=1), preferred_element_type=F32).astype(BF16)
        out += [o2[:, :LANES], o2[:, LANES:]]
    return jnp.concatenate(out, axis=0)


def _row_major(g_ref, perm):
    n = TOK_TILE * COL_BLOCKS
    cols = [[] for _ in range(COL_BLOCKS)]
    for r0 in range(0, g_ref.shape[0], 2 * n):
        m2 = jnp.concatenate([g_ref[r0:r0 + n, :], g_ref[r0 + n:r0 + 2 * n, :]], axis=1)
        o2 = jnp.dot(perm, m2, preferred_element_type=F32).astype(BF16)
        for j in range(COL_BLOCKS):
            blk = o2[TOK_TILE * j:TOK_TILE * (j + 1), :]
            cols[j] += [blk[:, :LANES], blk[:, LANES:]]
    return jnp.concatenate([jnp.concatenate(cj, axis=0) for cj in cols], axis=1)


def _outproj_kernel(hm_ref, ha_ref, x_ref, wt_ref, wb_ref, g_ref, b_ref, wr_ref, x1_ref, x1b_ref, xg_ref, aff_ref):
    y = (ALPHA * x_ref[...]
         + jnp.dot(hm_ref[...], wt_ref[...], preferred_element_type=F32)
         + jnp.dot(ha_ref[...], wb_ref[...], preferred_element_type=F32))
    x1 = _layer_norm(y, g_ref[...], b_ref[...])
    x1_ref[...] = x1
    xh, xl = _split_bf16(x1)
    wh, wl = _split_bf16(wr_ref[...])
    x1b_ref[...] = xh
    xg_ref[...] = _token_major(xh, _tile_perm())
    nt = (((1,), (1,)), ((), ()))
    logits = (lax.dot_general(wh, xh, nt, preferred_element_type=F32)
              + lax.dot_general(wh, xl, nt, preferred_element_type=F32)
              + lax.dot_general(wl, xh, nt, preferred_element_type=F32))
    e = jnp.exp(logits - jnp.max(logits, axis=0, keepdims=True))
    aff_ref[...] = e / jnp.sum(e, axis=0, keepdims=True)


def _outproj(hm, ha, x2, wt, wb, g, b, wr, tm=512):
    T, D = x2.shape
    return pl.pallas_call(
        _outproj_kernel,
        grid=(T // tm,),
        in_specs=[
            pl.BlockSpec((tm, MLSTM_WIDTH), lambda i: (i, 0)),
            pl.BlockSpec((tm, ATTN_WIDTH), lambda i: (i, 0)),
            pl.BlockSpec((tm, D), lambda i: (i, 0)),
            _resident((MLSTM_WIDTH, D), lambda i: (0, 0)),
            _resident((ATTN_WIDTH, D), lambda i: (0, 0)),
            _resident((1, D), lambda i: (0, 0)),
            _resident((1, D), lambda i: (0, 0)),
            _resident((N_EXPERTS, D), lambda i: (0, 0)),
        ],
        out_specs=[
            pl.BlockSpec((tm, D), lambda i: (i, 0)),
            pl.BlockSpec((tm, D), lambda i: (i, 0)),
            pl.BlockSpec((tm * TOK_TILE, LANES), lambda i: (i, 0)),
            pl.BlockSpec((N_EXPERTS, tm), lambda i: (0, i)),
        ],
        out_shape=[
            jax.ShapeDtypeStruct((T, D), F32),
            jax.ShapeDtypeStruct((T, D), BF16),
            jax.ShapeDtypeStruct((T * TOK_TILE, LANES), BF16),
            jax.ShapeDtypeStruct((N_EXPERTS, T), F32),
        ],
        compiler_params=_cparams(("parallel",)),
        name="outproj",
    )(hm, ha, x2, wt, wb, g, b, wr)


def _topk_kernel(aff_ref, pos_ref, idx_ref, tri_s, *, cap):
    S = aff_ref.shape[1]

    @pl.when(pl.program_id(0) == 0)
    def _():
        r = lax.broadcasted_iota(jnp.int32, (S, S), 0)
        c = lax.broadcasted_iota(jnp.int32, (S, S), 1)
        tri_s[...] = jnp.where(r < c, 1.0, 0.0).astype(BF16)

    a = aff_ref[...]

    def count_ge(v):
        return jnp.sum(jnp.where(a >= v, 1.0, 0.0), axis=1, keepdims=True)

    thr = jnp.zeros((a.shape[0], 1), jnp.int32)
    for bit in range(30, -1, -1):
        cand = thr | (1 << bit)
        thr = jnp.where(count_ge(pltpu.bitcast(cand, F32)) >= cap, cand, thr)
    lo = pltpu.bitcast(thr, F32)
    hi = pltpu.bitcast(thr + 1, F32)
    for _ in range(TOPK_REFINE_STEPS):
        mid = 0.5 * (lo + hi)
        ok = count_ge(mid) >= cap
        lo = jnp.where(ok, mid, lo)
        hi = jnp.where(ok, hi, mid)
    gt = a > lo
    eq = a == lo
    need = cap - jnp.sum(jnp.where(gt, 1.0, 0.0), axis=1, keepdims=True)
    tri = tri_s[...]
    eq_rank = jnp.dot(jnp.where(eq, 1.0, 0.0).astype(BF16), tri, preferred_element_type=F32)
    sel = jnp.logical_or(gt, jnp.logical_and(eq, eq_rank < need))
    pos = jnp.dot(jnp.where(sel, 1.0, 0.0).astype(BF16), tri, preferred_element_type=F32)
    posm = jnp.where(sel, pos, -1.0)
    pos_ref[...] = posm
    digit_row = lax.broadcasted_iota(jnp.int32, (2 * SUBLANES, S), 0)
    tok = lax.broadcasted_iota(jnp.int32, (2 * SUBLANES, S), 1)
    digits = jnp.where(digit_row == 0, tok >> 4, jnp.where(digit_row == 1, tok & 15, 0)).astype(F32).astype(BF16)
    slot = lax.broadcasted_iota(jnp.int32, (cap, S), 0).astype(F32)
    for e in range(a.shape[0]):
        hit = jnp.where(slot == posm[e:e + 1, :], 1.0, 0.0).astype(BF16)
        idx_ref[e] = lax.dot_general(digits, hit, (((1,), (1,)), ((), ())),
                                     preferred_element_type=F32)[:SUBLANES]


def _topk(aff, B, S, cap):
    E = aff.shape[0]
    return pl.pallas_call(
        functools.partial(_topk_kernel, cap=cap),
        grid=(B,),
        in_specs=[pl.BlockSpec((E, S), lambda b: (0, b))],
        out_specs=[pl.BlockSpec((E, S), lambda b: (0, b)),
                   pl.BlockSpec((E, None, SUBLANES, cap), lambda b: (0, b, 0, 0))],
        out_shape=[jax.ShapeDtypeStruct((E, B * S), F32),
                   jax.ShapeDtypeStruct((E, B, SUBLANES, cap), F32)],
        scratch_shapes=[pltpu.VMEM((S, S), BF16)],
        compiler_params=_cparams(("arbitrary",)),
        name="topk",
    )(aff)


def _moe_ffn_kernel(idx_ref, xg_ref, pos_ref, aff_ref, wg_ref, wu_ref, wd_ref, y_ref, gbuf_s, *, cap):
    S = pos_ref.shape[1]
    for c in range(cap):
        row = pl.multiple_of(idx_ref[0, c] * TOK_TILE, TOK_TILE)
        gbuf_s[TOK_TILE * c:TOK_TILE * (c + 1), :] = xg_ref[pl.ds(row, TOK_TILE), :]
    perm = _tile_perm()
    xs = _row_major(gbuf_s, perm)
    slot = lax.broadcasted_iota(jnp.int32, (cap, S), 0).astype(F32)
    gate = jnp.sum(jnp.where(slot == pos_ref[...], aff_ref[...], 0.0), axis=1, keepdims=True)
    hg = jnp.dot(xs, wg_ref[...], preferred_element_type=F32)
    hu = jnp.dot(xs, wu_ref[...], preferred_element_type=F32)
    hid = (hg * _sigmoid(hg) * hu).astype(BF16)
    y = (jnp.dot(hid, wd_ref[...], preferred_element_type=F32) * gate).astype(BF16)
    y_ref[...] = _token_major(y, perm)


def _moe_ffn(idx, xg, pos4, aff4, wg, wu, wd, B, S, cap):
    E, D, F = wg.shape
    return pl.pallas_call(
        functools.partial(_moe_ffn_kernel, cap=cap),
        grid=(E, B),
        in_specs=[
            pl.BlockSpec((None, None, 1, cap), lambda e, b: (e, b, 0, 0), memory_space=pltpu.SMEM),
            pl.BlockSpec((S * TOK_TILE, LANES), lambda e, b: (b, 0)),
            pl.BlockSpec((None, None, 1, S), lambda e, b: (e, b, 0, 0)),
            pl.BlockSpec((None, None, 1, S), lambda e, b: (e, b, 0, 0)),
            pl.BlockSpec((None, D, F), lambda e, b: (e, 0, 0)),
            pl.BlockSpec((None, D, F), lambda e, b: (e, 0, 0)),
            pl.BlockSpec((None, F, D), lambda e, b: (e, 0, 0)),
        ],
        out_specs=pl.BlockSpec((None, None, cap * TOK_TILE, LANES), lambda e, b: (b, e, 0, 0)),
        out_shape=jax.ShapeDtypeStruct((B, E, cap * TOK_TILE, LANES), BF16),
        scratch_shapes=[pltpu.VMEM((cap * TOK_TILE, LANES), BF16)],
        compiler_params=_cparams(("arbitrary", "arbitrary")),
        name="moe_ffn",
    )(idx, xg, pos4, aff4, wg, wu, wd)


def _row_major_f32(g_ref, r0, n_tok, perm):
    n = TOK_TILE * COL_BLOCKS
    cols = [[] for _ in range(COL_BLOCKS)]
    for k in range(0, n_tok * TOK_TILE, 2 * n):
        m2 = jnp.concatenate([g_ref[pl.ds(r0 + k, n), :], g_ref[pl.ds(r0 + k + n, n), :]], axis=1)
        hi = m2.astype(BF16)
        r1 = m2 - hi.astype(F32)
        mid = r1.astype(BF16)
        lo = (r1 - mid.astype(F32)).astype(BF16)
        o2 = (jnp.dot(perm, hi, preferred_element_type=F32) + jnp.dot(perm, mid, preferred_element_type=F32)
              + jnp.dot(perm, lo, preferred_element_type=F32))
        for j in range(COL_BLOCKS):
            blk = o2[TOK_TILE * j:TOK_TILE * (j + 1), :]
            cols[j] += [blk[:, :LANES], blk[:, LANES:]]
    return jnp.concatenate([jnp.concatenate(cj, axis=0) for cj in cols], axis=1)


def _moe_comb_kernel(idx_ref, y_ref, o_ref, acc_s, *, cap, n_exp):
    j = pl.program_id(1)

    @pl.when(j == 0)
    def _():
        acc_s[...] = jnp.zeros(acc_s.shape, F32)

    @pl.when(j < n_exp)
    def _():
        for c0 in range(0, cap, SCATTER_BATCH):
            rows = [pl.multiple_of(idx_ref[0, c0 + u] * TOK_TILE, TOK_TILE) for u in range(SCATTER_BATCH)]
            new = [acc_s[pl.ds(rows[u], TOK_TILE), :]
                   + y_ref[TOK_TILE * (c0 + u):TOK_TILE * (c0 + u + 1), :].astype(F32) for u in range(SCATTER_BATCH)]
            for u in range(SCATTER_BATCH):
                acc_s[pl.ds(rows[u], TOK_TILE), :] = new[u]

    @pl.when(j >= n_exp)
    def _():
        n_tok = o_ref.shape[0]
        r0 = pl.multiple_of((j - n_exp) * (n_tok * TOK_TILE), n_tok * TOK_TILE)
        o_ref[...] = _row_major_f32(acc_s, r0, n_tok, _tile_perm())


def _moe_comb(idx, y, B, S, cap, chunk=512):
    E = y.shape[1]
    D = D_MODEL
    nch = S // chunk
    return pl.pallas_call(
        functools.partial(_moe_comb_kernel, cap=cap, n_exp=E),
        grid=(B, E + nch),
        in_specs=[
            pl.BlockSpec((None, None, 1, cap), lambda b, j: (jnp.minimum(j, E - 1), b, 0, 0),
                         memory_space=pltpu.SMEM),
            pl.BlockSpec((None, None, cap * TOK_TILE, LANES), lambda b, j: (b, jnp.minimum(j, E - 1), 0, 0)),
        ],
        out_specs=pl.BlockSpec((chunk, D), lambda b, j: (b * nch + jnp.maximum(j - E, 0), 0)),
        out_shape=jax.ShapeDtypeStruct((B * S, D), F32),
        scratch_shapes=[pltpu.VMEM((S * TOK_TILE, LANES), F32)],
        compiler_params=_cparams(("arbitrary", "arbitrary")),
        name="moe_comb",
    )(idx, y)


def _final_kernel(x1_ref, x1b_ref, moe_ref, p_ref, wpg_ref, bpg_ref, wpp_ref, g_ref, b_ref, o_ref):
    for r0 in range(0, x1_ref.shape[0], ROW_CHAIN):
        rows = slice(r0, r0 + ROW_CHAIN)
        gate = _sigmoid(jnp.dot(x1b_ref[rows, :], wpg_ref[...], preferred_element_type=F32) + bpg_ref[...])
        plv = gate * jnp.dot(p_ref[rows, :].astype(BF16), wpp_ref[...], preferred_element_type=F32)
        o_ref[rows, :] = _layer_norm(ALPHA * x1_ref[rows, :] + moe_ref[rows, :] + plv, g_ref[...], b_ref[...])


def _final(x1, x1b, moe, p2, wpg, bpg, wpp, g, b, tm=512):
    T, D = x1.shape
    return pl.pallas_call(
        _final_kernel,
        grid=(T // tm,),
        in_specs=[
            pl.BlockSpec((tm, D), lambda i: (i, 0)),
            pl.BlockSpec((tm, D), lambda i: (i, 0)),
            pl.BlockSpec((tm, D), lambda i: (i, 0)),
            pl.BlockSpec((tm, P_DIM), lambda i: (i, 0)),
            _resident((D, D), lambda i: (0, 0)),
            _resident((1, D), lambda i: (0, 0)),
            _resident((P_DIM, D), lambda i: (0, 0)),
            _resident((1, D), lambda i: (0, 0)),
            _resident((1, D), lambda i: (0, 0)),
        ],
        out_specs=pl.BlockSpec((tm, D), lambda i: (i, 0)),
        out_shape=jax.ShapeDtypeStruct((T, D), F32),
        compiler_params=_cparams(("parallel",)),
        name="final",
    )(x1, x1b, moe, p2, wpg, bpg, wpp, g, b)


def _rope_tables(S):
    rows = S // GRID_W
    row_idx = jnp.broadcast_to(jnp.arange(rows, dtype=F32)[:, None], (rows, GRID_W)).reshape(-1)
    col_idx = jnp.broadcast_to(jnp.arange(GRID_W, dtype=F32)[None, :], (rows, GRID_W)).reshape(-1)
    inv_freq = ROPE_THETA ** (-jnp.arange(ROPE_FREQS, dtype=F32) / ROPE_FREQS)
    ar = row_idx[:, None] * inv_freq
    ac = col_idx[:, None] * inv_freq
    cos_t = jnp.concatenate([jnp.cos(ar), jnp.cos(ar), jnp.cos(ac), jnp.cos(ac)], axis=1)
    sin_t = jnp.concatenate([-jnp.sin(ar), jnp.sin(ar), -jnp.sin(ac), jnp.sin(ac)], axis=1)
    return cos_t, sin_t


def _pair_major(a):
    return a.reshape(2, HEAD_PAIRS, 2).transpose(1, 0, 2)


def _layer(x2, p2, B, S, w_in, conv_w, b_i, b_f, g_mlstm, g_q, g_k, w_out, ln1_g, ln1_b, w_router, w_gate, w_up,
           w_down, w_pl_proj, w_pl_gate, b_pl_gate, ln2_g, ln2_b):
    D = D_MODEL
    cap = CAPACITY_FACTOR * S // N_EXPERTS
    o_mg = 2 * MQ_COLS + 2 * MV_COLS
    w_main = jnp.concatenate([w_in[:, :o_mg], w_in[:, o_mg + MG_COLS:]], axis=1).astype(BF16)
    w_g = w_in[:, o_mg:o_mg + MG_COLS].reshape(D, 2, 2, HEAD_PAIRS, 2).transpose(0, 3, 2, 1, 4).reshape(D, MG_COLS)
    bias = jnp.stack([_pair_major(b_i), _pair_major(b_f)], axis=1).reshape(HEAD_PAIRS, GATES_PER_PAIR)

    proj, grow = _proj(x2, w_main, w_g.astype(BF16).T)
    h_m = _mlstm(proj, grow, conv_w, bias[:, :, None], g_mlstm.reshape(HEAD_PAIRS, 2, MLSTM_DV), B, S)
    cos_t, sin_t = _rope_tables(S)
    h_a = _attn(proj, cos_t, sin_t, g_q[None, :], g_k[None, :], B, S)
    w_o = w_out.astype(BF16)
    x1, x1b, xg, aff = _outproj(h_m, h_a, x2, w_o[:MLSTM_WIDTH], w_o[MLSTM_WIDTH:], ln1_g[None, :], ln1_b[None, :],
                                w_router.T)
    pos, idx_digits = _topk(aff, B, S, cap)
    pos4 = pos.reshape(N_EXPERTS, B, 1, S)
    aff4 = aff.reshape(N_EXPERTS, B, 1, S)
    idx = (idx_digits[:, :, 0:1, :] * TOK_TILE + idx_digits[:, :, 1:2, :]).astype(jnp.int32)
    y = _moe_ffn(idx, xg, pos4, aff4, w_gate.astype(BF16), w_up.astype(BF16), w_down.astype(BF16), B, S, cap)
    moe = _moe_comb(idx, y, B, S, cap)
    return _final(x1, x1b, moe, p2, w_pl_gate.astype(BF16), b_pl_gate[None, :], w_pl_proj.astype(BF16),
                  ln2_g[None, :], ln2_b[None, :])


def kernel(x, p, w_in, conv_w, b_igate, b_fgate, g_mlstm, g_q, g_k, w_out, ln1_g, ln1_b, w_router, w_gate, w_up,
           w_down, w_pl_proj, w_pl_gate, b_pl_gate, ln2_g, ln2_b):
    B, S, D = x.shape
    x2 = x.reshape(B * S, D)
    for i in range(DEPTH):
        x2 = _layer(x2, p[i].reshape(B * S, P_DIM), B, S, w_in[i], conv_w[i], b_igate[i], b_fgate[i], g_mlstm[i],
                    g_q[i], g_k[i], w_out[i], ln1_g[i], ln1_b[i], w_router[i], w_gate[i], w_up[i], w_down[i],
                    w_pl_proj[i], w_pl_gate[i], b_pl_gate[i], ln2_g[i], ln2_b[i])
    return x2.reshape(B, S, D)
```

```python
import functools

import jax
import jax.numpy as jnp
from jax import lax
from jax.experimental import pallas as pl
from jax.experimental.pallas import tpu as pltpu

F32 = jnp.float32
BF16 = jnp.bfloat16

D_MODEL = 2048
P_DIM = 256
GRID_W = 64
MLSTM_WIDTH = D_MODEL // 2
ATTN_WIDTH = D_MODEL - MLSTM_WIDTH
MLSTM_HEADS = 8
MLSTM_DV = MLSTM_WIDTH // MLSTM_HEADS
MLSTM_DQK = MLSTM_DV // 2
CONV_W = 5
ATTN_HEAD_DIM = 128
ATTN_Q_HEADS = ATTN_WIDTH // ATTN_HEAD_DIM
ATTN_KV_HEADS = 2
ATTN_GROUP = ATTN_Q_HEADS // ATTN_KV_HEADS
ROPE_FREQS = ATTN_HEAD_DIM // 4
ROPE_THETA = 10000.0
N_EXPERTS = 16
EXPERT_FF = D_MODEL // 2
CAPACITY_FACTOR = 2
NORM_EPS = 1e-6
DEPTH = 1
ALPHA = (2.0 * DEPTH) ** 0.25
LOG2E = 1.4426950408889634

MQ_COLS = MLSTM_HEADS * MLSTM_DQK
MV_COLS = MLSTM_WIDTH
MG_COLS = 2 * 2 * MLSTM_HEADS
AQ_COLS = ATTN_WIDTH
AKV_COLS = ATTN_KV_HEADS * ATTN_HEAD_DIM
PROJ_COLS = 2 * MQ_COLS + 2 * MV_COLS + AQ_COLS + 2 * AKV_COLS
OFF_MQ, OFF_MK, OFF_MV, OFF_MO = 0, MQ_COLS, 2 * MQ_COLS, 2 * MQ_COLS + MV_COLS
OFF_AQ = OFF_MO + MV_COLS
OFF_AK = OFF_AQ + AQ_COLS
OFF_AV = OFF_AK + AKV_COLS

HEAD_PAIRS = MLSTM_HEADS // 2
GATES_PER_PAIR = 8
CHAINS = 4
MLSTM_LC = 128
STATE_UNROLL = 4
ATTN_CHAIN_ROWS = 256
ROW_CHAIN = 256
LANES = 128
SUBLANES = 8
TOK_TILE = 16
COL_BLOCKS = D_MODEL // LANES
SCATTER_BATCH = 8
V7X_VMEM_LIMIT = 56 * 1024 * 1024
TOPK_REFINE_STEPS = 24


def _cparams(sem, vmem=V7X_VMEM_LIMIT):
    return pltpu.CompilerParams(dimension_semantics=sem, vmem_limit_bytes=vmem)


def _resident(shape, index_map):
    return pl.BlockSpec(shape, index_map, pipeline_mode=pl.Buffered(1))


def _sigmoid(x):
    return 0.5 * jnp.tanh(0.5 * x) + 0.5


def _log_sigmoid(x):
    return jnp.minimum(x, 0.0) - jnp.log1p(jnp.exp(-jnp.abs(x)))


def _proj_kernel(x_ref, w_ref, wgt_ref, o_ref, gr_ref, xb_ref):
    @pl.when(pl.program_id(1) == 0)
    def _():
        xb = x_ref[...].astype(BF16)
        xb_ref[...] = xb
        gr = lax.dot_general(wgt_ref[...], xb, (((1,), (1,)), ((), ())),
                             preferred_element_type=F32)
        for p in range(HEAD_PAIRS):
            gr_ref[p] = gr[GATES_PER_PAIR * p:GATES_PER_PAIR * (p + 1), :]

    o_ref[...] = jnp.dot(xb_ref[...], w_ref[...], preferred_element_type=F32).astype(o_ref.dtype)


def _proj(x2, w, wgt, tm=1024, tn=1536):
    T, D = x2.shape
    N = w.shape[1]
    return pl.pallas_call(
        _proj_kernel,
        grid=(T // tm, N // tn),
        in_specs=[
            pl.BlockSpec((tm, D), lambda i, j: (i, 0)),
            pl.BlockSpec((D, tn), lambda i, j: (0, j)),
            _resident((MG_COLS, D), lambda i, j: (0, 0)),
        ],
        out_specs=[
            pl.BlockSpec((tm, tn), lambda i, j: (i, j)),
            pl.BlockSpec((HEAD_PAIRS, GATES_PER_PAIR, tm), lambda i, j: (0, 0, i)),
        ],
        out_shape=[
            jax.ShapeDtypeStruct((T, N), BF16),
            jax.ShapeDtypeStruct((HEAD_PAIRS, GATES_PER_PAIR, T), F32),
        ],
        scratch_shapes=[pltpu.VMEM((tm, D), BF16)],
        compiler_params=_cparams(("parallel", "arbitrary")),
        name="proj",
    )(x2, w, wgt)


def _conv_silu(x, w, pad_s):
    S, C = x.shape
    half = CONV_W // 2
    halo = jnp.zeros((SUBLANES, C), F32)
    pad_s[0:SUBLANES, :] = halo
    pad_s[SUBLANES + S:2 * SUBLANES + S, :] = halo
    pad_s[SUBLANES:SUBLANES + S, :] = x
    acc = x * w[half:half + 1, :]
    for j in range(CONV_W):
        if j != half:
            acc = acc + pad_s[SUBLANES + j - half:SUBLANES + j - half + S, :] * w[j:j + 1, :]
    return acc * _sigmoid(acc)


def _mlstm_kernel(q_ref, k_ref, v_ref, o_ref, gr_ref, cwq_ref, cwk_ref, br_ref, gh_ref, out_ref,
                  q0_s, q1_s, kt_s, col_s, rrow_s, wl_s, dec_s, sc_s, cst_s, call_s, pad_s):
    S = q_ref.shape[0]
    L = MLSTM_LC
    NC = S // L
    DQ = MLSTM_DQK
    DV = MLSTM_DV
    hi = lax.Precision.HIGHEST
    neg = -jnp.inf

    g8 = gr_ref[...] + br_ref[...]
    li8 = g8 * LOG2E
    lf8 = pltpu.roll(_log_sigmoid(g8) * LOG2E, CHAINS, 0)
    row = lax.broadcasted_iota(jnp.int32, (GATES_PER_PAIR, L), 0)
    lane = lax.broadcasted_iota(jnp.int32, (GATES_PER_PAIR, L), 1)
    fwd = (row % CHAINS) < 2
    fwd1 = fwd[:, :1]
    si = lax.broadcasted_iota(jnp.int32, (L, 2 * L), 0)
    ti = lax.broadcasted_iota(jnp.int32, (L, 2 * L), 1)
    tri = jnp.where(ti < L, jnp.where(si <= ti, 1.0, 0.0), jnp.where(si >= ti - L, 1.0, 0.0))
    tot, mloc, b_l, cm_l = [], [], [], []
    for c in range(NC):
        pr = jnp.dot(lf8[:, c * L:(c + 1) * L], tri, precision=hi, preferred_element_type=F32)
        b_c = jnp.where(fwd, pr[:, :L], pr[:, L:])
        tot_c = pr[:, L - 1:L]
        li_c = li8[:, c * L:(c + 1) * L]
        r_c = li_c - b_c
        cm = r_c
        k = 1
        while k < L:
            pre = jnp.where(lane >= k, pltpu.roll(cm, k, 1), neg)
            suf = jnp.where(lane < L - k, pltpu.roll(cm, L - k, 1), neg)
            cm = jnp.maximum(cm, jnp.where(fwd, pre, suf))
            k *= 2
        g_c = tot_c - b_c + li_c
        mloc_c = jnp.max(g_c, axis=1, keepdims=True)
        wl_s[c] = jnp.exp2(g_c - mloc_c)
        rrow_s[c] = r_c
        tot.append(tot_c)
        mloc.append(mloc_c)
        b_l.append(b_c)
        cm_l.append(cm)

    def scan(order):
        m = jnp.zeros((GATES_PER_PAIR, 1), F32)
        m_in, dec, sc = [None] * NC, [None] * NC, [None] * NC
        for c in order:
            m_new = jnp.maximum(tot[c] + m, mloc[c])
            m_in[c] = m
            dec[c] = jnp.exp2(tot[c] + m - m_new)
            sc[c] = jnp.exp2(mloc[c] - m_new)
            m = m_new
        return m_in, dec, sc

    mf, df, sf = scan(range(NC))
    mb, db, sb = scan(range(NC - 1, -1, -1))
    pad = jnp.zeros((LANES - 3 * GATES_PER_PAIR, L), F32)
    for c in range(NC):
        m_in = jnp.where(fwd1, mf[c], mb[c])
        dec_s[c] = jnp.broadcast_to(jnp.where(fwd1, df[c], db[c]), (GATES_PER_PAIR, L))
        sc_s[c] = jnp.broadcast_to(jnp.where(fwd1, sf[c], sb[c]), (GATES_PER_PAIR, L))
        a_c = jnp.maximum(m_in, cm_l[c])
        per_t = jnp.concatenate([a_c, jnp.exp2(m_in - a_c), jnp.exp2(-(b_l[c] + a_c)), pad], axis=0)
        col_s[c * L:(c + 1) * L, :] = per_t.T

    qs = (_conv_silu(q_ref[...].astype(F32), cwq_ref[...], pad_s) * (DQ ** -0.5)).astype(BF16)
    q0_s[...] = qs[:, :DQ]
    q1_s[...] = qs[:, DQ:]
    kt = _conv_silu(k_ref[...].astype(F32), cwk_ref[...], pad_s).T
    for c in range(NC):
        kt_s[c] = kt[:, c * L:(c + 1) * L]

    ones_blk = jnp.ones((L, DV), BF16)

    def vext_of(sl, hh):
        return jnp.concatenate([v_ref[sl, DV * hh:DV * (hh + 1)], ones_blk], axis=1)

    cst_s[...] = jnp.zeros(cst_s.shape, F32)

    def state_step(c):
        for d in range(2):
            ch = c if d == 0 else NC - 1 - c
            sl = pl.ds(pl.multiple_of(ch * L, L), L)
            wl = wl_s[ch]
            ktc = kt_s[ch]
            decs = dec_s[ch]
            scs = sc_s[ch]
            for hh in range(2):
                ci = d * 2 + hh
                cx = cst_s[ci]
                call_s[ch, ci] = cx.astype(BF16)
                kw = (ktc[DQ * hh:DQ * (hh + 1), :] * wl[ci:ci + 1, :]).astype(BF16)
                cst_s[ci] = (decs[ci:ci + 1, 0:1] * cx
                             + scs[ci:ci + 1, 0:1] * jnp.dot(kw, vext_of(sl, hh), preferred_element_type=F32))

    def state_body(c, carry):
        for u in range(STATE_UNROLL):
            state_step(STATE_UNROLL * c + u)
        return carry

    lax.fori_loop(0, NC // STATE_UNROLL, state_body, 0)

    tt = lax.broadcasted_iota(jnp.int32, (L, L), 0)
    ss = lax.broadcasted_iota(jnp.int32, (L, L), 1)
    masks = (ss <= tt, ss >= tt)

    def chunk_out(ch, hh):
        sl = pl.ds(pl.multiple_of(ch * L, L), L)
        cols = col_s[sl, :]
        rrow = rrow_s[ch]
        og = _sigmoid(o_ref[sl, DV * hh:DV * (hh + 1)].astype(F32))
        qc = (q0_s if hh == 0 else q1_s)[sl, :]
        kth = kt_s[ch, DQ * hh:DQ * (hh + 1), :].astype(BF16)
        qk = jnp.dot(qc, kth, preferred_element_type=F32)
        vext = vext_of(sl, hh)
        hsum = None
        for d in range(2):
            ci = d * 2 + hh
            a_t = cols[:, ci:ci + 1]
            w_inter = cols[:, GATES_PER_PAIR + ci:GATES_PER_PAIR + ci + 1]
            emt = cols[:, 2 * GATES_PER_PAIR + ci:2 * GATES_PER_PAIR + ci + 1]
            w_intra = jnp.exp2(jnp.where(masks[d], rrow[ci:ci + 1, :] - a_t, neg))
            qcx = jnp.dot(qc, call_s[ch, ci], preferred_element_type=F32)
            nd = (jnp.dot((qk * w_intra).astype(BF16), vext, preferred_element_type=F32)
                  + w_inter * qcx)
            h = nd[:, :DV] / jnp.maximum(jnp.abs(nd[:, DV:]), emt)
            hsum = h if hsum is None else hsum + h
        y = hsum * lax.rsqrt(jnp.mean(hsum * hsum, axis=1, keepdims=True) + NORM_EPS) * gh_ref[hh:hh + 1, :]
        out_ref[sl, DV * hh:DV * (hh + 1)] = (og * y).astype(out_ref.dtype)

    def out_body(c, carry):
        chunk_out(c, 0)
        chunk_out(c, 1)
        return carry

    lax.fori_loop(0, NC, out_body, 0)


def _mlstm(proj, grow, conv_w, bias_r, g_head, B, S):
    T = B * S
    NC = S // MLSTM_LC
    pw = 2 * MLSTM_DQK
    vw = 2 * MLSTM_DV
    return pl.pallas_call(
        _mlstm_kernel,
        grid=(B, HEAD_PAIRS),
        in_specs=[
            pl.BlockSpec((S, pw), lambda b, p: (b, OFF_MQ // pw + p)),
            pl.BlockSpec((S, pw), lambda b, p: (b, OFF_MK // pw + p)),
            pl.BlockSpec((S, vw), lambda b, p: (b, OFF_MV // vw + p)),
            pl.BlockSpec((S, vw), lambda b, p: (b, OFF_MO // vw + p)),
            pl.BlockSpec((None, GATES_PER_PAIR, S), lambda b, p: (p, 0, b)),
            pl.BlockSpec((CONV_W, pw), lambda b, p: (0, p)),
            pl.BlockSpec((CONV_W, pw), lambda b, p: (0, MQ_COLS // pw + p)),
            pl.BlockSpec((None, GATES_PER_PAIR, 1), lambda b, p: (p, 0, 0)),
            pl.BlockSpec((None, 2, MLSTM_DV), lambda b, p: (p, 0, 0)),
        ],
        out_specs=pl.BlockSpec((S, vw), lambda b, p: (b, p)),
        out_shape=jax.ShapeDtypeStruct((T, MLSTM_WIDTH), BF16),
        scratch_shapes=[
            pltpu.VMEM((S, MLSTM_DQK), BF16),
            pltpu.VMEM((S, MLSTM_DQK), BF16),
            pltpu.VMEM((NC, pw, MLSTM_LC), F32),
            pltpu.VMEM((S, LANES), F32),
            pltpu.VMEM((NC, GATES_PER_PAIR, MLSTM_LC), F32),
            pltpu.VMEM((NC, GATES_PER_PAIR, MLSTM_LC), F32),
            pltpu.VMEM((NC, GATES_PER_PAIR, MLSTM_LC), F32),
            pltpu.VMEM((NC, GATES_PER_PAIR, MLSTM_LC), F32),
            pltpu.VMEM((CHAINS, MLSTM_DQK, 2 * MLSTM_DV), F32),
            pltpu.VMEM((NC, CHAINS, MLSTM_DQK, 2 * MLSTM_DV), BF16),
            pltpu.VMEM((S + 2 * SUBLANES, pw), F32),
        ],
        compiler_params=_cparams(("parallel", "parallel")),
        name="mlstm",
    )(proj, proj, proj, proj, grow, conv_w, conv_w, bias_r, g_head)


def _norm_rope(x, g, cos, sin_signed):
    xn = x * lax.rsqrt(jnp.mean(x * x, axis=1, keepdims=True) + NORM_EPS) * g
    lane = lax.broadcasted_iota(jnp.int32, x.shape, 1)
    first_half = (lane % (2 * ROPE_FREQS)) < ROPE_FREQS
    partner = jnp.where(first_half,
                        pltpu.roll(xn, LANES - ROPE_FREQS, 1),
                        pltpu.roll(xn, ROPE_FREQS, 1))
    return xn * cos + partner * sin_signed


def _attn_kernel(q_ref, k_ref, v_ref, cq_ref, sq_ref, ck_ref, sk_ref, gq_ref, gk_ref, o_ref, kr_s, vx_s):
    d = ATTN_HEAD_DIM

    @pl.when(pl.program_id(2) == 0)
    def _():
        kr_s[...] = _norm_rope(k_ref[...].astype(F32), gk_ref[...], ck_ref[...], sk_ref[...]).astype(BF16)
        vx_s[...] = jnp.concatenate([v_ref[...], jnp.ones(v_ref.shape, BF16)], axis=1)

    gq = gq_ref[...]
    kr = kr_s[...]
    vx = vx_s[...]
    for r0 in range(0, q_ref.shape[0], ATTN_CHAIN_ROWS):
        rows = slice(r0, r0 + ATTN_CHAIN_ROWS)
        cq, sq = cq_ref[rows, :], sq_ref[rows, :]
        for g in range(ATTN_GROUP):
            cols = slice(d * g, d * (g + 1))
            qg = (_norm_rope(q_ref[rows, cols].astype(F32), gq, cq, sq) * (d ** -0.5 * LOG2E)).astype(BF16)
            s = lax.dot_general(qg, kr, (((1,), (1,)), ((), ())), preferred_element_type=F32)
            p = jnp.exp2(s - jnp.max(s, axis=1, keepdims=True))
            ov = jnp.dot(p.astype(BF16), vx, preferred_element_type=F32)
            o_ref[rows, cols] = (ov[:, :d] / ov[:, d:]).astype(o_ref.dtype)


def _attn(proj, cos_t, sin_t, g_q, g_k, B, S, tq=512):
    T = B * S
    d = ATTN_HEAD_DIM
    gw = ATTN_GROUP * d
    nq = S // tq
    return pl.pallas_call(
        _attn_kernel,
        grid=(B, ATTN_KV_HEADS, nq),
        in_specs=[
            pl.BlockSpec((tq, gw), lambda b, kv, qi: (b * nq + qi, OFF_AQ // gw + kv)),
            pl.BlockSpec((S, d), lambda b, kv, qi: (b, OFF_AK // d + kv)),
            pl.BlockSpec((S, d), lambda b, kv, qi: (b, OFF_AV // d + kv)),
            pl.BlockSpec((tq, d), lambda b, kv, qi: (qi, 0)),
            pl.BlockSpec((tq, d), lambda b, kv, qi: (qi, 0)),
            _resident((S, d), lambda b, kv, qi: (0, 0)),
            _resident((S, d), lambda b, kv, qi: (0, 0)),
            _resident((1, d), lambda b, kv, qi: (0, 0)),
            _resident((1, d), lambda b, kv, qi: (0, 0)),
        ],
        out_specs=pl.BlockSpec((tq, gw), lambda b, kv, qi: (b * nq + qi, kv)),
        out_shape=jax.ShapeDtypeStruct((T, ATTN_WIDTH), BF16),
        scratch_shapes=[pltpu.VMEM((S, d), BF16), pltpu.VMEM((S, 2 * d), BF16)],
        compiler_params=_cparams(("parallel", "parallel", "arbitrary")),
        name="attn",
    )(proj, proj, proj, cos_t, sin_t, cos_t, sin_t, g_q, g_k)


def _layer_norm(y, g, b):
    mu = jnp.mean(y, axis=1, keepdims=True)
    yc = y - mu
    var = jnp.mean(yc * yc, axis=1, keepdims=True)
    return yc * lax.rsqrt(var + NORM_EPS) * g + b


def _split_bf16(x):
    hi = x.astype(BF16)
    return hi, (x - hi.astype(F32)).astype(BF16)


def _tile_perm():
    n = TOK_TILE * COL_BLOCKS
    a = lax.broadcasted_iota(jnp.int32, (n, n), 0)
    b = lax.broadcasted_iota(jnp.int32, (n, n), 1)
    return jnp.where((a >> 4) == (b & 15), jnp.where((a & 15) == (b >> 4), 1.0, 0.0), 0.0).astype(BF16)


def _token_major(xh, perm):
    out = []
    for r0 in range(0, xh.shape[0], 2 * TOK_TILE):
        halves = [jnp.concatenate([xh[r0 + h * TOK_TILE:r0 + (h + 1) * TOK_TILE, LANES * j:LANES * (j + 1)]
                                   for j in range(COL_BLOCKS)], axis=0) for h in range(2)]
        o2 = jnp.dot(perm, jnp.concatenate(halves, axis=1), preferred_element_type=F32).astype(BF16)
        out += [o2[:, :LANES], o2[:, LANES:]]
    return jnp.concatenate(out, axis=0)


def _row_major(g_ref, perm):
    n = TOK_TILE * COL_BLOCKS
    cols = [[] for _ in range(COL_BLOCKS)]
    for r0 in range(0, g_ref.shape[0], 2 * n):
        m2 = jnp.concatenate([g_ref[r0:r0 + n, :], g_ref[r0 + n:r0 + 2 * n, :]], axis=1)
        o2 = jnp.dot(perm, m2, preferred_element_type=F32).astype(BF16)
        for j in range(COL_BLOCKS):
            blk = o2[TOK_TILE * j:TOK_TILE * (j + 1), :]
            cols[j] += [blk[:, :LANES], blk[:, LANES:]]
    return jnp.concatenate([jnp.concatenate(cj, axis=0) for cj in cols], axis=1)


def _outproj_kernel(hm_ref, ha_ref, x_ref, wt_ref, wb_ref, g_ref, b_ref, wr_ref, x1_ref, x1b_ref, xg_ref, aff_ref):
    y = (ALPHA * x_ref[...]
         + jnp.dot(hm_ref[...], wt_ref[...], preferred_element_type=F32)
         + jnp.dot(ha_ref[...], wb_ref[...], preferred_element_type=F32))
    x1 = _layer_norm(y, g_ref[...], b_ref[...])
    x1_ref[...] = x1
    xh, xl = _split_bf16(x1)
    wh, wl = _split_bf16(wr_ref[...])
    x1b_ref[...] = xh
    xg_ref[...] = _token_major(xh, _tile_perm())
    nt = (((1,), (1,)), ((), ()))
    logits = (lax.dot_general(wh, xh, nt, preferred_element_type=F32)
              + lax.dot_general(wh, xl, nt, preferred_element_type=F32)
              + lax.dot_general(wl, xh, nt, preferred_element_type=F32))
    e = jnp.exp(logits - jnp.max(logits, axis=0, keepdims=True))
    aff_ref[:, 0, :] = e / jnp.sum(e, axis=0, keepdims=True)


def _outproj(hm, ha, x2, wt, wb, g, b, wr, S, tm=512):
    T, D = x2.shape
    nt = S // tm
    return pl.pallas_call(
        _outproj_kernel,
        grid=(T // tm,),
        in_specs=[
            pl.BlockSpec((tm, MLSTM_WIDTH), lambda i: (i, 0)),
            pl.BlockSpec((tm, ATTN_WIDTH), lambda i: (i, 0)),
            pl.BlockSpec((tm, D), lambda i: (i, 0)),
            _resident((MLSTM_WIDTH, D), lambda i: (0, 0)),
            _resident((ATTN_WIDTH, D), lambda i: (MLSTM_WIDTH // ATTN_WIDTH, 0)),
            _resident((1, D), lambda i: (0, 0)),
            _resident((1, D), lambda i: (0, 0)),
            _resident((N_EXPERTS, D), lambda i: (0, 0)),
        ],
        out_specs=[
            pl.BlockSpec((tm, D), lambda i: (i, 0)),
            pl.BlockSpec((tm, D), lambda i: (i, 0)),
            pl.BlockSpec((tm * TOK_TILE, LANES), lambda i: (i, 0)),
            pl.BlockSpec((N_EXPERTS, None, 1, tm), lambda i: (0, i // nt, 0, i % nt)),
        ],
        out_shape=[
            jax.ShapeDtypeStruct((T, D), F32),
            jax.ShapeDtypeStruct((T, D), BF16),
            jax.ShapeDtypeStruct((T * TOK_TILE, LANES), BF16),
            jax.ShapeDtypeStruct((N_EXPERTS, T // S, 1, S), F32),
        ],
        compiler_params=_cparams(("parallel",)),
        name="outproj",
    )(hm, ha, x2, wt, wb, g, b, wr)


def _topk_kernel(aff_ref, pos_ref, idx_ref, tri_s, *, cap):
    S = aff_ref.shape[2]

    @pl.when(pl.program_id(0) == 0)
    def _():
        r = lax.broadcasted_iota(jnp.int32, (S, S), 0)
        c = lax.broadcasted_iota(jnp.int32, (S, S), 1)
        tri_s[...] = jnp.where(r < c, 1.0, 0.0).astype(BF16)

    a = aff_ref[:, 0, :]

    def count_ge(v):
        return jnp.sum(jnp.where(a >= v, 1.0, 0.0), axis=1, keepdims=True)

    thr = jnp.zeros((a.shape[0], 1), jnp.int32)
    for bit in range(30, -1, -1):
        cand = thr | (1 << bit)
        thr = jnp.where(count_ge(pltpu.bitcast(cand, F32)) >= cap, cand, thr)
    lo = pltpu.bitcast(thr, F32)
    hi = pltpu.bitcast(thr + 1, F32)
    for _ in range(TOPK_REFINE_STEPS):
        mid = 0.5 * (lo + hi)
        ok = count_ge(mid) >= cap
        lo = jnp.where(ok, mid, lo)
        hi = jnp.where(ok, hi, mid)
    gt = a > lo
    eq = a == lo
    need = cap - jnp.sum(jnp.where(gt, 1.0, 0.0), axis=1, keepdims=True)
    tri = tri_s[...]
    eq_rank = jnp.dot(jnp.where(eq, 1.0, 0.0).astype(BF16), tri, preferred_element_type=F32)
    sel = jnp.logical_or(gt, jnp.logical_and(eq, eq_rank < need))
    pos = jnp.dot(jnp.where(sel, 1.0, 0.0).astype(BF16), tri, preferred_element_type=F32)
    posm = jnp.where(sel, pos, -1.0)
    pos_ref[:, 0, :] = posm
    digit_row = lax.broadcasted_iota(jnp.int32, (2 * SUBLANES, S), 0)
    tok = lax.broadcasted_iota(jnp.int32, (2 * SUBLANES, S), 1)
    digits = jnp.where(digit_row == 0, tok >> 4, jnp.where(digit_row == 1, tok & 15, 0)).astype(F32).astype(BF16)
    slot = lax.broadcasted_iota(jnp.int32, (cap, S), 0).astype(F32)
    for e in range(a.shape[0]):
        hit = jnp.where(slot == posm[e:e + 1, :], 1.0, 0.0).astype(BF16)
        idx_ref[e] = lax.dot_general(digits, hit, (((1,), (1,)), ((), ())),
                                     preferred_element_type=F32)[:SUBLANES]


def _topk(aff, B, S, cap):
    E = aff.shape[0]
    return pl.pallas_call(
        functools.partial(_topk_kernel, cap=cap),
        grid=(B,),
        in_specs=[pl.BlockSpec((E, None, 1, S), lambda b: (0, b, 0, 0))],
        out_specs=[pl.BlockSpec((E, None, 1, S), lambda b: (0, b, 0, 0)),
                   pl.BlockSpec((E, None, SUBLANES, cap), lambda b: (0, b, 0, 0))],
        out_shape=[jax.ShapeDtypeStruct((E, B, 1, S), F32),
                   jax.ShapeDtypeStruct((E, B, SUBLANES, cap), F32)],
        scratch_shapes=[pltpu.VMEM((S, S), BF16)],
        compiler_params=_cparams(("arbitrary",)),
        name="topk",
    )(aff)


def _moe_ffn_kernel(idx_ref, xg_ref, pos_ref, aff_ref, wg_ref, wu_ref, wd_ref, y_ref, gbuf_s, wg_s, wu_s, wd_s,
                    *, cap, n_exp):
    e = pl.program_id(0)
    b = pl.program_id(1)

    @pl.when(e < n_exp)
    def _():
        slot = e % 2
        rg = pl.multiple_of(b * wg_ref.shape[0], wg_ref.shape[0])
        rd = pl.multiple_of(b * wd_ref.shape[0], wd_ref.shape[0])
        wg_s[slot, pl.ds(rg, wg_ref.shape[0]), :] = wg_ref[...].astype(BF16)
        wu_s[slot, pl.ds(rg, wu_ref.shape[0]), :] = wu_ref[...].astype(BF16)
        wd_s[slot, pl.ds(rd, wd_ref.shape[0]), :] = wd_ref[...].astype(BF16)

    @pl.when(e > 0)
    def _():
        S = pos_ref.shape[1]
        slot = (e - 1) % 2
        for c in range(cap):
            row = pl.multiple_of(idx_ref[0, c] * TOK_TILE, TOK_TILE)
            gbuf_s[TOK_TILE * c:TOK_TILE * (c + 1), :] = xg_ref[pl.ds(row, TOK_TILE), :]
        perm = _tile_perm()
        xs = _row_major(gbuf_s, perm)
        sl = lax.broadcasted_iota(jnp.int32, (cap, S), 0).astype(F32)
        gate = jnp.sum(jnp.where(sl == pos_ref[...], aff_ref[...], 0.0), axis=1, keepdims=True)
        hg = jnp.dot(xs, wg_s[slot], preferred_element_type=F32)
        hu = jnp.dot(xs, wu_s[slot], preferred_element_type=F32)
        hid = (hg * _sigmoid(hg) * hu).astype(BF16)
        y = (jnp.dot(hid, wd_s[slot], preferred_element_type=F32) * gate).astype(BF16)
        y_ref[...] = _token_major(y, perm)


def _moe_ffn(idx, xg, pos4, aff4, wg, wu, wd, B, S, cap):
    E, D, F = wg.shape
    prev = lambda e: jnp.maximum(e - 1, 0)
    stage = lambda e, b: (jnp.minimum(e, E - 1), jnp.where(e < E, b, B - 1), 0)
    return pl.pallas_call(
        functools.partial(_moe_ffn_kernel, cap=cap, n_exp=E),
        grid=(E + 1, B),
        in_specs=[
            pl.BlockSpec((None, None, 1, cap), lambda e, b: (prev(e), b, 0, 0), memory_space=pltpu.SMEM),
            pl.BlockSpec((S * TOK_TILE, LANES), lambda e, b: (b, 0)),
            pl.BlockSpec((None, None, 1, S), lambda e, b: (prev(e), b, 0, 0)),
            pl.BlockSpec((None, None, 1, S), lambda e, b: (prev(e), b, 0, 0)),
            pl.BlockSpec((None, D // B, F), stage),
            pl.BlockSpec((None, D // B, F), stage),
            pl.BlockSpec((None, F // B, D), stage),
        ],
        out_specs=pl.BlockSpec((None, None, cap * TOK_TILE, LANES),
                               lambda e, b: (jnp.where(e > 0, b, 0), prev(e), 0, 0)),
        out_shape=jax.ShapeDtypeStruct((B, E, cap * TOK_TILE, LANES), BF16),
        scratch_shapes=[pltpu.VMEM((cap * TOK_TILE, LANES), BF16),
                        pltpu.VMEM((2, D, F), BF16), pltpu.VMEM((2, D, F), BF16), pltpu.VMEM((2, F, D), BF16)],
        compiler_params=_cparams(("arbitrary", "arbitrary")),
        name="moe_ffn",
    )(idx, xg, pos4, aff4, wg, wu, wd)


def _row_major_f32(g_ref, r0, n_tok, perm):
    n = TOK_TILE * COL_BLOCKS
    cols = [[] for _ in range(COL_BLOCKS)]
    for k in range(0, n_tok * TOK_TILE, 2 * n):
        m2 = jnp.concatenate([g_ref[pl.ds(r0 + k, n), :], g_ref[pl.ds(r0 + k + n, n), :]], axis=1)
        hi = m2.astype(BF16)
        r1 = m2 - hi.astype(F32)
        mid = r1.astype(BF16)
        lo = (r1 - mid.astype(F32)).astype(BF16)
        o2 = (jnp.dot(perm, hi, preferred_element_type=F32) + jnp.dot(perm, mid, preferred_element_type=F32)
              + jnp.dot(perm, lo, preferred_element_type=F32))
        for j in range(COL_BLOCKS):
            blk = o2[TOK_TILE * j:TOK_TILE * (j + 1), :]
            cols[j] += [blk[:, :LANES], blk[:, LANES:]]
    return jnp.concatenate([jnp.concatenate(cj, axis=0) for cj in cols], axis=1)


def _moe_comb_kernel(idx_ref, y_ref, o_ref, acc_s, *, cap, n_exp):
    j = pl.program_id(1)

    @pl.when(j == 0)
    def _():
        acc_s[...] = jnp.zeros(acc_s.shape, F32)

    @pl.when(j < n_exp)
    def _():
        for c0 in range(0, cap, SCATTER_BATCH):
            rows = [pl.multiple_of(idx_ref[0, c0 + u] * TOK_TILE, TOK_TILE) for u in range(SCATTER_BATCH)]
            new = [acc_s[pl.ds(rows[u], TOK_TILE), :]
                   + y_ref[TOK_TILE * (c0 + u):TOK_TILE * (c0 + u + 1), :].astype(F32) for u in range(SCATTER_BATCH)]
            for u in range(SCATTER_BATCH):
                acc_s[pl.ds(rows[u], TOK_TILE), :] = new[u]

    @pl.when(j >= n_exp)
    def _():
        n_tok = o_ref.shape[0]
        r0 = pl.multiple_of((j - n_exp) * (n_tok * TOK_TILE), n_tok * TOK_TILE)
        o_ref[...] = _row_major_f32(acc_s, r0, n_tok, _tile_perm())


def _moe_comb(idx, y, B, S, cap, chunk=512):
    E = y.shape[1]
    D = D_MODEL
    nch = S // chunk
    return pl.pallas_call(
        functools.partial(_moe_comb_kernel, cap=cap, n_exp=E),
        grid=(B, E + nch),
        in_specs=[
            pl.BlockSpec((None, None, 1, cap), lambda b, j: (jnp.minimum(j, E - 1), b, 0, 0),
                         memory_space=pltpu.SMEM),
            pl.BlockSpec((None, None, cap * TOK_TILE, LANES), lambda b, j: (b, jnp.minimum(j, E - 1), 0, 0)),
        ],
        out_specs=pl.BlockSpec((chunk, D), lambda b, j: (b * nch + jnp.maximum(j - E, 0), 0)),
        out_shape=jax.ShapeDtypeStruct((B * S, D), F32),
        scratch_shapes=[pltpu.VMEM((S * TOK_TILE, LANES), F32)],
        compiler_params=_cparams(("arbitrary", "arbitrary")),
        name="moe_comb",
    )(idx, y)


def _final_kernel(x1_ref, x1b_ref, moe_ref, p_ref, wpg_ref, bpg_ref, wpp_ref, g_ref, b_ref, o_ref):
    for r0 in range(0, x1_ref.shape[0], ROW_CHAIN):
        rows = slice(r0, r0 + ROW_CHAIN)
        gate = _sigmoid(jnp.dot(x1b_ref[rows, :], wpg_ref[...], preferred_element_type=F32) + bpg_ref[...])
        plv = gate * jnp.dot(p_ref[rows, :].astype(BF16), wpp_ref[...], preferred_element_type=F32)
        o_ref[rows, :] = _layer_norm(ALPHA * x1_ref[rows, :] + moe_ref[rows, :] + plv, g_ref[...], b_ref[...])


def _final(x1, x1b, moe, p2, wpg, bpg, wpp, g, b, tm=512):
    T, D = x1.shape
    return pl.pallas_call(
        _final_kernel,
        grid=(T // tm,),
        in_specs=[
            pl.BlockSpec((tm, D), lambda i: (i, 0)),
            pl.BlockSpec((tm, D), lambda i: (i, 0)),
            pl.BlockSpec((tm, D), lambda i: (i, 0)),
            pl.BlockSpec((tm, P_DIM), lambda i: (i, 0)),
            _resident((D, D), lambda i: (0, 0)),
            _resident((1, D), lambda i: (0, 0)),
            _resident((P_DIM, D), lambda i: (0, 0)),
            _resident((1, D), lambda i: (0, 0)),
            _resident((1, D), lambda i: (0, 0)),
        ],
        out_specs=pl.BlockSpec((tm, D), lambda i: (i, 0)),
        out_shape=jax.ShapeDtypeStruct((T, D), F32),
        compiler_params=_cparams(("parallel",)),
        name="final",
    )(x1, x1b, moe, p2, wpg, bpg, wpp, g, b)


def _rope_tables(S):
    rows = S // GRID_W
    row_idx = jnp.broadcast_to(jnp.arange(rows, dtype=F32)[:, None], (rows, GRID_W)).reshape(-1)
    col_idx = jnp.broadcast_to(jnp.arange(GRID_W, dtype=F32)[None, :], (rows, GRID_W)).reshape(-1)
    inv_freq = ROPE_THETA ** (-jnp.arange(ROPE_FREQS, dtype=F32) / ROPE_FREQS)
    ar = row_idx[:, None] * inv_freq
    ac = col_idx[:, None] * inv_freq
    cos_t = jnp.concatenate([jnp.cos(ar), jnp.cos(ar), jnp.cos(ac), jnp.cos(ac)], axis=1)
    sin_t = jnp.concatenate([-jnp.sin(ar), jnp.sin(ar), -jnp.sin(ac), jnp.sin(ac)], axis=1)
    return cos_t, sin_t


def _pair_major(a):
    return a.reshape(2, HEAD_PAIRS, 2).transpose(1, 0, 2)


def _layer(x2, p2, B, S, w_in, conv_w, b_i, b_f, g_mlstm, g_q, g_k, w_out, ln1_g, ln1_b, w_router, w_gate, w_up,
           w_down, w_pl_proj, w_pl_gate, b_pl_gate, ln2_g, ln2_b):
    D = D_MODEL
    cap = CAPACITY_FACTOR * S // N_EXPERTS
    o_mg = 2 * MQ_COLS + 2 * MV_COLS
    w_main = jnp.concatenate([w_in[:, :o_mg], w_in[:, o_mg + MG_COLS:]], axis=1).astype(BF16)
    w_g = w_in[:, o_mg:o_mg + MG_COLS].reshape(D, 2, 2, HEAD_PAIRS, 2).transpose(0, 3, 2, 1, 4).reshape(D, MG_COLS)
    bias = jnp.stack([_pair_major(b_i), _pair_major(b_f)], axis=1).reshape(HEAD_PAIRS, GATES_PER_PAIR)

    proj, grow = _proj(x2, w_main, w_g.astype(BF16).T)
    h_m = _mlstm(proj, grow, conv_w, bias[:, :, None], g_mlstm.reshape(HEAD_PAIRS, 2, MLSTM_DV), B, S)
    cos_t, sin_t = _rope_tables(S)
    h_a = _attn(proj, cos_t, sin_t, g_q[None, :], g_k[None, :], B, S)
    w_o = w_out.astype(BF16)
    x1, x1b, xg, aff4 = _outproj(h_m, h_a, x2, w_o, w_o, ln1_g[None, :], ln1_b[None, :],
                                 w_router.T, S)
    pos4, idx_digits = _topk(aff4, B, S, cap)
    idx =(idx_digits[:, :, 0:1, :] * TOK_TILE + idx_digits[:, :, 1:2, :]).astype(jnp.int32)
    y = _moe_ffn(idx, xg, pos4, aff4, w_gate, w_up, w_down, B, S, cap)
    moe = _moe_comb(idx, y, B, S, cap)
    return _final(x1, x1b, moe, p2, w_pl_gate.astype(BF16), b_pl_gate[None, :], w_pl_proj.astype(BF16),
                  ln2_g[None, :], ln2_b[None, :])


def kernel(x, p, w_in, conv_w, b_igate, b_fgate, g_mlstm, g_q, g_k, w_out, ln1_g, ln1_b, w_router, w_gate, w_up,
           w_down, w_pl_proj, w_pl_gate, b_pl_gate, ln2_g, ln2_b):
    B, S, D = x.shape
    x2 = x.reshape(B * S, D)
    for i in range(DEPTH):
        x2 = _layer(x2, p[i].reshape(B * S, P_DIM), B, S, w_in[i], conv_w[i], b_igate[i], b_fgate[i], g_mlstm[i],
                    g_q[i], g_k[i], w_out[i], ln1_g[i], ln1_b[i], w_router[i], w_gate[i], w_up[i], w_down[i],
                    w_pl_proj[i], w_pl_gate[i], b_pl_gate[i], ln2_g[i], ln2_b[i])
    return x2.reshape(B, S, D)
```

```python
import functools

import jax
import jax.numpy as jnp
from jax import lax
from jax.experimental import pallas as pl
from jax.experimental.pallas import tpu as pltpu

F32 = jnp.float32
BF16 = jnp.bfloat16

D_MODEL = 2048
P_DIM = 256
GRID_W = 64
MLSTM_WIDTH = D_MODEL // 2
ATTN_WIDTH = D_MODEL - MLSTM_WIDTH
MLSTM_HEADS = 8
MLSTM_DV = MLSTM_WIDTH // MLSTM_HEADS
MLSTM_DQK = MLSTM_DV // 2
CONV_W = 5
ATTN_HEAD_DIM = 128
ATTN_Q_HEADS = ATTN_WIDTH // ATTN_HEAD_DIM
ATTN_KV_HEADS = 2
ATTN_GROUP = ATTN_Q_HEADS // ATTN_KV_HEADS
ROPE_FREQS = ATTN_HEAD_DIM // 4
ROPE_THETA = 10000.0
N_EXPERTS = 16
EXPERT_FF = D_MODEL // 2
CAPACITY_FACTOR = 2
NORM_EPS = 1e-6
DEPTH = 1
ALPHA = (2.0 * DEPTH) ** 0.25
LOG2E = 1.4426950408889634

MQ_COLS = MLSTM_HEADS * MLSTM_DQK
MV_COLS = MLSTM_WIDTH
MG_COLS = 2 * 2 * MLSTM_HEADS
AQ_COLS = ATTN_WIDTH
AKV_COLS = ATTN_KV_HEADS * ATTN_HEAD_DIM
PROJ_COLS = 2 * MQ_COLS + 2 * MV_COLS + AQ_COLS + 2 * AKV_COLS
OFF_MQ, OFF_MK, OFF_MV, OFF_MO = 0, MQ_COLS, 2 * MQ_COLS, 2 * MQ_COLS + MV_COLS
OFF_AQ = OFF_MO + MV_COLS
OFF_AK = OFF_AQ + AQ_COLS
OFF_AV = OFF_AK + AKV_COLS

HEAD_PAIRS = MLSTM_HEADS // 2
GATES_PER_PAIR = 8
CHAINS = 4
MLSTM_LC = 128
STATE_UNROLL = 4
OUT_UNROLL = 8
ATTN_CHAIN_ROWS = 256
ROW_CHAIN = 256
LANES = 128
SUBLANES = 8
TOK_TILE = 16
COL_BLOCKS = D_MODEL // LANES
SCATTER_BATCH = 16
V7X_VMEM_LIMIT = 56 * 1024 * 1024
TOPK_REFINE_STEPS = 8


def _cparams(sem, vmem=V7X_VMEM_LIMIT):
    return pltpu.CompilerParams(dimension_semantics=sem, vmem_limit_bytes=vmem)


def _resident(shape, index_map):
    return pl.BlockSpec(shape, index_map, pipeline_mode=pl.Buffered(1))


def _sigmoid(x):
    return 0.5 * jnp.tanh(0.5 * x) + 0.5


def _log_sigmoid(x):
    return jnp.minimum(x, 0.0) - jnp.log1p(jnp.exp(-jnp.abs(x)))


def _proj_kernel(x_ref, w_ref, wgt_ref, o_ref, gr_ref, xb_ref):
    @pl.when(pl.program_id(1) == 0)
    def _():
        xb = x_ref[...].astype(BF16)
        xb_ref[...] = xb
        gr = lax.dot_general(wgt_ref[...], xb, (((1,), (1,)), ((), ())),
                             preferred_element_type=F32)
        for p in range(HEAD_PAIRS):
            gr_ref[p] = gr[GATES_PER_PAIR * p:GATES_PER_PAIR * (p + 1), :]

    o_ref[...] = jnp.dot(xb_ref[...], w_ref[...], preferred_element_type=F32).astype(o_ref.dtype)


def _proj(x2, w, wgt, tm=1024, tn=1536):
    T, D = x2.shape
    N = w.shape[1]
    return pl.pallas_call(
        _proj_kernel,
        grid=(T // tm, N // tn),
        in_specs=[
            pl.BlockSpec((tm, D), lambda i, j: (i, 0)),
            pl.BlockSpec((D, tn), lambda i, j: (0, j)),
            _resident((MG_COLS, D), lambda i, j: (0, 0)),
        ],
        out_specs=[
            pl.BlockSpec((tm, tn), lambda i, j: (i, j)),
            pl.BlockSpec((HEAD_PAIRS, GATES_PER_PAIR, tm), lambda i, j: (0, 0, i)),
        ],
        out_shape=[
            jax.ShapeDtypeStruct((T, N), BF16),
            jax.ShapeDtypeStruct((HEAD_PAIRS, GATES_PER_PAIR, T), F32),
        ],
        scratch_shapes=[pltpu.VMEM((tm, D), BF16)],
        compiler_params=_cparams(("parallel", "arbitrary")),
        name="proj",
    )(x2, w, wgt)


def _conv_silu(x, w, pad_s):
    S, C = x.shape
    half = CONV_W // 2
    halo = jnp.zeros((SUBLANES, C), F32)
    pad_s[0:SUBLANES, :] = halo
    pad_s[SUBLANES + S:2 * SUBLANES + S, :] = halo
    pad_s[SUBLANES:SUBLANES + S, :] = x
    acc = x * w[half:half + 1, :]
    for j in range(CONV_W):
        if j != half:
            acc = acc + pad_s[SUBLANES + j - half:SUBLANES + j - half + S, :] * w[j:j + 1, :]
    return acc * _sigmoid(acc)


def _mlstm_kernel(q_ref, k_ref, v_ref, o_ref, gr_ref, cwq_ref, cwk_ref, br_ref, gh_ref, out_ref,
                  q0_s, q1_s, kt_s, col_s, winter_s, wl_s, dec_s, sc_s, cst_s, call_s, pad_s, acol_s, rmat_s):
    S = q_ref.shape[0]
    L = MLSTM_LC
    NC = S // L
    DQ = MLSTM_DQK
    DV = MLSTM_DV
    hi = lax.Precision.HIGHEST
    neg = -jnp.inf

    g8 = gr_ref[...] + br_ref[...]
    li8 = g8 * LOG2E
    lf8 = pltpu.roll(_log_sigmoid(g8) * LOG2E, CHAINS, 0)
    row = lax.broadcasted_iota(jnp.int32, (GATES_PER_PAIR, L), 0)
    lane = lax.broadcasted_iota(jnp.int32, (GATES_PER_PAIR, L), 1)
    fwd = (row % CHAINS) < 2
    fwd1 = fwd[:, :1]
    si = lax.broadcasted_iota(jnp.int32, (L, 2 * L), 0)
    ti = lax.broadcasted_iota(jnp.int32, (L, 2 * L), 1)
    tri = jnp.where(ti < L, jnp.where(si <= ti, 1.0, 0.0), jnp.where(si >= ti - L, 1.0, 0.0))
    tot, mloc, b_l, cm_l, r_l = [], [], [], [], []
    for c in range(NC):
        pr = jnp.dot(lf8[:, c * L:(c + 1) * L], tri, precision=hi, preferred_element_type=F32)
        b_c = jnp.where(fwd, pr[:, :L], pr[:, L:])
        tot_c = pr[:, L - 1:L]
        li_c = li8[:, c * L:(c + 1) * L]
        r_c = li_c - b_c
        cm = r_c
        k = 1
        while k < L:
            pre = jnp.where(lane >= k, pltpu.roll(cm, k, 1), neg)
            suf = jnp.where(lane < L - k, pltpu.roll(cm, L - k, 1), neg)
            cm = jnp.maximum(cm, jnp.where(fwd, pre, suf))
            k *= 2
        g_c = tot_c - b_c + li_c
        mloc_c = jnp.max(g_c, axis=1, keepdims=True)
        wl_s[c] = jnp.exp2(g_c - mloc_c)
        r_l.append(r_c)
        tot.append(tot_c)
        mloc.append(mloc_c)
        b_l.append(b_c)
        cm_l.append(cm)

    def scan(order):
        m = jnp.zeros((GATES_PER_PAIR, 1), F32)
        m_in, dec, sc = [None] * NC, [None] * NC, [None] * NC
        for c in order:
            m_new = jnp.maximum(tot[c] + m, mloc[c])
            m_in[c] = m
            dec[c] = jnp.exp2(tot[c] + m - m_new)
            sc[c] = jnp.exp2(mloc[c] - m_new)
            m = m_new
        return m_in, dec, sc

    mf, df, sf = scan(range(NC))
    mb, db, sb = scan(range(NC - 1, -1, -1))
    def terms3(x):
        t0 = x.astype(BF16).astype(F32)
        rem = x - t0
        t1 = rem.astype(BF16).astype(F32)
        return t0, t1, (rem - t1).astype(BF16).astype(F32)

    blk_rows = 2 * SUBLANES
    rowi = lax.broadcasted_iota(jnp.int32, (blk_rows, L), 0)
    bro = lambda x, ci: jnp.broadcast_to(x[ci:ci + 1, :], (blk_rows, L))
    zblk = jnp.zeros((blk_rows, L), F32)
    pad = jnp.zeros((LANES - GATES_PER_PAIR, L), F32)
    for c in range(NC):
        m_in = jnp.where(fwd1, mf[c], mb[c])
        dec_s[c] = jnp.broadcast_to(jnp.where(fwd1, df[c], db[c]), (GATES_PER_PAIR, L))
        sc_s[c] = jnp.broadcast_to(jnp.where(fwd1, sf[c], sb[c]), (GATES_PER_PAIR, L))
        a_c = jnp.maximum(m_in, cm_l[c])
        winter_s[c] = jnp.exp2(m_in - a_c)
        emt_c = jnp.exp2(-(b_l[c] + a_c))
        col_s[c * L:(c + 1) * L, :] = jnp.concatenate([emt_c, pad], axis=0).T
        a0, a1, a2 = terms3(a_c)
        r0, r1, r2 = terms3(r_l[c])
        a_blks, r_cols = [], []
        for ci in range(CHAINS):
            a_blks.append(jnp.where(rowi == 0, bro(a0, ci), jnp.where(rowi == 1, bro(a1, ci), jnp.where(
                rowi == 2, bro(a2, ci), jnp.where(rowi < 6, 1.0, 0.0)))))
            r_blk = jnp.where(rowi < 3, -1.0, jnp.where(rowi == 3, bro(r0, ci), jnp.where(
                rowi == 4, bro(r1, ci), jnp.where(rowi == 5, bro(r2, ci), 0.0))))
            r_cols.append(jnp.concatenate([zblk] * ci + [r_blk] + [zblk] * (LANES // blk_rows - 1 - ci), axis=0))
        a_rows = jnp.concatenate(a_blks + [zblk] * (LANES // blk_rows - CHAINS), axis=0)
        acol_s[c * L:(c + 1) * L, :] = a_rows.T.astype(BF16)
        rmat_s[c] = jnp.concatenate(r_cols, axis=1).astype(BF16)

    qs = (_conv_silu(q_ref[...].astype(F32), cwq_ref[...], pad_s) * (DQ ** -0.5)).astype(BF16)
    q0_s[...] = qs[:, :DQ]
    q1_s[...] = qs[:, DQ:]
    kt = _conv_silu(k_ref[...].astype(F32), cwk_ref[...], pad_s).T
    for c in range(NC):
        kt_s[c] = kt[:, c * L:(c + 1) * L]

    ones_blk = jnp.ones((L, DV), BF16)

    def vext_of(sl, hh):
        return jnp.concatenate([v_ref[sl, DV * hh:DV * (hh + 1)], ones_blk], axis=1)

    cst_s[...] = jnp.zeros(cst_s.shape, F32)

    def state_step(c):
        for d in range(2):
            ch = c if d == 0 else NC - 1 - c
            sl = pl.ds(pl.multiple_of(ch * L, L), L)
            wl = wl_s[ch]
            ktc = kt_s[ch]
            decs = dec_s[ch]
            scs = sc_s[ch]
            for hh in range(2):
                ci = d * 2 + hh
                cx = cst_s[ci]
                call_s[ch, ci] = cx.astype(BF16)
                kw = (ktc[DQ * hh:DQ * (hh + 1), :] * wl[ci:ci + 1, :]).astype(BF16)
                cst_s[ci] = (decs[ci:ci + 1, 0:1] * cx
                             + scs[ci:ci + 1, 0:1] * jnp.dot(kw, vext_of(sl, hh), preferred_element_type=F32))

    def state_body(c, carry):
        for u in range(STATE_UNROLL):
            state_step(STATE_UNROLL * c + u)
        return carry

    lax.fori_loop(0, NC // STATE_UNROLL, state_body, 0)

    tt = lax.broadcasted_iota(jnp.int32, (L, L), 0)
    ss = lax.broadcasted_iota(jnp.int32, (L, L), 1)
    masks = (ss <= tt, ss >= tt)

    eye = tt == ss

    def chunk_out(ch, hh, sl, cols, winter, e_all):
        og = _sigmoid(o_ref[sl, DV * hh:DV * (hh + 1)].astype(F32))
        qc = (q0_s if hh == 0 else q1_s)[sl, :]
        kth = kt_s[ch, DQ * hh:DQ * (hh + 1), :].astype(BF16)
        qk = jnp.dot(qc, kth, preferred_element_type=F32)
        vext = vext_of(sl, hh)
        hsum = None
        for d in range(2):
            ci = d * 2 + hh
            w_intra = jnp.exp2(jnp.where(masks[d], e_all[:, L * ci:L * (ci + 1)], neg))
            qcx = jnp.dot(qc, call_s[ch, ci], preferred_element_type=F32).astype(BF16)
            w_diag = jnp.where(eye, winter[ci:ci + 1, :], 0.0).astype(BF16)
            nd = jnp.dot(jnp.concatenate([(qk * w_intra).astype(BF16), w_diag], axis=1),
                         jnp.concatenate([vext, qcx], axis=0),
                         preferred_element_type=F32)
            h = nd[:, :DV] / jnp.maximum(jnp.abs(nd[:, DV:]), cols[:, ci:ci + 1])
            hsum = h if hsum is None else hsum + h
        y = hsum * lax.rsqrt(jnp.mean(hsum * hsum, axis=1, keepdims=True) + NORM_EPS) * gh_ref[hh:hh + 1, :]
        out_ref[sl, DV * hh:DV * (hh + 1)] = (og * y).astype(out_ref.dtype)

    def out_chunk(c):
        sl = pl.ds(pl.multiple_of(c * L, L), L)
        e_all = jnp.dot(acol_s[sl, :], rmat_s[c], preferred_element_type=F32)
        cols = col_s[sl, :]
        winter = winter_s[c]
        chunk_out(c, 0, sl, cols, winter, e_all)
        chunk_out(c, 1, sl, cols, winter, e_all)

    def out_body(c, carry):
        for u in range(OUT_UNROLL):
            out_chunk(OUT_UNROLL * c + u)
        return carry

    lax.fori_loop(0, NC // OUT_UNROLL, out_body, 0)


def _mlstm(proj, grow, conv_w, bias_r, g_head, B, S):
    T = B * S
    NC = S // MLSTM_LC
    pw = 2 * MLSTM_DQK
    vw = 2 * MLSTM_DV
    return pl.pallas_call(
        _mlstm_kernel,
        grid=(B, HEAD_PAIRS),
        in_specs=[
            pl.BlockSpec((S, pw), lambda b, p: (b, OFF_MQ // pw + p)),
            pl.BlockSpec((S, pw), lambda b, p: (b, OFF_MK // pw + p)),
            pl.BlockSpec((S, vw), lambda b, p: (b, OFF_MV // vw + p)),
            pl.BlockSpec((S, vw), lambda b, p: (b, OFF_MO // vw + p)),
            pl.BlockSpec((None, GATES_PER_PAIR, S), lambda b, p: (p, 0, b)),
            pl.BlockSpec((CONV_W, pw), lambda b, p: (0, p)),
            pl.BlockSpec((CONV_W, pw), lambda b, p: (0, MQ_COLS // pw + p)),
            pl.BlockSpec((None, GATES_PER_PAIR, 1), lambda b, p: (p, 0, 0)),
            pl.BlockSpec((None, 2, MLSTM_DV), lambda b, p: (p, 0, 0)),
        ],
        out_specs=pl.BlockSpec((S, vw), lambda b, p: (b, p)),
        out_shape=jax.ShapeDtypeStruct((T, MLSTM_WIDTH), BF16),
        scratch_shapes=[
            pltpu.VMEM((S, MLSTM_DQK), BF16),
            pltpu.VMEM((S, MLSTM_DQK), BF16),
            pltpu.VMEM((NC, pw, MLSTM_LC), F32),
            pltpu.VMEM((S, LANES), F32),
            pltpu.VMEM((NC, GATES_PER_PAIR, MLSTM_LC), F32),
            pltpu.VMEM((NC, GATES_PER_PAIR, MLSTM_LC), F32),
            pltpu.VMEM((NC, GATES_PER_PAIR, MLSTM_LC), F32),
            pltpu.VMEM((NC, GATES_PER_PAIR, MLSTM_LC), F32),
            pltpu.VMEM((CHAINS, MLSTM_DQK, 2 * MLSTM_DV), F32),
            pltpu.VMEM((NC, CHAINS, MLSTM_DQK, 2 * MLSTM_DV), BF16),
            pltpu.VMEM((S + 2 * SUBLANES, pw), F32),
            pltpu.VMEM((S, LANES), BF16),
            pltpu.VMEM((NC, LANES, CHAINS * MLSTM_LC), BF16),
        ],
        compiler_params=_cparams(("parallel", "parallel")),
        name="mlstm",
    )(proj, proj, proj, proj, grow, conv_w, conv_w, bias_r, g_head)


def _norm_rope(x, g, cos, sin_signed):
    xn = x * lax.rsqrt(jnp.mean(x * x, axis=1, keepdims=True) + NORM_EPS) * g
    lane = lax.broadcasted_iota(jnp.int32, x.shape, 1)
    first_half = (lane % (2 * ROPE_FREQS)) < ROPE_FREQS
    partner = jnp.where(first_half,
                        pltpu.roll(xn, LANES - ROPE_FREQS, 1),
                        pltpu.roll(xn, ROPE_FREQS, 1))
    return xn * cos + partner * sin_signed


def _attn_kernel(q_ref, k_ref, v_ref, cq_ref, sq_ref, ck_ref, sk_ref, gq_ref, gk_ref, o_ref, kr_s, vx_s):
    d = ATTN_HEAD_DIM

    @pl.when(pl.program_id(2) == 0)
    def _():
        kr_s[...] = _norm_rope(k_ref[...].astype(F32), gk_ref[...], ck_ref[...], sk_ref[...]).astype(BF16)
        vx_s[...] = jnp.concatenate([v_ref[...], jnp.ones(v_ref.shape, BF16)], axis=1)

    gq = gq_ref[...]
    kr = kr_s[...]
    vx = vx_s[...]
    for r0 in range(0, q_ref.shape[0], ATTN_CHAIN_ROWS):
        rows = slice(r0, r0 + ATTN_CHAIN_ROWS)
        cq, sq = cq_ref[rows, :], sq_ref[rows, :]
        for g in range(ATTN_GROUP):
            cols = slice(d * g, d * (g + 1))
            qg = (_norm_rope(q_ref[rows, cols].astype(F32), gq, cq, sq) * (d ** -0.5 * LOG2E)).astype(BF16)
            s = lax.dot_general(qg, kr, (((1,), (1,)), ((), ())), preferred_element_type=F32)
            p = jnp.exp2(s - jnp.max(s, axis=1, keepdims=True))
            ov = jnp.dot(p.astype(BF16), vx, preferred_element_type=F32)
            o_ref[rows, cols] = (ov[:, :d] / ov[:, d:]).astype(o_ref.dtype)


def _attn(proj, cos_t, sin_t, g_q, g_k, B, S, tq=512):
    T = B * S
    d = ATTN_HEAD_DIM
    gw = ATTN_GROUP * d
    nq = S // tq
    return pl.pallas_call(
        _attn_kernel,
        grid=(B, ATTN_KV_HEADS, nq),
        in_specs=[
            pl.BlockSpec((tq, gw), lambda b, kv, qi: (b * nq + qi, OFF_AQ // gw + kv)),
            pl.BlockSpec((S, d), lambda b, kv, qi: (b, OFF_AK // d + kv)),
            pl.BlockSpec((S, d), lambda b, kv, qi: (b, OFF_AV // d + kv)),
            pl.BlockSpec((tq, d), lambda b, kv, qi: (qi, 0)),
            pl.BlockSpec((tq, d), lambda b, kv, qi: (qi, 0)),
            _resident((S, d), lambda b, kv, qi: (0, 0)),
            _resident((S, d), lambda b, kv, qi: (0, 0)),
            _resident((1, d), lambda b, kv, qi: (0, 0)),
            _resident((1, d), lambda b, kv, qi: (0, 0)),
        ],
        out_specs=pl.BlockSpec((tq, gw), lambda b, kv, qi: (b * nq + qi, kv)),
        out_shape=jax.ShapeDtypeStruct((T, ATTN_WIDTH), BF16),
        scratch_shapes=[pltpu.VMEM((S, d), BF16), pltpu.VMEM((S, 2 * d), BF16)],
        compiler_params=_cparams(("parallel", "parallel", "arbitrary")),
        name="attn",
    )(proj, proj, proj, cos_t, sin_t, cos_t, sin_t, g_q, g_k)


def _layer_norm(y, g, b):
    mu = jnp.mean(y, axis=1, keepdims=True)
    yc = y - mu
    var = jnp.mean(yc * yc, axis=1, keepdims=True)
    return yc * lax.rsqrt(var + NORM_EPS) * g + b


def _split_bf16(x):
    hi = x.astype(BF16)
    return hi, (x - hi.astype(F32)).astype(BF16)


def _tile_perm():
    n = TOK_TILE * COL_BLOCKS
    a = lax.broadcasted_iota(jnp.int32, (n, n), 0)
    b = lax.broadcasted_iota(jnp.int32, (n, n), 1)
    return jnp.where((a >> 4) == (b & 15), jnp.where((a & 15) == (b >> 4), 1.0, 0.0), 0.0).astype(BF16)


def _token_major(xh, perm):
    out = []
    for r0 in range(0, xh.shape[0], 2 * TOK_TILE):
        halves = [jnp.concatenate([xh[r0 + h * TOK_TILE:r0 + (h + 1) * TOK_TILE, LANES * j:LANES * (j + 1)]
                                   for j in range(COL_BLOCKS)], axis=0) for h in range(2)]
        o2 = jnp.dot(perm, jnp.concatenate(halves, axis=1), preferred_element_type=F32).astype(BF16)
        out += [o2[:, :LANES], o2[:, LANES:]]
    return jnp.concatenate(out, axis=0)


def _row_major(g_ref, perm):
    n = TOK_TILE * COL_BLOCKS
    cols = [[] for _ in range(COL_BLOCKS)]
    for r0 in range(0, g_ref.shape[0], 2 * n):
        m2 = jnp.concatenate([g_ref[r0:r0 + n, :], g_ref[r0 + n:r0 + 2 * n, :]], axis=1)
        o2 = jnp.dot(perm, m2, preferred_element_type=F32).astype(BF16)
        for j in range(COL_BLOCKS):
            blk = o2[TOK_TILE * j:TOK_TILE * (j + 1), :]
            cols[j] += [blk[:, :LANES], blk[:, LANES:]]
    return jnp.concatenate([jnp.concatenate(cj, axis=0) for cj in cols], axis=1)


def _outproj_kernel(hm_ref, ha_ref, x_ref, wt_ref, wb_ref, g_ref, b_ref, wr_ref, x1_ref, x1b_ref, xg_ref, aff_ref):
    y = (ALPHA * x_ref[...]
         + jnp.dot(hm_ref[...], wt_ref[...], preferred_element_type=F32)
         + jnp.dot(ha_ref[...], wb_ref[...], preferred_element_type=F32))
    x1 = _layer_norm(y, g_ref[...], b_ref[...])
    x1_ref[...] = x1
    xh, xl = _split_bf16(x1)
    wh, wl = _split_bf16(wr_ref[...])
    x1b_ref[...] = xh
    xg_ref[...] = _token_major(xh, _tile_perm())
    nt = (((1,), (1,)), ((), ()))
    logits = (lax.dot_general(wh, xh, nt, preferred_element_type=F32)
              + lax.dot_general(wh, xl, nt, preferred_element_type=F32)
              + lax.dot_general(wl, xh, nt, preferred_element_type=F32))
    e = jnp.exp(logits - jnp.max(logits, axis=0, keepdims=True))
    aff_ref[:, 0, :] = e / jnp.sum(e, axis=0, keepdims=True)


def _outproj(hm, ha, x2, wt, wb, g, b, wr, S, tm=512):
    T, D = x2.shape
    nt = S // tm
    return pl.pallas_call(
        _outproj_kernel,
        grid=(T // tm,),
        in_specs=[
            pl.BlockSpec((tm, MLSTM_WIDTH), lambda i: (i, 0)),
            pl.BlockSpec((tm, ATTN_WIDTH), lambda i: (i, 0)),
            pl.BlockSpec((tm, D), lambda i: (i, 0)),
            _resident((MLSTM_WIDTH, D), lambda i: (0, 0)),
            _resident((ATTN_WIDTH, D), lambda i: (MLSTM_WIDTH // ATTN_WIDTH, 0)),
            _resident((1, D), lambda i: (0, 0)),
            _resident((1, D), lambda i: (0, 0)),
            _resident((N_EXPERTS, D), lambda i: (0, 0)),
        ],
        out_specs=[
            pl.BlockSpec((tm, D), lambda i: (i, 0)),
            pl.BlockSpec((tm, D), lambda i: (i, 0)),
            pl.BlockSpec((tm * TOK_TILE, LANES), lambda i: (i, 0)),
            pl.BlockSpec((N_EXPERTS, None, 1, tm), lambda i: (0, i // nt, 0, i % nt)),
        ],
        out_shape=[
            jax.ShapeDtypeStruct((T, D), F32),
            jax.ShapeDtypeStruct((T, D), BF16),
            jax.ShapeDtypeStruct((T * TOK_TILE, LANES), BF16),
            jax.ShapeDtypeStruct((N_EXPERTS, T // S, 1, S), F32),
        ],
        compiler_params=_cparams(("parallel",)),
        name="outproj",
    )(hm, ha, x2, wt, wb, g, b, wr)


def _excl_prefix_count(m):
    E, S = m.shape
    nb = S // LANES
    s_i = lax.broadcasted_iota(jnp.int32, (LANES, 2 * LANES), 0)
    t_i = lax.broadcasted_iota(jnp.int32, (LANES, 2 * LANES), 1)
    tri_ext = jnp.where(t_i >= LANES, 1.0, jnp.where(s_i < t_i, 1.0, 0.0)).astype(BF16)
    r_i = lax.broadcasted_iota(jnp.int32, (nb * E, nb * E), 0)
    c_i = lax.broadcasted_iota(jnp.int32, (nb * E, nb * E), 1)
    earlier = jnp.where(c_i // E < r_i // E, jnp.where(c_i % E == r_i % E, 1.0, 0.0), 0.0).astype(BF16)
    stacked = jnp.concatenate([m[:, LANES * j:LANES * (j + 1)] for j in range(nb)], axis=0)
    r = jnp.dot(stacked, tri_ext, preferred_element_type=F32)
    off = jnp.dot(earlier, r[:, LANES:].astype(BF16), preferred_element_type=F32)
    tot = r[:, :LANES] + off
    return jnp.concatenate([tot[E * j:E * (j + 1), :] for j in range(nb)], axis=1)


def _topk_kernel(aff_ref, pos_ref, idx_ref, *, cap):
    S = aff_ref.shape[2]
    a = aff_ref[:, 0, :]

    def count_ge(v):
        return jnp.sum(jnp.where(a >= v, 1.0, 0.0), axis=1, keepdims=True)

    thr = jnp.zeros((a.shape[0], 1), jnp.int32)
    for bit in range(30, -1, -1):
        cand = thr | (1 << bit)
        thr = jnp.where(count_ge(pltpu.bitcast(cand, F32)) >= cap, cand, thr)
    lo = pltpu.bitcast(thr, F32)
    hi = pltpu.bitcast(thr + 1, F32)
    for _ in range(TOPK_REFINE_STEPS):
        mid = 0.5 * (lo + hi)
        ok = count_ge(mid) >= cap
        lo = jnp.where(ok, mid, lo)
        hi = jnp.where(ok, hi, mid)
    gt = a > lo
    eq = a == lo
    need = cap - jnp.sum(jnp.where(gt, 1.0, 0.0), axis=1, keepdims=True)
    eq_rank = _excl_prefix_count(jnp.where(eq, 1.0, 0.0).astype(BF16))
    sel = jnp.logical_or(gt, jnp.logical_and(eq, eq_rank < need))
    pos = _excl_prefix_count(jnp.where(sel, 1.0, 0.0).astype(BF16))
    posm = jnp.where(sel, pos, -1.0)
    pos_ref[:, 0, :] = posm
    digit_row = lax.broadcasted_iota(jnp.int32, (2 * SUBLANES, S), 0)
    tok = lax.broadcasted_iota(jnp.int32, (2 * SUBLANES, S), 1)
    digits = jnp.where(digit_row == 0, tok >> 4, jnp.where(digit_row == 1, tok & 15, 0)).astype(F32).astype(BF16)
    slot = lax.broadcasted_iota(jnp.int32, (cap, S), 0).astype(F32)
    for e in range(a.shape[0]):
        hit = jnp.where(slot == posm[e:e + 1, :], 1.0, 0.0).astype(BF16)
        idx_ref[e] = lax.dot_general(digits, hit, (((1,), (1,)), ((), ())),
                                     preferred_element_type=F32)[:SUBLANES]


def _topk(aff, B, S, cap):
    E = aff.shape[0]
    return pl.pallas_call(
        functools.partial(_topk_kernel, cap=cap),
        grid=(B,),
        in_specs=[pl.BlockSpec((E, None, 1, S), lambda b: (0, b, 0, 0))],
        out_specs=[pl.BlockSpec((E, None, 1, S), lambda b: (0, b, 0, 0)),
                   pl.BlockSpec((E, None, SUBLANES, cap), lambda b: (0, b, 0, 0))],
        out_shape=[jax.ShapeDtypeStruct((E, B, 1, S), F32),
                   jax.ShapeDtypeStruct((E, B, SUBLANES, cap), F32)],
        compiler_params=_cparams(("parallel",)),
        name="topk",
    )(aff)


def _moe_ffn_kernel(idx_ref, xg_ref, pos_ref, aff_ref, wg_ref, wu_ref, wd_ref, y_ref, gbuf_s, wg_s, wu_s, wd_s,
                    *, cap, n_exp):
    e = pl.program_id(0)
    b = pl.program_id(1)

    @pl.when(e < n_exp)
    def _():
        slot = e % 2
        rg = pl.multiple_of(b * wg_ref.shape[0], wg_ref.shape[0])
        rd = pl.multiple_of(b * wd_ref.shape[0], wd_ref.shape[0])
        wg_s[slot, pl.ds(rg, wg_ref.shape[0]), :] = wg_ref[...].astype(BF16)
        wu_s[slot, pl.ds(rg, wu_ref.shape[0]), :] = wu_ref[...].astype(BF16)
        wd_s[slot, pl.ds(rd, wd_ref.shape[0]), :] = wd_ref[...].astype(BF16)

    @pl.when(e > 0)
    def _():
        S = pos_ref.shape[1]
        slot = (e - 1) % 2
        for c in range(cap):
            row = pl.multiple_of(idx_ref[0, c] * TOK_TILE, TOK_TILE)
            gbuf_s[TOK_TILE * c:TOK_TILE * (c + 1), :] = xg_ref[pl.ds(row, TOK_TILE), :]
        perm = _tile_perm()
        xs = _row_major(gbuf_s, perm)
        sl = lax.broadcasted_iota(jnp.int32, (cap, S), 0).astype(F32)
        gate = jnp.sum(jnp.where(sl == pos_ref[...], aff_ref[...], 0.0), axis=1, keepdims=True)
        hg = jnp.dot(xs, wg_s[slot], preferred_element_type=F32)
        hu = jnp.dot(xs, wu_s[slot], preferred_element_type=F32)
        hid = (hg * _sigmoid(hg) * hu).astype(BF16)
        y = (jnp.dot(hid, wd_s[slot], preferred_element_type=F32) * gate).astype(BF16)
        y_ref[...] = _token_major(y, perm)


def _moe_ffn(idx, xg, pos4, aff4, wg, wu, wd, B, S, cap):
    E, D, F = wg.shape
    prev = lambda e: jnp.maximum(e - 1, 0)
    stage = lambda e, b: (jnp.minimum(e, E - 1), jnp.where(e < E, b, B - 1), 0)
    return pl.pallas_call(
        functools.partial(_moe_ffn_kernel, cap=cap, n_exp=E),
        grid=(E + 1, B),
        in_specs=[
            pl.BlockSpec((None, None, 1, cap), lambda e, b: (prev(e), b, 0, 0), memory_space=pltpu.SMEM),
            pl.BlockSpec((S * TOK_TILE, LANES), lambda e, b: (b, 0)),
            pl.BlockSpec((None, None, 1, S), lambda e, b: (prev(e), b, 0, 0)),
            pl.BlockSpec((None, None, 1, S), lambda e, b: (prev(e), b, 0, 0)),
            pl.BlockSpec((None, D // B, F), stage),
            pl.BlockSpec((None, D // B, F), stage),
            pl.BlockSpec((None, F // B, D), stage),
        ],
        out_specs=pl.BlockSpec((None, None, cap * TOK_TILE, LANES),
                               lambda e, b: (jnp.where(e > 0, b, 0), prev(e), 0, 0)),
        out_shape=jax.ShapeDtypeStruct((B, E, cap * TOK_TILE, LANES), BF16),
        scratch_shapes=[pltpu.VMEM((cap * TOK_TILE, LANES), BF16),
                        pltpu.VMEM((2, D, F), BF16), pltpu.VMEM((2, D, F), BF16), pltpu.VMEM((2, F, D), BF16)],
        compiler_params=_cparams(("arbitrary", "arbitrary")),
        name="moe_ffn",
    )(idx, xg, pos4, aff4, wg, wu, wd)


def _row_major_f32(g_ref, r0, n_tok, perm):
    n = TOK_TILE * COL_BLOCKS
    cols = [[] for _ in range(COL_BLOCKS)]
    for k in range(0, n_tok * TOK_TILE, 2 * n):
        m2 = jnp.concatenate([g_ref[pl.ds(r0 + k, n), :], g_ref[pl.ds(r0 + k + n, n), :]], axis=1)
        hi = m2.astype(BF16)
        r1 = m2 - hi.astype(F32)
        mid = r1.astype(BF16)
        lo = (r1 - mid.astype(F32)).astype(BF16)
        o2 = (jnp.dot(perm, hi, preferred_element_type=F32) + jnp.dot(perm, mid, preferred_element_type=F32)
              + jnp.dot(perm, lo, preferred_element_type=F32))
        for j in range(COL_BLOCKS):
            blk = o2[TOK_TILE * j:TOK_TILE * (j + 1), :]
            cols[j] += [blk[:, :LANES], blk[:, LANES:]]
    return jnp.concatenate([jnp.concatenate(cj, axis=0) for cj in cols], axis=1)


def _moe_comb_kernel(idx_ref, y_ref, o_ref, acc_s, *, cap, n_exp):
    j = pl.program_id(1)

    @pl.when(j == 0)
    def _():
        acc_s[...] = jnp.zeros(acc_s.shape, F32)

    @pl.when(j < n_exp)
    def _():
        for c0 in range(0, cap, SCATTER_BATCH):
            rows = [pl.multiple_of(idx_ref[0, c0 + u] * TOK_TILE, TOK_TILE) for u in range(SCATTER_BATCH)]
            new = [acc_s[pl.ds(rows[u], TOK_TILE), :]
                   + y_ref[TOK_TILE * (c0 + u):TOK_TILE * (c0 + u + 1), :].astype(F32) for u in range(SCATTER_BATCH)]
            for u in range(SCATTER_BATCH):
                acc_s[pl.ds(rows[u], TOK_TILE), :] = new[u]

    @pl.when(j >= n_exp)
    def _():
        n_tok = o_ref.shape[0]
        r0 = pl.multiple_of((j - n_exp) * (n_tok * TOK_TILE), n_tok * TOK_TILE)
        o_ref[...] = _row_major_f32(acc_s, r0, n_tok, _tile_perm())


def _moe_comb(idx, y, B, S, cap, chunk=512):
    E = y.shape[1]
    D = D_MODEL
    nch = S // chunk
    return pl.pallas_call(
        functools.partial(_moe_comb_kernel, cap=cap, n_exp=E),
        grid=(B, E + nch),
        in_specs=[
            pl.BlockSpec((None, None, 1, cap), lambda b, j: (jnp.minimum(j, E - 1), b, 0, 0),
                         memory_space=pltpu.SMEM),
            pl.BlockSpec((None, None, cap * TOK_TILE, LANES), lambda b, j: (b, jnp.minimum(j, E - 1), 0, 0)),
        ],
        out_specs=pl.BlockSpec((chunk, D), lambda b, j: (b * nch + jnp.maximum(j - E, 0), 0)),
        out_shape=jax.ShapeDtypeStruct((B * S, D), F32),
        scratch_shapes=[pltpu.VMEM((S * TOK_TILE, LANES), F32)],
        compiler_params=_cparams(("arbitrary", "arbitrary")),
        name="moe_comb",
    )(idx, y)


def _final_kernel(x1_ref, x1b_ref, moe_ref, p_ref, wpg_ref, bpg_ref, wpp_ref, g_ref, b_ref, o_ref):
    for r0 in range(0, x1_ref.shape[0], ROW_CHAIN):
        rows = slice(r0, r0 + ROW_CHAIN)
        gate = _sigmoid(jnp.dot(x1b_ref[rows, :], wpg_ref[...], preferred_element_type=F32) + bpg_ref[...])
        plv = gate * jnp.dot(p_ref[rows, :].astype(BF16), wpp_ref[...], preferred_element_type=F32)
        o_ref[rows, :] = _layer_norm(ALPHA * x1_ref[rows, :] + moe_ref[rows, :] + plv, g_ref[...], b_ref[...])


def _final(x1, x1b, moe, p2, wpg, bpg, wpp, g, b, tm=512):
    T, D = x1.shape
    return pl.pallas_call(
        _final_kernel,
        grid=(T // tm,),
        in_specs=[
            pl.BlockSpec((tm, D), lambda i: (i, 0)),
            pl.BlockSpec((tm, D), lambda i: (i, 0)),
            pl.BlockSpec((tm, D), lambda i: (i, 0)),
            pl.BlockSpec((tm, P_DIM), lambda i: (i, 0)),
            _resident((D, D), lambda i: (0, 0)),
            _resident((1, D), lambda i: (0, 0)),
            _resident((P_DIM, D), lambda i: (0, 0)),
            _resident((1, D), lambda i: (0, 0)),
            _resident((1, D), lambda i: (0, 0)),
        ],
        out_specs=pl.BlockSpec((tm, D), lambda i: (i, 0)),
        out_shape=jax.ShapeDtypeStruct((T, D), F32),
        compiler_params=_cparams(("parallel",)),
        name="final",
    )(x1, x1b, moe, p2, wpg, bpg, wpp, g, b)


def _rope_tables(S):
    rows = S // GRID_W
    row_idx = jnp.broadcast_to(jnp.arange(rows, dtype=F32)[:, None], (rows, GRID_W)).reshape(-1)
    col_idx = jnp.broadcast_to(jnp.arange(GRID_W, dtype=F32)[None, :], (rows, GRID_W)).reshape(-1)
    inv_freq = ROPE_THETA ** (-jnp.arange(ROPE_FREQS, dtype=F32) / ROPE_FREQS)
    ar = row_idx[:, None] * inv_freq
    ac = col_idx[:, None] * inv_freq
    cos_t = jnp.concatenate([jnp.cos(ar), jnp.cos(ar), jnp.cos(ac), jnp.cos(ac)], axis=1)
    sin_t = jnp.concatenate([-jnp.sin(ar), jnp.sin(ar), -jnp.sin(ac), jnp.sin(ac)], axis=1)
    return cos_t, sin_t


def _pair_major(a):
    return a.reshape(2, HEAD_PAIRS, 2).transpose(1, 0, 2)


def _layer(x2, p2, B, S, w_in, conv_w, b_i, b_f, g_mlstm, g_q, g_k, w_out, ln1_g, ln1_b, w_router, w_gate, w_up,
           w_down, w_pl_proj, w_pl_gate, b_pl_gate, ln2_g, ln2_b):
    D = D_MODEL
    cap = CAPACITY_FACTOR * S // N_EXPERTS
    o_mg = 2 * MQ_COLS + 2 * MV_COLS
    w_main = jnp.concatenate([w_in[:, :o_mg], w_in[:, o_mg + MG_COLS:]], axis=1).astype(BF16)
    w_g = w_in[:, o_mg:o_mg + MG_COLS].reshape(D, 2, 2, HEAD_PAIRS, 2).transpose(0, 3, 2, 1, 4).reshape(D, MG_COLS)
    bias = jnp.stack([_pair_major(b_i), _pair_major(b_f)], axis=1).reshape(HEAD_PAIRS, GATES_PER_PAIR)

    proj, grow = _proj(x2, w_main, w_g.astype(BF16).T)
    h_m = _mlstm(proj, grow, conv_w, bias[:, :, None], g_mlstm.reshape(HEAD_PAIRS, 2, MLSTM_DV), B, S)
    cos_t, sin_t = _rope_tables(S)
    h_a = _attn(proj, cos_t, sin_t, g_q[None, :], g_k[None, :], B, S)
    w_o = w_out.astype(BF16)
    x1, x1b, xg, aff4 = _outproj(h_m, h_a, x2, w_o, w_o, ln1_g[None, :], ln1_b[None, :],
                                 w_router.T, S)
    pos4, idx_digits = _topk(aff4, B, S, cap)
    idx =(idx_digits[:, :, 0:1, :] * TOK_TILE + idx_digits[:, :, 1:2, :]).astype(jnp.int32)
    y = _moe_ffn(idx, xg, pos4, aff4, w_gate, w_up, w_down, B, S, cap)
    moe = _moe_comb(idx, y, B, S, cap)
    return _final(x1, x1b, moe, p2, w_pl_gate.astype(BF16), b_pl_gate[None, :], w_pl_proj.astype(BF16),
                  ln2_g[None, :], ln2_b[None, :])


def kernel(x, p, w_in, conv_w, b_igate, b_fgate, g_mlstm, g_q, g_k, w_out, ln1_g, ln1_b, w_router, w_gate, w_up,
           w_down, w_pl_proj, w_pl_gate, b_pl_gate, ln2_g, ln2_b):
    B, S, D = x.shape
    x2 = x.reshape(B * S, D)
    for i in range(DEPTH):
        x2 = _layer(x2, p[i].reshape(B * S, P_DIM), B, S, w_in[i], conv_w[i], b_igate[i], b_fgate[i], g_mlstm[i],
                    g_q[i], g_k[i], w_out[i], ln1_g[i], ln1_b[i], w_router[i], w_gate[i], w_up[i], w_down[i],
                    w_pl_proj[i], w_pl_gate[i], b_pl_gate[i], ln2_g[i], ln2_b[i])
    return x2.reshape(B, S, D)
```

```python
import functools

import jax
import jax.numpy as jnp
from jax import lax
from jax.experimental import pallas as pl
from jax.experimental.pallas import tpu as pltpu

F32 = jnp.float32
BF16 = jnp.bfloat16

D_MODEL = 2048
P_DIM = 256
GRID_W = 64
MLSTM_WIDTH = D_MODEL // 2
ATTN_WIDTH = D_MODEL - MLSTM_WIDTH
MLSTM_HEADS = 8
MLSTM_DV = MLSTM_WIDTH // MLSTM_HEADS
MLSTM_DQK = MLSTM_DV // 2
CONV_W = 5
ATTN_HEAD_DIM = 128
ATTN_Q_HEADS = ATTN_WIDTH // ATTN_HEAD_DIM
ATTN_KV_HEADS = 2
ATTN_GROUP = ATTN_Q_HEADS // ATTN_KV_HEADS
ROPE_FREQS = ATTN_HEAD_DIM // 4
ROPE_THETA = 10000.0
N_EXPERTS = 16
EXPERT_FF = D_MODEL // 2
CAPACITY_FACTOR = 2
NORM_EPS = 1e-6
DEPTH = 1
ALPHA = (2.0 * DEPTH) ** 0.25
LOG2E = 1.4426950408889634

MQ_COLS = MLSTM_HEADS * MLSTM_DQK
MV_COLS = MLSTM_WIDTH
MG_COLS = 2 * 2 * MLSTM_HEADS
AQ_COLS = ATTN_WIDTH
AKV_COLS = ATTN_KV_HEADS * ATTN_HEAD_DIM
PROJ_COLS = 2 * MQ_COLS + 2 * MV_COLS + AQ_COLS + 2 * AKV_COLS
OFF_MQ, OFF_MK, OFF_MV, OFF_MO = 0, MQ_COLS, 2 * MQ_COLS, 2 * MQ_COLS + MV_COLS
OFF_AQ = OFF_MO + MV_COLS
OFF_AK = OFF_AQ + AQ_COLS
OFF_AV = OFF_AK + AKV_COLS

HEAD_PAIRS = MLSTM_HEADS // 2
GATES_PER_PAIR = 8
CHAINS = 4
MLSTM_LC = 128
STATE_UNROLL = 4
OUT_UNROLL = 8
ATTN_CHAIN_ROWS = 256
ROW_CHAIN = 256
LANES = 128
SUBLANES = 8
TOK_TILE = 16
COL_BLOCKS = D_MODEL // LANES
SCATTER_BATCH = 16
V7X_VMEM_LIMIT = 56 * 1024 * 1024
TOPK_REFINE_STEPS = 8


def _cparams(sem, vmem=V7X_VMEM_LIMIT):
    return pltpu.CompilerParams(dimension_semantics=sem, vmem_limit_bytes=vmem)


def _resident(shape, index_map):
    return pl.BlockSpec(shape, index_map, pipeline_mode=pl.Buffered(1))


def _sigmoid(x):
    return 0.5 * jnp.tanh(0.5 * x) + 0.5


def _log_sigmoid(x):
    return jnp.minimum(x, 0.0) - jnp.log1p(jnp.exp(-jnp.abs(x)))


def _proj_kernel(x_ref, w_ref, wgt_ref, o_ref, gr_ref, xb_ref):
    @pl.when(pl.program_id(1) == 0)
    def _():
        xb = x_ref[...].astype(BF16)
        xb_ref[...] = xb
        gr = lax.dot_general(wgt_ref[...], xb, (((1,), (1,)), ((), ())),
                             preferred_element_type=F32)
        for p in range(HEAD_PAIRS):
            gr_ref[p] = gr[GATES_PER_PAIR * p:GATES_PER_PAIR * (p + 1), :]

    o_ref[...] = jnp.dot(xb_ref[...], w_ref[...], preferred_element_type=F32).astype(o_ref.dtype)


def _proj(x2, w, wgt, tm=1024, tn=1536):
    T, D = x2.shape
    N = w.shape[1]
    return pl.pallas_call(
        _proj_kernel,
        grid=(T // tm, N // tn),
        in_specs=[
            pl.BlockSpec((tm, D), lambda i, j: (i, 0)),
            pl.BlockSpec((D, tn), lambda i, j: (0, j)),
            _resident((MG_COLS, D), lambda i, j: (0, 0)),
        ],
        out_specs=[
            pl.BlockSpec((tm, tn), lambda i, j: (i, j)),
            pl.BlockSpec((HEAD_PAIRS, GATES_PER_PAIR, tm), lambda i, j: (0, 0, i)),
        ],
        out_shape=[
            jax.ShapeDtypeStruct((T, N), BF16),
            jax.ShapeDtypeStruct((HEAD_PAIRS, GATES_PER_PAIR, T), F32),
        ],
        scratch_shapes=[pltpu.VMEM((tm, D), BF16)],
        compiler_params=_cparams(("parallel", "arbitrary")),
        name="proj",
    )(x2, w, wgt)


def _conv_silu(x, w, pad_s):
    S, C = x.shape
    half = CONV_W // 2
    halo = jnp.zeros((SUBLANES, C), F32)
    pad_s[0:SUBLANES, :] = halo
    pad_s[SUBLANES + S:2 * SUBLANES + S, :] = halo
    pad_s[SUBLANES:SUBLANES + S, :] = x
    acc = x * w[half:half + 1, :]
    for j in range(CONV_W):
        if j != half:
            acc = acc + pad_s[SUBLANES + j - half:SUBLANES + j - half + S, :] * w[j:j + 1, :]
    return acc * _sigmoid(acc)


def _mlstm_kernel(q_ref, k_ref, v_ref, o_ref, gr_ref, cwq_ref, cwk_ref, br_ref, gh_ref, out_ref,
                  q0_s, q1_s, kt_s, col_s, winter_s, wl_s, dec_s, sc_s, cst_s, call_s, pad_s, acol_s, rmat_s):
    S = q_ref.shape[0]
    L = MLSTM_LC
    NC = S // L
    DQ = MLSTM_DQK
    DV = MLSTM_DV
    hi = lax.Precision.HIGHEST
    neg = -jnp.inf

    g8 = gr_ref[...] + br_ref[...]
    li8 = g8 * LOG2E
    lf8 = pltpu.roll(_log_sigmoid(g8) * LOG2E, CHAINS, 0)
    row = lax.broadcasted_iota(jnp.int32, (GATES_PER_PAIR, L), 0)
    lane = lax.broadcasted_iota(jnp.int32, (GATES_PER_PAIR, L), 1)
    fwd = (row % CHAINS) < 2
    fwd1 = fwd[:, :1]
    si = lax.broadcasted_iota(jnp.int32, (L, 2 * L), 0)
    ti = lax.broadcasted_iota(jnp.int32, (L, 2 * L), 1)
    tri = jnp.where(ti < L, jnp.where(si <= ti, 1.0, 0.0), jnp.where(si >= ti - L, 1.0, 0.0))
    tot, mloc, b_l, cm_l, r_l = [], [], [], [], []
    for c in range(NC):
        pr = jnp.dot(lf8[:, c * L:(c + 1) * L], tri, precision=hi, preferred_element_type=F32)
        b_c = jnp.where(fwd, pr[:, :L], pr[:, L:])
        tot_c = pr[:, L - 1:L]
        li_c = li8[:, c * L:(c + 1) * L]
        r_c = li_c - b_c
        cm = r_c
        k = 1
        while k < L:
            pre = jnp.where(lane >= k, pltpu.roll(cm, k, 1), neg)
            suf = jnp.where(lane < L - k, pltpu.roll(cm, L - k, 1), neg)
            cm = jnp.maximum(cm, jnp.where(fwd, pre, suf))
            k *= 2
        g_c = tot_c - b_c + li_c
        mloc_c = jnp.max(g_c, axis=1, keepdims=True)
        wl_s[c] = jnp.exp2(g_c - mloc_c)
        r_l.append(r_c)
        tot.append(tot_c)
        mloc.append(mloc_c)
        b_l.append(b_c)
        cm_l.append(cm)

    def scan(order):
        m = jnp.zeros((GATES_PER_PAIR, 1), F32)
        m_in, dec, sc = [None] * NC, [None] * NC, [None] * NC
        for c in order:
            m_new = jnp.maximum(tot[c] + m, mloc[c])
            m_in[c] = m
            dec[c] = jnp.exp2(tot[c] + m - m_new)
            sc[c] = jnp.exp2(mloc[c] - m_new)
            m = m_new
        return m_in, dec, sc

    mf, df, sf = scan(range(NC))
    mb, db, sb = scan(range(NC - 1, -1, -1))
    def terms3(x):
        t0 = x.astype(BF16).astype(F32)
        rem = x - t0
        t1 = rem.astype(BF16).astype(F32)
        return t0, t1, (rem - t1).astype(BF16).astype(F32)

    blk_rows = 2 * SUBLANES
    rowi = lax.broadcasted_iota(jnp.int32, (blk_rows, L), 0)
    bro = lambda x, ci: jnp.broadcast_to(x[ci:ci + 1, :], (blk_rows, L))
    zblk = jnp.zeros((blk_rows, L), F32)
    pad = jnp.zeros((LANES - GATES_PER_PAIR, L), F32)
    for c in range(NC):
        m_in = jnp.where(fwd1, mf[c], mb[c])
        dec_s[c] = jnp.broadcast_to(jnp.where(fwd1, df[c], db[c]), (GATES_PER_PAIR, L))
        sc_s[c] = jnp.broadcast_to(jnp.where(fwd1, sf[c], sb[c]), (GATES_PER_PAIR, L))
        a_c = jnp.maximum(m_in, cm_l[c])
        winter_s[c] = jnp.exp2(m_in - a_c)
        emt_c = jnp.exp2(-(b_l[c] + a_c))
        col_s[c * L:(c + 1) * L, :] = jnp.concatenate([emt_c, pad], axis=0).T
        a0, a1, a2 = terms3(a_c)
        r0, r1, r2 = terms3(r_l[c])
        a_blks, r_cols = [], []
        for ci in range(CHAINS):
            a_blks.append(jnp.where(rowi == 0, bro(a0, ci), jnp.where(rowi == 1, bro(a1, ci), jnp.where(
                rowi == 2, bro(a2, ci), jnp.where(rowi < 6, 1.0, 0.0)))))
            r_blk = jnp.where(rowi < 3, -1.0, jnp.where(rowi == 3, bro(r0, ci), jnp.where(
                rowi == 4, bro(r1, ci), jnp.where(rowi == 5, bro(r2, ci), 0.0))))
            r_cols.append(jnp.concatenate([zblk] * ci + [r_blk] + [zblk] * (LANES // blk_rows - 1 - ci), axis=0))
        a_rows = jnp.concatenate(a_blks + [zblk] * (LANES // blk_rows - CHAINS), axis=0)
        acol_s[c * L:(c + 1) * L, :] = a_rows.T.astype(BF16)
        rmat_s[c] = jnp.concatenate(r_cols, axis=1).astype(BF16)

    qs = (_conv_silu(q_ref[...].astype(F32), cwq_ref[...], pad_s) * (DQ ** -0.5)).astype(BF16)
    q0_s[...] = qs[:, :DQ]
    q1_s[...] = qs[:, DQ:]
    kt = _conv_silu(k_ref[...].astype(F32), cwk_ref[...], pad_s).T
    for c in range(NC):
        kt_s[c] = kt[:, c * L:(c + 1) * L]

    ones_blk = jnp.ones((L, DV), BF16)

    def vext_of(sl, hh):
        return jnp.concatenate([v_ref[sl, DV * hh:DV * (hh + 1)], ones_blk], axis=1)

    cst_s[...] = jnp.zeros(cst_s.shape, F32)

    def state_step(c):
        for d in range(2):
            ch = c if d == 0 else NC - 1 - c
            sl = pl.ds(pl.multiple_of(ch * L, L), L)
            wl = wl_s[ch]
            ktc = kt_s[ch]
            decs = dec_s[ch]
            scs = sc_s[ch]
            for hh in range(2):
                ci = d * 2 + hh
                cx = cst_s[ci]
                call_s[ch, ci] = cx.astype(BF16)
                kw = (ktc[DQ * hh:DQ * (hh + 1), :] * wl[ci:ci + 1, :]).astype(BF16)
                cst_s[ci] = (decs[ci:ci + 1, 0:1] * cx
                             + scs[ci:ci + 1, 0:1] * jnp.dot(kw, vext_of(sl, hh), preferred_element_type=F32))

    def state_body(c, carry):
        for u in range(STATE_UNROLL):
            state_step(STATE_UNROLL * c + u)
        return carry

    lax.fori_loop(0, NC // STATE_UNROLL, state_body, 0)

    tt = lax.broadcasted_iota(jnp.int32, (L, L), 0)
    ss = lax.broadcasted_iota(jnp.int32, (L, L), 1)
    masks = (ss <= tt, ss >= tt)

    eye = tt == ss

    def chunk_out(ch, hh, sl, cols, winter, e_all):
        og = _sigmoid(o_ref[sl, DV * hh:DV * (hh + 1)].astype(F32))
        qc = (q0_s if hh == 0 else q1_s)[sl, :]
        kth = kt_s[ch, DQ * hh:DQ * (hh + 1), :].astype(BF16)
        qk = jnp.dot(qc, kth, preferred_element_type=F32)
        vext = vext_of(sl, hh)
        hsum = None
        for d in range(2):
            ci = d * 2 + hh
            w_intra = jnp.exp2(jnp.where(masks[d], e_all[:, L * ci:L * (ci + 1)], neg))
            qcx = jnp.dot(qc, call_s[ch, ci], preferred_element_type=F32).astype(BF16)
            w_diag = jnp.where(eye, winter[ci:ci + 1, :], 0.0).astype(BF16)
            nd = jnp.dot(jnp.concatenate([(qk * w_intra).astype(BF16), w_diag], axis=1),
                         jnp.concatenate([vext, qcx], axis=0),
                         preferred_element_type=F32)
            h = nd[:, :DV] / jnp.maximum(jnp.abs(nd[:, DV:]), cols[:, ci:ci + 1])
            hsum = h if hsum is None else hsum + h
        y = hsum * lax.rsqrt(jnp.mean(hsum * hsum, axis=1, keepdims=True) + NORM_EPS) * gh_ref[hh:hh + 1, :]
        out_ref[sl, DV * hh:DV * (hh + 1)] = (og * y).astype(out_ref.dtype)

    def out_chunk(c):
        sl = pl.ds(pl.multiple_of(c * L, L), L)
        e_all = jnp.dot(acol_s[sl, :], rmat_s[c], preferred_element_type=F32)
        cols = col_s[sl, :]
        winter = winter_s[c]
        chunk_out(c, 0, sl, cols, winter, e_all)
        chunk_out(c, 1, sl, cols, winter, e_all)

    def out_body(c, carry):
        for u in range(OUT_UNROLL):
            out_chunk(OUT_UNROLL * c + u)
        return carry

    lax.fori_loop(0, NC // OUT_UNROLL, out_body, 0)


def _mlstm(proj, grow, conv_w, bias_r, g_head, B, S):
    T = B * S
    NC = S // MLSTM_LC
    pw = 2 * MLSTM_DQK
    vw = 2 * MLSTM_DV
    return pl.pallas_call(
        _mlstm_kernel,
        grid=(B, HEAD_PAIRS),
        in_specs=[
            pl.BlockSpec((S, pw), lambda b, p: (b, OFF_MQ // pw + p)),
            pl.BlockSpec((S, pw), lambda b, p: (b, OFF_MK // pw + p)),
            pl.BlockSpec((S, vw), lambda b, p: (b, OFF_MV // vw + p)),
            pl.BlockSpec((S, vw), lambda b, p: (b, OFF_MO // vw + p)),
            pl.BlockSpec((None, GATES_PER_PAIR, S), lambda b, p: (p, 0, b)),
            pl.BlockSpec((CONV_W, pw), lambda b, p: (0, p)),
            pl.BlockSpec((CONV_W, pw), lambda b, p: (0, MQ_COLS // pw + p)),
            pl.BlockSpec((None, GATES_PER_PAIR, 1), lambda b, p: (p, 0, 0)),
            pl.BlockSpec((None, 2, MLSTM_DV), lambda b, p: (p, 0, 0)),
        ],
        out_specs=pl.BlockSpec((S, vw), lambda b, p: (b, p)),
        out_shape=jax.ShapeDtypeStruct((T, MLSTM_WIDTH), BF16),
        scratch_shapes=[
            pltpu.VMEM((S, MLSTM_DQK), BF16),
            pltpu.VMEM((S, MLSTM_DQK), BF16),
            pltpu.VMEM((NC, pw, MLSTM_LC), F32),
            pltpu.VMEM((S, LANES), F32),
            pltpu.VMEM((NC, GATES_PER_PAIR, MLSTM_LC), F32),
            pltpu.VMEM((NC, GATES_PER_PAIR, MLSTM_LC), F32),
            pltpu.VMEM((NC, GATES_PER_PAIR, MLSTM_LC), F32),
            pltpu.VMEM((NC, GATES_PER_PAIR, MLSTM_LC), F32),
            pltpu.VMEM((CHAINS, MLSTM_DQK, 2 * MLSTM_DV), F32),
            pltpu.VMEM((NC, CHAINS, MLSTM_DQK, 2 * MLSTM_DV), BF16),
            pltpu.VMEM((S + 2 * SUBLANES, pw), F32),
            pltpu.VMEM((S, LANES), BF16),
            pltpu.VMEM((NC, LANES, CHAINS * MLSTM_LC), BF16),
        ],
        compiler_params=_cparams(("parallel", "parallel")),
        name="mlstm",
    )(proj, proj, proj, proj, grow, conv_w, conv_w, bias_r, g_head)


def _norm_rope(x, g, cos, sin_signed):
    xn = x * lax.rsqrt(jnp.mean(x * x, axis=1, keepdims=True) + NORM_EPS) * g
    lane = lax.broadcasted_iota(jnp.int32, x.shape, 1)
    first_half = (lane % (2 * ROPE_FREQS)) < ROPE_FREQS
    partner = jnp.where(first_half,
                        pltpu.roll(xn, LANES - ROPE_FREQS, 1),
                        pltpu.roll(xn, ROPE_FREQS, 1))
    return xn * cos + partner * sin_signed


def _attn_kernel(q_ref, k_ref, v_ref, cq_ref, sq_ref, ck_ref, sk_ref, gq_ref, gk_ref, o_ref, kr_s, vx_s):
    d = ATTN_HEAD_DIM

    @pl.when(pl.program_id(2) == 0)
    def _():
        kr_s[...] = _norm_rope(k_ref[...].astype(F32), gk_ref[...], ck_ref[...], sk_ref[...]).astype(BF16)
        vx_s[...] = jnp.concatenate([v_ref[...], jnp.ones(v_ref.shape, BF16)], axis=1)

    gq = gq_ref[...]
    kr = kr_s[...]
    vx = vx_s[...]
    for r0 in range(0, q_ref.shape[0], ATTN_CHAIN_ROWS):
        rows = slice(r0, r0 + ATTN_CHAIN_ROWS)
        cq, sq = cq_ref[rows, :], sq_ref[rows, :]
        for g in range(ATTN_GROUP):
            cols = slice(d * g, d * (g + 1))
            qg = (_norm_rope(q_ref[rows, cols].astype(F32), gq, cq, sq) * (d ** -0.5 * LOG2E)).astype(BF16)
            s = lax.dot_general(qg, kr, (((1,), (1,)), ((), ())), preferred_element_type=F32)
            p = jnp.exp2(s - jnp.max(s, axis=1, keepdims=True))
            ov = jnp.dot(p.astype(BF16), vx, preferred_element_type=F32)
            o_ref[rows, cols] = (ov[:, :d] / ov[:, d:]).astype(o_ref.dtype)


def _attn(proj, cos_t, sin_t, g_q, g_k, B, S, tq=1024):
    T = B * S
    d = ATTN_HEAD_DIM
    gw = ATTN_GROUP * d
    nq = S // tq
    return pl.pallas_call(
        _attn_kernel,
        grid=(B, ATTN_KV_HEADS, nq),
        in_specs=[
            pl.BlockSpec((tq, gw), lambda b, kv, qi: (b * nq + qi, OFF_AQ // gw + kv)),
            pl.BlockSpec((S, d), lambda b, kv, qi: (b, OFF_AK // d + kv)),
            pl.BlockSpec((S, d), lambda b, kv, qi: (b, OFF_AV // d + kv)),
            pl.BlockSpec((tq, d), lambda b, kv, qi: (qi, 0)),
            pl.BlockSpec((tq, d), lambda b, kv, qi: (qi, 0)),
            _resident((S, d), lambda b, kv, qi: (0, 0)),
            _resident((S, d), lambda b, kv, qi: (0, 0)),
            _resident((1, d), lambda b, kv, qi: (0, 0)),
            _resident((1, d), lambda b, kv, qi: (0, 0)),
        ],
        out_specs=pl.BlockSpec((tq, gw), lambda b, kv, qi: (b * nq + qi, kv)),
        out_shape=jax.ShapeDtypeStruct((T, ATTN_WIDTH), BF16),
        scratch_shapes=[pltpu.VMEM((S, d), BF16), pltpu.VMEM((S, 2 * d), BF16)],
        compiler_params=_cparams(("parallel", "parallel", "arbitrary")),
        name="attn",
    )(proj, proj, proj, cos_t, sin_t, cos_t, sin_t, g_q, g_k)


def _layer_norm(y, g, b):
    mu = jnp.mean(y, axis=1, keepdims=True)
    yc = y - mu
    var = jnp.mean(yc * yc, axis=1, keepdims=True)
    return yc * lax.rsqrt(var + NORM_EPS) * g + b


def _split_bf16(x):
    hi = x.astype(BF16)
    return hi, (x - hi.astype(F32)).astype(BF16)


def _tile_perm():
    n = TOK_TILE * COL_BLOCKS
    a = lax.broadcasted_iota(jnp.int32, (n, n), 0)
    b = lax.broadcasted_iota(jnp.int32, (n, n), 1)
    return jnp.where((a >> 4) == (b & 15), jnp.where((a & 15) == (b >> 4), 1.0, 0.0), 0.0).astype(BF16)


def _token_major(xh, perm):
    out = []
    for r0 in range(0, xh.shape[0], 2 * TOK_TILE):
        halves = [jnp.concatenate([xh[r0 + h * TOK_TILE:r0 + (h + 1) * TOK_TILE, LANES * j:LANES * (j + 1)]
                                   for j in range(COL_BLOCKS)], axis=0) for h in range(2)]
        o2 = jnp.dot(perm, jnp.concatenate(halves, axis=1), preferred_element_type=F32).astype(BF16)
        out += [o2[:, :LANES], o2[:, LANES:]]
    return jnp.concatenate(out, axis=0)


def _row_major(g_ref, perm):
    n = TOK_TILE * COL_BLOCKS
    cols = [[] for _ in range(COL_BLOCKS)]
    for r0 in range(0, g_ref.shape[0], 2 * n):
        m2 = jnp.concatenate([g_ref[r0:r0 + n, :], g_ref[r0 + n:r0 + 2 * n, :]], axis=1)
        o2 = jnp.dot(perm, m2, preferred_element_type=F32).astype(BF16)
        for j in range(COL_BLOCKS):
            blk = o2[TOK_TILE * j:TOK_TILE * (j + 1), :]
            cols[j] += [blk[:, :LANES], blk[:, LANES:]]
    return jnp.concatenate([jnp.concatenate(cj, axis=0) for cj in cols], axis=1)


def _outproj_kernel(hm_ref, ha_ref, x_ref, wt_ref, wb_ref, g_ref, b_ref, wr_ref, x1_ref, x1b_ref, xg_ref, aff_ref):
    y = (ALPHA * x_ref[...]
         + jnp.dot(hm_ref[...], wt_ref[...], preferred_element_type=F32)
         + jnp.dot(ha_ref[...], wb_ref[...], preferred_element_type=F32))
    x1 = _layer_norm(y, g_ref[...], b_ref[...])
    x1_ref[...] = x1
    xh, xl = _split_bf16(x1)
    wh, wl = _split_bf16(wr_ref[...])
    x1b_ref[...] = xh
    xg_ref[...] = _token_major(xh, _tile_perm())
    nt = (((1,), (1,)), ((), ()))
    logits = (lax.dot_general(wh, xh, nt, preferred_element_type=F32)
              + lax.dot_general(wh, xl, nt, preferred_element_type=F32)
              + lax.dot_general(wl, xh, nt, preferred_element_type=F32))
    e = jnp.exp(logits - jnp.max(logits, axis=0, keepdims=True))
    aff_ref[:, 0, :] = e / jnp.sum(e, axis=0, keepdims=True)


def _outproj(hm, ha, x2, wt, wb, g, b, wr, S, tm=512):
    T, D = x2.shape
    nt = S // tm
    return pl.pallas_call(
        _outproj_kernel,
        grid=(T // tm,),
        in_specs=[
            pl.BlockSpec((tm, MLSTM_WIDTH), lambda i: (i, 0)),
            pl.BlockSpec((tm, ATTN_WIDTH), lambda i: (i, 0)),
            pl.BlockSpec((tm, D), lambda i: (i, 0)),
            _resident((MLSTM_WIDTH, D), lambda i: (0, 0)),
            _resident((ATTN_WIDTH, D), lambda i: (MLSTM_WIDTH // ATTN_WIDTH, 0)),
            _resident((1, D), lambda i: (0, 0)),
            _resident((1, D), lambda i: (0, 0)),
            _resident((N_EXPERTS, D), lambda i: (0, 0)),
        ],
        out_specs=[
            pl.BlockSpec((tm, D), lambda i: (i, 0)),
            pl.BlockSpec((tm, D), lambda i: (i, 0)),
            pl.BlockSpec((tm * TOK_TILE, LANES), lambda i: (i, 0)),
            pl.BlockSpec((N_EXPERTS, None, 1, tm), lambda i: (0, i // nt, 0, i % nt)),
        ],
        out_shape=[
            jax.ShapeDtypeStruct((T, D), F32),
            jax.ShapeDtypeStruct((T, D), BF16),
            jax.ShapeDtypeStruct((T * TOK_TILE, LANES), BF16),
            jax.ShapeDtypeStruct((N_EXPERTS, T // S, 1, S), F32),
        ],
        compiler_params=_cparams(("parallel",)),
        name="outproj",
    )(hm, ha, x2, wt, wb, g, b, wr)


def _excl_prefix_count(m):
    E, S = m.shape
    nb = S // LANES
    s_i = lax.broadcasted_iota(jnp.int32, (LANES, 2 * LANES), 0)
    t_i = lax.broadcasted_iota(jnp.int32, (LANES, 2 * LANES), 1)
    tri_ext = jnp.where(t_i >= LANES, 1.0, jnp.where(s_i < t_i, 1.0, 0.0)).astype(BF16)
    r_i = lax.broadcasted_iota(jnp.int32, (nb * E, nb * E), 0)
    c_i = lax.broadcasted_iota(jnp.int32, (nb * E, nb * E), 1)
    earlier = jnp.where(c_i // E < r_i // E, jnp.where(c_i % E == r_i % E, 1.0, 0.0), 0.0).astype(BF16)
    stacked = jnp.concatenate([m[:, LANES * j:LANES * (j + 1)] for j in range(nb)], axis=0)
    r = jnp.dot(stacked, tri_ext, preferred_element_type=F32)
    off = jnp.dot(earlier, r[:, LANES:].astype(BF16), preferred_element_type=F32)
    tot = r[:, :LANES] + off
    return jnp.concatenate([tot[E * j:E * (j + 1), :] for j in range(nb)], axis=1)


def _topk_kernel(aff_ref, pos_ref, idx_ref, *, cap):
    S = aff_ref.shape[2]
    a = aff_ref[:, 0, :]

    def count_ge(v):
        return jnp.sum(jnp.where(a >= v, 1.0, 0.0), axis=1, keepdims=True)

    thr = jnp.zeros((a.shape[0], 1), jnp.int32)
    for bit in range(30, -1, -1):
        cand = thr | (1 << bit)
        thr = jnp.where(count_ge(pltpu.bitcast(cand, F32)) >= cap, cand, thr)
    lo = pltpu.bitcast(thr, F32)
    hi = pltpu.bitcast(thr + 1, F32)
    for _ in range(TOPK_REFINE_STEPS):
        mid = 0.5 * (lo + hi)
        ok = count_ge(mid) >= cap
        lo = jnp.where(ok, mid, lo)
        hi = jnp.where(ok, hi, mid)
    gt = a > lo
    eq = a == lo
    need = cap - jnp.sum(jnp.where(gt, 1.0, 0.0), axis=1, keepdims=True)
    eq_rank = _excl_prefix_count(jnp.where(eq, 1.0, 0.0).astype(BF16))
    sel = jnp.logical_or(gt, jnp.logical_and(eq, eq_rank < need))
    pos = _excl_prefix_count(jnp.where(sel, 1.0, 0.0).astype(BF16))
    posm = jnp.where(sel, pos, -1.0)
    pos_ref[:, 0, :] = posm
    digit_row = lax.broadcasted_iota(jnp.int32, (2 * SUBLANES, S), 0)
    tok = lax.broadcasted_iota(jnp.int32, (2 * SUBLANES, S), 1)
    digits = jnp.where(digit_row == 0, tok >> 4, jnp.where(digit_row == 1, tok & 15, 0)).astype(F32).astype(BF16)
    slot = lax.broadcasted_iota(jnp.int32, (cap, S), 0).astype(F32)
    for e in range(a.shape[0]):
        hit = jnp.where(slot == posm[e:e + 1, :], 1.0, 0.0).astype(BF16)
        idx_ref[e] = lax.dot_general(digits, hit, (((1,), (1,)), ((), ())),
                                     preferred_element_type=F32)[:SUBLANES]


def _topk(aff, B, S, cap):
    E = aff.shape[0]
    return pl.pallas_call(
        functools.partial(_topk_kernel, cap=cap),
        grid=(B,),
        in_specs=[pl.BlockSpec((E, None, 1, S), lambda b: (0, b, 0, 0))],
        out_specs=[pl.BlockSpec((E, None, 1, S), lambda b: (0, b, 0, 0)),
                   pl.BlockSpec((E, None, SUBLANES, cap), lambda b: (0, b, 0, 0))],
        out_shape=[jax.ShapeDtypeStruct((E, B, 1, S), F32),
                   jax.ShapeDtypeStruct((E, B, SUBLANES, cap), F32)],
        compiler_params=_cparams(("parallel",)),
        name="topk",
    )(aff)


def _moe_ffn_kernel(idx_ref, xg_ref, pos_ref, aff_ref, wg_ref, wu_ref, wd_ref, y_ref, gbuf_s, wg_s, wu_s, wd_s,
                    *, cap, n_exp):
    e = pl.program_id(0)
    b = pl.program_id(1)

    @pl.when(e < n_exp)
    def _():
        slot = e % 2
        rg = pl.multiple_of(b * wg_ref.shape[0], wg_ref.shape[0])
        rd = pl.multiple_of(b * wd_ref.shape[0], wd_ref.shape[0])
        wg_s[slot, pl.ds(rg, wg_ref.shape[0]), :] = wg_ref[...].astype(BF16)
        wu_s[slot, pl.ds(rg, wu_ref.shape[0]), :] = wu_ref[...].astype(BF16)
        wd_s[slot, pl.ds(rd, wd_ref.shape[0]), :] = wd_ref[...].astype(BF16)

    @pl.when(e > 0)
    def _():
        S = pos_ref.shape[1]
        slot = (e - 1) % 2
        for c in range(cap):
            row = pl.multiple_of(idx_ref[0, c] * TOK_TILE, TOK_TILE)
            gbuf_s[TOK_TILE * c:TOK_TILE * (c + 1), :] = xg_ref[pl.ds(row, TOK_TILE), :]
        perm = _tile_perm()
        xs = _row_major(gbuf_s, perm)
        sl = lax.broadcasted_iota(jnp.int32, (cap, S), 0).astype(F32)
        gate = jnp.sum(jnp.where(sl == pos_ref[...], aff_ref[...], 0.0), axis=1, keepdims=True)
        hg = jnp.dot(xs, wg_s[slot], preferred_element_type=F32)
        hu = jnp.dot(xs, wu_s[slot], preferred_element_type=F32)
        hid = (hg * _sigmoid(hg) * hu).astype(BF16)
        y = (jnp.dot(hid, wd_s[slot], preferred_element_type=F32) * gate).astype(BF16)
        y_ref[...] = _token_major(y, perm)


def _moe_ffn(idx, xg, pos4, aff4, wg, wu, wd, B, S, cap):
    E, D, F = wg.shape
    prev = lambda e: jnp.maximum(e - 1, 0)
    stage = lambda e, b: (jnp.minimum(e, E - 1), jnp.where(e < E, b, B - 1), 0)
    return pl.pallas_call(
        functools.partial(_moe_ffn_kernel, cap=cap, n_exp=E),
        grid=(E + 1, B),
        in_specs=[
            pl.BlockSpec((None, None, 1, cap), lambda e, b: (prev(e), b, 0, 0), memory_space=pltpu.SMEM),
            pl.BlockSpec((S * TOK_TILE, LANES), lambda e, b: (b, 0)),
            pl.BlockSpec((None, None, 1, S), lambda e, b: (prev(e), b, 0, 0)),
            pl.BlockSpec((None, None, 1, S), lambda e, b: (prev(e), b, 0, 0)),
            pl.BlockSpec((None, D // B, F), stage),
            pl.BlockSpec((None, D // B, F), stage),
            pl.BlockSpec((None, F // B, D), stage),
        ],
        out_specs=pl.BlockSpec((None, None, cap * TOK_TILE, LANES),
                               lambda e, b: (jnp.where(e > 0, b, 0), prev(e), 0, 0)),
        out_shape=jax.ShapeDtypeStruct((B, E, cap * TOK_TILE, LANES), BF16),
        scratch_shapes=[pltpu.VMEM((cap * TOK_TILE, LANES), BF16),
                        pltpu.VMEM((2, D, F), BF16), pltpu.VMEM((2, D, F), BF16), pltpu.VMEM((2, F, D), BF16)],
        compiler_params=_cparams(("arbitrary", "arbitrary")),
        name="moe_ffn",
    )(idx, xg, pos4, aff4, wg, wu, wd)


def _row_major_f32(g_ref, r0, n_tok, perm):
    n = TOK_TILE * COL_BLOCKS
    cols = [[] for _ in range(COL_BLOCKS)]
    for k in range(0, n_tok * TOK_TILE, 2 * n):
        m2 = jnp.concatenate([g_ref[pl.ds(r0 + k, n), :], g_ref[pl.ds(r0 + k + n, n), :]], axis=1)
        hi = m2.astype(BF16)
        r1 = m2 - hi.astype(F32)
        mid = r1.astype(BF16)
        lo = (r1 - mid.astype(F32)).astype(BF16)
        o2 = (jnp.dot(perm, hi, preferred_element_type=F32) + jnp.dot(perm, mid, preferred_element_type=F32)
              + jnp.dot(perm, lo, preferred_element_type=F32))
        for j in range(COL_BLOCKS):
            blk = o2[TOK_TILE * j:TOK_TILE * (j + 1), :]
            cols[j] += [blk[:, :LANES], blk[:, LANES:]]
    return jnp.concatenate([jnp.concatenate(cj, axis=0) for cj in cols], axis=1)


def _moe_comb_kernel(idx_ref, y_ref, o_ref, acc_s, *, cap, n_grp):
    j = pl.program_id(1)

    @pl.when(j == 0)
    def _():
        acc_s[...] = jnp.zeros(acc_s.shape, F32)

    @pl.when(j < n_grp)
    def _():
        for g in range(idx_ref.shape[0]):
            for c0 in range(0, cap, SCATTER_BATCH):
                rows = [pl.multiple_of(idx_ref[g, 0, c0 + u] * TOK_TILE, TOK_TILE) for u in range(SCATTER_BATCH)]
                new = [acc_s[pl.ds(rows[u], TOK_TILE), :]
                       + y_ref[g, TOK_TILE * (c0 + u):TOK_TILE * (c0 + u + 1), :].astype(F32)
                       for u in range(SCATTER_BATCH)]
                for u in range(SCATTER_BATCH):
                    acc_s[pl.ds(rows[u], TOK_TILE), :] = new[u]

    @pl.when(j >= n_grp)
    def _():
        n_tok = o_ref.shape[0]
        r0 = pl.multiple_of((j - n_grp) * (n_tok * TOK_TILE), n_tok * TOK_TILE)
        o_ref[...] = _row_major_f32(acc_s, r0, n_tok, _tile_perm())


def _moe_comb(idx, y, B, S, cap, chunk=1024, eg=4):
    E = y.shape[1]
    D = D_MODEL
    nch = S // chunk
    ng = E // eg
    return pl.pallas_call(
        functools.partial(_moe_comb_kernel, cap=cap, n_grp=ng),
        grid=(B, ng + nch),
        in_specs=[
            pl.BlockSpec((eg, None, 1, cap), lambda b, j: (jnp.minimum(j, ng - 1), b, 0, 0),
                         memory_space=pltpu.SMEM),
            pl.BlockSpec((None, eg, cap * TOK_TILE, LANES), lambda b, j: (b, jnp.minimum(j, ng - 1), 0, 0)),
        ],
        out_specs=pl.BlockSpec((chunk, D), lambda b, j: (b * nch + jnp.maximum(j - ng, 0), 0)),
        out_shape=jax.ShapeDtypeStruct((B * S, D), F32),
        scratch_shapes=[pltpu.VMEM((S * TOK_TILE, LANES), F32)],
        compiler_params=_cparams(("arbitrary", "arbitrary")),
        name="moe_comb",
    )(idx, y)


def _final_kernel(x1_ref, x1b_ref, moe_ref, p_ref, wpg_ref, bpg_ref, wpp_ref, g_ref, b_ref, o_ref):
    for r0 in range(0, x1_ref.shape[0], ROW_CHAIN):
        rows = slice(r0, r0 + ROW_CHAIN)
        gate = _sigmoid(jnp.dot(x1b_ref[rows, :], wpg_ref[...], preferred_element_type=F32) + bpg_ref[...])
        plv = gate * jnp.dot(p_ref[rows, :].astype(BF16), wpp_ref[...], preferred_element_type=F32)
        o_ref[rows, :] = _layer_norm(ALPHA * x1_ref[rows, :] + moe_ref[rows, :] + plv, g_ref[...], b_ref[...])


def _final(x1, x1b, moe, p2, wpg, bpg, wpp, g, b, tm=512):
    T, D = x1.shape
    return pl.pallas_call(
        _final_kernel,
        grid=(T // tm,),
        in_specs=[
            pl.BlockSpec((tm, D), lambda i: (i, 0)),
            pl.BlockSpec((tm, D), lambda i: (i, 0)),
            pl.BlockSpec((tm, D), lambda i: (i, 0)),
            pl.BlockSpec((tm, P_DIM), lambda i: (i, 0)),
            _resident((D, D), lambda i: (0, 0)),
            _resident((1, D), lambda i: (0, 0)),
            _resident((P_DIM, D), lambda i: (0, 0)),
            _resident((1, D), lambda i: (0, 0)),
            _resident((1, D), lambda i: (0, 0)),
        ],
        out_specs=pl.BlockSpec((tm, D), lambda i: (i, 0)),
        out_shape=jax.ShapeDtypeStruct((T, D), F32),
        compiler_params=_cparams(("parallel",)),
        name="final",
    )(x1, x1b, moe, p2, wpg, bpg, wpp, g, b)


def _rope_tables(S):
    rows = S // GRID_W
    row_idx = jnp.broadcast_to(jnp.arange(rows, dtype=F32)[:, None], (rows, GRID_W)).reshape(-1)
    col_idx = jnp.broadcast_to(jnp.arange(GRID_W, dtype=F32)[None, :], (rows, GRID_W)).reshape(-1)
    inv_freq = ROPE_THETA ** (-jnp.arange(ROPE_FREQS, dtype=F32) / ROPE_FREQS)
    ar = row_idx[:, None] * inv_freq
    ac = col_idx[:, None] * inv_freq
    cos_t = jnp.concatenate([jnp.cos(ar), jnp.cos(ar), jnp.cos(ac), jnp.cos(ac)], axis=1)
    sin_t = jnp.concatenate([-jnp.sin(ar), jnp.sin(ar), -jnp.sin(ac), jnp.sin(ac)], axis=1)
    return cos_t, sin_t


def _pair_major(a):
    return a.reshape(2, HEAD_PAIRS, 2).transpose(1, 0, 2)


def _layer(x2, p2, B, S, w_in, conv_w, b_i, b_f, g_mlstm, g_q, g_k, w_out, ln1_g, ln1_b, w_router, w_gate, w_up,
           w_down, w_pl_proj, w_pl_gate, b_pl_gate, ln2_g, ln2_b):
    D = D_MODEL
    cap = CAPACITY_FACTOR * S // N_EXPERTS
    o_mg = 2 * MQ_COLS + 2 * MV_COLS
    w_main = jnp.concatenate([w_in[:, :o_mg], w_in[:, o_mg + MG_COLS:]], axis=1).astype(BF16)
    w_g = w_in[:, o_mg:o_mg + MG_COLS].reshape(D, 2, 2, HEAD_PAIRS, 2).transpose(0, 3, 2, 1, 4).reshape(D, MG_COLS)
    bias = jnp.stack([_pair_major(b_i), _pair_major(b_f)], axis=1).reshape(HEAD_PAIRS, GATES_PER_PAIR)

    proj, grow = _proj(x2, w_main, w_g.astype(BF16).T)
    h_m = _mlstm(proj, grow, conv_w, bias[:, :, None], g_mlstm.reshape(HEAD_PAIRS, 2, MLSTM_DV), B, S)
    cos_t, sin_t = _rope_tables(S)
    h_a = _attn(proj, cos_t, sin_t, g_q[None, :], g_k[None, :], B, S)
    w_o = w_out.astype(BF16)
    x1, x1b, xg, aff4 = _outproj(h_m, h_a, x2, w_o, w_o, ln1_g[None, :], ln1_b[None, :],
                                 w_router.T, S)
    pos4, idx_digits = _topk(aff4, B, S, cap)
    idx =(idx_digits[:, :, 0:1, :] * TOK_TILE + idx_digits[:, :, 1:2, :]).astype(jnp.int32)
    y = _moe_ffn(idx, xg, pos4, aff4, w_gate, w_up, w_down, B, S, cap)
    moe = _moe_comb(idx, y, B, S, cap)
    return _final(x1, x1b, moe, p2, w_pl_gate.astype(BF16), b_pl_gate[None, :], w_pl_proj.astype(BF16),
                  ln2_g[None, :], ln2_b[None, :])


def kernel(x, p, w_in, conv_w, b_igate, b_fgate, g_mlstm, g_q, g_k, w_out, ln1_g, ln1_b, w_router, w_gate, w_up,
           w_down, w_pl_proj, w_pl_gate, b_pl_gate, ln2_g, ln2_b):
    B, S, D = x.shape
    x2 = x.reshape(B * S, D)
    for i in range(DEPTH):
        x2 = _layer(x2, p[i].reshape(B * S, P_DIM), B, S, w_in[i], conv_w[i], b_igate[i], b_fgate[i], g_mlstm[i],
                    g_q[i], g_k[i], w_out[i], ln1_g[i], ln1_b[i], w_router[i], w_gate[i], w_up[i], w_down[i],
                    w_pl_proj[i], w_pl_gate[i], b_pl_gate[i], ln2_g[i], ln2_b[i])
    return x2.reshape(B, S, D)
```

```python
import functools

import jax
import jax.numpy as jnp
from jax import lax
from jax.experimental import pallas as pl
from jax.experimental.pallas import tpu as pltpu

F32 = jnp.float32
BF16 = jnp.bfloat16

D_MODEL = 2048
P_DIM = 256
GRID_W = 64
MLSTM_WIDTH = D_MODEL // 2
ATTN_WIDTH = D_MODEL - MLSTM_WIDTH
MLSTM_HEADS = 8
MLSTM_DV = MLSTM_WIDTH // MLSTM_HEADS
MLSTM_DQK = MLSTM_DV // 2
CONV_W = 5
ATTN_HEAD_DIM = 128
ATTN_Q_HEADS = ATTN_WIDTH // ATTN_HEAD_DIM
ATTN_KV_HEADS = 2
ATTN_GROUP = ATTN_Q_HEADS // ATTN_KV_HEADS
ROPE_FREQS = ATTN_HEAD_DIM // 4
ROPE_THETA = 10000.0
N_EXPERTS = 16
EXPERT_FF = D_MODEL // 2
CAPACITY_FACTOR = 2
NORM_EPS = 1e-6
DEPTH = 1
ALPHA = (2.0 * DEPTH) ** 0.25
LOG2E = 1.4426950408889634

MQ_COLS = MLSTM_HEADS * MLSTM_DQK
MV_COLS = MLSTM_WIDTH
MG_COLS = 2 * 2 * MLSTM_HEADS
AQ_COLS = ATTN_WIDTH
AKV_COLS = ATTN_KV_HEADS * ATTN_HEAD_DIM
PROJ_COLS = 2 * MQ_COLS + 2 * MV_COLS + AQ_COLS + 2 * AKV_COLS
OFF_MQ, OFF_MK, OFF_MV, OFF_MO = 0, MQ_COLS, 2 * MQ_COLS, 2 * MQ_COLS + MV_COLS
OFF_AQ = OFF_MO + MV_COLS
OFF_AK = OFF_AQ + AQ_COLS
OFF_AV = OFF_AK + AKV_COLS

HEAD_PAIRS = MLSTM_HEADS // 2
GATES_PER_PAIR = 8
CHAINS = 4
MLSTM_LC = 128
STATE_UNROLL = 4
OUT_UNROLL = 8
ATTN_CHAIN_ROWS = 256
ROW_CHAIN = 256
LANES = 128
SUBLANES = 8
TOK_TILE = 16
COL_BLOCKS = D_MODEL // LANES
SCATTER_BATCH = 16
V7X_VMEM_LIMIT = 56 * 1024 * 1024
TOPK_REFINE_STEPS = 8


def _cparams(sem, vmem=V7X_VMEM_LIMIT):
    return pltpu.CompilerParams(dimension_semantics=sem, vmem_limit_bytes=vmem)


def _resident(shape, index_map):
    return pl.BlockSpec(shape, index_map, pipeline_mode=pl.Buffered(1))


def _sigmoid(x):
    return 0.5 * jnp.tanh(0.5 * x) + 0.5


def _log_sigmoid(x):
    return jnp.minimum(x, 0.0) - jnp.log1p(jnp.exp(-jnp.abs(x)))


def _proj_kernel(x_ref, w_ref, wgt_ref, o_ref, gr_ref, xb_ref):
    @pl.when(pl.program_id(1) == 0)
    def _():
        xb = x_ref[...].astype(BF16)
        xb_ref[...] = xb
        gr = lax.dot_general(wgt_ref[...], xb, (((1,), (1,)), ((), ())),
                             preferred_element_type=F32)
        for p in range(HEAD_PAIRS):
            gr_ref[p] = gr[GATES_PER_PAIR * p:GATES_PER_PAIR * (p + 1), :]

    o_ref[...] = jnp.dot(xb_ref[...], w_ref[...], preferred_element_type=F32).astype(o_ref.dtype)


def _proj(x2, w, wgt, tm=1024, tn=1536):
    T, D = x2.shape
    N = w.shape[1]
    return pl.pallas_call(
        _proj_kernel,
        grid=(T // tm, N // tn),
        in_specs=[
            pl.BlockSpec((tm, D), lambda i, j: (i, 0)),
            pl.BlockSpec((D, tn), lambda i, j: (0, j)),
            _resident((MG_COLS, D), lambda i, j: (0, 0)),
        ],
        out_specs=[
            pl.BlockSpec((tm, tn), lambda i, j: (i, j)),
            pl.BlockSpec((HEAD_PAIRS, GATES_PER_PAIR, tm), lambda i, j: (0, 0, i)),
        ],
        out_shape=[
            jax.ShapeDtypeStruct((T, N), BF16),
            jax.ShapeDtypeStruct((HEAD_PAIRS, GATES_PER_PAIR, T), F32),
        ],
        scratch_shapes=[pltpu.VMEM((tm, D), BF16)],
        compiler_params=_cparams(("parallel", "arbitrary")),
        name="proj",
    )(x2, w, wgt)


def _conv_silu(x, w, pad_s):
    S, C = x.shape
    half = CONV_W // 2
    halo = jnp.zeros((SUBLANES, C), F32)
    pad_s[0:SUBLANES, :] = halo
    pad_s[SUBLANES + S:2 * SUBLANES + S, :] = halo
    pad_s[SUBLANES:SUBLANES + S, :] = x
    acc = x * w[half:half + 1, :]
    for j in range(CONV_W):
        if j != half:
            acc = acc + pad_s[SUBLANES + j - half:SUBLANES + j - half + S, :] * w[j:j + 1, :]
    return acc * _sigmoid(acc)


def _mlstm_kernel(q_ref, k_ref, v_ref, o_ref, gr_ref, cwq_ref, cwk_ref, br_ref, gh_ref, out_ref,
                  q0_s, q1_s, kt_s, col_s, winter_s, wl_s, dec_s, sc_s, cst_s, call_s, pad_s, acol_s, rmat_s):
    S = q_ref.shape[0]
    L = MLSTM_LC
    NC = S // L
    DQ = MLSTM_DQK
    DV = MLSTM_DV
    hi = lax.Precision.HIGHEST
    neg = -jnp.inf

    g8 = gr_ref[...] + br_ref[...]
    li8 = g8 * LOG2E
    lf8 = pltpu.roll(_log_sigmoid(g8) * LOG2E, CHAINS, 0)
    row = lax.broadcasted_iota(jnp.int32, (GATES_PER_PAIR, L), 0)
    lane = lax.broadcasted_iota(jnp.int32, (GATES_PER_PAIR, L), 1)
    fwd = (row % CHAINS) < 2
    fwd1 = fwd[:, :1]
    si = lax.broadcasted_iota(jnp.int32, (L, 2 * L), 0)
    ti = lax.broadcasted_iota(jnp.int32, (L, 2 * L), 1)
    tri = jnp.where(ti < L, jnp.where(si <= ti, 1.0, 0.0), jnp.where(si >= ti - L, 1.0, 0.0))
    tot, mloc, b_l, cm_l, r_l = [], [], [], [], []
    for c in range(NC):
        pr = jnp.dot(lf8[:, c * L:(c + 1) * L], tri, precision=hi, preferred_element_type=F32)
        b_c = jnp.where(fwd, pr[:, :L], pr[:, L:])
        tot_c = pr[:, L - 1:L]
        li_c = li8[:, c * L:(c + 1) * L]
        r_c = li_c - b_c
        cm = r_c
        k = 1
        while k < L:
            pre = jnp.where(lane >= k, pltpu.roll(cm, k, 1), neg)
            suf = jnp.where(lane < L - k, pltpu.roll(cm, L - k, 1), neg)
            cm = jnp.maximum(cm, jnp.where(fwd, pre, suf))
            k *= 2
        g_c = tot_c - b_c + li_c
        mloc_c = jnp.max(g_c, axis=1, keepdims=True)
        wl_s[c] = jnp.exp2(g_c - mloc_c)
        r_l.append(r_c)
        tot.append(tot_c)
        mloc.append(mloc_c)
        b_l.append(b_c)
        cm_l.append(cm)

    def scan(order):
        m = jnp.zeros((GATES_PER_PAIR, 1), F32)
        m_in, dec, sc = [None] * NC, [None] * NC, [None] * NC
        for c in order:
            m_new = jnp.maximum(tot[c] + m, mloc[c])
            m_in[c] = m
            dec[c] = jnp.exp2(tot[c] + m - m_new)
            sc[c] = jnp.exp2(mloc[c] - m_new)
            m = m_new
        return m_in, dec, sc

    mf, df, sf = scan(range(NC))
    mb, db, sb = scan(range(NC - 1, -1, -1))
    def terms3(x):
        t0 = x.astype(BF16).astype(F32)
        rem = x - t0
        t1 = rem.astype(BF16).astype(F32)
        return t0, t1, (rem - t1).astype(BF16).astype(F32)

    blk_rows = 2 * SUBLANES
    rowi = lax.broadcasted_iota(jnp.int32, (blk_rows, L), 0)
    bro = lambda x, ci: jnp.broadcast_to(x[ci:ci + 1, :], (blk_rows, L))
    zblk = jnp.zeros((blk_rows, L), F32)
    pad = jnp.zeros((LANES - GATES_PER_PAIR, L), F32)
    for c in range(NC):
        m_in = jnp.where(fwd1, mf[c], mb[c])
        dec_s[c] = jnp.broadcast_to(jnp.where(fwd1, df[c], db[c]), (GATES_PER_PAIR, L))
        sc_s[c] = jnp.broadcast_to(jnp.where(fwd1, sf[c], sb[c]), (GATES_PER_PAIR, L))
        a_c = jnp.maximum(m_in, cm_l[c])
        winter_s[c] = jnp.exp2(m_in - a_c)
        emt_c = jnp.exp2(-(b_l[c] + a_c))
        col_s[c * L:(c + 1) * L, :] = jnp.concatenate([emt_c, pad], axis=0).T
        a0, a1, a2 = terms3(a_c)
        r0, r1, r2 = terms3(r_l[c])
        a_blks, r_cols = [], []
        for ci in range(CHAINS):
            a_blks.append(jnp.where(rowi == 0, bro(a0, ci), jnp.where(rowi == 1, bro(a1, ci), jnp.where(
                rowi == 2, bro(a2, ci), jnp.where(rowi < 6, 1.0, 0.0)))))
            r_blk = jnp.where(rowi < 3, -1.0, jnp.where(rowi == 3, bro(r0, ci), jnp.where(
                rowi == 4, bro(r1, ci), jnp.where(rowi == 5, bro(r2, ci), 0.0))))
            r_cols.append(jnp.concatenate([zblk] * ci + [r_blk] + [zblk] * (LANES // blk_rows - 1 - ci), axis=0))
        a_rows = jnp.concatenate(a_blks + [zblk] * (LANES // blk_rows - CHAINS), axis=0)
        acol_s[c * L:(c + 1) * L, :] = a_rows.T.astype(BF16)
        rmat_s[c] = jnp.concatenate(r_cols, axis=1).astype(BF16)

    qs = (_conv_silu(q_ref[...].astype(F32), cwq_ref[...], pad_s) * (DQ ** -0.5)).astype(BF16)
    q0_s[...] = qs[:, :DQ]
    q1_s[...] = qs[:, DQ:]
    kt = _conv_silu(k_ref[...].astype(F32), cwk_ref[...], pad_s).T
    for c in range(NC):
        kt_s[c] = kt[:, c * L:(c + 1) * L]

    ones_blk = jnp.ones((L, DV), BF16)

    def vext_of(sl, hh):
        return jnp.concatenate([v_ref[sl, DV * hh:DV * (hh + 1)], ones_blk], axis=1)

    cst_s[...] = jnp.zeros(cst_s.shape, F32)

    def state_step(c):
        for d in range(2):
            ch = c if d == 0 else NC - 1 - c
            sl = pl.ds(pl.multiple_of(ch * L, L), L)
            wl = wl_s[ch]
            ktc = kt_s[ch]
            decs = dec_s[ch]
            scs = sc_s[ch]
            for hh in range(2):
                ci = d * 2 + hh
                cx = cst_s[ci]
                call_s[ch, ci] = cx.astype(BF16)
                kw = (ktc[DQ * hh:DQ * (hh + 1), :] * wl[ci:ci + 1, :]).astype(BF16)
                cst_s[ci] = (decs[ci:ci + 1, 0:1] * cx
                             + scs[ci:ci + 1, 0:1] * jnp.dot(kw, vext_of(sl, hh), preferred_element_type=F32))

    def state_body(c, carry):
        for u in range(STATE_UNROLL):
            state_step(STATE_UNROLL * c + u)
        return carry

    lax.fori_loop(0, NC // STATE_UNROLL, state_body, 0)

    tt = lax.broadcasted_iota(jnp.int32, (L, L), 0)
    ss = lax.broadcasted_iota(jnp.int32, (L, L), 1)
    masks = (ss <= tt, ss >= tt)

    eye = tt == ss

    def chunk_out(ch, hh, sl, cols, winter, e_all):
        og = _sigmoid(o_ref[sl, DV * hh:DV * (hh + 1)].astype(F32))
        qc = (q0_s if hh == 0 else q1_s)[sl, :]
        kth = kt_s[ch, DQ * hh:DQ * (hh + 1), :].astype(BF16)
        qk = jnp.dot(qc, kth, preferred_element_type=F32)
        vext = vext_of(sl, hh)
        hsum = None
        for d in range(2):
            ci = d * 2 + hh
            w_intra = jnp.exp2(jnp.where(masks[d], e_all[:, L * ci:L * (ci + 1)], neg))
            qcx = jnp.dot(qc, call_s[ch, ci], preferred_element_type=F32).astype(BF16)
            w_diag = jnp.where(eye, winter[ci:ci + 1, :], 0.0).astype(BF16)
            nd = jnp.dot(jnp.concatenate([(qk * w_intra).astype(BF16), w_diag], axis=1),
                         jnp.concatenate([vext, qcx], axis=0),
                         preferred_element_type=F32)
            h = nd[:, :DV] / jnp.maximum(jnp.abs(nd[:, DV:]), cols[:, ci:ci + 1])
            hsum = h if hsum is None else hsum + h
        y = hsum * lax.rsqrt(jnp.mean(hsum * hsum, axis=1, keepdims=True) + NORM_EPS) * gh_ref[hh:hh + 1, :]
        out_ref[sl, DV * hh:DV * (hh + 1)] = (og * y).astype(out_ref.dtype)

    def out_chunk(c):
        sl = pl.ds(pl.multiple_of(c * L, L), L)
        e_all = jnp.dot(acol_s[sl, :], rmat_s[c], preferred_element_type=F32)
        cols = col_s[sl, :]
        winter = winter_s[c]
        chunk_out(c, 0, sl, cols, winter, e_all)
        chunk_out(c, 1, sl, cols, winter, e_all)

    def out_body(c, carry):
        for u in range(OUT_UNROLL):
            out_chunk(OUT_UNROLL * c + u)
        return carry

    lax.fori_loop(0, NC // OUT_UNROLL, out_body, 0)


def _mlstm(proj, grow, conv_w, bias_r, g_head, B, S):
    T = B * S
    NC = S // MLSTM_LC
    pw = 2 * MLSTM_DQK
    vw = 2 * MLSTM_DV
    return pl.pallas_call(
        _mlstm_kernel,
        grid=(B, HEAD_PAIRS),
        in_specs=[
            pl.BlockSpec((S, pw), lambda b, p: (b, OFF_MQ // pw + p)),
            pl.BlockSpec((S, pw), lambda b, p: (b, OFF_MK // pw + p)),
            pl.BlockSpec((S, vw), lambda b, p: (b, OFF_MV // vw + p)),
            pl.BlockSpec((S, vw), lambda b, p: (b, OFF_MO // vw + p)),
            pl.BlockSpec((None, GATES_PER_PAIR, S), lambda b, p: (p, 0, b)),
            pl.BlockSpec((CONV_W, pw), lambda b, p: (0, p)),
            pl.BlockSpec((CONV_W, pw), lambda b, p: (0, MQ_COLS // pw + p)),
            pl.BlockSpec((None, GATES_PER_PAIR, 1), lambda b, p: (p, 0, 0)),
            pl.BlockSpec((None, 2, MLSTM_DV), lambda b, p: (p, 0, 0)),
        ],
        out_specs=pl.BlockSpec((S, vw), lambda b, p: (b, p)),
        out_shape=jax.ShapeDtypeStruct((T, MLSTM_WIDTH), BF16),
        scratch_shapes=[
            pltpu.VMEM((S, MLSTM_DQK), BF16),
            pltpu.VMEM((S, MLSTM_DQK), BF16),
            pltpu.VMEM((NC, pw, MLSTM_LC), F32),
            pltpu.VMEM((S, LANES), F32),
            pltpu.VMEM((NC, GATES_PER_PAIR, MLSTM_LC), F32),
            pltpu.VMEM((NC, GATES_PER_PAIR, MLSTM_LC), F32),
            pltpu.VMEM((NC, GATES_PER_PAIR, MLSTM_LC), F32),
            pltpu.VMEM((NC, GATES_PER_PAIR, MLSTM_LC), F32),
            pltpu.VMEM((CHAINS, MLSTM_DQK, 2 * MLSTM_DV), F32),
            pltpu.VMEM((NC, CHAINS, MLSTM_DQK, 2 * MLSTM_DV), BF16),
            pltpu.VMEM((S + 2 * SUBLANES, pw), F32),
            pltpu.VMEM((S, LANES), BF16),
            pltpu.VMEM((NC, LANES, CHAINS * MLSTM_LC), BF16),
        ],
        compiler_params=_cparams(("parallel", "parallel")),
        name="mlstm",
    )(proj, proj, proj, proj, grow, conv_w, conv_w, bias_r, g_head)


def _norm_rope(x, g, cos, sin_signed):
    xn = x * lax.rsqrt(jnp.mean(x * x, axis=1, keepdims=True) + NORM_EPS) * g
    lane = lax.broadcasted_iota(jnp.int32, x.shape, 1)
    first_half = (lane % (2 * ROPE_FREQS)) < ROPE_FREQS
    partner = jnp.where(first_half,
                        pltpu.roll(xn, LANES - ROPE_FREQS, 1),
                        pltpu.roll(xn, ROPE_FREQS, 1))
    return xn * cos + partner * sin_signed


def _attn_kernel(q_ref, k_ref, v_ref, cq_ref, sq_ref, ck_ref, sk_ref, gq_ref, gk_ref, o_ref, kr_s, vx_s):
    d = ATTN_HEAD_DIM

    @pl.when(pl.program_id(2) == 0)
    def _():
        kr_s[...] = _norm_rope(k_ref[...].astype(F32), gk_ref[...], ck_ref[...], sk_ref[...]).astype(BF16)
        vx_s[...] = jnp.concatenate([v_ref[...], jnp.ones(v_ref.shape, BF16)], axis=1)

    gq = gq_ref[...]
    kr = kr_s[...]
    vx = vx_s[...]
    for r0 in range(0, q_ref.shape[0], ATTN_CHAIN_ROWS):
        rows = slice(r0, r0 + ATTN_CHAIN_ROWS)
        cq, sq = cq_ref[rows, :], sq_ref[rows, :]
        for g in range(ATTN_GROUP):
            cols = slice(d * g, d * (g + 1))
            qg = (_norm_rope(q_ref[rows, cols].astype(F32), gq, cq, sq) * (d ** -0.5 * LOG2E)).astype(BF16)
            s = lax.dot_general(qg, kr, (((1,), (1,)), ((), ())), preferred_element_type=F32)
            p = jnp.exp2(s - jnp.max(s, axis=1, keepdims=True))
            ov = jnp.dot(p.astype(BF16), vx, preferred_element_type=F32)
            o_ref[rows, cols] = (ov[:, :d] / ov[:, d:]).astype(o_ref.dtype)


def _attn(proj, cos_t, sin_t, g_q, g_k, B, S, tq=1024):
    T = B * S
    d = ATTN_HEAD_DIM
    gw = ATTN_GROUP * d
    nq = S // tq
    return pl.pallas_call(
        _attn_kernel,
        grid=(B, ATTN_KV_HEADS, nq),
        in_specs=[
            pl.BlockSpec((tq, gw), lambda b, kv, qi: (b * nq + qi, OFF_AQ // gw + kv)),
            pl.BlockSpec((S, d), lambda b, kv, qi: (b, OFF_AK // d + kv)),
            pl.BlockSpec((S, d), lambda b, kv, qi: (b, OFF_AV // d + kv)),
            pl.BlockSpec((tq, d), lambda b, kv, qi: (qi, 0)),
            pl.BlockSpec((tq, d), lambda b, kv, qi: (qi, 0)),
            _resident((S, d), lambda b, kv, qi: (0, 0)),
            _resident((S, d), lambda b, kv, qi: (0, 0)),
            _resident((1, d), lambda b, kv, qi: (0, 0)),
            _resident((1, d), lambda b, kv, qi: (0, 0)),
        ],
        out_specs=pl.BlockSpec((tq, gw), lambda b, kv, qi: (b * nq + qi, kv)),
        out_shape=jax.ShapeDtypeStruct((T, ATTN_WIDTH), BF16),
        scratch_shapes=[pltpu.VMEM((S, d), BF16), pltpu.VMEM((S, 2 * d), BF16)],
        compiler_params=_cparams(("parallel", "parallel", "arbitrary")),
        name="attn",
    )(proj, proj, proj, cos_t, sin_t, cos_t, sin_t, g_q, g_k)


def _layer_norm(y, g, b):
    mu = jnp.mean(y, axis=1, keepdims=True)
    yc = y - mu
    var = jnp.mean(yc * yc, axis=1, keepdims=True)
    return yc * lax.rsqrt(var + NORM_EPS) * g + b


def _split_bf16(x):
    hi = x.astype(BF16)
    return hi, (x - hi.astype(F32)).astype(BF16)


def _tile_perm():
    n = TOK_TILE * COL_BLOCKS
    a = lax.broadcasted_iota(jnp.int32, (n, n), 0)
    b = lax.broadcasted_iota(jnp.int32, (n, n), 1)
    return jnp.where((a >> 4) == (b & 15), jnp.where((a & 15) == (b >> 4), 1.0, 0.0), 0.0).astype(BF16)


def _token_major(xh, perm):
    out = []
    for r0 in range(0, xh.shape[0], 2 * TOK_TILE):
        halves = [jnp.concatenate([xh[r0 + h * TOK_TILE:r0 + (h + 1) * TOK_TILE, LANES * j:LANES * (j + 1)]
                                   for j in range(COL_BLOCKS)], axis=0) for h in range(2)]
        o2 = jnp.dot(perm, jnp.concatenate(halves, axis=1), preferred_element_type=F32).astype(BF16)
        out += [o2[:, :LANES], o2[:, LANES:]]
    return jnp.concatenate(out, axis=0)


def _row_major(g_ref, perm):
    n = TOK_TILE * COL_BLOCKS
    cols = [[] for _ in range(COL_BLOCKS)]
    for r0 in range(0, g_ref.shape[0], 2 * n):
        m2 = jnp.concatenate([g_ref[r0:r0 + n, :], g_ref[r0 + n:r0 + 2 * n, :]], axis=1)
        o2 = jnp.dot(perm, m2, preferred_element_type=F32).astype(BF16)
        for j in range(COL_BLOCKS):
            blk = o2[TOK_TILE * j:TOK_TILE * (j + 1), :]
            cols[j] += [blk[:, :LANES], blk[:, LANES:]]
    return jnp.concatenate([jnp.concatenate(cj, axis=0) for cj in cols], axis=1)


def _outproj_kernel(hm_ref, ha_ref, x_ref, wt_ref, wb_ref, g_ref, b_ref, wr_ref, x1_ref, xg_ref, aff_ref):
    y = (ALPHA * x_ref[...]
         + jnp.dot(hm_ref[...], wt_ref[...], preferred_element_type=F32)
         + jnp.dot(ha_ref[...], wb_ref[...], preferred_element_type=F32))
    x1 = _layer_norm(y, g_ref[...], b_ref[...])
    x1_ref[...] = x1
    xh, xl = _split_bf16(x1)
    wh, wl = _split_bf16(wr_ref[...])
    xg_ref[...] = _token_major(xh, _tile_perm())
    nt = (((1,), (1,)), ((), ()))
    logits = (lax.dot_general(wh, xh, nt, preferred_element_type=F32)
              + lax.dot_general(wh, xl, nt, preferred_element_type=F32)
              + lax.dot_general(wl, xh, nt, preferred_element_type=F32))
    e = jnp.exp(logits - jnp.max(logits, axis=0, keepdims=True))
    aff_ref[:, 0, :] = e / jnp.sum(e, axis=0, keepdims=True)


def _outproj(hm, ha, x2, wt, wb, g, b, wr, S, tm=512):
    T, D = x2.shape
    nt = S // tm
    return pl.pallas_call(
        _outproj_kernel,
        grid=(T // tm,),
        in_specs=[
            pl.BlockSpec((tm, MLSTM_WIDTH), lambda i: (i, 0)),
            pl.BlockSpec((tm, ATTN_WIDTH), lambda i: (i, 0)),
            pl.BlockSpec((tm, D), lambda i: (i, 0)),
            _resident((MLSTM_WIDTH, D), lambda i: (0, 0)),
            _resident((ATTN_WIDTH, D), lambda i: (MLSTM_WIDTH // ATTN_WIDTH, 0)),
            _resident((1, D), lambda i: (0, 0)),
            _resident((1, D), lambda i: (0, 0)),
            _resident((N_EXPERTS, D), lambda i: (0, 0)),
        ],
        out_specs=[
            pl.BlockSpec((tm, D), lambda i: (i, 0)),
            pl.BlockSpec((tm * TOK_TILE, LANES), lambda i: (i, 0)),
            pl.BlockSpec((N_EXPERTS, None, 1, tm), lambda i: (0, i // nt, 0, i % nt)),
        ],
        out_shape=[
            jax.ShapeDtypeStruct((T, D), F32),
            jax.ShapeDtypeStruct((T * TOK_TILE, LANES), BF16),
            jax.ShapeDtypeStruct((N_EXPERTS, T // S, 1, S), F32),
        ],
        compiler_params=_cparams(("parallel",)),
        name="outproj",
    )(hm, ha, x2, wt, wb, g, b, wr)


def _excl_prefix_count(m):
    E, S = m.shape
    nb = S // LANES
    s_i = lax.broadcasted_iota(jnp.int32, (LANES, 2 * LANES), 0)
    t_i = lax.broadcasted_iota(jnp.int32, (LANES, 2 * LANES), 1)
    tri_ext = jnp.where(t_i >= LANES, 1.0, jnp.where(s_i < t_i, 1.0, 0.0)).astype(BF16)
    r_i = lax.broadcasted_iota(jnp.int32, (nb * E, nb * E), 0)
    c_i = lax.broadcasted_iota(jnp.int32, (nb * E, nb * E), 1)
    earlier = jnp.where(c_i // E < r_i // E, jnp.where(c_i % E == r_i % E, 1.0, 0.0), 0.0).astype(BF16)
    stacked = jnp.concatenate([m[:, LANES * j:LANES * (j + 1)] for j in range(nb)], axis=0)
    r = jnp.dot(stacked, tri_ext, preferred_element_type=F32)
    off = jnp.dot(earlier, r[:, LANES:].astype(BF16), preferred_element_type=F32)
    tot = r[:, :LANES] + off
    return jnp.concatenate([tot[E * j:E * (j + 1), :] for j in range(nb)], axis=1)


def _topk_kernel(aff_ref, pos_ref, idx_ref, *, cap):
    S = aff_ref.shape[2]
    a = aff_ref[:, 0, :]

    def count_ge(v):
        return jnp.sum(jnp.where(a >= v, 1.0, 0.0), axis=1, keepdims=True)

    thr = jnp.zeros((a.shape[0], 1), jnp.int32)
    for bit in range(30, -1, -1):
        cand = thr | (1 << bit)
        thr = jnp.where(count_ge(pltpu.bitcast(cand, F32)) >= cap, cand, thr)
    lo = pltpu.bitcast(thr, F32)
    hi = pltpu.bitcast(thr + 1, F32)
    for _ in range(TOPK_REFINE_STEPS):
        mid = 0.5 * (lo + hi)
        ok = count_ge(mid) >= cap
        lo = jnp.where(ok, mid, lo)
        hi = jnp.where(ok, hi, mid)
    gt = a > lo
    eq = a == lo
    need = cap - jnp.sum(jnp.where(gt, 1.0, 0.0), axis=1, keepdims=True)
    eq_rank = _excl_prefix_count(jnp.where(eq, 1.0, 0.0).astype(BF16))
    sel = jnp.logical_or(gt, jnp.logical_and(eq, eq_rank < need))
    pos = _excl_prefix_count(jnp.where(sel, 1.0, 0.0).astype(BF16))
    posm = jnp.where(sel, pos, -1.0)
    pos_ref[:, 0, :] = posm
    digit_row = lax.broadcasted_iota(jnp.int32, (2 * SUBLANES, S), 0)
    tok = lax.broadcasted_iota(jnp.int32, (2 * SUBLANES, S), 1)
    digits = jnp.where(digit_row == 0, tok >> 4, jnp.where(digit_row == 1, tok & 15, 0)).astype(F32).astype(BF16)
    slot = lax.broadcasted_iota(jnp.int32, (cap, S), 0).astype(F32)
    for e in range(a.shape[0]):
        hit = jnp.where(slot == posm[e:e + 1, :], 1.0, 0.0).astype(BF16)
        idx_ref[e] = lax.dot_general(digits, hit, (((1,), (1,)), ((), ())),
                                     preferred_element_type=F32)[:SUBLANES]


def _topk(aff, B, S, cap):
    E = aff.shape[0]
    return pl.pallas_call(
        functools.partial(_topk_kernel, cap=cap),
        grid=(B,),
        in_specs=[pl.BlockSpec((E, None, 1, S), lambda b: (0, b, 0, 0))],
        out_specs=[pl.BlockSpec((E, None, 1, S), lambda b: (0, b, 0, 0)),
                   pl.BlockSpec((E, None, SUBLANES, cap), lambda b: (0, b, 0, 0))],
        out_shape=[jax.ShapeDtypeStruct((E, B, 1, S), F32),
                   jax.ShapeDtypeStruct((E, B, SUBLANES, cap), F32)],
        compiler_params=_cparams(("parallel",)),
        name="topk",
    )(aff)


def _moe_ffn_kernel(idx_ref, xg_ref, pos_ref, aff_ref, wg_ref, wu_ref, wd_ref, y_ref, gbuf_s, wg_s, wu_s, wd_s,
                    *, cap, n_exp):
    e = pl.program_id(0)
    b = pl.program_id(1)

    @pl.when(e < n_exp)
    def _():
        slot = e % 2
        rg = pl.multiple_of(b * wg_ref.shape[0], wg_ref.shape[0])
        rd = pl.multiple_of(b * wd_ref.shape[0], wd_ref.shape[0])
        wg_s[slot, pl.ds(rg, wg_ref.shape[0]), :] = wg_ref[...].astype(BF16)
        wu_s[slot, pl.ds(rg, wu_ref.shape[0]), :] = wu_ref[...].astype(BF16)
        wd_s[slot, pl.ds(rd, wd_ref.shape[0]), :] = wd_ref[...].astype(BF16)

    @pl.when(e > 0)
    def _():
        S = pos_ref.shape[1]
        slot = (e - 1) % 2
        for c in range(cap):
            row = pl.multiple_of(idx_ref[0, c] * TOK_TILE, TOK_TILE)
            gbuf_s[TOK_TILE * c:TOK_TILE * (c + 1), :] = xg_ref[pl.ds(row, TOK_TILE), :]
        perm = _tile_perm()
        xs = _row_major(gbuf_s, perm)
        sl = lax.broadcasted_iota(jnp.int32, (cap, S), 0).astype(F32)
        gate = jnp.sum(jnp.where(sl == pos_ref[...], aff_ref[...], 0.0), axis=1, keepdims=True)
        hg = jnp.dot(xs, wg_s[slot], preferred_element_type=F32)
        hu = jnp.dot(xs, wu_s[slot], preferred_element_type=F32)
        hid = (hg * _sigmoid(hg) * hu).astype(BF16)
        y = (jnp.dot(hid, wd_s[slot], preferred_element_type=F32) * gate).astype(BF16)
        y_ref[...] = _token_major(y, perm)


def _moe_ffn(idx, xg, pos4, aff4, wg, wu, wd, B, S, cap):
    E, D, F = wg.shape
    prev = lambda e: jnp.maximum(e - 1, 0)
    stage = lambda e, b: (jnp.minimum(e, E - 1), jnp.where(e < E, b, B - 1), 0)
    return pl.pallas_call(
        functools.partial(_moe_ffn_kernel, cap=cap, n_exp=E),
        grid=(E + 1, B),
        in_specs=[
            pl.BlockSpec((None, None, 1, cap), lambda e, b: (prev(e), b, 0, 0), memory_space=pltpu.SMEM),
            pl.BlockSpec((S * TOK_TILE, LANES), lambda e, b: (b, 0)),
            pl.BlockSpec((None, None, 1, S), lambda e, b: (prev(e), b, 0, 0)),
            pl.BlockSpec((None, None, 1, S), lambda e, b: (prev(e), b, 0, 0)),
            pl.BlockSpec((None, D // B, F), stage),
            pl.BlockSpec((None, D // B, F), stage),
            pl.BlockSpec((None, F // B, D), stage),
        ],
        out_specs=pl.BlockSpec((None, None, cap * TOK_TILE, LANES),
                               lambda e, b: (jnp.where(e > 0, b, 0), prev(e), 0, 0)),
        out_shape=jax.ShapeDtypeStruct((B, E, cap * TOK_TILE, LANES), BF16),
        scratch_shapes=[pltpu.VMEM((cap * TOK_TILE, LANES), BF16),
                        pltpu.VMEM((2, D, F), BF16), pltpu.VMEM((2, D, F), BF16), pltpu.VMEM((2, F, D), BF16)],
        compiler_params=_cparams(("arbitrary", "arbitrary")),
        name="moe_ffn",
    )(idx, xg, pos4, aff4, wg, wu, wd)


def _row_major_f32(g_ref, r0, n_tok, perm):
    n = TOK_TILE * COL_BLOCKS
    cols = [[] for _ in range(COL_BLOCKS)]
    for k in range(0, n_tok * TOK_TILE, 2 * n):
        m2 = jnp.concatenate([g_ref[pl.ds(r0 + k, n), :], g_ref[pl.ds(r0 + k + n, n), :]], axis=1)
        hi = m2.astype(BF16)
        r1 = m2 - hi.astype(F32)
        mid = r1.astype(BF16)
        lo = (r1 - mid.astype(F32)).astype(BF16)
        o2 = (jnp.dot(perm, hi, preferred_element_type=F32) + jnp.dot(perm, mid, preferred_element_type=F32)
              + jnp.dot(perm, lo, preferred_element_type=F32))
        for j in range(COL_BLOCKS):
            blk = o2[TOK_TILE * j:TOK_TILE * (j + 1), :]
            cols[j] += [blk[:, :LANES], blk[:, LANES:]]
    return jnp.concatenate([jnp.concatenate(cj, axis=0) for cj in cols], axis=1)


def _moe_comb_kernel(idx_ref, y_ref, o_ref, acc_s, *, cap, n_grp):
    j = pl.program_id(1)

    @pl.when(j == 0)
    def _():
        acc_s[...] = jnp.zeros(acc_s.shape, F32)

    @pl.when(j < n_grp)
    def _():
        for g in range(idx_ref.shape[0]):
            for c0 in range(0, cap, SCATTER_BATCH):
                rows = [pl.multiple_of(idx_ref[g, 0, c0 + u] * TOK_TILE, TOK_TILE) for u in range(SCATTER_BATCH)]
                new = [acc_s[pl.ds(rows[u], TOK_TILE), :]
                       + y_ref[g, TOK_TILE * (c0 + u):TOK_TILE * (c0 + u + 1), :].astype(F32)
                       for u in range(SCATTER_BATCH)]
                for u in range(SCATTER_BATCH):
                    acc_s[pl.ds(rows[u], TOK_TILE), :] = new[u]

    @pl.when(j >= n_grp)
    def _():
        n_tok = o_ref.shape[0]
        r0 = pl.multiple_of((j - n_grp) * (n_tok * TOK_TILE), n_tok * TOK_TILE)
        o_ref[...] = _row_major_f32(acc_s, r0, n_tok, _tile_perm())


def _moe_comb(idx, y, B, S, cap, chunk=1024, eg=4):
    E = y.shape[1]
    D = D_MODEL
    nch = S // chunk
    ng = E // eg
    return pl.pallas_call(
        functools.partial(_moe_comb_kernel, cap=cap, n_grp=ng),
        grid=(B, ng + nch),
        in_specs=[
            pl.BlockSpec((eg, None, 1, cap), lambda b, j: (jnp.minimum(j, ng - 1), b, 0, 0),
                         memory_space=pltpu.SMEM),
            pl.BlockSpec((None, eg, cap * TOK_TILE, LANES), lambda b, j: (b, jnp.minimum(j, ng - 1), 0, 0)),
        ],
        out_specs=pl.BlockSpec((chunk, D), lambda b, j: (b * nch + jnp.maximum(j - ng, 0), 0)),
        out_shape=jax.ShapeDtypeStruct((B * S, D), F32),
        scratch_shapes=[pltpu.VMEM((S * TOK_TILE, LANES), F32)],
        compiler_params=_cparams(("arbitrary", "arbitrary")),
        name="moe_comb",
    )(idx, y)


def _final_kernel(x1_ref, moe_ref, p_ref, wpg_ref, bpg_ref, wpp_ref, g_ref, b_ref, o_ref):
    for r0 in range(0, x1_ref.shape[0], ROW_CHAIN):
        rows = slice(r0, r0 + ROW_CHAIN)
        x1 = x1_ref[rows, :]
        gate = _sigmoid(jnp.dot(x1.astype(BF16), wpg_ref[...], preferred_element_type=F32) + bpg_ref[...])
        plv = gate * jnp.dot(p_ref[rows, :].astype(BF16), wpp_ref[...], preferred_element_type=F32)
        o_ref[rows, :] = _layer_norm(ALPHA * x1 + moe_ref[rows, :] + plv, g_ref[...], b_ref[...])


def _final(x1, moe, p2, wpg, bpg, wpp, g, b, tm=512):
    T, D = x1.shape
    return pl.pallas_call(
        _final_kernel,
        grid=(T // tm,),
        in_specs=[
            pl.BlockSpec((tm, D), lambda i: (i, 0)),
            pl.BlockSpec((tm, D), lambda i: (i, 0)),
            pl.BlockSpec((tm, P_DIM), lambda i: (i, 0)),
            _resident((D, D), lambda i: (0, 0)),
            _resident((1, D), lambda i: (0, 0)),
            _resident((P_DIM, D), lambda i: (0, 0)),
            _resident((1, D), lambda i: (0, 0)),
            _resident((1, D), lambda i: (0, 0)),
        ],
        out_specs=pl.BlockSpec((tm, D), lambda i: (i, 0)),
        out_shape=jax.ShapeDtypeStruct((T, D), F32),
        compiler_params=_cparams(("parallel",)),
        name="final",
    )(x1, moe, p2, wpg, bpg, wpp, g, b)


def _rope_tables(S):
    rows = S // GRID_W
    row_idx = jnp.broadcast_to(jnp.arange(rows, dtype=F32)[:, None], (rows, GRID_W)).reshape(-1)
    col_idx = jnp.broadcast_to(jnp.arange(GRID_W, dtype=F32)[None, :], (rows, GRID_W)).reshape(-1)
    inv_freq = ROPE_THETA ** (-jnp.arange(ROPE_FREQS, dtype=F32) / ROPE_FREQS)
    ar = row_idx[:, None] * inv_freq
    ac = col_idx[:, None] * inv_freq
    cos_t = jnp.concatenate([jnp.cos(ar), jnp.cos(ar), jnp.cos(ac), jnp.cos(ac)], axis=1)
    sin_t = jnp.concatenate([-jnp.sin(ar), jnp.sin(ar), -jnp.sin(ac), jnp.sin(ac)], axis=1)
    return cos_t, sin_t


def _pair_major(a):
    return a.reshape(2, HEAD_PAIRS, 2).transpose(1, 0, 2)


def _layer(x2, p2, B, S, w_in, conv_w, b_i, b_f, g_mlstm, g_q, g_k, w_out, ln1_g, ln1_b, w_router, w_gate, w_up,
           w_down, w_pl_proj, w_pl_gate, b_pl_gate, ln2_g, ln2_b):
    D = D_MODEL
    cap = CAPACITY_FACTOR * S // N_EXPERTS
    o_mg = 2 * MQ_COLS + 2 * MV_COLS
    w_main = jnp.concatenate([w_in[:, :o_mg], w_in[:, o_mg + MG_COLS:]], axis=1).astype(BF16)
    w_g = w_in[:, o_mg:o_mg + MG_COLS].reshape(D, 2, 2, HEAD_PAIRS, 2).transpose(0, 3, 2, 1, 4).reshape(D, MG_COLS)
    bias = jnp.stack([_pair_major(b_i), _pair_major(b_f)], axis=1).reshape(HEAD_PAIRS, GATES_PER_PAIR)

    proj, grow = _proj(x2, w_main, w_g.astype(BF16).T)
    h_m = _mlstm(proj, grow, conv_w, bias[:, :, None], g_mlstm.reshape(HEAD_PAIRS, 2, MLSTM_DV), B, S)
    cos_t, sin_t = _rope_tables(S)
    h_a = _attn(proj, cos_t, sin_t, g_q[None, :], g_k[None, :], B, S)
    w_o = w_out.astype(BF16)
    x1, xg, aff4 = _outproj(h_m, h_a, x2, w_o, w_o, ln1_g[None, :], ln1_b[None, :], w_router.T, S)
    pos4, idx_digits = _topk(aff4, B, S, cap)
    idx = (idx_digits[:, :, 0:1, :] * TOK_TILE + idx_digits[:, :, 1:2, :]).astype(jnp.int32)
    y = _moe_ffn(idx, xg, pos4, aff4, w_gate, w_up, w_down, B, S, cap)
    moe = _moe_comb(idx, y, B, S, cap)
    return _final(x1, moe, p2, w_pl_gate.astype(BF16), b_pl_gate[None, :], w_pl_proj.astype(BF16),
                  ln2_g[None, :], ln2_b[None, :])


def kernel(x, p, w_in, conv_w, b_igate, b_fgate, g_mlstm, g_q, g_k, w_out, ln1_g, ln1_b, w_router, w_gate, w_up,
           w_down, w_pl_proj, w_pl_gate, b_pl_gate, ln2_g, ln2_b):
    B, S, D = x.shape
    x2 = x.reshape(B * S, D)
    for i in range(DEPTH):
        x2 = _layer(x2, p[i].reshape(B * S, P_DIM), B, S, w_in[i], conv_w[i], b_igate[i], b_fgate[i], g_mlstm[i],
                    g_q[i], g_k[i], w_out[i], ln1_g[i], ln1_b[i], w_router[i], w_gate[i], w_up[i], w_down[i],
                    w_pl_proj[i], w_pl_gate[i], b_pl_gate[i], ln2_g[i], ln2_b[i])
    return x2.reshape(B, S, D)
```

```python
import functools

import jax
import jax.numpy as jnp
from jax import lax
from jax.experimental import pallas as pl
from jax.experimental.pallas import tpu as pltpu

F32 = jnp.float32
BF16 = jnp.bfloat16

D_MODEL = 2048
P_DIM = 256
GRID_W = 64
MLSTM_WIDTH = D_MODEL // 2
ATTN_WIDTH = D_MODEL - MLSTM_WIDTH
MLSTM_HEADS = 8
MLSTM_DV = MLSTM_WIDTH // MLSTM_HEADS
MLSTM_DQK = MLSTM_DV // 2
CONV_W = 5
ATTN_HEAD_DIM = 128
ATTN_Q_HEADS = ATTN_WIDTH // ATTN_HEAD_DIM
ATTN_KV_HEADS = 2
ATTN_GROUP = ATTN_Q_HEADS // ATTN_KV_HEADS
ROPE_FREQS = ATTN_HEAD_DIM // 4
ROPE_THETA = 10000.0
N_EXPERTS = 16
EXPERT_FF = D_MODEL // 2
CAPACITY_FACTOR = 2
NORM_EPS = 1e-6
DEPTH = 1
ALPHA = (2.0 * DEPTH) ** 0.25
LOG2E = 1.4426950408889634

MQ_COLS = MLSTM_HEADS * MLSTM_DQK
MV_COLS = MLSTM_WIDTH
MG_COLS = 2 * 2 * MLSTM_HEADS
AQ_COLS = ATTN_WIDTH
AKV_COLS = ATTN_KV_HEADS * ATTN_HEAD_DIM
PROJ_COLS = 2 * MQ_COLS + 2 * MV_COLS + AQ_COLS + 2 * AKV_COLS
OFF_MQ, OFF_MK, OFF_MV, OFF_MO = 0, MQ_COLS, 2 * MQ_COLS, 2 * MQ_COLS + MV_COLS
OFF_AQ = OFF_MO + MV_COLS
OFF_AK = OFF_AQ + AQ_COLS
OFF_AV = OFF_AK + AKV_COLS
PROJ_COL_STEP = 1536

HEAD_PAIRS = MLSTM_HEADS // 2
GATES_PER_PAIR = 8
CHAINS = 4
MLSTM_LC = 128
STATE_UNROLL = 8
OUT_UNROLL = 8
ATTN_CHAIN_ROWS = 256
ROW_CHAIN = 256
LANES = 128
SUBLANES = 8
TOK_TILE = 16
COL_BLOCKS = D_MODEL // LANES
SCATTER_BATCH = 16
V7X_VMEM_LIMIT = 56 * 1024 * 1024
TOPK_REFINE_STEPS = 8


def _cparams(sem, vmem=V7X_VMEM_LIMIT):
    return pltpu.CompilerParams(dimension_semantics=sem, vmem_limit_bytes=vmem)


def _resident(shape, index_map):
    return pl.BlockSpec(shape, index_map, pipeline_mode=pl.Buffered(1))


def _sigmoid(x):
    return 0.5 * jnp.tanh(0.5 * x) + 0.5


def _log_sigmoid(x):
    return jnp.minimum(x, 0.0) - jnp.log1p(jnp.exp(-jnp.abs(x)))


def _proj_kernel(x_ref, w_ref, wgt_ref, o_ref, gr_ref):
    xb = x_ref[...].astype(BF16)
    gr = lax.dot_general(wgt_ref[...], xb, (((1,), (1,)), ((), ())), preferred_element_type=F32)
    for p in range(HEAD_PAIRS):
        gr_ref[p] = gr[GATES_PER_PAIR * p:GATES_PER_PAIR * (p + 1), :]
    tn = PROJ_COL_STEP
    for n0 in range(0, w_ref.shape[1], tn):
        o_ref[:, n0:n0 + tn] = jnp.dot(xb, w_ref[:, n0:n0 + tn], preferred_element_type=F32).astype(o_ref.dtype)


def _proj(x2, w, wgt, tm=512):
    T, D = x2.shape
    N = w.shape[1]
    return pl.pallas_call(
        _proj_kernel,
        grid=(T // tm,),
        in_specs=[
            pl.BlockSpec((tm, D), lambda i: (i, 0)),
            _resident((D, N), lambda i: (0, 0)),
            _resident((MG_COLS, D), lambda i: (0, 0)),
        ],
        out_specs=[
            pl.BlockSpec((tm, N), lambda i: (i, 0)),
            pl.BlockSpec((HEAD_PAIRS, GATES_PER_PAIR, tm), lambda i: (0, 0, i)),
        ],
        out_shape=[
            jax.ShapeDtypeStruct((T, N), BF16),
            jax.ShapeDtypeStruct((HEAD_PAIRS, GATES_PER_PAIR, T), F32),
        ],
        compiler_params=_cparams(("parallel",)),
        name="proj",
    )(x2, w, wgt)


def _conv_silu(x, w, pad_s):
    S, C = x.shape
    half = CONV_W // 2
    halo = jnp.zeros((SUBLANES, C), F32)
    pad_s[0:SUBLANES, :] = halo
    pad_s[SUBLANES + S:2 * SUBLANES + S, :] = halo
    pad_s[SUBLANES:SUBLANES + S, :] = x
    acc = x * w[half:half + 1, :]
    for j in range(CONV_W):
        if j != half:
            acc = acc + pad_s[SUBLANES + j - half:SUBLANES + j - half + S, :] * w[j:j + 1, :]
    return acc * _sigmoid(acc)


def _mlstm_kernel(q_ref, k_ref, v_ref, o_ref, gr_ref, cwq_ref, cwk_ref, br_ref, gh_ref, out_ref,
                  q0_s, q1_s, kt_s, col_s, winter_s, wl_s, dec_s, sc_s, cst_s, call_s, pad_s, acol_s, rmat_s):
    S = q_ref.shape[0]
    L = MLSTM_LC
    NC = S // L
    DQ = MLSTM_DQK
    DV = MLSTM_DV
    hi = lax.Precision.HIGHEST
    neg = -jnp.inf

    g8 = gr_ref[...] + br_ref[...]
    li8 = g8 * LOG2E
    lf8 = pltpu.roll(_log_sigmoid(g8) * LOG2E, CHAINS, 0)
    row = lax.broadcasted_iota(jnp.int32, (GATES_PER_PAIR, L), 0)
    lane = lax.broadcasted_iota(jnp.int32, (GATES_PER_PAIR, L), 1)
    fwd = (row % CHAINS) < 2
    fwd1 = fwd[:, :1]
    si = lax.broadcasted_iota(jnp.int32, (L, 2 * L), 0)
    ti = lax.broadcasted_iota(jnp.int32, (L, 2 * L), 1)
    tri = jnp.where(ti < L, jnp.where(si <= ti, 1.0, 0.0), jnp.where(si >= ti - L, 1.0, 0.0))
    tot, mloc, b_l, cm_l, r_l = [], [], [], [], []
    for c in range(NC):
        pr = jnp.dot(lf8[:, c * L:(c + 1) * L], tri, precision=hi, preferred_element_type=F32)
        b_c = jnp.where(fwd, pr[:, :L], pr[:, L:])
        tot_c = pr[:, L - 1:L]
        li_c = li8[:, c * L:(c + 1) * L]
        r_c = li_c - b_c
        cm = r_c
        k = 1
        while k < L:
            pre = jnp.where(lane >= k, pltpu.roll(cm, k, 1), neg)
            suf = jnp.where(lane < L - k, pltpu.roll(cm, L - k, 1), neg)
            cm = jnp.maximum(cm, jnp.where(fwd, pre, suf))
            k *= 2
        g_c = tot_c - b_c + li_c
        mloc_c = jnp.max(g_c, axis=1, keepdims=True)
        wl_s[c] = jnp.exp2(g_c - mloc_c)
        r_l.append(r_c)
        tot.append(tot_c)
        mloc.append(mloc_c)
        b_l.append(b_c)
        cm_l.append(cm)

    def scan(order):
        m = jnp.zeros((GATES_PER_PAIR, 1), F32)
        m_in, dec, sc = [None] * NC, [None] * NC, [None] * NC
        for c in order:
            m_new = jnp.maximum(tot[c] + m, mloc[c])
            m_in[c] = m
            dec[c] = jnp.exp2(tot[c] + m - m_new)
            sc[c] = jnp.exp2(mloc[c] - m_new)
            m = m_new
        return m_in, dec, sc

    mf, df, sf = scan(range(NC))
    mb, db, sb = scan(range(NC - 1, -1, -1))
    def terms3(x):
        t0 = x.astype(BF16).astype(F32)
        rem = x - t0
        t1 = rem.astype(BF16).astype(F32)
        return t0, t1, (rem - t1).astype(BF16).astype(F32)

    blk_rows = 2 * SUBLANES
    rowi = lax.broadcasted_iota(jnp.int32, (blk_rows, L), 0)
    bro = lambda x, ci: jnp.broadcast_to(x[ci:ci + 1, :], (blk_rows, L))
    zblk = jnp.zeros((blk_rows, L), F32)
    pad = jnp.zeros((LANES - GATES_PER_PAIR, L), F32)
    for c in range(NC):
        m_in = jnp.where(fwd1, mf[c], mb[c])
        dec_s[c] = jnp.broadcast_to(jnp.where(fwd1, df[c], db[c]), (GATES_PER_PAIR, L))
        sc_s[c] = jnp.broadcast_to(jnp.where(fwd1, sf[c], sb[c]), (GATES_PER_PAIR, L))
        a_c = jnp.maximum(m_in, cm_l[c])
        winter_s[c] = jnp.exp2(m_in - a_c)
        emt_c = jnp.exp2(-(b_l[c] + a_c))
        col_s[c * L:(c + 1) * L, :] = jnp.concatenate([emt_c, pad], axis=0).T
        a0, a1, a2 = terms3(a_c)
        r0, r1, r2 = terms3(r_l[c])
        a_blks, r_cols = [], []
        for ci in range(CHAINS):
            a_blks.append(jnp.where(rowi == 0, bro(a0, ci), jnp.where(rowi == 1, bro(a1, ci), jnp.where(
                rowi == 2, bro(a2, ci), jnp.where(rowi < 6, 1.0, 0.0)))))
            r_blk = jnp.where(rowi < 3, -1.0, jnp.where(rowi == 3, bro(r0, ci), jnp.where(
                rowi == 4, bro(r1, ci), jnp.where(rowi == 5, bro(r2, ci), 0.0))))
            r_cols.append(jnp.concatenate([zblk] * ci + [r_blk] + [zblk] * (LANES // blk_rows - 1 - ci), axis=0))
        a_rows = jnp.concatenate(a_blks + [zblk] * (LANES // blk_rows - CHAINS), axis=0)
        acol_s[c * L:(c + 1) * L, :] = a_rows.T.astype(BF16)
        rmat_s[c] = jnp.concatenate(r_cols, axis=1).astype(BF16)

    qs = (_conv_silu(q_ref[...].astype(F32), cwq_ref[...], pad_s) * (DQ ** -0.5)).astype(BF16)
    q0_s[...] = qs[:, :DQ]
    q1_s[...] = qs[:, DQ:]
    kt = _conv_silu(k_ref[...].astype(F32), cwk_ref[...], pad_s).T
    for c in range(NC):
        kt_s[c] = kt[:, c * L:(c + 1) * L]

    ones_blk = jnp.ones((L, DV), BF16)

    def vext_of(sl, hh):
        return jnp.concatenate([v_ref[sl, DV * hh:DV * (hh + 1)], ones_blk], axis=1)

    cst_s[...] = jnp.zeros(cst_s.shape, F32)

    def state_step(c):
        for d in range(2):
            ch = c if d == 0 else NC - 1 - c
            sl = pl.ds(pl.multiple_of(ch * L, L), L)
            wl = wl_s[ch]
            ktc = kt_s[ch]
            decs = dec_s[ch]
            scs = sc_s[ch]
            for hh in range(2):
                ci = d * 2 + hh
                cx = cst_s[ci]
                call_s[ch, ci] = cx.astype(BF16)
                kw = (ktc[DQ * hh:DQ * (hh + 1), :] * wl[ci:ci + 1, :]).astype(BF16)
                cst_s[ci] = (decs[ci:ci + 1, 0:1] * cx
                             + scs[ci:ci + 1, 0:1] * jnp.dot(kw, vext_of(sl, hh), preferred_element_type=F32))

    def state_body(c, carry):
        for u in range(STATE_UNROLL):
            state_step(STATE_UNROLL * c + u)
        return carry

    lax.fori_loop(0, NC // STATE_UNROLL, state_body, 0)

    tt = lax.broadcasted_iota(jnp.int32, (L, L), 0)
    ss = lax.broadcasted_iota(jnp.int32, (L, L), 1)
    masks = (ss <= tt, ss >= tt)

    eye = tt == ss

    def chunk_out(ch, hh, sl, cols, winter, e_all):
        og = _sigmoid(o_ref[sl, DV * hh:DV * (hh + 1)].astype(F32))
        qc = (q0_s if hh == 0 else q1_s)[sl, :]
        kth = kt_s[ch, DQ * hh:DQ * (hh + 1), :].astype(BF16)
        qk = jnp.dot(qc, kth, preferred_element_type=F32)
        vext = vext_of(sl, hh)
        hsum = None
        for d in range(2):
            ci = d * 2 + hh
            w_intra = jnp.exp2(jnp.where(masks[d], e_all[:, L * ci:L * (ci + 1)], neg))
            qcx = jnp.dot(qc, call_s[ch, ci], preferred_element_type=F32).astype(BF16)
            w_diag = jnp.where(eye, winter[ci:ci + 1, :], 0.0).astype(BF16)
            nd = jnp.dot(jnp.concatenate([(qk * w_intra).astype(BF16), w_diag], axis=1),
                         jnp.concatenate([vext, qcx], axis=0),
                         preferred_element_type=F32)
            h = nd[:, :DV] / jnp.maximum(jnp.abs(nd[:, DV:]), cols[:, ci:ci + 1])
            hsum = h if hsum is None else hsum + h
        y = hsum * lax.rsqrt(jnp.mean(hsum * hsum, axis=1, keepdims=True) + NORM_EPS) * gh_ref[hh:hh + 1, :]
        out_ref[sl, DV * hh:DV * (hh + 1)] = (og * y).astype(out_ref.dtype)

    def out_chunk(c):
        sl = pl.ds(pl.multiple_of(c * L, L), L)
        e_all = jnp.dot(acol_s[sl, :], rmat_s[c], preferred_element_type=F32)
        cols = col_s[sl, :]
        winter = winter_s[c]
        chunk_out(c, 0, sl, cols, winter, e_all)
        chunk_out(c, 1, sl, cols, winter, e_all)

    def out_body(c, carry):
        for u in range(OUT_UNROLL):
            out_chunk(OUT_UNROLL * c + u)
        return carry

    lax.fori_loop(0, NC // OUT_UNROLL, out_body, 0)


def _mlstm(proj, grow, conv_w, bias_r, g_head, B, S):
    T = B * S
    NC = S // MLSTM_LC
    pw = 2 * MLSTM_DQK
    vw = 2 * MLSTM_DV
    return pl.pallas_call(
        _mlstm_kernel,
        grid=(B, HEAD_PAIRS),
        in_specs=[
            pl.BlockSpec((S, pw), lambda b, p: (b, OFF_MQ // pw + p)),
            pl.BlockSpec((S, pw), lambda b, p: (b, OFF_MK // pw + p)),
            pl.BlockSpec((S, vw), lambda b, p: (b, OFF_MV // vw + p)),
            pl.BlockSpec((S, vw), lambda b, p: (b, OFF_MO // vw + p)),
            pl.BlockSpec((None, GATES_PER_PAIR, S), lambda b, p: (p, 0, b)),
            pl.BlockSpec((CONV_W, pw), lambda b, p: (0, p)),
            pl.BlockSpec((CONV_W, pw), lambda b, p: (0, MQ_COLS // pw + p)),
            pl.BlockSpec((None, GATES_PER_PAIR, 1), lambda b, p: (p, 0, 0)),
            pl.BlockSpec((None, 2, MLSTM_DV), lambda b, p: (p, 0, 0)),
        ],
        out_specs=pl.BlockSpec((S, vw), lambda b, p: (b, p)),
        out_shape=jax.ShapeDtypeStruct((T, MLSTM_WIDTH), BF16),
        scratch_shapes=[
            pltpu.VMEM((S, MLSTM_DQK), BF16),
            pltpu.VMEM((S, MLSTM_DQK), BF16),
            pltpu.VMEM((NC, pw, MLSTM_LC), F32),
            pltpu.VMEM((S, LANES), F32),
            pltpu.VMEM((NC, GATES_PER_PAIR, MLSTM_LC), F32),
            pltpu.VMEM((NC, GATES_PER_PAIR, MLSTM_LC), F32),
            pltpu.VMEM((NC, GATES_PER_PAIR, MLSTM_LC), F32),
            pltpu.VMEM((NC, GATES_PER_PAIR, MLSTM_LC), F32),
            pltpu.VMEM((CHAINS, MLSTM_DQK, 2 * MLSTM_DV), F32),
            pltpu.VMEM((NC, CHAINS, MLSTM_DQK, 2 * MLSTM_DV), BF16),
            pltpu.VMEM((S + 2 * SUBLANES, pw), F32),
            pltpu.VMEM((S, LANES), BF16),
            pltpu.VMEM((NC, LANES, CHAINS * MLSTM_LC), BF16),
        ],
        compiler_params=_cparams(("parallel", "parallel")),
        name="mlstm",
    )(proj, proj, proj, proj, grow, conv_w, conv_w, bias_r, g_head)


def _norm_rope(x, g, cos, sin_signed):
    xn = x * lax.rsqrt(jnp.mean(x * x, axis=1, keepdims=True) + NORM_EPS) * g
    lane = lax.broadcasted_iota(jnp.int32, x.shape, 1)
    first_half = (lane % (2 * ROPE_FREQS)) < ROPE_FREQS
    partner = jnp.where(first_half,
                        pltpu.roll(xn, LANES - ROPE_FREQS, 1),
                        pltpu.roll(xn, ROPE_FREQS, 1))
    return xn * cos + partner * sin_signed


def _attn_kernel(q_ref, k_ref, v_ref, cq_ref, sq_ref, ck_ref, sk_ref, gq_ref, gk_ref, o_ref, kr_s, vx_s):
    d = ATTN_HEAD_DIM

    @pl.when(pl.program_id(2) == 0)
    def _():
        kr_s[...] = _norm_rope(k_ref[...].astype(F32), gk_ref[...], ck_ref[...], sk_ref[...]).astype(BF16)
        vx_s[...] = jnp.concatenate([v_ref[...], jnp.ones(v_ref.shape, BF16)], axis=1)

    gq = gq_ref[...]
    kr = kr_s[...]
    vx = vx_s[...]
    for r0 in range(0, q_ref.shape[0], ATTN_CHAIN_ROWS):
        rows = slice(r0, r0 + ATTN_CHAIN_ROWS)
        cq, sq = cq_ref[rows, :], sq_ref[rows, :]
        for g in range(ATTN_GROUP):
            cols = slice(d * g, d * (g + 1))
            qg = (_norm_rope(q_ref[rows, cols].astype(F32), gq, cq, sq) * (d ** -0.5 * LOG2E)).astype(BF16)
            s = lax.dot_general(qg, kr, (((1,), (1,)), ((), ())), preferred_element_type=F32)
            p = jnp.exp2(s - jnp.max(s, axis=1, keepdims=True))
            ov = jnp.dot(p.astype(BF16), vx, preferred_element_type=F32)
            o_ref[rows, cols] = (ov[:, :d] / ov[:, d:]).astype(o_ref.dtype)


def _attn(proj, cos_t, sin_t, g_q, g_k, B, S, tq=1024):
    T = B * S
    d = ATTN_HEAD_DIM
    gw = ATTN_GROUP * d
    nq = S // tq
    return pl.pallas_call(
        _attn_kernel,
        grid=(B, ATTN_KV_HEADS, nq),
        in_specs=[
            pl.BlockSpec((tq, gw), lambda b, kv, qi: (b * nq + qi, OFF_AQ // gw + kv)),
            pl.BlockSpec((S, d), lambda b, kv, qi: (b, OFF_AK // d + kv)),
            pl.BlockSpec((S, d), lambda b, kv, qi: (b, OFF_AV // d + kv)),
            pl.BlockSpec((tq, d), lambda b, kv, qi: (qi, 0)),
            pl.BlockSpec((tq, d), lambda b, kv, qi: (qi, 0)),
            _resident((S, d), lambda b, kv, qi: (0, 0)),
            _resident((S, d), lambda b, kv, qi: (0, 0)),
            _resident((1, d), lambda b, kv, qi: (0, 0)),
            _resident((1, d), lambda b, kv, qi: (0, 0)),
        ],
        out_specs=pl.BlockSpec((tq, gw), lambda b, kv, qi: (b * nq + qi, kv)),
        out_shape=jax.ShapeDtypeStruct((T, ATTN_WIDTH), BF16),
        scratch_shapes=[pltpu.VMEM((S, d), BF16), pltpu.VMEM((S, 2 * d), BF16)],
        compiler_params=_cparams(("parallel", "parallel", "arbitrary")),
        name="attn",
    )(proj, proj, proj, cos_t, sin_t, cos_t, sin_t, g_q, g_k)


def _layer_norm(y, g, b):
    mu = jnp.mean(y, axis=1, keepdims=True)
    yc = y - mu
    var = jnp.mean(yc * yc, axis=1, keepdims=True)
    return yc * lax.rsqrt(var + NORM_EPS) * g + b


def _split_bf16(x):
    hi = x.astype(BF16)
    return hi, (x - hi.astype(F32)).astype(BF16)


def _tile_perm():
    n = TOK_TILE * COL_BLOCKS
    a = lax.broadcasted_iota(jnp.int32, (n, n), 0)
    b = lax.broadcasted_iota(jnp.int32, (n, n), 1)
    return jnp.where((a >> 4) == (b & 15), jnp.where((a & 15) == (b >> 4), 1.0, 0.0), 0.0).astype(BF16)


def _token_major(xh, perm):
    out = []
    for r0 in range(0, xh.shape[0], 2 * TOK_TILE):
        halves = [jnp.concatenate([xh[r0 + h * TOK_TILE:r0 + (h + 1) * TOK_TILE, LANES * j:LANES * (j + 1)]
                                   for j in range(COL_BLOCKS)], axis=0) for h in range(2)]
        o2 = jnp.dot(perm, jnp.concatenate(halves, axis=1), preferred_element_type=F32).astype(BF16)
        out += [o2[:, :LANES], o2[:, LANES:]]
    return jnp.concatenate(out, axis=0)


def _row_major(g_ref, perm):
    n = TOK_TILE * COL_BLOCKS
    cols = [[] for _ in range(COL_BLOCKS)]
    for r0 in range(0, g_ref.shape[0], 2 * n):
        m2 = jnp.concatenate([g_ref[r0:r0 + n, :], g_ref[r0 + n:r0 + 2 * n, :]], axis=1)
        o2 = jnp.dot(perm, m2, preferred_element_type=F32).astype(BF16)
        for j in range(COL_BLOCKS):
            blk = o2[TOK_TILE * j:TOK_TILE * (j + 1), :]
            cols[j] += [blk[:, :LANES], blk[:, LANES:]]
    return jnp.concatenate([jnp.concatenate(cj, axis=0) for cj in cols], axis=1)


def _outproj_kernel(hm_ref, ha_ref, x_ref, wt_ref, wb_ref, g_ref, b_ref, wr_ref, x1_ref, xg_ref, aff_ref):
    y = (ALPHA * x_ref[...]
         + jnp.dot(hm_ref[...], wt_ref[...], preferred_element_type=F32)
         + jnp.dot(ha_ref[...], wb_ref[...], preferred_element_type=F32))
    x1 = _layer_norm(y, g_ref[...], b_ref[...])
    x1_ref[...] = x1
    xh, xl = _split_bf16(x1)
    wh, wl = _split_bf16(wr_ref[...])
    xg_ref[...] = _token_major(xh, _tile_perm())
    nt = (((1,), (1,)), ((), ()))
    logits = (lax.dot_general(wh, xh, nt, preferred_element_type=F32)
              + lax.dot_general(wh, xl, nt, preferred_element_type=F32)
              + lax.dot_general(wl, xh, nt, preferred_element_type=F32))
    e = jnp.exp(logits - jnp.max(logits, axis=0, keepdims=True))
    aff_ref[:, 0, :] = e / jnp.sum(e, axis=0, keepdims=True)


def _outproj(hm, ha, x2, wt, wb, g, b, wr, S, tm=512):
    T, D = x2.shape
    nt = S // tm
    return pl.pallas_call(
        _outproj_kernel,
        grid=(T // tm,),
        in_specs=[
            pl.BlockSpec((tm, MLSTM_WIDTH), lambda i: (i, 0)),
            pl.BlockSpec((tm, ATTN_WIDTH), lambda i: (i, 0)),
            pl.BlockSpec((tm, D), lambda i: (i, 0)),
            _resident((MLSTM_WIDTH, D), lambda i: (0, 0)),
            _resident((ATTN_WIDTH, D), lambda i: (MLSTM_WIDTH // ATTN_WIDTH, 0)),
            _resident((1, D), lambda i: (0, 0)),
            _resident((1, D), lambda i: (0, 0)),
            _resident((N_EXPERTS, D), lambda i: (0, 0)),
        ],
        out_specs=[
            pl.BlockSpec((tm, D), lambda i: (i, 0)),
            pl.BlockSpec((tm * TOK_TILE, LANES), lambda i: (i, 0)),
            pl.BlockSpec((N_EXPERTS, None, 1, tm), lambda i: (0, i // nt, 0, i % nt)),
        ],
        out_shape=[
            jax.ShapeDtypeStruct((T, D), F32),
            jax.ShapeDtypeStruct((T * TOK_TILE, LANES), BF16),
            jax.ShapeDtypeStruct((N_EXPERTS, T // S, 1, S), F32),
        ],
        compiler_params=_cparams(("parallel",)),
        name="outproj",
    )(hm, ha, x2, wt, wb, g, b, wr)


def _excl_prefix_count(m):
    E, S = m.shape
    nb = S // LANES
    s_i = lax.broadcasted_iota(jnp.int32, (LANES, 2 * LANES), 0)
    t_i = lax.broadcasted_iota(jnp.int32, (LANES, 2 * LANES), 1)
    tri_ext = jnp.where(t_i >= LANES, 1.0, jnp.where(s_i < t_i, 1.0, 0.0)).astype(BF16)
    r_i = lax.broadcasted_iota(jnp.int32, (nb * E, nb * E), 0)
    c_i = lax.broadcasted_iota(jnp.int32, (nb * E, nb * E), 1)
    earlier = jnp.where(c_i // E < r_i // E, jnp.where(c_i % E == r_i % E, 1.0, 0.0), 0.0).astype(BF16)
    stacked = jnp.concatenate([m[:, LANES * j:LANES * (j + 1)] for j in range(nb)], axis=0)
    r = jnp.dot(stacked, tri_ext, preferred_element_type=F32)
    off = jnp.dot(earlier, r[:, LANES:].astype(BF16), preferred_element_type=F32)
    tot = r[:, :LANES] + off
    return jnp.concatenate([tot[E * j:E * (j + 1), :] for j in range(nb)], axis=1)


def _topk_kernel(aff_ref, pos_ref, idx_ref, *, cap):
    S = aff_ref.shape[2]
    a = aff_ref[:, 0, :]

    def count_ge(v):
        return jnp.sum(jnp.where(a >= v, 1.0, 0.0), axis=1, keepdims=True)

    thr = jnp.zeros((a.shape[0], 1), jnp.int32)
    for bit in range(30, -1, -1):
        cand = thr | (1 << bit)
        thr = jnp.where(count_ge(pltpu.bitcast(cand, F32)) >= cap, cand, thr)
    lo = pltpu.bitcast(thr, F32)
    hi = pltpu.bitcast(thr + 1, F32)
    for _ in range(TOPK_REFINE_STEPS):
        mid = 0.5 * (lo + hi)
        ok = count_ge(mid) >= cap
        lo = jnp.where(ok, mid, lo)
        hi = jnp.where(ok, hi, mid)
    gt = a > lo
    eq = a == lo
    need = cap - jnp.sum(jnp.where(gt, 1.0, 0.0), axis=1, keepdims=True)
    eq_rank = _excl_prefix_count(jnp.where(eq, 1.0, 0.0).astype(BF16))
    sel = jnp.logical_or(gt, jnp.logical_and(eq, eq_rank < need))
    pos = _excl_prefix_count(jnp.where(sel, 1.0, 0.0).astype(BF16))
    posm = jnp.where(sel, pos, -1.0)
    pos_ref[:, 0, :] = posm
    digit_row = lax.broadcasted_iota(jnp.int32, (2 * SUBLANES, S), 0)
    tok = lax.broadcasted_iota(jnp.int32, (2 * SUBLANES, S), 1)
    digits = jnp.where(digit_row == 0, tok >> 4, jnp.where(digit_row == 1, tok & 15, 0)).astype(F32).astype(BF16)
    slot = lax.broadcasted_iota(jnp.int32, (cap, S), 0).astype(F32)
    for e in range(a.shape[0]):
        hit = jnp.where(slot == posm[e:e + 1, :], 1.0, 0.0).astype(BF16)
        idx_ref[e] = lax.dot_general(digits, hit, (((1,), (1,)), ((), ())),
                                     preferred_element_type=F32)[:SUBLANES]


def _topk(aff, B, S, cap):
    E = aff.shape[0]
    return pl.pallas_call(
        functools.partial(_topk_kernel, cap=cap),
        grid=(B,),
        in_specs=[pl.BlockSpec((E, None, 1, S), lambda b: (0, b, 0, 0))],
        out_specs=[pl.BlockSpec((E, None, 1, S), lambda b: (0, b, 0, 0)),
                   pl.BlockSpec((E, None, SUBLANES, cap), lambda b: (0, b, 0, 0))],
        out_shape=[jax.ShapeDtypeStruct((E, B, 1, S), F32),
                   jax.ShapeDtypeStruct((E, B, SUBLANES, cap), F32)],
        compiler_params=_cparams(("parallel",)),
        name="topk",
    )(aff)


def _moe_ffn_kernel(idx_ref, xg_ref, pos_ref, aff_ref, wg_ref, wu_ref, wd_ref, y_ref, gbuf_s, wg_s, wu_s, wd_s,
                    *, cap, n_exp):
    e = pl.program_id(0)
    b = pl.program_id(1)

    @pl.when(e < n_exp)
    def _():
        slot = e % 2
        rg = pl.multiple_of(b * wg_ref.shape[0], wg_ref.shape[0])
        rd = pl.multiple_of(b * wd_ref.shape[0], wd_ref.shape[0])
        wg_s[slot, pl.ds(rg, wg_ref.shape[0]), :] = wg_ref[...].astype(BF16)
        wu_s[slot, pl.ds(rg, wu_ref.shape[0]), :] = wu_ref[...].astype(BF16)
        wd_s[slot, pl.ds(rd, wd_ref.shape[0]), :] = wd_ref[...].astype(BF16)

    @pl.when(e > 0)
    def _():
        S = pos_ref.shape[1]
        slot = (e - 1) % 2
        for c in range(cap):
            row = pl.multiple_of(idx_ref[0, c] * TOK_TILE, TOK_TILE)
            gbuf_s[TOK_TILE * c:TOK_TILE * (c + 1), :] = xg_ref[pl.ds(row, TOK_TILE), :]
        perm = _tile_perm()
        xs = _row_major(gbuf_s, perm)
        sl = lax.broadcasted_iota(jnp.int32, (cap, S), 0).astype(F32)
        gate = jnp.sum(jnp.where(sl == pos_ref[...], aff_ref[...], 0.0), axis=1, keepdims=True)
        hg = jnp.dot(xs, wg_s[slot], preferred_element_type=F32)
        hu = jnp.dot(xs, wu_s[slot], preferred_element_type=F32)
        hid = (hg * _sigmoid(hg) * hu).astype(BF16)
        y = (jnp.dot(hid, wd_s[slot], preferred_element_type=F32) * gate).astype(BF16)
        y_ref[...] = _token_major(y, perm)


def _moe_ffn(idx, xg, pos4, aff4, wg, wu, wd, B, S, cap):
    E, D, F = wg.shape
    prev = lambda e: jnp.maximum(e - 1, 0)
    stage = lambda e, b: (jnp.minimum(e, E - 1), jnp.where(e < E, b, B - 1), 0)
    return pl.pallas_call(
        functools.partial(_moe_ffn_kernel, cap=cap, n_exp=E),
        grid=(E + 1, B),
        in_specs=[
            pl.BlockSpec((None, None, 1, cap), lambda e, b: (prev(e), b, 0, 0), memory_space=pltpu.SMEM),
            pl.BlockSpec((S * TOK_TILE, LANES), lambda e, b: (b, 0)),
            pl.BlockSpec((None, None, 1, S), lambda e, b: (prev(e), b, 0, 0)),
            pl.BlockSpec((None, None, 1, S), lambda e, b: (prev(e), b, 0, 0)),
            pl.BlockSpec((None, D // B, F), stage),
            pl.BlockSpec((None, D // B, F), stage),
            pl.BlockSpec((None, F // B, D), stage),
        ],
        out_specs=pl.BlockSpec((None, None, cap * TOK_TILE, LANES),
                               lambda e, b: (jnp.where(e > 0, b, 0), prev(e), 0, 0)),
        out_shape=jax.ShapeDtypeStruct((B, E, cap * TOK_TILE, LANES), BF16),
        scratch_shapes=[pltpu.VMEM((cap * TOK_TILE, LANES), BF16),
                        pltpu.VMEM((2, D, F), BF16), pltpu.VMEM((2, D, F), BF16), pltpu.VMEM((2, F, D), BF16)],
        compiler_params=_cparams(("arbitrary", "arbitrary")),
        name="moe_ffn",
    )(idx, xg, pos4, aff4, wg, wu, wd)


def _row_major_f32(g_ref, r0, n_tok, perm):
    n = TOK_TILE * COL_BLOCKS
    cols = [[] for _ in range(COL_BLOCKS)]
    for k in range(0, n_tok * TOK_TILE, 2 * n):
        m2 = jnp.concatenate([g_ref[pl.ds(r0 + k, n), :], g_ref[pl.ds(r0 + k + n, n), :]], axis=1)
        hi = m2.astype(BF16)
        r1 = m2 - hi.astype(F32)
        mid = r1.astype(BF16)
        lo = (r1 - mid.astype(F32)).astype(BF16)
        o2 = (jnp.dot(perm, hi, preferred_element_type=F32) + jnp.dot(perm, mid, preferred_element_type=F32)
              + jnp.dot(perm, lo, preferred_element_type=F32))
        for j in range(COL_BLOCKS):
            blk = o2[TOK_TILE * j:TOK_TILE * (j + 1), :]
            cols[j] += [blk[:, :LANES], blk[:, LANES:]]
    return jnp.concatenate([jnp.concatenate(cj, axis=0) for cj in cols], axis=1)


def _moe_comb_kernel(idx_ref, y_ref, o_ref, acc_s, *, cap, n_grp):
    j = pl.program_id(1)

    @pl.when(j == 0)
    def _():
        acc_s[...] = jnp.zeros(acc_s.shape, F32)

    @pl.when(j < n_grp)
    def _():
        for g in range(idx_ref.shape[0]):
            for c0 in range(0, cap, SCATTER_BATCH):
                rows = [pl.multiple_of(idx_ref[g, 0, c0 + u] * TOK_TILE, TOK_TILE) for u in range(SCATTER_BATCH)]
                new = [acc_s[pl.ds(rows[u], TOK_TILE), :]
                       + y_ref[g, TOK_TILE * (c0 + u):TOK_TILE * (c0 + u + 1), :].astype(F32)
                       for u in range(SCATTER_BATCH)]
                for u in range(SCATTER_BATCH):
                    acc_s[pl.ds(rows[u], TOK_TILE), :] = new[u]

    @pl.when(j >= n_grp)
    def _():
        n_tok = o_ref.shape[0]
        r0 = pl.multiple_of((j - n_grp) * (n_tok * TOK_TILE), n_tok * TOK_TILE)
        o_ref[...] = _row_major_f32(acc_s, r0, n_tok, _tile_perm())


def _moe_comb(idx, y, B, S, cap, chunk=1024, eg=4):
    E = y.shape[1]
    D = D_MODEL
    nch = S // chunk
    ng = E // eg
    return pl.pallas_call(
        functools.partial(_moe_comb_kernel, cap=cap, n_grp=ng),
        grid=(B, ng + nch),
        in_specs=[
            pl.BlockSpec((eg, None, 1, cap), lambda b, j: (jnp.minimum(j, ng - 1), b, 0, 0),
                         memory_space=pltpu.SMEM),
            pl.BlockSpec((None, eg, cap * TOK_TILE, LANES), lambda b, j: (b, jnp.minimum(j, ng - 1), 0, 0)),
        ],
        out_specs=pl.BlockSpec((chunk, D), lambda b, j: (b * nch + jnp.maximum(j - ng, 0), 0)),
        out_shape=jax.ShapeDtypeStruct((B * S, D), F32),
        scratch_shapes=[pltpu.VMEM((S * TOK_TILE, LANES), F32)],
        compiler_params=_cparams(("arbitrary", "arbitrary")),
        name="moe_comb",
    )(idx, y)


def _final_kernel(x1_ref, moe_ref, p_ref, wpg_ref, bpg_ref, wpp_ref, g_ref, b_ref, o_ref):
    for r0 in range(0, x1_ref.shape[0], ROW_CHAIN):
        rows = slice(r0, r0 + ROW_CHAIN)
        x1 = x1_ref[rows, :]
        gate = _sigmoid(jnp.dot(x1.astype(BF16), wpg_ref[...], preferred_element_type=F32) + bpg_ref[...])
        plv = gate * jnp.dot(p_ref[rows, :].astype(BF16), wpp_ref[...], preferred_element_type=F32)
        o_ref[rows, :] = _layer_norm(ALPHA * x1 + moe_ref[rows, :] + plv, g_ref[...], b_ref[...])


def _final(x1, moe, p2, wpg, bpg, wpp, g, b, tm=512):
    T, D = x1.shape
    return pl.pallas_call(
        _final_kernel,
        grid=(T // tm,),
        in_specs=[
            pl.BlockSpec((tm, D), lambda i: (i, 0)),
            pl.BlockSpec((tm, D), lambda i: (i, 0)),
            pl.BlockSpec((tm, P_DIM), lambda i: (i, 0)),
            _resident((D, D), lambda i: (0, 0)),
            _resident((1, D), lambda i: (0, 0)),
            _resident((P_DIM, D), lambda i: (0, 0)),
            _resident((1, D), lambda i: (0, 0)),
            _resident((1, D), lambda i: (0, 0)),
        ],
        out_specs=pl.BlockSpec((tm, D), lambda i: (i, 0)),
        out_shape=jax.ShapeDtypeStruct((T, D), F32),
        compiler_params=_cparams(("parallel",)),
        name="final",
    )(x1, moe, p2, wpg, bpg, wpp, g, b)


def _rope_tables(S):
    rows = S // GRID_W
    row_idx = jnp.broadcast_to(jnp.arange(rows, dtype=F32)[:, None], (rows, GRID_W)).reshape(-1)
    col_idx = jnp.broadcast_to(jnp.arange(GRID_W, dtype=F32)[None, :], (rows, GRID_W)).reshape(-1)
    inv_freq = ROPE_THETA ** (-jnp.arange(ROPE_FREQS, dtype=F32) / ROPE_FREQS)
    ar = row_idx[:, None] * inv_freq
    ac = col_idx[:, None] * inv_freq
    cos_t = jnp.concatenate([jnp.cos(ar), jnp.cos(ar), jnp.cos(ac), jnp.cos(ac)], axis=1)
    sin_t = jnp.concatenate([-jnp.sin(ar), jnp.sin(ar), -jnp.sin(ac), jnp.sin(ac)], axis=1)
    return cos_t, sin_t


def _pair_major(a):
    return a.reshape(2, HEAD_PAIRS, 2).transpose(1, 0, 2)


def _layer(x2, p2, B, S, w_in, conv_w, b_i, b_f, g_mlstm, g_q, g_k, w_out, ln1_g, ln1_b, w_router, w_gate, w_up,
           w_down, w_pl_proj, w_pl_gate, b_pl_gate, ln2_g, ln2_b):
    D = D_MODEL
    cap = CAPACITY_FACTOR * S // N_EXPERTS
    o_mg = 2 * MQ_COLS + 2 * MV_COLS
    w_main = jnp.concatenate([w_in[:, :o_mg], w_in[:, o_mg + MG_COLS:]], axis=1).astype(BF16)
    w_g = w_in[:, o_mg:o_mg + MG_COLS].reshape(D, 2, 2, HEAD_PAIRS, 2).transpose(0, 3, 2, 1, 4).reshape(D, MG_COLS)
    bias = jnp.stack([_pair_major(b_i), _pair_major(b_f)], axis=1).reshape(HEAD_PAIRS, GATES_PER_PAIR)

    proj, grow = _proj(x2, w_main, w_g.astype(BF16).T)
    h_m = _mlstm(proj, grow, conv_w, bias[:, :, None], g_mlstm.reshape(HEAD_PAIRS, 2, MLSTM_DV), B, S)
    cos_t, sin_t = _rope_tables(S)
    h_a = _attn(proj, cos_t, sin_t, g_q[None, :], g_k[None, :], B, S)
    w_o = w_out.astype(BF16)
    x1, xg, aff4 = _outproj(h_m, h_a, x2, w_o, w_o, ln1_g[None, :], ln1_b[None, :], w_router.T, S)
    pos4, idx_digits = _topk(aff4, B, S, cap)
    idx = (idx_digits[:, :, 0:1, :] * TOK_TILE + idx_digits[:, :, 1:2, :]).astype(jnp.int32)
    y = _moe_ffn(idx, xg, pos4, aff4, w_gate, w_up, w_down, B, S, cap)
    moe = _moe_comb(idx, y, B, S, cap)
    return _final(x1, moe, p2, w_pl_gate.astype(BF16), b_pl_gate[None, :], w_pl_proj.astype(BF16),
                  ln2_g[None, :], ln2_b[None, :])


def kernel(x, p, w_in, conv_w, b_igate, b_fgate, g_mlstm, g_q, g_k, w_out, ln1_g, ln1_b, w_router, w_gate, w_up,
           w_down, w_pl_proj, w_pl_gate, b_pl_gate, ln2_g, ln2_b):
    B, S, D = x.shape
    x2 = x.reshape(B * S, D)
    for i in range(DEPTH):
        x2 = _layer(x2, p[i].reshape(B * S, P_DIM), B, S, w_in[i], conv_w[i], b_igate[i], b_fgate[i], g_mlstm[i],
                    g_q[i], g_k[i], w_out[i], ln1_g[i], ln1_b[i], w_router[i], w_gate[i], w_up[i], w_down[i],
                    w_pl_proj[i], w_pl_gate[i], b_pl_gate[i], ln2_g[i], ln2_b[i])
    return x2.reshape(B, S, D)
```

```python
import functools

import jax
import jax.numpy as jnp
from jax import lax
from jax.experimental import pallas as pl
from jax.experimental.pallas import tpu as pltpu

F32 = jnp.float32
BF16 = jnp.bfloat16

D_MODEL = 2048
P_DIM = 256
GRID_W = 64
MLSTM_WIDTH = D_MODEL // 2
ATTN_WIDTH = D_MODEL - MLSTM_WIDTH
MLSTM_HEADS = 8
MLSTM_DV = MLSTM_WIDTH // MLSTM_HEADS
MLSTM_DQK = MLSTM_DV // 2
CONV_W = 5
ATTN_HEAD_DIM = 128
ATTN_Q_HEADS = ATTN_WIDTH // ATTN_HEAD_DIM
ATTN_KV_HEADS = 2
ATTN_GROUP = ATTN_Q_HEADS // ATTN_KV_HEADS
ROPE_FREQS = ATTN_HEAD_DIM // 4
ROPE_THETA = 10000.0
N_EXPERTS = 16
EXPERT_FF = D_MODEL // 2
CAPACITY_FACTOR = 2
NORM_EPS = 1e-6
DEPTH = 1
ALPHA = (2.0 * DEPTH) ** 0.25
LOG2E = 1.4426950408889634

MQ_COLS = MLSTM_HEADS * MLSTM_DQK
MV_COLS = MLSTM_WIDTH
MG_COLS = 2 * 2 * MLSTM_HEADS
AQ_COLS = ATTN_WIDTH
AKV_COLS = ATTN_KV_HEADS * ATTN_HEAD_DIM
PROJ_COLS = 2 * MQ_COLS + 2 * MV_COLS + AQ_COLS + 2 * AKV_COLS
OFF_MQ, OFF_MK, OFF_MV, OFF_MO = 0, MQ_COLS, 2 * MQ_COLS, 2 * MQ_COLS + MV_COLS
OFF_AQ = OFF_MO + MV_COLS
OFF_AK = OFF_AQ + AQ_COLS
OFF_AV = OFF_AK + AKV_COLS
PROJ_COL_STEP = 1536

HEAD_PAIRS = MLSTM_HEADS // 2
GATES_PER_PAIR = 8
CHAINS = 4
MLSTM_LC = 128
STATE_UNROLL = 8
OUT_UNROLL = 8
ATTN_CHAIN_ROWS = 256
ROW_CHAIN = 256
LANES = 128
SUBLANES = 8
TOK_TILE = 16
COL_BLOCKS = D_MODEL // LANES
SCATTER_BATCH = 16
V7X_VMEM_LIMIT = 56 * 1024 * 1024
TOPK_REFINE_STEPS = 8


def _cparams(sem, vmem=V7X_VMEM_LIMIT):
    return pltpu.CompilerParams(dimension_semantics=sem, vmem_limit_bytes=vmem)


def _resident(shape, index_map):
    return pl.BlockSpec(shape, index_map, pipeline_mode=pl.Buffered(1))


def _sigmoid(x):
    return 0.5 * jnp.tanh(0.5 * x) + 0.5


def _log_sigmoid(x):
    return jnp.minimum(x, 0.0) - jnp.log1p(jnp.exp(-jnp.abs(x)))


def _proj_kernel(x_ref, w_ref, wgt_ref, o_ref, gr_ref):
    xb = x_ref[...].astype(BF16)
    gr = lax.dot_general(wgt_ref[...], xb, (((1,), (1,)), ((), ())), preferred_element_type=F32)
    for p in range(HEAD_PAIRS):
        gr_ref[p] = gr[GATES_PER_PAIR * p:GATES_PER_PAIR * (p + 1), :]
    tn = PROJ_COL_STEP
    for n0 in range(0, w_ref.shape[1], tn):
        o_ref[:, n0:n0 + tn] = jnp.dot(xb, w_ref[:, n0:n0 + tn], preferred_element_type=F32).astype(o_ref.dtype)


def _proj(x2, w, wgt, tm=512):
    T, D = x2.shape
    N = w.shape[1]
    return pl.pallas_call(
        _proj_kernel,
        grid=(T // tm,),
        in_specs=[
            pl.BlockSpec((tm, D), lambda i: (i, 0)),
            _resident((D, N), lambda i: (0, 0)),
            _resident((MG_COLS, D), lambda i: (0, 0)),
        ],
        out_specs=[
            pl.BlockSpec((tm, N), lambda i: (i, 0)),
            pl.BlockSpec((HEAD_PAIRS, GATES_PER_PAIR, tm), lambda i: (0, 0, i)),
        ],
        out_shape=[
            jax.ShapeDtypeStruct((T, N), BF16),
            jax.ShapeDtypeStruct((HEAD_PAIRS, GATES_PER_PAIR, T), F32),
        ],
        compiler_params=_cparams(("parallel",)),
        name="proj",
    )(x2, w, wgt)


def _conv_silu(x, w, pad_s):
    S, C = x.shape
    half = CONV_W // 2
    halo = jnp.zeros((SUBLANES, C), F32)
    pad_s[0:SUBLANES, :] = halo
    pad_s[SUBLANES + S:2 * SUBLANES + S, :] = halo
    pad_s[SUBLANES:SUBLANES + S, :] = x
    acc = x * w[half:half + 1, :]
    for j in range(CONV_W):
        if j != half:
            acc = acc + pad_s[SUBLANES + j - half:SUBLANES + j - half + S, :] * w[j:j + 1, :]
    return acc * _sigmoid(acc)


def _mlstm_kernel(q_ref, k_ref, v_ref, o_ref, gr_ref, cwq_ref, cwk_ref, br_ref, gh_ref, out_ref,
                  q0_s, q1_s, kt_s, col_s, winter_s, wl_s, dec_s, sc_s, cst_s, call_s, pad_s, acol_s, rmat_s):
    S = q_ref.shape[0]
    L = MLSTM_LC
    NC = S // L
    DQ = MLSTM_DQK
    DV = MLSTM_DV
    hi = lax.Precision.HIGHEST
    neg = -jnp.inf

    g8 = gr_ref[...] + br_ref[...]
    li8 = g8 * LOG2E
    lf8 = pltpu.roll(_log_sigmoid(g8) * LOG2E, CHAINS, 0)
    row = lax.broadcasted_iota(jnp.int32, (GATES_PER_PAIR, L), 0)
    lane = lax.broadcasted_iota(jnp.int32, (GATES_PER_PAIR, L), 1)
    fwd = (row % CHAINS) < 2
    fwd1 = fwd[:, :1]
    si = lax.broadcasted_iota(jnp.int32, (L, 2 * L), 0)
    ti = lax.broadcasted_iota(jnp.int32, (L, 2 * L), 1)
    tri = jnp.where(ti < L, jnp.where(si <= ti, 1.0, 0.0), jnp.where(si >= ti - L, 1.0, 0.0))
    tot, mloc, b_l, cm_l, r_l = [], [], [], [], []
    for c in range(NC):
        pr = jnp.dot(lf8[:, c * L:(c + 1) * L], tri, precision=hi, preferred_element_type=F32)
        b_c = jnp.where(fwd, pr[:, :L], pr[:, L:])
        tot_c = pr[:, L - 1:L]
        li_c = li8[:, c * L:(c + 1) * L]
        r_c = li_c - b_c
        cm = r_c
        k = 1
        while k < L:
            pre = jnp.where(lane >= k, pltpu.roll(cm, k, 1), neg)
            suf = jnp.where(lane < L - k, pltpu.roll(cm, L - k, 1), neg)
            cm = jnp.maximum(cm, jnp.where(fwd, pre, suf))
            k *= 2
        g_c = tot_c - b_c + li_c
        mloc_c = jnp.max(g_c, axis=1, keepdims=True)
        wl_s[c] = jnp.exp2(g_c - mloc_c)
        r_l.append(r_c)
        tot.append(tot_c)
        mloc.append(mloc_c)
        b_l.append(b_c)
        cm_l.append(cm)

    def scan(order):
        m = jnp.zeros((GATES_PER_PAIR, 1), F32)
        m_in, dec, sc = [None] * NC, [None] * NC, [None] * NC
        for c in order:
            m_new = jnp.maximum(tot[c] + m, mloc[c])
            m_in[c] = m
            dec[c] = jnp.exp2(tot[c] + m - m_new)
            sc[c] = jnp.exp2(mloc[c] - m_new)
            m = m_new
        return m_in, dec, sc

    mf, df, sf = scan(range(NC))
    mb, db, sb = scan(range(NC - 1, -1, -1))
    def terms3(x):
        t0 = x.astype(BF16).astype(F32)
        rem = x - t0
        t1 = rem.astype(BF16).astype(F32)
        return t0, t1, (rem - t1).astype(BF16).astype(F32)

    blk_rows = 2 * SUBLANES
    rowi = lax.broadcasted_iota(jnp.int32, (blk_rows, L), 0)
    bro = lambda x, ci: jnp.broadcast_to(x[ci:ci + 1, :], (blk_rows, L))
    zblk = jnp.zeros((blk_rows, L), F32)
    pad = jnp.zeros((LANES - GATES_PER_PAIR, L), F32)
    for c in range(NC):
        m_in = jnp.where(fwd1, mf[c], mb[c])
        dec_s[c] = jnp.broadcast_to(jnp.where(fwd1, df[c], db[c]), (GATES_PER_PAIR, L))
        sc_s[c] = jnp.broadcast_to(jnp.where(fwd1, sf[c], sb[c]), (GATES_PER_PAIR, L))
        a_c = jnp.maximum(m_in, cm_l[c])
        winter_s[c] = jnp.exp2(m_in - a_c)
        emt_c = jnp.exp2(-(b_l[c] + a_c))
        col_s[c * L:(c + 1) * L, :] = jnp.concatenate([emt_c, pad], axis=0).T
        a0, a1, a2 = terms3(a_c)
        r0, r1, r2 = terms3(r_l[c])
        a_blks, r_cols = [], []
        for ci in range(CHAINS):
            a_blks.append(jnp.where(rowi == 0, bro(a0, ci), jnp.where(rowi == 1, bro(a1, ci), jnp.where(
                rowi == 2, bro(a2, ci), jnp.where(rowi < 6, 1.0, 0.0)))))
            r_blk = jnp.where(rowi < 3, -1.0, jnp.where(rowi == 3, bro(r0, ci), jnp.where(
                rowi == 4, bro(r1, ci), jnp.where(rowi == 5, bro(r2, ci), 0.0))))
            r_cols.append(jnp.concatenate([zblk] * ci + [r_blk] + [zblk] * (LANES // blk_rows - 1 - ci), axis=0))
        a_rows = jnp.concatenate(a_blks + [zblk] * (LANES // blk_rows - CHAINS), axis=0)
        acol_s[c * L:(c + 1) * L, :] = a_rows.T.astype(BF16)
        rmat_s[c] = jnp.concatenate(r_cols, axis=1).astype(BF16)

    qs = (_conv_silu(q_ref[...].astype(F32), cwq_ref[...], pad_s) * (DQ ** -0.5)).astype(BF16)
    q0_s[...] = qs[:, :DQ]
    q1_s[...] = qs[:, DQ:]
    kt = _conv_silu(k_ref[...].astype(F32), cwk_ref[...], pad_s).T
    for c in range(NC):
        kt_s[c] = kt[:, c * L:(c + 1) * L]

    ones_blk = jnp.ones((L, DV), BF16)

    def vext_of(sl, hh):
        return jnp.concatenate([v_ref[sl, DV * hh:DV * (hh + 1)], ones_blk], axis=1)

    cst_s[...] = jnp.zeros(cst_s.shape, F32)

    def state_step(c):
        for d in range(2):
            ch = c if d == 0 else NC - 1 - c
            sl = pl.ds(pl.multiple_of(ch * L, L), L)
            wl = wl_s[ch]
            ktc = kt_s[ch]
            decs = dec_s[ch]
            scs = sc_s[ch]
            for hh in range(2):
                ci = d * 2 + hh
                cx = cst_s[ci]
                call_s[ch, ci] = cx.astype(BF16)
                kw = (ktc[DQ * hh:DQ * (hh + 1), :] * wl[ci:ci + 1, :]).astype(BF16)
                cst_s[ci] = (decs[ci:ci + 1, 0:1] * cx
                             + scs[ci:ci + 1, 0:1] * jnp.dot(kw, vext_of(sl, hh), preferred_element_type=F32))

    def state_body(c, carry):
        for u in range(STATE_UNROLL):
            state_step(STATE_UNROLL * c + u)
        return carry

    lax.fori_loop(0, NC // STATE_UNROLL, state_body, 0)

    tt = lax.broadcasted_iota(jnp.int32, (L, L), 0)
    ss = lax.broadcasted_iota(jnp.int32, (L, L), 1)
    masks = (ss <= tt, ss >= tt)

    eye = tt == ss

    def chunk_out(ch, hh, sl, cols, winter, e_all):
        og = _sigmoid(o_ref[sl, DV * hh:DV * (hh + 1)].astype(F32))
        qc = (q0_s if hh == 0 else q1_s)[sl, :]
        kth = kt_s[ch, DQ * hh:DQ * (hh + 1), :].astype(BF16)
        qk = jnp.dot(qc, kth, preferred_element_type=F32)
        vext = vext_of(sl, hh)
        hsum = None
        for d in range(2):
            ci = d * 2 + hh
            w_intra = jnp.exp2(jnp.where(masks[d], e_all[:, L * ci:L * (ci + 1)], neg))
            qcx = jnp.dot(qc, call_s[ch, ci], preferred_element_type=F32).astype(BF16)
            w_diag = jnp.where(eye, winter[ci:ci + 1, :], 0.0).astype(BF16)
            nd = jnp.dot(jnp.concatenate([(qk * w_intra).astype(BF16), w_diag], axis=1),
                         jnp.concatenate([vext, qcx], axis=0),
                         preferred_element_type=F32)
            h = nd[:, :DV] / jnp.maximum(jnp.abs(nd[:, DV:]), cols[:, ci:ci + 1])
            hsum = h if hsum is None else hsum + h
        y = hsum * lax.rsqrt(jnp.mean(hsum * hsum, axis=1, keepdims=True) + NORM_EPS) * gh_ref[hh:hh + 1, :]
        out_ref[sl, DV * hh:DV * (hh + 1)] = (og * y).astype(out_ref.dtype)

    def out_chunk(c):
        sl = pl.ds(pl.multiple_of(c * L, L), L)
        e_all = jnp.dot(acol_s[sl, :], rmat_s[c], preferred_element_type=F32)
        cols = col_s[sl, :]
        winter = winter_s[c]
        chunk_out(c, 0, sl, cols, winter, e_all)
        chunk_out(c, 1, sl, cols, winter, e_all)

    def out_body(c, carry):
        for u in range(OUT_UNROLL):
            out_chunk(OUT_UNROLL * c + u)
        return carry

    lax.fori_loop(0, NC // OUT_UNROLL, out_body, 0)


def _mlstm(proj, grow, conv_w, bias_r, g_head, B, S):
    T = B * S
    NC = S // MLSTM_LC
    pw = 2 * MLSTM_DQK
    vw = 2 * MLSTM_DV
    return pl.pallas_call(
        _mlstm_kernel,
        grid=(B, HEAD_PAIRS),
        in_specs=[
            pl.BlockSpec((S, pw), lambda b, p: (b, OFF_MQ // pw + p)),
            pl.BlockSpec((S, pw), lambda b, p: (b, OFF_MK // pw + p)),
            pl.BlockSpec((S, vw), lambda b, p: (b, OFF_MV // vw + p)),
            pl.BlockSpec((S, vw), lambda b, p: (b, OFF_MO // vw + p)),
            pl.BlockSpec((None, GATES_PER_PAIR, S), lambda b, p: (p, 0, b)),
            pl.BlockSpec((CONV_W, pw), lambda b, p: (0, p)),
            pl.BlockSpec((CONV_W, pw), lambda b, p: (0, MQ_COLS // pw + p)),
            pl.BlockSpec((None, GATES_PER_PAIR, 1), lambda b, p: (p, 0, 0)),
            pl.BlockSpec((None, 2, MLSTM_DV), lambda b, p: (p, 0, 0)),
        ],
        out_specs=pl.BlockSpec((S, vw), lambda b, p: (b, p)),
        out_shape=jax.ShapeDtypeStruct((T, MLSTM_WIDTH), BF16),
        scratch_shapes=[
            pltpu.VMEM((S, MLSTM_DQK), BF16),
            pltpu.VMEM((S, MLSTM_DQK), BF16),
            pltpu.VMEM((NC, pw, MLSTM_LC), F32),
            pltpu.VMEM((S, LANES), F32),
            pltpu.VMEM((NC, GATES_PER_PAIR, MLSTM_LC), F32),
            pltpu.VMEM((NC, GATES_PER_PAIR, MLSTM_LC), F32),
            pltpu.VMEM((NC, GATES_PER_PAIR, MLSTM_LC), F32),
            pltpu.VMEM((NC, GATES_PER_PAIR, MLSTM_LC), F32),
            pltpu.VMEM((CHAINS, MLSTM_DQK, 2 * MLSTM_DV), F32),
            pltpu.VMEM((NC, CHAINS, MLSTM_DQK, 2 * MLSTM_DV), BF16),
            pltpu.VMEM((S + 2 * SUBLANES, pw), F32),
            pltpu.VMEM((S, LANES), BF16),
            pltpu.VMEM((NC, LANES, CHAINS * MLSTM_LC), BF16),
        ],
        compiler_params=_cparams(("parallel", "parallel")),
        name="mlstm",
    )(proj, proj, proj, proj, grow, conv_w, conv_w, bias_r, g_head)


def _norm_rope(x, g, cos, sin_signed):
    xn = x * lax.rsqrt(jnp.mean(x * x, axis=1, keepdims=True) + NORM_EPS) * g
    lane = lax.broadcasted_iota(jnp.int32, x.shape, 1)
    first_half = (lane % (2 * ROPE_FREQS)) < ROPE_FREQS
    partner = jnp.where(first_half,
                        pltpu.roll(xn, LANES - ROPE_FREQS, 1),
                        pltpu.roll(xn, ROPE_FREQS, 1))
    return xn * cos + partner * sin_signed


def _attn_kernel(q_ref, k_ref, v_ref, cq_ref, sq_ref, ck_ref, sk_ref, gq_ref, gk_ref, o_ref, kr_s, vx_s):
    d = ATTN_HEAD_DIM

    @pl.when(pl.program_id(2) == 0)
    def _():
        kr_s[...] = _norm_rope(k_ref[...].astype(F32), gk_ref[...], ck_ref[...], sk_ref[...]).astype(BF16)
        vx_s[...] = jnp.concatenate([v_ref[...], jnp.ones(v_ref.shape, BF16)], axis=1)

    gq = gq_ref[...]
    kr = kr_s[...]
    vx = vx_s[...]
    for r0 in range(0, q_ref.shape[0], ATTN_CHAIN_ROWS):
        rows = slice(r0, r0 + ATTN_CHAIN_ROWS)
        cq, sq = cq_ref[rows, :], sq_ref[rows, :]
        for g in range(ATTN_GROUP):
            cols = slice(d * g, d * (g + 1))
            qg = (_norm_rope(q_ref[rows, cols].astype(F32), gq, cq, sq) * (d ** -0.5 * LOG2E)).astype(BF16)
            s = lax.dot_general(qg, kr, (((1,), (1,)), ((), ())), preferred_element_type=F32)
            p = jnp.exp2(s - jnp.max(s, axis=1, keepdims=True))
            ov = jnp.dot(p.astype(BF16), vx, preferred_element_type=F32)
            o_ref[rows, cols] = (ov[:, :d] / ov[:, d:]).astype(o_ref.dtype)


def _attn(proj, cos_t, sin_t, g_q, g_k, B, S, tq=1024):
    T = B * S
    d = ATTN_HEAD_DIM
    gw = ATTN_GROUP * d
    nq = S // tq
    return pl.pallas_call(
        _attn_kernel,
        grid=(B, ATTN_KV_HEADS, nq),
        in_specs=[
            pl.BlockSpec((tq, gw), lambda b, kv, qi: (b * nq + qi, OFF_AQ // gw + kv)),
            pl.BlockSpec((S, d), lambda b, kv, qi: (b, OFF_AK // d + kv)),
            pl.BlockSpec((S, d), lambda b, kv, qi: (b, OFF_AV // d + kv)),
            pl.BlockSpec((tq, d), lambda b, kv, qi: (qi, 0)),
            pl.BlockSpec((tq, d), lambda b, kv, qi: (qi, 0)),
            _resident((S, d), lambda b, kv, qi: (0, 0)),
            _resident((S, d), lambda b, kv, qi: (0, 0)),
            _resident((1, d), lambda b, kv, qi: (0, 0)),
            _resident((1, d), lambda b, kv, qi: (0, 0)),
        ],
        out_specs=pl.BlockSpec((tq, gw), lambda b, kv, qi: (b * nq + qi, kv)),
        out_shape=jax.ShapeDtypeStruct((T, ATTN_WIDTH), BF16),
        scratch_shapes=[pltpu.VMEM((S, d), BF16), pltpu.VMEM((S, 2 * d), BF16)],
        compiler_params=_cparams(("parallel", "parallel", "arbitrary")),
        name="attn",
    )(proj, proj, proj, cos_t, sin_t, cos_t, sin_t, g_q, g_k)


def _layer_norm(y, g, b):
    mu = jnp.mean(y, axis=1, keepdims=True)
    yc = y - mu
    var = jnp.mean(yc * yc, axis=1, keepdims=True)
    return yc * lax.rsqrt(var + NORM_EPS) * g + b


def _split_bf16(x):
    hi = x.astype(BF16)
    return hi, (x - hi.astype(F32)).astype(BF16)


def _tile_perm():
    n = TOK_TILE * COL_BLOCKS
    a = lax.broadcasted_iota(jnp.int32, (n, n), 0)
    b = lax.broadcasted_iota(jnp.int32, (n, n), 1)
    t = TOK_TILE
    return jnp.where(a // t == b % t, jnp.where(a % t == b // t, 1.0, 0.0), 0.0).astype(BF16)


def _token_major(xh, perm):
    out = []
    for r0 in range(0, xh.shape[0], 2 * TOK_TILE):
        halves = [jnp.concatenate([xh[r0 + h * TOK_TILE:r0 + (h + 1) * TOK_TILE, LANES * j:LANES * (j + 1)]
                                   for j in range(COL_BLOCKS)], axis=0) for h in range(2)]
        o2 = jnp.dot(perm, jnp.concatenate(halves, axis=1), preferred_element_type=F32).astype(BF16)
        out += [o2[:, :LANES], o2[:, LANES:]]
    return jnp.concatenate(out, axis=0)


def _row_major(g_ref, perm):
    n = TOK_TILE * COL_BLOCKS
    cols = [[] for _ in range(COL_BLOCKS)]
    for r0 in range(0, g_ref.shape[0], 2 * n):
        m2 = jnp.concatenate([g_ref[r0:r0 + n, :], g_ref[r0 + n:r0 + 2 * n, :]], axis=1)
        o2 = jnp.dot(perm, m2, preferred_element_type=F32).astype(BF16)
        for j in range(COL_BLOCKS):
            blk = o2[TOK_TILE * j:TOK_TILE * (j + 1), :]
            cols[j] += [blk[:, :LANES], blk[:, LANES:]]
    return jnp.concatenate([jnp.concatenate(cj, axis=0) for cj in cols], axis=1)


def _outproj_kernel(hm_ref, ha_ref, x_ref, wt_ref, wb_ref, g_ref, b_ref, wr_ref, x1_ref, xg_ref, aff_ref):
    y = (ALPHA * x_ref[...]
         + jnp.dot(hm_ref[...], wt_ref[...], preferred_element_type=F32)
         + jnp.dot(ha_ref[...], wb_ref[...], preferred_element_type=F32))
    x1 = _layer_norm(y, g_ref[...], b_ref[...])
    x1_ref[...] = x1
    xh, xl = _split_bf16(x1)
    wh, wl = _split_bf16(wr_ref[...])
    xg_ref[...] = _token_major(xh, _tile_perm())
    nt = (((1,), (1,)), ((), ()))
    logits = (lax.dot_general(wh, xh, nt, preferred_element_type=F32)
              + lax.dot_general(wh, xl, nt, preferred_element_type=F32)
              + lax.dot_general(wl, xh, nt, preferred_element_type=F32))
    e = jnp.exp(logits - jnp.max(logits, axis=0, keepdims=True))
    aff_ref[:, 0, :] = e / jnp.sum(e, axis=0, keepdims=True)


def _outproj(hm, ha, x2, wt, wb, g, b, wr, S, tm=512):
    T, D = x2.shape
    nt = S // tm
    return pl.pallas_call(
        _outproj_kernel,
        grid=(T // tm,),
        in_specs=[
            pl.BlockSpec((tm, MLSTM_WIDTH), lambda i: (i, 0)),
            pl.BlockSpec((tm, ATTN_WIDTH), lambda i: (i, 0)),
            pl.BlockSpec((tm, D), lambda i: (i, 0)),
            _resident((MLSTM_WIDTH, D), lambda i: (0, 0)),
            _resident((ATTN_WIDTH, D), lambda i: (MLSTM_WIDTH // ATTN_WIDTH, 0)),
            _resident((1, D), lambda i: (0, 0)),
            _resident((1, D), lambda i: (0, 0)),
            _resident((N_EXPERTS, D), lambda i: (0, 0)),
        ],
        out_specs=[
            pl.BlockSpec((tm, D), lambda i: (i, 0)),
            pl.BlockSpec((tm * TOK_TILE, LANES), lambda i: (i, 0)),
            pl.BlockSpec((N_EXPERTS, None, 1, tm), lambda i: (0, i // nt, 0, i % nt)),
        ],
        out_shape=[
            jax.ShapeDtypeStruct((T, D), F32),
            jax.ShapeDtypeStruct((T * TOK_TILE, LANES), BF16),
            jax.ShapeDtypeStruct((N_EXPERTS, T // S, 1, S), F32),
        ],
        compiler_params=_cparams(("parallel",)),
        name="outproj",
    )(hm, ha, x2, wt, wb, g, b, wr)


def _excl_prefix_count(m):
    E, S = m.shape
    nb = S // LANES
    s_i = lax.broadcasted_iota(jnp.int32, (LANES, 2 * LANES), 0)
    t_i = lax.broadcasted_iota(jnp.int32, (LANES, 2 * LANES), 1)
    tri_ext = jnp.where(t_i >= LANES, 1.0, jnp.where(s_i < t_i, 1.0, 0.0)).astype(BF16)
    r_i = lax.broadcasted_iota(jnp.int32, (nb * E, nb * E), 0)
    c_i = lax.broadcasted_iota(jnp.int32, (nb * E, nb * E), 1)
    earlier = jnp.where(c_i // E < r_i // E, jnp.where(c_i % E == r_i % E, 1.0, 0.0), 0.0).astype(BF16)
    stacked = jnp.concatenate([m[:, LANES * j:LANES * (j + 1)] for j in range(nb)], axis=0)
    r = jnp.dot(stacked, tri_ext, preferred_element_type=F32)
    off = jnp.dot(earlier, r[:, LANES:].astype(BF16), preferred_element_type=F32)
    tot = r[:, :LANES] + off
    return jnp.concatenate([tot[E * j:E * (j + 1), :] for j in range(nb)], axis=1)


def _topk_kernel(aff_ref, pos_ref, idx_ref, *, cap):
    S = aff_ref.shape[2]
    a = aff_ref[:, 0, :]

    def count_ge(v):
        return jnp.sum(jnp.where(a >= v, 1.0, 0.0), axis=1, keepdims=True)

    thr = jnp.zeros((a.shape[0], 1), jnp.int32)
    for bit in range(30, -1, -1):
        cand = thr | (1 << bit)
        thr = jnp.where(count_ge(pltpu.bitcast(cand, F32)) >= cap, cand, thr)
    lo = pltpu.bitcast(thr, F32)
    hi = pltpu.bitcast(thr + 1, F32)
    for _ in range(TOPK_REFINE_STEPS):
        mid = 0.5 * (lo + hi)
        ok = count_ge(mid) >= cap
        lo = jnp.where(ok, mid, lo)
        hi = jnp.where(ok, hi, mid)
    gt = a > lo
    eq = a == lo
    need = cap - jnp.sum(jnp.where(gt, 1.0, 0.0), axis=1, keepdims=True)
    eq_rank = _excl_prefix_count(jnp.where(eq, 1.0, 0.0).astype(BF16))
    sel = jnp.logical_or(gt, jnp.logical_and(eq, eq_rank < need))
    pos = _excl_prefix_count(jnp.where(sel, 1.0, 0.0).astype(BF16))
    posm = jnp.where(sel, pos, -1.0)
    pos_ref[:, 0, :] = posm
    digit_row = lax.broadcasted_iota(jnp.int32, (2 * SUBLANES, S), 0)
    tok = lax.broadcasted_iota(jnp.int32, (2 * SUBLANES, S), 1)
    digits = jnp.where(digit_row == 0, tok // TOK_TILE,
                       jnp.where(digit_row == 1, tok % TOK_TILE, 0)).astype(F32).astype(BF16)
    slot = lax.broadcasted_iota(jnp.int32, (cap, S), 0).astype(F32)
    for e in range(a.shape[0]):
        hit = jnp.where(slot == posm[e:e + 1, :], 1.0, 0.0).astype(BF16)
        idx_ref[e] = lax.dot_general(digits, hit, (((1,), (1,)), ((), ())),
                                     preferred_element_type=F32)[:SUBLANES]


def _topk(aff, B, S, cap):
    E = aff.shape[0]
    assert S <= 256 * TOK_TILE, "token index digits must be integers bf16 holds exactly"
    return pl.pallas_call(
        functools.partial(_topk_kernel, cap=cap),
        grid=(B,),
        in_specs=[pl.BlockSpec((E, None, 1, S), lambda b: (0, b, 0, 0))],
        out_specs=[pl.BlockSpec((E, None, 1, S), lambda b: (0, b, 0, 0)),
                   pl.BlockSpec((E, None, SUBLANES, cap), lambda b: (0, b, 0, 0))],
        out_shape=[jax.ShapeDtypeStruct((E, B, 1, S), F32),
                   jax.ShapeDtypeStruct((E, B, SUBLANES, cap), F32)],
        compiler_params=_cparams(("parallel",)),
        name="topk",
    )(aff)


def _moe_ffn_kernel(idx_ref, xg_ref, pos_ref, aff_ref, wg_ref, wu_ref, wd_ref, y_ref, gbuf_s, wg_s, wu_s, wd_s,
                    *, cap, n_exp):
    e = pl.program_id(0)
    b = pl.program_id(1)

    @pl.when(e < n_exp)
    def _():
        slot = e % 2
        rg = pl.multiple_of(b * wg_ref.shape[0], wg_ref.shape[0])
        rd = pl.multiple_of(b * wd_ref.shape[0], wd_ref.shape[0])
        wg_s[slot, pl.ds(rg, wg_ref.shape[0]), :] = wg_ref[...].astype(BF16)
        wu_s[slot, pl.ds(rg, wu_ref.shape[0]), :] = wu_ref[...].astype(BF16)
        wd_s[slot, pl.ds(rd, wd_ref.shape[0]), :] = wd_ref[...].astype(BF16)

    @pl.when(e > 0)
    def _():
        S = pos_ref.shape[1]
        slot = (e - 1) % 2
        for c in range(cap):
            row = pl.multiple_of(idx_ref[0, c] * TOK_TILE, TOK_TILE)
            gbuf_s[TOK_TILE * c:TOK_TILE * (c + 1), :] = xg_ref[pl.ds(row, TOK_TILE), :]
        perm = _tile_perm()
        xs = _row_major(gbuf_s, perm)
        sl = lax.broadcasted_iota(jnp.int32, (cap, S), 0).astype(F32)
        gate = jnp.sum(jnp.where(sl == pos_ref[...], aff_ref[...], 0.0), axis=1, keepdims=True)
        hg = jnp.dot(xs, wg_s[slot], preferred_element_type=F32)
        hu = jnp.dot(xs, wu_s[slot], preferred_element_type=F32)
        hid = (hg * _sigmoid(hg) * hu).astype(BF16)
        y = (jnp.dot(hid, wd_s[slot], preferred_element_type=F32) * gate).astype(BF16)
        y_ref[...] = _token_major(y, perm)


def _moe_ffn(idx, xg, pos4, aff4, wg, wu, wd, B, S, cap):
    E, D, F = wg.shape
    prev = lambda e: jnp.maximum(e - 1, 0)
    stage = lambda e, b: (jnp.minimum(e, E - 1), jnp.where(e < E, b, B - 1), 0)
    return pl.pallas_call(
        functools.partial(_moe_ffn_kernel, cap=cap, n_exp=E),
        grid=(E + 1, B),
        in_specs=[
            pl.BlockSpec((None, None, 1, cap), lambda e, b: (prev(e), b, 0, 0), memory_space=pltpu.SMEM),
            pl.BlockSpec((S * TOK_TILE, LANES), lambda e, b: (b, 0)),
            pl.BlockSpec((None, None, 1, S), lambda e, b: (prev(e), b, 0, 0)),
            pl.BlockSpec((None, None, 1, S), lambda e, b: (prev(e), b, 0, 0)),
            pl.BlockSpec((None, D // B, F), stage),
            pl.BlockSpec((None, D // B, F), stage),
            pl.BlockSpec((None, F // B, D), stage),
        ],
        out_specs=pl.BlockSpec((None, None, cap * TOK_TILE, LANES),
                               lambda e, b: (jnp.where(e > 0, b, 0), prev(e), 0, 0)),
        out_shape=jax.ShapeDtypeStruct((B, E, cap * TOK_TILE, LANES), BF16),
        scratch_shapes=[pltpu.VMEM((cap * TOK_TILE, LANES), BF16),
                        pltpu.VMEM((2, D, F), BF16), pltpu.VMEM((2, D, F), BF16), pltpu.VMEM((2, F, D), BF16)],
        compiler_params=_cparams(("arbitrary", "arbitrary")),
        name="moe_ffn",
    )(idx, xg, pos4, aff4, wg, wu, wd)


def _row_major_f32(g_ref, r0, n_tok, perm):
    n = TOK_TILE * COL_BLOCKS
    cols = [[] for _ in range(COL_BLOCKS)]
    perm3 = jnp.concatenate([perm, perm, perm], axis=1)
    for k in range(0, n_tok * TOK_TILE, 2 * n):
        m2 = jnp.concatenate([g_ref[pl.ds(r0 + k, n), :], g_ref[pl.ds(r0 + k + n, n), :]], axis=1)
        hi = m2.astype(BF16)
        r1 = m2 - hi.astype(F32)
        mid = r1.astype(BF16)
        lo = (r1 - mid.astype(F32)).astype(BF16)
        o2 = jnp.dot(perm3, jnp.concatenate([hi, mid, lo], axis=0),
                     preferred_element_type=F32)
        for j in range(COL_BLOCKS):
            blk = o2[TOK_TILE * j:TOK_TILE * (j + 1), :]
            cols[j] += [blk[:, :LANES], blk[:, LANES:]]
    return jnp.concatenate([jnp.concatenate(cj, axis=0) for cj in cols], axis=1)


def _moe_comb_kernel(idx_ref, y_ref, o_ref, acc_s, *, cap, n_grp):
    j = pl.program_id(1)

    @pl.when(j == 0)
    def _():
        acc_s[...] = jnp.zeros(acc_s.shape, F32)

    @pl.when(j < n_grp)
    def _():
        for g in range(idx_ref.shape[0]):
            for c0 in range(0, cap, SCATTER_BATCH):
                rows = [pl.multiple_of(idx_ref[g, 0, c0 + u] * TOK_TILE, TOK_TILE) for u in range(SCATTER_BATCH)]
                new = [acc_s[pl.ds(rows[u], TOK_TILE), :]
                       + y_ref[g, TOK_TILE * (c0 + u):TOK_TILE * (c0 + u + 1), :].astype(F32)
                       for u in range(SCATTER_BATCH)]
                for u in range(SCATTER_BATCH):
                    acc_s[pl.ds(rows[u], TOK_TILE), :] = new[u]

    @pl.when(j >= n_grp)
    def _():
        n_tok = o_ref.shape[0]
        r0 = pl.multiple_of((j - n_grp) * (n_tok * TOK_TILE), n_tok * TOK_TILE)
        o_ref[...] = _row_major_f32(acc_s, r0, n_tok, _tile_perm())


def _moe_comb(idx, y, B, S, cap, chunk=1024, eg=4):
    E = y.shape[1]
    D = D_MODEL
    nch = S // chunk
    ng = E // eg
    return pl.pallas_call(
        functools.partial(_moe_comb_kernel, cap=cap, n_grp=ng),
        grid=(B, ng + nch),
        in_specs=[
            pl.BlockSpec((eg, None, 1, cap), lambda b, j: (jnp.minimum(j, ng - 1), b, 0, 0),
                         memory_space=pltpu.SMEM),
            pl.BlockSpec((None, eg, cap * TOK_TILE, LANES), lambda b, j: (b, jnp.minimum(j, ng - 1), 0, 0)),
        ],
        out_specs=pl.BlockSpec((chunk, D), lambda b, j: (b * nch + jnp.maximum(j - ng, 0), 0)),
        out_shape=jax.ShapeDtypeStruct((B * S, D), F32),
        scratch_shapes=[pltpu.VMEM((S * TOK_TILE, LANES), F32)],
        compiler_params=_cparams(("arbitrary", "arbitrary")),
        name="moe_comb",
    )(idx, y)


def _final_kernel(x1_ref, moe_ref, p_ref, wpg_ref, bpg_ref, wpp_ref, g_ref, b_ref, o_ref):
    for r0 in range(0, x1_ref.shape[0], ROW_CHAIN):
        rows = slice(r0, r0 + ROW_CHAIN)
        x1 = x1_ref[rows, :]
        gate = _sigmoid(jnp.dot(x1.astype(BF16), wpg_ref[...], preferred_element_type=F32) + bpg_ref[...])
        plv = gate * jnp.dot(p_ref[rows, :].astype(BF16), wpp_ref[...], preferred_element_type=F32)
        o_ref[rows, :] = _layer_norm(ALPHA * x1 + moe_ref[rows, :] + plv, g_ref[...], b_ref[...])


def _final(x1, moe, p2, wpg, bpg, wpp, g, b, tm=512):
    T, D = x1.shape
    return pl.pallas_call(
        _final_kernel,
        grid=(T // tm,),
        in_specs=[
            pl.BlockSpec((tm, D), lambda i: (i, 0)),
            pl.BlockSpec((tm, D), lambda i: (i, 0)),
            pl.BlockSpec((tm, P_DIM), lambda i: (i, 0)),
            _resident((D, D), lambda i: (0, 0)),
            _resident((1, D), lambda i: (0, 0)),
            _resident((P_DIM, D), lambda i: (0, 0)),
            _resident((1, D), lambda i: (0, 0)),
            _resident((1, D), lambda i: (0, 0)),
        ],
        out_specs=pl.BlockSpec((tm, D), lambda i: (i, 0)),
        out_shape=jax.ShapeDtypeStruct((T, D), F32),
        compiler_params=_cparams(("parallel",)),
        name="final",
    )(x1, moe, p2, wpg, bpg, wpp, g, b)


def _rope_tables(S):
    rows = S // GRID_W
    row_idx = jnp.broadcast_to(jnp.arange(rows, dtype=F32)[:, None], (rows, GRID_W)).reshape(-1)
    col_idx = jnp.broadcast_to(jnp.arange(GRID_W, dtype=F32)[None, :], (rows, GRID_W)).reshape(-1)
    inv_freq = ROPE_THETA ** (-jnp.arange(ROPE_FREQS, dtype=F32) / ROPE_FREQS)
    ar = row_idx[:, None] * inv_freq
    ac = col_idx[:, None] * inv_freq
    cos_t = jnp.concatenate([jnp.cos(ar), jnp.cos(ar), jnp.cos(ac), jnp.cos(ac)], axis=1)
    sin_t = jnp.concatenate([-jnp.sin(ar), jnp.sin(ar), -jnp.sin(ac), jnp.sin(ac)], axis=1)
    return cos_t, sin_t


def _pair_major(a):
    return a.reshape(2, HEAD_PAIRS, 2).transpose(1, 0, 2)


def _layer(x2, p2, B, S, w_in, conv_w, b_i, b_f, g_mlstm, g_q, g_k, w_out, ln1_g, ln1_b, w_router, w_gate, w_up,
           w_down, w_pl_proj, w_pl_gate, b_pl_gate, ln2_g, ln2_b):
    D = D_MODEL
    cap = CAPACITY_FACTOR * S // N_EXPERTS
    o_mg = 2 * MQ_COLS + 2 * MV_COLS
    w_main = jnp.concatenate([w_in[:, :o_mg], w_in[:, o_mg + MG_COLS:]], axis=1).astype(BF16)
    w_g = w_in[:, o_mg:o_mg + MG_COLS].reshape(D, 2, 2, HEAD_PAIRS, 2).transpose(0, 3, 2, 1, 4).reshape(D, MG_COLS)
    bias = jnp.stack([_pair_major(b_i), _pair_major(b_f)], axis=1).reshape(HEAD_PAIRS, GATES_PER_PAIR)

    proj, grow = _proj(x2, w_main, w_g.astype(BF16).T)
    h_m = _mlstm(proj, grow, conv_w, bias[:, :, None], g_mlstm.reshape(HEAD_PAIRS, 2, MLSTM_DV), B, S)
    cos_t, sin_t = _rope_tables(S)
    h_a = _attn(proj, cos_t, sin_t, g_q[None, :], g_k[None, :], B, S)
    w_o = w_out.astype(BF16)
    x1, xg, aff4 = _outproj(h_m, h_a, x2, w_o, w_o, ln1_g[None, :], ln1_b[None, :], w_router.T, S)
    pos4, idx_digits = _topk(aff4, B, S, cap)
    idx = (idx_digits[:, :, 0:1, :] * TOK_TILE + idx_digits[:, :, 1:2, :]).astype(jnp.int32)
    y = _moe_ffn(idx, xg, pos4, aff4, w_gate, w_up, w_down, B, S, cap)
    moe = _moe_comb(idx, y, B, S, cap)
    return _final(x1, moe, p2, w_pl_gate.astype(BF16), b_pl_gate[None, :], w_pl_proj.astype(BF16),
                  ln2_g[None, :], ln2_b[None, :])


def kernel(x, p, w_in, conv_w, b_igate, b_fgate, g_mlstm, g_q, g_k, w_out, ln1_g, ln1_b, w_router, w_gate, w_up,
           w_down, w_pl_proj, w_pl_gate, b_pl_gate, ln2_g, ln2_b):
    B, S, D = x.shape
    x2 = x.reshape(B * S, D)
    for i in range(DEPTH):
        x2 = _layer(x2, p[i].reshape(B * S, P_DIM), B, S, w_in[i], conv_w[i], b_igate[i], b_fgate[i], g_mlstm[i],
                    g_q[i], g_k[i], w_out[i], ln1_g[i], ln1_b[i], w_router[i], w_gate[i], w_up[i], w_down[i],
                    w_pl_proj[i], w_pl_gate[i], b_pl_gate[i], ln2_g[i], ln2_b[i])
    return x2.reshape(B, S, D)
```

```python
import functools

import jax
import jax.numpy as jnp
from jax import lax
from jax.experimental import pallas as pl
from jax.experimental.pallas import tpu as pltpu

F32 = jnp.float32
BF16 = jnp.bfloat16

D_MODEL = 2048
P_DIM = 256
GRID_W = 64
MLSTM_WIDTH = D_MODEL // 2
ATTN_WIDTH = D_MODEL - MLSTM_WIDTH
MLSTM_HEADS = 8
MLSTM_DV = MLSTM_WIDTH // MLSTM_HEADS
MLSTM_DQK = MLSTM_DV // 2
CONV_W = 5
ATTN_HEAD_DIM = 128
ATTN_Q_HEADS = ATTN_WIDTH // ATTN_HEAD_DIM
ATTN_KV_HEADS = 2
ATTN_GROUP = ATTN_Q_HEADS // ATTN_KV_HEADS
ROPE_FREQS = ATTN_HEAD_DIM // 4
ROPE_THETA = 10000.0
N_EXPERTS = 16
EXPERT_FF = D_MODEL // 2
CAPACITY_FACTOR = 2
NORM_EPS = 1e-6
DEPTH = 1
ALPHA = (2.0 * DEPTH) ** 0.25
LOG2E = 1.4426950408889634

MQ_COLS = MLSTM_HEADS * MLSTM_DQK
MV_COLS = MLSTM_WIDTH
MG_COLS = 2 * 2 * MLSTM_HEADS
AQ_COLS = ATTN_WIDTH
AKV_COLS = ATTN_KV_HEADS * ATTN_HEAD_DIM
PROJ_COLS = 2 * MQ_COLS + 2 * MV_COLS + AQ_COLS + 2 * AKV_COLS
OFF_MQ, OFF_MK, OFF_MV, OFF_MO = 0, MQ_COLS, 2 * MQ_COLS, 2 * MQ_COLS + MV_COLS
OFF_AQ = OFF_MO + MV_COLS
OFF_AK = OFF_AQ + AQ_COLS
OFF_AV = OFF_AK + AKV_COLS
PROJ_COL_STEP = 1536

HEAD_PAIRS = MLSTM_HEADS // 2
GATES_PER_PAIR = 8
CHAINS = 4
MLSTM_LC = 128
STATE_UNROLL = 8
OUT_UNROLL = 8
ATTN_CHAIN_ROWS = 256
ROW_CHAIN = 256
LANES = 128
SUBLANES = 8
TOK_TILE = 16
COL_BLOCKS = D_MODEL // LANES
SCATTER_BATCH = 16
V7X_VMEM_LIMIT = 56 * 1024 * 1024
TOPK_REFINE_STEPS = 8


def _cparams(sem, vmem=V7X_VMEM_LIMIT):
    return pltpu.CompilerParams(dimension_semantics=sem, vmem_limit_bytes=vmem)


def _resident(shape, index_map):
    return pl.BlockSpec(shape, index_map, pipeline_mode=pl.Buffered(1))


def _sigmoid(x):
    return 0.5 * jnp.tanh(0.5 * x) + 0.5


def _log_sigmoid(x):
    return jnp.minimum(x, 0.0) - jnp.log1p(jnp.exp(-jnp.abs(x)))


def _proj_kernel(x_ref, w_ref, wgt_ref, o_ref, gr_ref):
    xb = x_ref[...].astype(BF16)
    gr = lax.dot_general(wgt_ref[...], xb, (((1,), (1,)), ((), ())), preferred_element_type=F32)
    for p in range(HEAD_PAIRS):
        gr_ref[p] = gr[GATES_PER_PAIR * p:GATES_PER_PAIR * (p + 1), :]
    tn = PROJ_COL_STEP
    for n0 in range(0, w_ref.shape[1], tn):
        o_ref[:, n0:n0 + tn] = jnp.dot(xb, w_ref[:, n0:n0 + tn], preferred_element_type=F32).astype(o_ref.dtype)


def _proj(x2, w, wgt, tm=512):
    T, D = x2.shape
    N = w.shape[1]
    return pl.pallas_call(
        _proj_kernel,
        grid=(T // tm,),
        in_specs=[
            pl.BlockSpec((tm, D), lambda i: (i, 0)),
            _resident((D, N), lambda i: (0, 0)),
            _resident((MG_COLS, D), lambda i: (0, 0)),
        ],
        out_specs=[
            pl.BlockSpec((tm, N), lambda i: (i, 0)),
            pl.BlockSpec((HEAD_PAIRS, GATES_PER_PAIR, tm), lambda i: (0, 0, i)),
        ],
        out_shape=[
            jax.ShapeDtypeStruct((T, N), BF16),
            jax.ShapeDtypeStruct((HEAD_PAIRS, GATES_PER_PAIR, T), F32),
        ],
        compiler_params=_cparams(("parallel",)),
        name="proj",
    )(x2, w, wgt)


def _conv_silu(x, w, pad_s):
    S, C = x.shape
    half = CONV_W // 2
    halo = jnp.zeros((SUBLANES, C), F32)
    pad_s[0:SUBLANES, :] = halo
    pad_s[SUBLANES + S:2 * SUBLANES + S, :] = halo
    pad_s[SUBLANES:SUBLANES + S, :] = x
    acc = x * w[half:half + 1, :]
    for j in range(CONV_W):
        if j != half:
            acc = acc + pad_s[SUBLANES + j - half:SUBLANES + j - half + S, :] * w[j:j + 1, :]
    return acc * _sigmoid(acc)


def _mlstm_kernel(q_ref, k_ref, v_ref, o_ref, gr_ref, cwq_ref, cwk_ref, br_ref, gh_ref, out_ref,
                  q0_s, q1_s, kt_s, col_s, winter_s, wl_s, dec_s, sc_s, cst_s, call_s, pad_s, acol_s, rmat_s):
    S = q_ref.shape[0]
    L = MLSTM_LC
    NC = S // L
    DQ = MLSTM_DQK
    DV = MLSTM_DV
    hi = lax.Precision.HIGHEST
    neg = -jnp.inf

    g8 = gr_ref[...] + br_ref[...]
    li8 = g8 * LOG2E
    lf8 = pltpu.roll(_log_sigmoid(g8) * LOG2E, CHAINS, 0)
    row = lax.broadcasted_iota(jnp.int32, (GATES_PER_PAIR, L), 0)
    lane = lax.broadcasted_iota(jnp.int32, (GATES_PER_PAIR, L), 1)
    fwd = (row % CHAINS) < 2
    fwd1 = fwd[:, :1]
    si = lax.broadcasted_iota(jnp.int32, (L, 2 * L), 0)
    ti = lax.broadcasted_iota(jnp.int32, (L, 2 * L), 1)
    tri = jnp.where(ti < L, jnp.where(si <= ti, 1.0, 0.0), jnp.where(si >= ti - L, 1.0, 0.0))
    tot, mloc, b_l, cm_l, r_l = [], [], [], [], []
    for c in range(NC):
        pr = jnp.dot(lf8[:, c * L:(c + 1) * L], tri, precision=hi, preferred_element_type=F32)
        b_c = jnp.where(fwd, pr[:, :L], pr[:, L:])
        tot_c = pr[:, L - 1:L]
        li_c = li8[:, c * L:(c + 1) * L]
        r_c = li_c - b_c
        cm = r_c
        k = 1
        while k < L:
            pre = jnp.where(lane >= k, pltpu.roll(cm, k, 1), neg)
            suf = jnp.where(lane < L - k, pltpu.roll(cm, L - k, 1), neg)
            cm = jnp.maximum(cm, jnp.where(fwd, pre, suf))
            k *= 2
        g_c = tot_c - b_c + li_c
        mloc_c = jnp.max(g_c, axis=1, keepdims=True)
        wl_s[c] = jnp.exp2(g_c - mloc_c)
        r_l.append(r_c)
        tot.append(tot_c)
        mloc.append(mloc_c)
        b_l.append(b_c)
        cm_l.append(cm)

    def scan(order):
        m = jnp.zeros((GATES_PER_PAIR, 1), F32)
        m_in, dec, sc = [None] * NC, [None] * NC, [None] * NC
        for c in order:
            m_new = jnp.maximum(tot[c] + m, mloc[c])
            m_in[c] = m
            dec[c] = jnp.exp2(tot[c] + m - m_new)
            sc[c] = jnp.exp2(mloc[c] - m_new)
            m = m_new
        return m_in, dec, sc

    mf, df, sf = scan(range(NC))
    mb, db, sb = scan(range(NC - 1, -1, -1))
    def terms3(x):
        t0 = x.astype(BF16).astype(F32)
        rem = x - t0
        t1 = rem.astype(BF16).astype(F32)
        return t0, t1, (rem - t1).astype(BF16).astype(F32)

    blk_rows = 2 * SUBLANES
    rowi = lax.broadcasted_iota(jnp.int32, (blk_rows, L), 0)
    bro = lambda x, ci: jnp.broadcast_to(x[ci:ci + 1, :], (blk_rows, L))
    zblk = jnp.zeros((blk_rows, L), F32)
    pad = jnp.zeros((LANES - GATES_PER_PAIR, L), F32)
    for c in range(NC):
        m_in = jnp.where(fwd1, mf[c], mb[c])
        dec_s[c] = jnp.broadcast_to(jnp.where(fwd1, df[c], db[c]), (GATES_PER_PAIR, L))
        sc_s[c] = jnp.broadcast_to(jnp.where(fwd1, sf[c], sb[c]), (GATES_PER_PAIR, L))
        a_c = jnp.maximum(m_in, cm_l[c])
        winter_s[c] = jnp.exp2(m_in - a_c)
        emt_c = jnp.exp2(-(b_l[c] + a_c))
        col_s[c * L:(c + 1) * L, :] = jnp.concatenate([emt_c, pad], axis=0).T
        a0, a1, a2 = terms3(a_c)
        r0, r1, r2 = terms3(r_l[c])
        a_blks, r_cols = [], []
        for ci in range(CHAINS):
            a_blks.append(jnp.where(rowi == 0, bro(a0, ci), jnp.where(rowi == 1, bro(a1, ci), jnp.where(
                rowi == 2, bro(a2, ci), jnp.where(rowi < 6, 1.0, 0.0)))))
            r_blk = jnp.where(rowi < 3, -1.0, jnp.where(rowi == 3, bro(r0, ci), jnp.where(
                rowi == 4, bro(r1, ci), jnp.where(rowi == 5, bro(r2, ci), 0.0))))
            r_cols.append(jnp.concatenate([zblk] * ci + [r_blk] + [zblk] * (LANES // blk_rows - 1 - ci), axis=0))
        a_rows = jnp.concatenate(a_blks + [zblk] * (LANES // blk_rows - CHAINS), axis=0)
        acol_s[c * L:(c + 1) * L, :] = a_rows.T.astype(BF16)
        rmat_s[c] = jnp.concatenate(r_cols, axis=1).astype(BF16)

    qs = (_conv_silu(q_ref[...].astype(F32), cwq_ref[...], pad_s) * (DQ ** -0.5)).astype(BF16)
    q0_s[...] = qs[:, :DQ]
    q1_s[...] = qs[:, DQ:]
    kt = _conv_silu(k_ref[...].astype(F32), cwk_ref[...], pad_s).T
    for c in range(NC):
        kt_s[c] = kt[:, c * L:(c + 1) * L]

    ones_blk = jnp.ones((L, DV), BF16)

    def vext_of(sl, hh):
        return jnp.concatenate([v_ref[sl, DV * hh:DV * (hh + 1)], ones_blk], axis=1)

    cst_s[...] = jnp.zeros(cst_s.shape, F32)

    def state_step(c):
        for d in range(2):
            ch = c if d == 0 else NC - 1 - c
            sl = pl.ds(pl.multiple_of(ch * L, L), L)
            wl = wl_s[ch]
            ktc = kt_s[ch]
            decs = dec_s[ch]
            scs = sc_s[ch]
            for hh in range(2):
                ci = d * 2 + hh
                cx = cst_s[ci]
                call_s[ch, ci] = cx.astype(BF16)
                kw = (ktc[DQ * hh:DQ * (hh + 1), :] * wl[ci:ci + 1, :]).astype(BF16)
                cst_s[ci] = (decs[ci:ci + 1, 0:1] * cx
                             + scs[ci:ci + 1, 0:1] * jnp.dot(kw, vext_of(sl, hh), preferred_element_type=F32))

    def state_body(c, carry):
        for u in range(STATE_UNROLL):
            state_step(STATE_UNROLL * c + u)
        return carry

    lax.fori_loop(0, NC // STATE_UNROLL, state_body, 0)

    tt = lax.broadcasted_iota(jnp.int32, (L, L), 0)
    ss = lax.broadcasted_iota(jnp.int32, (L, L), 1)
    masks = (ss <= tt, ss >= tt)

    eye = tt == ss

    def chunk_out(ch, hh, sl, cols, winter, e_all):
        og = _sigmoid(o_ref[sl, DV * hh:DV * (hh + 1)].astype(F32))
        qc = (q0_s if hh == 0 else q1_s)[sl, :]
        kth = kt_s[ch, DQ * hh:DQ * (hh + 1), :].astype(BF16)
        qk = jnp.dot(qc, kth, preferred_element_type=F32)
        vext = vext_of(sl, hh)
        hsum = None
        for d in range(2):
            ci = d * 2 + hh
            w_intra = jnp.exp2(jnp.where(masks[d], e_all[:, L * ci:L * (ci + 1)], neg))
            qcx = jnp.dot(qc, call_s[ch, ci], preferred_element_type=F32).astype(BF16)
            w_diag = jnp.where(eye, winter[ci:ci + 1, :], 0.0).astype(BF16)
            nd = jnp.dot(jnp.concatenate([(qk * w_intra).astype(BF16), w_diag], axis=1),
                         jnp.concatenate([vext, qcx], axis=0),
                         preferred_element_type=F32)
            h = nd[:, :DV] / jnp.maximum(jnp.abs(nd[:, DV:]), cols[:, ci:ci + 1])
            hsum = h if hsum is None else hsum + h
        y = hsum * lax.rsqrt(jnp.mean(hsum * hsum, axis=1, keepdims=True) + NORM_EPS) * gh_ref[hh:hh + 1, :]
        out_ref[sl, DV * hh:DV * (hh + 1)] = (og * y).astype(out_ref.dtype)

    def out_chunk(c):
        sl = pl.ds(pl.multiple_of(c * L, L), L)
        e_all = jnp.dot(acol_s[sl, :], rmat_s[c], preferred_element_type=F32)
        cols = col_s[sl, :]
        winter = winter_s[c]
        chunk_out(c, 0, sl, cols, winter, e_all)
        chunk_out(c, 1, sl, cols, winter, e_all)

    def out_body(c, carry):
        for u in range(OUT_UNROLL):
            out_chunk(OUT_UNROLL * c + u)
        return carry

    lax.fori_loop(0, NC // OUT_UNROLL, out_body, 0)


def _mlstm(proj, grow, conv_w, bias_r, g_head, B, S):
    T = B * S
    NC = S // MLSTM_LC
    pw = 2 * MLSTM_DQK
    vw = 2 * MLSTM_DV
    return pl.pallas_call(
        _mlstm_kernel,
        grid=(B, HEAD_PAIRS),
        in_specs=[
            pl.BlockSpec((S, pw), lambda b, p: (b, OFF_MQ // pw + p)),
            pl.BlockSpec((S, pw), lambda b, p: (b, OFF_MK // pw + p)),
            pl.BlockSpec((S, vw), lambda b, p: (b, OFF_MV // vw + p)),
            pl.BlockSpec((S, vw), lambda b, p: (b, OFF_MO // vw + p)),
            pl.BlockSpec((None, GATES_PER_PAIR, S), lambda b, p: (p, 0, b)),
            pl.BlockSpec((CONV_W, pw), lambda b, p: (0, p)),
            pl.BlockSpec((CONV_W, pw), lambda b, p: (0, MQ_COLS // pw + p)),
            pl.BlockSpec((None, GATES_PER_PAIR, 1), lambda b, p: (p, 0, 0)),
            pl.BlockSpec((None, 2, MLSTM_DV), lambda b, p: (p, 0, 0)),
        ],
        out_specs=pl.BlockSpec((S, vw), lambda b, p: (b, p)),
        out_shape=jax.ShapeDtypeStruct((T, MLSTM_WIDTH), BF16),
        scratch_shapes=[
            pltpu.VMEM((S, MLSTM_DQK), BF16),
            pltpu.VMEM((S, MLSTM_DQK), BF16),
            pltpu.VMEM((NC, pw, MLSTM_LC), F32),
            pltpu.VMEM((S, LANES), F32),
            pltpu.VMEM((NC, GATES_PER_PAIR, MLSTM_LC), F32),
            pltpu.VMEM((NC, GATES_PER_PAIR, MLSTM_LC), F32),
            pltpu.VMEM((NC, GATES_PER_PAIR, MLSTM_LC), F32),
            pltpu.VMEM((NC, GATES_PER_PAIR, MLSTM_LC), F32),
            pltpu.VMEM((CHAINS, MLSTM_DQK, 2 * MLSTM_DV), F32),
            pltpu.VMEM((NC, CHAINS, MLSTM_DQK, 2 * MLSTM_DV), BF16),
            pltpu.VMEM((S + 2 * SUBLANES, pw), F32),
            pltpu.VMEM((S, LANES), BF16),
            pltpu.VMEM((NC, LANES, CHAINS * MLSTM_LC), BF16),
        ],
        compiler_params=_cparams(("parallel", "parallel")),
        name="mlstm",
    )(proj, proj, proj, proj, grow, conv_w, conv_w, bias_r, g_head)


def _norm_rope(x, g, cos, sin_signed):
    xn = x * lax.rsqrt(jnp.mean(x * x, axis=1, keepdims=True) + NORM_EPS) * g
    lane = lax.broadcasted_iota(jnp.int32, x.shape, 1)
    first_half = (lane % (2 * ROPE_FREQS)) < ROPE_FREQS
    partner = jnp.where(first_half,
                        pltpu.roll(xn, LANES - ROPE_FREQS, 1),
                        pltpu.roll(xn, ROPE_FREQS, 1))
    return xn * cos + partner * sin_signed


def _attn_kernel(q_ref, k_ref, v_ref, cq_ref, sq_ref, ck_ref, sk_ref, gq_ref, gk_ref, o_ref, kr_s, vx_s):
    d = ATTN_HEAD_DIM

    @pl.when(pl.program_id(2) == 0)
    def _():
        kr_s[...] = _norm_rope(k_ref[...].astype(F32), gk_ref[...], ck_ref[...], sk_ref[...]).astype(BF16)
        vx_s[...] = jnp.concatenate([v_ref[...], jnp.ones(v_ref.shape, BF16)], axis=1)

    gq = gq_ref[...]
    kr = kr_s[...]
    vx = vx_s[...]
    for r0 in range(0, q_ref.shape[0], ATTN_CHAIN_ROWS):
        rows = slice(r0, r0 + ATTN_CHAIN_ROWS)
        cq, sq = cq_ref[rows, :], sq_ref[rows, :]
        for g in range(ATTN_GROUP):
            cols = slice(d * g, d * (g + 1))
            qg = (_norm_rope(q_ref[rows, cols].astype(F32), gq, cq, sq) * (d ** -0.5 * LOG2E)).astype(BF16)
            s = lax.dot_general(qg, kr, (((1,), (1,)), ((), ())), preferred_element_type=F32)
            p = jnp.exp2(s - jnp.max(s, axis=1, keepdims=True))
            ov = jnp.dot(p.astype(BF16), vx, preferred_element_type=F32)
            o_ref[rows, cols] = (ov[:, :d] / ov[:, d:]).astype(o_ref.dtype)


def _attn(proj, cos_t, sin_t, g_q, g_k, B, S, tq=1024):
    T = B * S
    d = ATTN_HEAD_DIM
    gw = ATTN_GROUP * d
    nq = S // tq
    return pl.pallas_call(
        _attn_kernel,
        grid=(B, ATTN_KV_HEADS, nq),
        in_specs=[
            pl.BlockSpec((tq, gw), lambda b, kv, qi: (b * nq + qi, OFF_AQ // gw + kv)),
            pl.BlockSpec((S, d), lambda b, kv, qi: (b, OFF_AK // d + kv)),
            pl.BlockSpec((S, d), lambda b, kv, qi: (b, OFF_AV // d + kv)),
            pl.BlockSpec((tq, d), lambda b, kv, qi: (qi, 0)),
            pl.BlockSpec((tq, d), lambda b, kv, qi: (qi, 0)),
            _resident((S, d), lambda b, kv, qi: (0, 0)),
            _resident((S, d), lambda b, kv, qi: (0, 0)),
            _resident((1, d), lambda b, kv, qi: (0, 0)),
            _resident((1, d), lambda b, kv, qi: (0, 0)),
        ],
        out_specs=pl.BlockSpec((tq, gw), lambda b, kv, qi: (b * nq + qi, kv)),
        out_shape=jax.ShapeDtypeStruct((T, ATTN_WIDTH), BF16),
        scratch_shapes=[pltpu.VMEM((S, d), BF16), pltpu.VMEM((S, 2 * d), BF16)],
        compiler_params=_cparams(("parallel", "parallel", "arbitrary")),
        name="attn",
    )(proj, proj, proj, cos_t, sin_t, cos_t, sin_t, g_q, g_k)


def _layer_norm(y, g, b):
    mu = jnp.mean(y, axis=1, keepdims=True)
    yc = y - mu
    var = jnp.mean(yc * yc, axis=1, keepdims=True)
    return yc * lax.rsqrt(var + NORM_EPS) * g + b


def _split_bf16(x):
    hi = x.astype(BF16)
    return hi, (x - hi.astype(F32)).astype(BF16)


def _tile_perm():
    n = TOK_TILE * COL_BLOCKS
    a = lax.broadcasted_iota(jnp.int32, (n, n), 0)
    b = lax.broadcasted_iota(jnp.int32, (n, n), 1)
    t = TOK_TILE
    return jnp.where(a // t == b % t, jnp.where(a % t == b // t, 1.0, 0.0), 0.0).astype(BF16)


def _token_major(xh, perm):
    out = []
    for r0 in range(0, xh.shape[0], 2 * TOK_TILE):
        halves = [jnp.concatenate([xh[r0 + h * TOK_TILE:r0 + (h + 1) * TOK_TILE, LANES * j:LANES * (j + 1)]
                                   for j in range(COL_BLOCKS)], axis=0) for h in range(2)]
        o2 = jnp.dot(perm, jnp.concatenate(halves, axis=1), preferred_element_type=F32).astype(BF16)
        out += [o2[:, :LANES], o2[:, LANES:]]
    return jnp.concatenate(out, axis=0)


def _row_major(g_ref, perm):
    n = TOK_TILE * COL_BLOCKS
    cols = [[] for _ in range(COL_BLOCKS)]
    for r0 in range(0, g_ref.shape[0], 2 * n):
        m2 = jnp.concatenate([g_ref[r0:r0 + n, :], g_ref[r0 + n:r0 + 2 * n, :]], axis=1)
        o2 = jnp.dot(perm, m2, preferred_element_type=F32).astype(BF16)
        for j in range(COL_BLOCKS):
            blk = o2[TOK_TILE * j:TOK_TILE * (j + 1), :]
            cols[j] += [blk[:, :LANES], blk[:, LANES:]]
    return jnp.concatenate([jnp.concatenate(cj, axis=0) for cj in cols], axis=1)


def _outproj_kernel(hm_ref, ha_ref, x_ref, wt_ref, wb_ref, g_ref, b_ref, wr_ref, x1_ref, xg_ref, aff_ref):
    y = (ALPHA * x_ref[...]
         + jnp.dot(hm_ref[...], wt_ref[...], preferred_element_type=F32)
         + jnp.dot(ha_ref[...], wb_ref[...], preferred_element_type=F32))
    x1 = _layer_norm(y, g_ref[...], b_ref[...])
    x1_ref[...] = x1
    xh, xl = _split_bf16(x1)
    wh, wl = _split_bf16(wr_ref[...])
    xg_ref[...] = _token_major(xh, _tile_perm())
    nt = (((1,), (1,)), ((), ()))
    logits = (lax.dot_general(wh, xh, nt, preferred_element_type=F32)
              + lax.dot_general(wh, xl, nt, preferred_element_type=F32)
              + lax.dot_general(wl, xh, nt, preferred_element_type=F32))
    e = jnp.exp(logits - jnp.max(logits, axis=0, keepdims=True))
    aff_ref[:, 0, :] = e / jnp.sum(e, axis=0, keepdims=True)


def _outproj(hm, ha, x2, wt, wb, g, b, wr, S, tm=512):
    T, D = x2.shape
    nt = S // tm
    return pl.pallas_call(
        _outproj_kernel,
        grid=(T // tm,),
        in_specs=[
            pl.BlockSpec((tm, MLSTM_WIDTH), lambda i: (i, 0)),
            pl.BlockSpec((tm, ATTN_WIDTH), lambda i: (i, 0)),
            pl.BlockSpec((tm, D), lambda i: (i, 0)),
            _resident((MLSTM_WIDTH, D), lambda i: (0, 0)),
            _resident((ATTN_WIDTH, D), lambda i: (MLSTM_WIDTH // ATTN_WIDTH, 0)),
            _resident((1, D), lambda i: (0, 0)),
            _resident((1, D), lambda i: (0, 0)),
            _resident((N_EXPERTS, D), lambda i: (0, 0)),
        ],
        out_specs=[
            pl.BlockSpec((tm, D), lambda i: (i, 0)),
            pl.BlockSpec((tm * TOK_TILE, LANES), lambda i: (i, 0)),
            pl.BlockSpec((N_EXPERTS, None, 1, tm), lambda i: (0, i // nt, 0, i % nt)),
        ],
        out_shape=[
            jax.ShapeDtypeStruct((T, D), F32),
            jax.ShapeDtypeStruct((T * TOK_TILE, LANES), BF16),
            jax.ShapeDtypeStruct((N_EXPERTS, T // S, 1, S), F32),
        ],
        compiler_params=_cparams(("parallel",)),
        name="outproj",
    )(hm, ha, x2, wt, wb, g, b, wr)


def _excl_prefix_count(m):
    E, S = m.shape
    nb = S // LANES
    s_i = lax.broadcasted_iota(jnp.int32, (LANES, 2 * LANES), 0)
    t_i = lax.broadcasted_iota(jnp.int32, (LANES, 2 * LANES), 1)
    tri_ext = jnp.where(t_i >= LANES, 1.0, jnp.where(s_i < t_i, 1.0, 0.0)).astype(BF16)
    r_i = lax.broadcasted_iota(jnp.int32, (nb * E, nb * E), 0)
    c_i = lax.broadcasted_iota(jnp.int32, (nb * E, nb * E), 1)
    earlier = jnp.where(c_i // E < r_i // E, jnp.where(c_i % E == r_i % E, 1.0, 0.0), 0.0).astype(BF16)
    stacked = jnp.concatenate([m[:, LANES * j:LANES * (j + 1)] for j in range(nb)], axis=0)
    r = jnp.dot(stacked, tri_ext, preferred_element_type=F32)
    off = jnp.dot(earlier, r[:, LANES:].astype(BF16), preferred_element_type=F32)
    tot = r[:, :LANES] + off
    return jnp.concatenate([tot[E * j:E * (j + 1), :] for j in range(nb)], axis=1)


def _topk_kernel(aff_ref, pos_ref, idx_ref, *, cap):
    S = aff_ref.shape[2]
    a = aff_ref[:, 0, :]

    def count_ge(v):
        return jnp.sum(jnp.where(a >= v, 1.0, 0.0), axis=1, keepdims=True)

    def keep_if_enough(cand, thr):
        return jnp.where(count_ge(pltpu.bitcast(cand, F32)) >= cap, cand, thr)

    thr = keep_if_enough(jnp.full((a.shape[0], 1), 1 << 30, jnp.int32), jnp.zeros((a.shape[0], 1), jnp.int32))
    for bit in range(28, -1, -2):
        b1, b0 = 1 << (bit + 1), 1 << bit
        thr = keep_if_enough(thr | b1 | b0, keep_if_enough(thr | b1, keep_if_enough(thr | b0, thr)))
    lo = pltpu.bitcast(thr, F32)
    hi = pltpu.bitcast(thr + 1, F32)
    for _ in range(TOPK_REFINE_STEPS):
        mid = 0.5 * (lo + hi)
        ok = count_ge(mid) >= cap
        lo = jnp.where(ok, mid, lo)
        hi = jnp.where(ok, hi, mid)
    gt = a > lo
    eq = a == lo
    need = cap - jnp.sum(jnp.where(gt, 1.0, 0.0), axis=1, keepdims=True)
    eq_rank = _excl_prefix_count(jnp.where(eq, 1.0, 0.0).astype(BF16))
    sel = jnp.logical_or(gt, jnp.logical_and(eq, eq_rank < need))
    pos = _excl_prefix_count(jnp.where(sel, 1.0, 0.0).astype(BF16))
    posm = jnp.where(sel, pos, -1.0)
    pos_ref[:, 0, :] = posm
    digit_row = lax.broadcasted_iota(jnp.int32, (2 * SUBLANES, S), 0)
    tok = lax.broadcasted_iota(jnp.int32, (2 * SUBLANES, S), 1)
    digits = jnp.where(digit_row == 0, tok // TOK_TILE,
                       jnp.where(digit_row == 1, tok % TOK_TILE, 0)).astype(F32).astype(BF16)
    slot = lax.broadcasted_iota(jnp.int32, (cap, S), 0).astype(F32)
    for e in range(a.shape[0]):
        hit = jnp.where(slot == posm[e:e + 1, :], 1.0, 0.0).astype(BF16)
        idx_ref[e] = lax.dot_general(digits, hit, (((1,), (1,)), ((), ())),
                                     preferred_element_type=F32)[:SUBLANES]


def _topk(aff, B, S, cap):
    E = aff.shape[0]
    assert S <= 256 * TOK_TILE, "token index digits must be integers bf16 holds exactly"
    return pl.pallas_call(
        functools.partial(_topk_kernel, cap=cap),
        grid=(B,),
        in_specs=[pl.BlockSpec((E, None, 1, S), lambda b: (0, b, 0, 0))],
        out_specs=[pl.BlockSpec((E, None, 1, S), lambda b: (0, b, 0, 0)),
                   pl.BlockSpec((E, None, SUBLANES, cap), lambda b: (0, b, 0, 0))],
        out_shape=[jax.ShapeDtypeStruct((E, B, 1, S), F32),
                   jax.ShapeDtypeStruct((E, B, SUBLANES, cap), F32)],
        compiler_params=_cparams(("parallel",)),
        name="topk",
    )(aff)


def _moe_ffn_kernel(idx_ref, xg_ref, pos_ref, aff_ref, wg_ref, wu_ref, wd_ref, y_ref, gbuf_s, wg_s, wu_s, wd_s,
                    *, cap, n_exp):
    e = pl.program_id(0)
    b = pl.program_id(1)

    @pl.when(e < n_exp)
    def _():
        slot = e % 2
        rg = pl.multiple_of(b * wg_ref.shape[0], wg_ref.shape[0])
        rd = pl.multiple_of(b * wd_ref.shape[0], wd_ref.shape[0])
        wg_s[slot, pl.ds(rg, wg_ref.shape[0]), :] = wg_ref[...].astype(BF16)
        wu_s[slot, pl.ds(rg, wu_ref.shape[0]), :] = wu_ref[...].astype(BF16)
        wd_s[slot, pl.ds(rd, wd_ref.shape[0]), :] = wd_ref[...].astype(BF16)

    @pl.when(e > 0)
    def _():
        S = pos_ref.shape[1]
        slot = (e - 1) % 2
        for c in range(cap):
            row = pl.multiple_of(idx_ref[0, c] * TOK_TILE, TOK_TILE)
            gbuf_s[TOK_TILE * c:TOK_TILE * (c + 1), :] = xg_ref[pl.ds(row, TOK_TILE), :]
        perm = _tile_perm()
        xs = _row_major(gbuf_s, perm)
        sl = lax.broadcasted_iota(jnp.int32, (cap, S), 0).astype(F32)
        gate = jnp.sum(jnp.where(sl == pos_ref[...], aff_ref[...], 0.0), axis=1, keepdims=True)
        hg = jnp.dot(xs, wg_s[slot], preferred_element_type=F32)
        hu = jnp.dot(xs, wu_s[slot], preferred_element_type=F32)
        hid = (hg * _sigmoid(hg) * hu).astype(BF16)
        y = (jnp.dot(hid, wd_s[slot], preferred_element_type=F32) * gate).astype(BF16)
        y_ref[...] = _token_major(y, perm)


def _moe_ffn(idx, xg, pos4, aff4, wg, wu, wd, B, S, cap):
    E, D, F = wg.shape
    prev = lambda e: jnp.maximum(e - 1, 0)
    stage = lambda e, b: (jnp.minimum(e, E - 1), jnp.where(e < E, b, B - 1), 0)
    return pl.pallas_call(
        functools.partial(_moe_ffn_kernel, cap=cap, n_exp=E),
        grid=(E + 1, B),
        in_specs=[
            pl.BlockSpec((None, None, 1, cap), lambda e, b: (prev(e), b, 0, 0), memory_space=pltpu.SMEM),
            pl.BlockSpec((S * TOK_TILE, LANES), lambda e, b: (b, 0)),
            pl.BlockSpec((None, None, 1, S), lambda e, b: (prev(e), b, 0, 0)),
            pl.BlockSpec((None, None, 1, S), lambda e, b: (prev(e), b, 0, 0)),
            pl.BlockSpec((None, D // B, F), stage),
            pl.BlockSpec((None, D // B, F), stage),
            pl.BlockSpec((None, F // B, D), stage),
        ],
        out_specs=pl.BlockSpec((None, None, cap * TOK_TILE, LANES),
                               lambda e, b: (jnp.where(e > 0, b, 0), prev(e), 0, 0)),
        out_shape=jax.ShapeDtypeStruct((B, E, cap * TOK_TILE, LANES), BF16),
        scratch_shapes=[pltpu.VMEM((cap * TOK_TILE, LANES), BF16),
                        pltpu.VMEM((2, D, F), BF16), pltpu.VMEM((2, D, F), BF16), pltpu.VMEM((2, F, D), BF16)],
        compiler_params=_cparams(("arbitrary", "arbitrary")),
        name="moe_ffn",
    )(idx, xg, pos4, aff4, wg, wu, wd)


def _row_major_f32(g_ref, r0, n_tok, perm):
    n = TOK_TILE * COL_BLOCKS
    cols = [[] for _ in range(COL_BLOCKS)]
    perm3 = jnp.concatenate([perm, perm, perm], axis=1)
    for k in range(0, n_tok * TOK_TILE, 2 * n):
        m2 = jnp.concatenate([g_ref[pl.ds(r0 + k, n), :], g_ref[pl.ds(r0 + k + n, n), :]], axis=1)
        hi = m2.astype(BF16)
        r1 = m2 - hi.astype(F32)
        mid = r1.astype(BF16)
        lo = (r1 - mid.astype(F32)).astype(BF16)
        o2 = jnp.dot(perm3, jnp.concatenate([hi, mid, lo], axis=0),
                     preferred_element_type=F32)
        for j in range(COL_BLOCKS):
            blk = o2[TOK_TILE * j:TOK_TILE * (j + 1), :]
            cols[j] += [blk[:, :LANES], blk[:, LANES:]]
    return jnp.concatenate([jnp.concatenate(cj, axis=0) for cj in cols], axis=1)


def _moe_comb_kernel(idx_ref, y_ref, o_ref, acc_s, *, cap, n_grp):
    j = pl.program_id(1)

    @pl.when(j == 0)
    def _():
        acc_s[...] = jnp.zeros(acc_s.shape, F32)

    @pl.when(j < n_grp)
    def _():
        for g in range(idx_ref.shape[0]):
            for c0 in range(0, cap, SCATTER_BATCH):
                rows = [pl.multiple_of(idx_ref[g, 0, c0 + u] * TOK_TILE, TOK_TILE) for u in range(SCATTER_BATCH)]
                new = [acc_s[pl.ds(rows[u], TOK_TILE), :]
                       + y_ref[g, TOK_TILE * (c0 + u):TOK_TILE * (c0 + u + 1), :].astype(F32)
                       for u in range(SCATTER_BATCH)]
                for u in range(SCATTER_BATCH):
                    acc_s[pl.ds(rows[u], TOK_TILE), :] = new[u]

    @pl.when(j >= n_grp)
    def _():
        n_tok = o_ref.shape[0]
        r0 = pl.multiple_of((j - n_grp) * (n_tok * TOK_TILE), n_tok * TOK_TILE)
        o_ref[...] = _row_major_f32(acc_s, r0, n_tok, _tile_perm())


def _moe_comb(idx, y, B, S, cap, chunk=1024, eg=4):
    E = y.shape[1]
    D = D_MODEL
    nch = S // chunk
    ng = E // eg
    return pl.pallas_call(
        functools.partial(_moe_comb_kernel, cap=cap, n_grp=ng),
        grid=(B, ng + nch),
        in_specs=[
            pl.BlockSpec((eg, None, 1, cap), lambda b, j: (jnp.minimum(j, ng - 1), b, 0, 0),
                         memory_space=pltpu.SMEM),
            pl.BlockSpec((None, eg, cap * TOK_TILE, LANES), lambda b, j: (b, jnp.minimum(j, ng - 1), 0, 0)),
        ],
        out_specs=pl.BlockSpec((chunk, D), lambda b, j: (b * nch + jnp.maximum(j - ng, 0), 0)),
        out_shape=jax.ShapeDtypeStruct((B * S, D), F32),
        scratch_shapes=[pltpu.VMEM((S * TOK_TILE, LANES), F32)],
        compiler_params=_cparams(("arbitrary", "arbitrary")),
        name="moe_comb",
    )(idx, y)


def _final_kernel(x1_ref, moe_ref, p_ref, wpg_ref, bpg_ref, wpp_ref, g_ref, b_ref, o_ref):
    for r0 in range(0, x1_ref.shape[0], ROW_CHAIN):
        rows = slice(r0, r0 + ROW_CHAIN)
        x1 = x1_ref[rows, :]
        gate = _sigmoid(jnp.dot(x1.astype(BF16), wpg_ref[...], preferred_element_type=F32) + bpg_ref[...])
        plv = gate * jnp.dot(p_ref[rows, :].astype(BF16), wpp_ref[...], preferred_element_type=F32)
        o_ref[rows, :] = _layer_norm(ALPHA * x1 + moe_ref[rows, :] + plv, g_ref[...], b_ref[...])


def _final(x1, moe, p2, wpg, bpg, wpp, g, b, tm=512):
    T, D = x1.shape
    return pl.pallas_call(
        _final_kernel,
        grid=(T // tm,),
        in_specs=[
            pl.BlockSpec((tm, D), lambda i: (i, 0)),
            pl.BlockSpec((tm, D), lambda i: (i, 0)),
            pl.BlockSpec((tm, P_DIM), lambda i: (i, 0)),
            _resident((D, D), lambda i: (0, 0)),
            _resident((1, D), lambda i: (0, 0)),
            _resident((P_DIM, D), lambda i: (0, 0)),
            _resident((1, D), lambda i: (0, 0)),
            _resident((1, D), lambda i: (0, 0)),
        ],
        out_specs=pl.BlockSpec((tm, D), lambda i: (i, 0)),
        out_shape=jax.ShapeDtypeStruct((T, D), F32),
        compiler_params=_cparams(("parallel",)),
        name="final",
    )(x1, moe, p2, wpg, bpg, wpp, g, b)


def _rope_tables(S):
    rows = S // GRID_W
    row_idx = jnp.broadcast_to(jnp.arange(rows, dtype=F32)[:, None], (rows, GRID_W)).reshape(-1)
    col_idx = jnp.broadcast_to(jnp.arange(GRID_W, dtype=F32)[None, :], (rows, GRID_W)).reshape(-1)
    inv_freq = ROPE_THETA ** (-jnp.arange(ROPE_FREQS, dtype=F32) / ROPE_FREQS)
    ar = row_idx[:, None] * inv_freq
    ac = col_idx[:, None] * inv_freq
    cos_t = jnp.concatenate([jnp.cos(ar), jnp.cos(ar), jnp.cos(ac), jnp.cos(ac)], axis=1)
    sin_t = jnp.concatenate([-jnp.sin(ar), jnp.sin(ar), -jnp.sin(ac), jnp.sin(ac)], axis=1)
    return cos_t, sin_t


def _pair_major(a):
    return a.reshape(2, HEAD_PAIRS, 2).transpose(1, 0, 2)


def _layer(x2, p2, B, S, w_in, conv_w, b_i, b_f, g_mlstm, g_q, g_k, w_out, ln1_g, ln1_b, w_router, w_gate, w_up,
           w_down, w_pl_proj, w_pl_gate, b_pl_gate, ln2_g, ln2_b):
    D = D_MODEL
    cap = CAPACITY_FACTOR * S // N_EXPERTS
    o_mg = 2 * MQ_COLS + 2 * MV_COLS
    w_main = jnp.concatenate([w_in[:, :o_mg], w_in[:, o_mg + MG_COLS:]], axis=1).astype(BF16)
    w_g = w_in[:, o_mg:o_mg + MG_COLS].reshape(D, 2, 2, HEAD_PAIRS, 2).transpose(0, 3, 2, 1, 4).reshape(D, MG_COLS)
    bias = jnp.stack([_pair_major(b_i), _pair_major(b_f)], axis=1).reshape(HEAD_PAIRS, GATES_PER_PAIR)

    proj, grow = _proj(x2, w_main, w_g.astype(BF16).T)
    h_m = _mlstm(proj, grow, conv_w, bias[:, :, None], g_mlstm.reshape(HEAD_PAIRS, 2, MLSTM_DV), B, S)
    cos_t, sin_t = _rope_tables(S)
    h_a = _attn(proj, cos_t, sin_t, g_q[None, :], g_k[None, :], B, S)
    w_o = w_out.astype(BF16)
    x1, xg, aff4 = _outproj(h_m, h_a, x2, w_o, w_o, ln1_g[None, :], ln1_b[None, :], w_router.T, S)
    pos4, idx_digits = _topk(aff4, B, S, cap)
    idx = (idx_digits[:, :, 0:1, :] * TOK_TILE + idx_digits[:, :, 1:2, :]).astype(jnp.int32)
    y = _moe_ffn(idx, xg, pos4, aff4, w_gate, w_up, w_down, B, S, cap)
    moe = _moe_comb(idx, y, B, S, cap)
    return _final(x1, moe, p2, w_pl_gate.astype(BF16), b_pl_gate[None, :], w_pl_proj.astype(BF16),
                  ln2_g[None, :], ln2_b[None, :])


def kernel(x, p, w_in, conv_w, b_igate, b_fgate, g_mlstm, g_q, g_k, w_out, ln1_g, ln1_b, w_router, w_gate, w_up,
           w_down, w_pl_proj, w_pl_gate, b_pl_gate, ln2_g, ln2_b):
    B, S, D = x.shape
    x2 = x.reshape(B * S, D)
    for i in range(DEPTH):
        x2 = _layer(x2, p[i].reshape(B * S, P_DIM), B, S, w_in[i], conv_w[i], b_igate[i], b_fgate[i], g_mlstm[i],
                    g_q[i], g_k[i], w_out[i], ln1_g[i], ln1_b[i], w_router[i], w_gate[i], w_up[i], w_down[i],
                    w_pl_proj[i], w_pl_gate[i], b_pl_gate[i], ln2_g[i], ln2_b[i])
    return x2.reshape(B, S, D)
```

```python
import functools

import jax
import jax.numpy as jnp
from jax import lax
from jax.experimental import pallas as pl
from jax.experimental.pallas import tpu as pltpu

F32 = jnp.float32
BF16 = jnp.bfloat16

D_MODEL = 2048
P_DIM = 256
GRID_W = 64
MLSTM_WIDTH = D_MODEL // 2
ATTN_WIDTH = D_MODEL - MLSTM_WIDTH
MLSTM_HEADS = 8
MLSTM_DV = MLSTM_WIDTH // MLSTM_HEADS
MLSTM_DQK = MLSTM_DV // 2
CONV_W = 5
ATTN_HEAD_DIM = 128
ATTN_Q_HEADS = ATTN_WIDTH // ATTN_HEAD_DIM
ATTN_KV_HEADS = 2
ATTN_GROUP = ATTN_Q_HEADS // ATTN_KV_HEADS
ROPE_FREQS = ATTN_HEAD_DIM // 4
ROPE_THETA = 10000.0
N_EXPERTS = 16
EXPERT_FF = D_MODEL // 2
CAPACITY_FACTOR = 2
NORM_EPS = 1e-6
DEPTH = 1
ALPHA = (2.0 * DEPTH) ** 0.25
LOG2E = 1.4426950408889634

MQ_COLS = MLSTM_HEADS * MLSTM_DQK
MV_COLS = MLSTM_WIDTH
MG_COLS = 2 * 2 * MLSTM_HEADS
AQ_COLS = ATTN_WIDTH
AKV_COLS = ATTN_KV_HEADS * ATTN_HEAD_DIM
PROJ_COLS = 2 * MQ_COLS + 2 * MV_COLS + AQ_COLS + 2 * AKV_COLS
OFF_MQ, OFF_MK, OFF_MV, OFF_MO = 0, MQ_COLS, 2 * MQ_COLS, 2 * MQ_COLS + MV_COLS
OFF_AQ = OFF_MO + MV_COLS
OFF_AK = OFF_AQ + AQ_COLS
OFF_AV = OFF_AK + AKV_COLS
PROJ_COL_STEP = 1536

HEAD_PAIRS = MLSTM_HEADS // 2
GATES_PER_PAIR = 8
CHAINS = 4
MLSTM_LC = 128
STATE_UNROLL = 8
OUT_UNROLL = 8
ATTN_CHAIN_ROWS = 256
ROW_CHAIN = 256
LANES = 128
SUBLANES = 8
TOK_TILE = 16
COL_BLOCKS = D_MODEL // LANES
SCATTER_BATCH = 16
V7X_VMEM_LIMIT = 56 * 1024 * 1024
TOPK_REFINE_STEPS = 8


def _cparams(sem, vmem=V7X_VMEM_LIMIT):
    return pltpu.CompilerParams(dimension_semantics=sem, vmem_limit_bytes=vmem)


def _resident(shape, index_map):
    return pl.BlockSpec(shape, index_map, pipeline_mode=pl.Buffered(1))


def _sigmoid(x):
    return 0.5 * jnp.tanh(0.5 * x) + 0.5


def _log_sigmoid(x):
    return jnp.minimum(x, 0.0) - jnp.log1p(jnp.exp(-jnp.abs(x)))


def _proj_kernel(x_ref, w_ref, wgt_ref, o_ref, gr_ref):
    xb = x_ref[...].astype(BF16)
    gr = lax.dot_general(wgt_ref[...], xb, (((1,), (1,)), ((), ())), preferred_element_type=F32)
    for p in range(HEAD_PAIRS):
        gr_ref[p] = gr[GATES_PER_PAIR * p:GATES_PER_PAIR * (p + 1), :]
    tn = PROJ_COL_STEP
    for n0 in range(0, w_ref.shape[1], tn):
        o_ref[:, n0:n0 + tn] = jnp.dot(xb, w_ref[:, n0:n0 + tn], preferred_element_type=F32).astype(o_ref.dtype)


def _proj(x2, w, wgt, tm=512):
    T, D = x2.shape
    N = w.shape[1]
    return pl.pallas_call(
        _proj_kernel,
        grid=(T // tm,),
        in_specs=[
            pl.BlockSpec((tm, D), lambda i: (i, 0)),
            _resident((D, N), lambda i: (0, 0)),
            _resident((MG_COLS, D), lambda i: (0, 0)),
        ],
        out_specs=[
            pl.BlockSpec((tm, N), lambda i: (i, 0)),
            pl.BlockSpec((HEAD_PAIRS, GATES_PER_PAIR, tm), lambda i: (0, 0, i)),
        ],
        out_shape=[
            jax.ShapeDtypeStruct((T, N), BF16),
            jax.ShapeDtypeStruct((HEAD_PAIRS, GATES_PER_PAIR, T), F32),
        ],
        compiler_params=_cparams(("parallel",)),
        name="proj",
    )(x2, w, wgt)


def _conv_silu(x, w, pad_s):
    S, C = x.shape
    half = CONV_W // 2
    halo = jnp.zeros((SUBLANES, C), F32)
    pad_s[0:SUBLANES, :] = halo
    pad_s[SUBLANES + S:2 * SUBLANES + S, :] = halo
    pad_s[SUBLANES:SUBLANES + S, :] = x
    acc = x * w[half:half + 1, :]
    for j in range(CONV_W):
        if j != half:
            acc = acc + pad_s[SUBLANES + j - half:SUBLANES + j - half + S, :] * w[j:j + 1, :]
    return acc * _sigmoid(acc)


def _mlstm_kernel(q_ref, k_ref, v_ref, o_ref, gr_ref, cwq_ref, cwk_ref, br_ref, gh_ref, out_ref,
                  q0_s, q1_s, kt_s, col_s, winter_s, wl_s, dec_s, sc_s, cst_s, call_s, pad_s, acol_s, rmat_s):
    S = q_ref.shape[0]
    L = MLSTM_LC
    NC = S // L
    DQ = MLSTM_DQK
    DV = MLSTM_DV
    hi = lax.Precision.HIGHEST
    neg = -jnp.inf

    g8 = gr_ref[...] + br_ref[...]
    li8 = g8 * LOG2E
    lf8 = pltpu.roll(_log_sigmoid(g8) * LOG2E, CHAINS, 0)
    row = lax.broadcasted_iota(jnp.int32, (GATES_PER_PAIR, L), 0)
    lane = lax.broadcasted_iota(jnp.int32, (GATES_PER_PAIR, L), 1)
    fwd = (row % CHAINS) < 2
    fwd1 = fwd[:, :1]
    si = lax.broadcasted_iota(jnp.int32, (L, 2 * L), 0)
    ti = lax.broadcasted_iota(jnp.int32, (L, 2 * L), 1)
    tri = jnp.where(ti < L, jnp.where(si <= ti, 1.0, 0.0), jnp.where(si >= ti - L, 1.0, 0.0))
    tot, mloc, b_l, cm_l, r_l = [], [], [], [], []
    for c in range(NC):
        pr = jnp.dot(lf8[:, c * L:(c + 1) * L], tri, precision=hi, preferred_element_type=F32)
        b_c = jnp.where(fwd, pr[:, :L], pr[:, L:])
        tot_c = pr[:, L - 1:L]
        li_c = li8[:, c * L:(c + 1) * L]
        r_c = li_c - b_c
        cm = r_c
        k = 1
        while k < L:
            pre = jnp.where(lane >= k, pltpu.roll(cm, k, 1), neg)
            suf = jnp.where(lane < L - k, pltpu.roll(cm, L - k, 1), neg)
            cm = jnp.maximum(cm, jnp.where(fwd, pre, suf))
            k *= 2
        g_c = tot_c - b_c + li_c
        mloc_c = jnp.max(g_c, axis=1, keepdims=True)
        wl_s[c] = jnp.exp2(g_c - mloc_c)
        r_l.append(r_c)
        tot.append(tot_c)
        mloc.append(mloc_c)
        b_l.append(b_c)
        cm_l.append(cm)

    def scan(order):
        m = jnp.zeros((GATES_PER_PAIR, 1), F32)
        m_in, dec, sc = [None] * NC, [None] * NC, [None] * NC
        for c in order:
            m_new = jnp.maximum(tot[c] + m, mloc[c])
            m_in[c] = m
            dec[c] = jnp.exp2(tot[c] + m - m_new)
            sc[c] = jnp.exp2(mloc[c] - m_new)
            m = m_new
        return m_in, dec, sc

    mf, df, sf = scan(range(NC))
    mb, db, sb = scan(range(NC - 1, -1, -1))
    def terms3(x):
        t0 = x.astype(BF16).astype(F32)
        rem = x - t0
        t1 = rem.astype(BF16).astype(F32)
        return t0, t1, (rem - t1).astype(BF16).astype(F32)

    blk_rows = 2 * SUBLANES
    rowi = lax.broadcasted_iota(jnp.int32, (blk_rows, L), 0)
    bro = lambda x, ci: jnp.broadcast_to(x[ci:ci + 1, :], (blk_rows, L))
    zblk = jnp.zeros((blk_rows, L), F32)
    pad = jnp.zeros((LANES - GATES_PER_PAIR, L), F32)
    for c in range(NC):
        m_in = jnp.where(fwd1, mf[c], mb[c])
        dec_s[c] = jnp.broadcast_to(jnp.where(fwd1, df[c], db[c]), (GATES_PER_PAIR, L))
        sc_s[c] = jnp.broadcast_to(jnp.where(fwd1, sf[c], sb[c]), (GATES_PER_PAIR, L))
        a_c = jnp.maximum(m_in, cm_l[c])
        winter_s[c] = jnp.exp2(m_in - a_c)
        emt_c = jnp.exp2(-(b_l[c] + a_c))
        col_s[c * L:(c + 1) * L, :] = jnp.concatenate([emt_c, pad], axis=0).T
        a0, a1, a2 = terms3(a_c)
        r0, r1, r2 = terms3(r_l[c])
        a_blks, r_cols = [], []
        for ci in range(CHAINS):
            a_blks.append(jnp.where(rowi == 0, bro(a0, ci), jnp.where(rowi == 1, bro(a1, ci), jnp.where(
                rowi == 2, bro(a2, ci), jnp.where(rowi < 6, 1.0, 0.0)))))
            r_blk = jnp.where(rowi < 3, -1.0, jnp.where(rowi == 3, bro(r0, ci), jnp.where(
                rowi == 4, bro(r1, ci), jnp.where(rowi == 5, bro(r2, ci), 0.0))))
            r_cols.append(jnp.concatenate([zblk] * ci + [r_blk] + [zblk] * (LANES // blk_rows - 1 - ci), axis=0))
        a_rows = jnp.concatenate(a_blks + [zblk] * (LANES // blk_rows - CHAINS), axis=0)
        acol_s[c * L:(c + 1) * L, :] = a_rows.T.astype(BF16)
        rmat_s[c] = jnp.concatenate(r_cols, axis=1).astype(BF16)

    qs = (_conv_silu(q_ref[...].astype(F32), cwq_ref[...], pad_s) * (DQ ** -0.5)).astype(BF16)
    q0_s[...] = qs[:, :DQ]
    q1_s[...] = qs[:, DQ:]
    kt = _conv_silu(k_ref[...].astype(F32), cwk_ref[...], pad_s).T
    for c in range(NC):
        kt_s[c] = kt[:, c * L:(c + 1) * L]

    ones_blk = jnp.ones((L, DV), BF16)

    def vext_of(sl, hh):
        return jnp.concatenate([v_ref[sl, DV * hh:DV * (hh + 1)], ones_blk], axis=1)

    cst_s[...] = jnp.zeros(cst_s.shape, F32)

    def state_step(c):
        for d in range(2):
            ch = c if d == 0 else NC - 1 - c
            sl = pl.ds(pl.multiple_of(ch * L, L), L)
            wl = wl_s[ch]
            ktc = kt_s[ch]
            decs = dec_s[ch]
            scs = sc_s[ch]
            for hh in range(2):
                ci = d * 2 + hh
                cx = cst_s[ci]
                call_s[ch, ci] = cx.astype(BF16)
                kw = (ktc[DQ * hh:DQ * (hh + 1), :] * wl[ci:ci + 1, :]).astype(BF16)
                cst_s[ci] = (decs[ci:ci + 1, 0:1] * cx
                             + scs[ci:ci + 1, 0:1] * jnp.dot(kw, vext_of(sl, hh), preferred_element_type=F32))

    def state_body(c, carry):
        for u in range(STATE_UNROLL):
            state_step(STATE_UNROLL * c + u)
        return carry

    lax.fori_loop(0, NC // STATE_UNROLL, state_body, 0)

    tt = lax.broadcasted_iota(jnp.int32, (L, L), 0)
    ss = lax.broadcasted_iota(jnp.int32, (L, L), 1)
    masks = (ss <= tt, ss >= tt)

    eye = tt == ss

    def chunk_out(ch, hh, sl, cols, winter, e_all):
        og = _sigmoid(o_ref[sl, DV * hh:DV * (hh + 1)].astype(F32))
        qc = (q0_s if hh == 0 else q1_s)[sl, :]
        kth = kt_s[ch, DQ * hh:DQ * (hh + 1), :].astype(BF16)
        qk = jnp.dot(qc, kth, preferred_element_type=F32)
        vext = vext_of(sl, hh)
        hsum = None
        for d in range(2):
            ci = d * 2 + hh
            w_intra = jnp.exp2(jnp.where(masks[d], e_all[:, L * ci:L * (ci + 1)], neg))
            qcx = jnp.dot(qc, call_s[ch, ci], preferred_element_type=F32).astype(BF16)
            w_diag = jnp.where(eye, winter[ci:ci + 1, :], 0.0).astype(BF16)
            nd = jnp.dot(jnp.concatenate([(qk * w_intra).astype(BF16), w_diag], axis=1),
                         jnp.concatenate([vext, qcx], axis=0),
                         preferred_element_type=F32)
            h = nd[:, :DV] / jnp.maximum(jnp.abs(nd[:, DV:]), cols[:, ci:ci + 1])
            hsum = h if hsum is None else hsum + h
        y = hsum * lax.rsqrt(jnp.mean(hsum * hsum, axis=1, keepdims=True) + NORM_EPS) * gh_ref[hh:hh + 1, :]
        out_ref[sl, DV * hh:DV * (hh + 1)] = (og * y).astype(out_ref.dtype)

    def out_chunk(c):
        sl = pl.ds(pl.multiple_of(c * L, L), L)
        e_all = jnp.dot(acol_s[sl, :], rmat_s[c], preferred_element_type=F32)
        cols = col_s[sl, :]
        winter = winter_s[c]
        chunk_out(c, 0, sl, cols, winter, e_all)
        chunk_out(c, 1, sl, cols, winter, e_all)

    def out_body(c, carry):
        for u in range(OUT_UNROLL):
            out_chunk(OUT_UNROLL * c + u)
        return carry

    lax.fori_loop(0, NC // OUT_UNROLL, out_body, 0)


def _mlstm(proj, grow, conv_w, bias_r, g_head, B, S):
    T = B * S
    NC = S // MLSTM_LC
    pw = 2 * MLSTM_DQK
    vw = 2 * MLSTM_DV
    return pl.pallas_call(
        _mlstm_kernel,
        grid=(B, HEAD_PAIRS),
        in_specs=[
            pl.BlockSpec((S, pw), lambda b, p: (b, OFF_MQ // pw + p)),
            pl.BlockSpec((S, pw), lambda b, p: (b, OFF_MK // pw + p)),
            pl.BlockSpec((S, vw), lambda b, p: (b, OFF_MV // vw + p)),
            pl.BlockSpec((S, vw), lambda b, p: (b, OFF_MO // vw + p)),
            pl.BlockSpec((None, GATES_PER_PAIR, S), lambda b, p: (p, 0, b)),
            pl.BlockSpec((CONV_W, pw), lambda b, p: (0, p)),
            pl.BlockSpec((CONV_W, pw), lambda b, p: (0, MQ_COLS // pw + p)),
            pl.BlockSpec((None, GATES_PER_PAIR, 1), lambda b, p: (p, 0, 0)),
            pl.BlockSpec((None, 2, MLSTM_DV), lambda b, p: (p, 0, 0)),
        ],
        out_specs=pl.BlockSpec((S, vw), lambda b, p: (b, p)),
        out_shape=jax.ShapeDtypeStruct((T, MLSTM_WIDTH), BF16),
        scratch_shapes=[
            pltpu.VMEM((S, MLSTM_DQK), BF16),
            pltpu.VMEM((S, MLSTM_DQK), BF16),
            pltpu.VMEM((NC, pw, MLSTM_LC), F32),
            pltpu.VMEM((S, LANES), F32),
            pltpu.VMEM((NC, GATES_PER_PAIR, MLSTM_LC), F32),
            pltpu.VMEM((NC, GATES_PER_PAIR, MLSTM_LC), F32),
            pltpu.VMEM((NC, GATES_PER_PAIR, MLSTM_LC), F32),
            pltpu.VMEM((NC, GATES_PER_PAIR, MLSTM_LC), F32),
            pltpu.VMEM((CHAINS, MLSTM_DQK, 2 * MLSTM_DV), F32),
            pltpu.VMEM((NC, CHAINS, MLSTM_DQK, 2 * MLSTM_DV), BF16),
            pltpu.VMEM((S + 2 * SUBLANES, pw), F32),
            pltpu.VMEM((S, LANES), BF16),
            pltpu.VMEM((NC, LANES, CHAINS * MLSTM_LC), BF16),
        ],
        compiler_params=_cparams(("parallel", "parallel")),
        name="mlstm",
    )(proj, proj, proj, proj, grow, conv_w, conv_w, bias_r, g_head)


def _norm_rope(x, g, cos, sin_signed):
    xn = x * lax.rsqrt(jnp.mean(x * x, axis=1, keepdims=True) + NORM_EPS) * g
    lane = lax.broadcasted_iota(jnp.int32, x.shape, 1)
    first_half = (lane % (2 * ROPE_FREQS)) < ROPE_FREQS
    partner = jnp.where(first_half,
                        pltpu.roll(xn, LANES - ROPE_FREQS, 1),
                        pltpu.roll(xn, ROPE_FREQS, 1))
    return xn * cos + partner * sin_signed


def _attn_kernel(q_ref, k_ref, v_ref, cq_ref, sq_ref, ck_ref, sk_ref, gq_ref, gk_ref, o_ref, kr_s, vx_s):
    d = ATTN_HEAD_DIM

    @pl.when(pl.program_id(2) == 0)
    def _():
        kr_s[...] = _norm_rope(k_ref[...].astype(F32), gk_ref[...], ck_ref[...], sk_ref[...]).astype(BF16)
        vx_s[...] = jnp.concatenate([v_ref[...], jnp.ones(v_ref.shape, BF16)], axis=1)

    gq = gq_ref[...]
    kr = kr_s[...]
    vx = vx_s[...]
    for r0 in range(0, q_ref.shape[0], ATTN_CHAIN_ROWS):
        rows = slice(r0, r0 + ATTN_CHAIN_ROWS)
        cq, sq = cq_ref[rows, :], sq_ref[rows, :]
        for g in range(ATTN_GROUP):
            cols = slice(d * g, d * (g + 1))
            qg = (_norm_rope(q_ref[rows, cols].astype(F32), gq, cq, sq) * (d ** -0.5 * LOG2E)).astype(BF16)
            s = lax.dot_general(qg, kr, (((1,), (1,)), ((), ())), preferred_element_type=F32)
            p = jnp.exp2(s - jnp.max(s, axis=1, keepdims=True))
            ov = jnp.dot(p.astype(BF16), vx, preferred_element_type=F32)
            o_ref[rows, cols] = (ov[:, :d] / ov[:, d:]).astype(o_ref.dtype)


def _attn(proj, cos_t, sin_t, g_q, g_k, B, S, tq=1024):
    T = B * S
    d = ATTN_HEAD_DIM
    gw = ATTN_GROUP * d
    nq = S // tq
    return pl.pallas_call(
        _attn_kernel,
        grid=(B, ATTN_KV_HEADS, nq),
        in_specs=[
            pl.BlockSpec((tq, gw), lambda b, kv, qi: (b * nq + qi, OFF_AQ // gw + kv)),
            pl.BlockSpec((S, d), lambda b, kv, qi: (b, OFF_AK // d + kv)),
            pl.BlockSpec((S, d), lambda b, kv, qi: (b, OFF_AV // d + kv)),
            pl.BlockSpec((tq, d), lambda b, kv, qi: (qi, 0)),
            pl.BlockSpec((tq, d), lambda b, kv, qi: (qi, 0)),
            _resident((S, d), lambda b, kv, qi: (0, 0)),
            _resident((S, d), lambda b, kv, qi: (0, 0)),
            _resident((1, d), lambda b, kv, qi: (0, 0)),
            _resident((1, d), lambda b, kv, qi: (0, 0)),
        ],
        out_specs=pl.BlockSpec((tq, gw), lambda b, kv, qi: (b * nq + qi, kv)),
        out_shape=jax.ShapeDtypeStruct((T, ATTN_WIDTH), BF16),
        scratch_shapes=[pltpu.VMEM((S, d), BF16), pltpu.VMEM((S, 2 * d), BF16)],
        compiler_params=_cparams(("parallel", "parallel", "arbitrary")),
        name="attn",
    )(proj, proj, proj, cos_t, sin_t, cos_t, sin_t, g_q, g_k)


def _layer_norm(y, g, b):
    mu = jnp.mean(y, axis=1, keepdims=True)
    yc = y - mu
    var = jnp.mean(yc * yc, axis=1, keepdims=True)
    return yc * lax.rsqrt(var + NORM_EPS) * g + b


def _split_bf16(x):
    hi = x.astype(BF16)
    return hi, (x - hi.astype(F32)).astype(BF16)


def _tile_perm():
    n = TOK_TILE * COL_BLOCKS
    a = lax.broadcasted_iota(jnp.int32, (n, n), 0)
    b = lax.broadcasted_iota(jnp.int32, (n, n), 1)
    t = TOK_TILE
    return jnp.where(a // t == b % t, jnp.where(a % t == b // t, 1.0, 0.0), 0.0).astype(BF16)


def _token_major(xh, perm):
    out = []
    for r0 in range(0, xh.shape[0], 2 * TOK_TILE):
        halves = [jnp.concatenate([xh[r0 + h * TOK_TILE:r0 + (h + 1) * TOK_TILE, LANES * j:LANES * (j + 1)]
                                   for j in range(COL_BLOCKS)], axis=0) for h in range(2)]
        o2 = jnp.dot(perm, jnp.concatenate(halves, axis=1), preferred_element_type=F32).astype(BF16)
        out += [o2[:, :LANES], o2[:, LANES:]]
    return jnp.concatenate(out, axis=0)


def _row_major(g_ref, perm):
    n = TOK_TILE * COL_BLOCKS
    cols = [[] for _ in range(COL_BLOCKS)]
    for r0 in range(0, g_ref.shape[0], 2 * n):
        m2 = jnp.concatenate([g_ref[r0:r0 + n, :], g_ref[r0 + n:r0 + 2 * n, :]], axis=1)
        o2 = jnp.dot(perm, m2, preferred_element_type=F32).astype(BF16)
        for j in range(COL_BLOCKS):
            blk = o2[TOK_TILE * j:TOK_TILE * (j + 1), :]
            cols[j] += [blk[:, :LANES], blk[:, LANES:]]
    return jnp.concatenate([jnp.concatenate(cj, axis=0) for cj in cols], axis=1)


def _outproj_kernel(hm_ref, ha_ref, x_ref, wt_ref, wb_ref, g_ref, b_ref, wr_ref, x1_ref, xg_ref, aff_ref):
    y = (ALPHA * x_ref[...]
         + jnp.dot(hm_ref[...], wt_ref[...], preferred_element_type=F32)
         + jnp.dot(ha_ref[...], wb_ref[...], preferred_element_type=F32))
    x1 = _layer_norm(y, g_ref[...], b_ref[...])
    x1_ref[...] = x1
    xh, xl = _split_bf16(x1)
    wh, wl = _split_bf16(wr_ref[...])
    xg_ref[...] = _token_major(xh, _tile_perm())
    nt = (((1,), (1,)), ((), ()))
    logits = (lax.dot_general(wh, xh, nt, preferred_element_type=F32)
              + lax.dot_general(wh, xl, nt, preferred_element_type=F32)
              + lax.dot_general(wl, xh, nt, preferred_element_type=F32))
    e = jnp.exp(logits - jnp.max(logits, axis=0, keepdims=True))
    aff_ref[:, 0, :] = e / jnp.sum(e, axis=0, keepdims=True)


def _outproj(hm, ha, x2, wt, wb, g, b, wr, S, tm=512):
    T, D = x2.shape
    nt = S // tm
    return pl.pallas_call(
        _outproj_kernel,
        grid=(T // tm,),
        in_specs=[
            pl.BlockSpec((tm, MLSTM_WIDTH), lambda i: (i, 0)),
            pl.BlockSpec((tm, ATTN_WIDTH), lambda i: (i, 0)),
            pl.BlockSpec((tm, D), lambda i: (i, 0)),
            _resident((MLSTM_WIDTH, D), lambda i: (0, 0)),
            _resident((ATTN_WIDTH, D), lambda i: (MLSTM_WIDTH // ATTN_WIDTH, 0)),
            _resident((1, D), lambda i: (0, 0)),
            _resident((1, D), lambda i: (0, 0)),
            _resident((N_EXPERTS, D), lambda i: (0, 0)),
        ],
        out_specs=[
            pl.BlockSpec((tm, D), lambda i: (i, 0)),
            pl.BlockSpec((tm * TOK_TILE, LANES), lambda i: (i, 0)),
            pl.BlockSpec((N_EXPERTS, None, 1, tm), lambda i: (0, i // nt, 0, i % nt)),
        ],
        out_shape=[
            jax.ShapeDtypeStruct((T, D), F32),
            jax.ShapeDtypeStruct((T * TOK_TILE, LANES), BF16),
            jax.ShapeDtypeStruct((N_EXPERTS, T // S, 1, S), F32),
        ],
        compiler_params=_cparams(("parallel",)),
        name="outproj",
    )(hm, ha, x2, wt, wb, g, b, wr)


def _excl_prefix_count(m):
    E, S = m.shape
    nb = S // LANES
    s_i = lax.broadcasted_iota(jnp.int32, (LANES, 2 * LANES), 0)
    t_i = lax.broadcasted_iota(jnp.int32, (LANES, 2 * LANES), 1)
    tri_ext = jnp.where(t_i >= LANES, 1.0, jnp.where(s_i < t_i, 1.0, 0.0)).astype(BF16)
    r_i = lax.broadcasted_iota(jnp.int32, (nb * E, nb * E), 0)
    c_i = lax.broadcasted_iota(jnp.int32, (nb * E, nb * E), 1)
    earlier = jnp.where(c_i // E < r_i // E, jnp.where(c_i % E == r_i % E, 1.0, 0.0), 0.0).astype(BF16)
    stacked = jnp.concatenate([m[:, LANES * j:LANES * (j + 1)] for j in range(nb)], axis=0)
    r = jnp.dot(stacked, tri_ext, preferred_element_type=F32)
    off = jnp.dot(earlier, r[:, LANES:].astype(BF16), preferred_element_type=F32)
    tot = r[:, :LANES] + off
    return jnp.concatenate([tot[E * j:E * (j + 1), :] for j in range(nb)], axis=1)


def _topk_kernel(aff_ref, pos_ref, idx_ref, *, cap):
    S = aff_ref.shape[2]
    a = aff_ref[:, 0, :]

    def count_ge(v):
        return jnp.sum(jnp.where(a >= v, 1.0, 0.0), axis=1, keepdims=True)

    def keep_if_enough(cand, thr):
        return jnp.where(count_ge(pltpu.bitcast(cand, F32)) >= cap, cand, thr)

    thr = keep_if_enough(jnp.full((a.shape[0], 1), 1 << 30, jnp.int32), jnp.zeros((a.shape[0], 1), jnp.int32))
    for bit in range(28, -1, -2):
        b1, b0 = 1 << (bit + 1), 1 << bit
        thr = keep_if_enough(thr | b1 | b0, keep_if_enough(thr | b1, keep_if_enough(thr | b0, thr)))
    lo = pltpu.bitcast(thr, F32)
    hi = pltpu.bitcast(thr + 1, F32)
    for _ in range(TOPK_REFINE_STEPS):
        mid = 0.5 * (lo + hi)
        ok = count_ge(mid) >= cap
        lo = jnp.where(ok, mid, lo)
        hi = jnp.where(ok, hi, mid)
    gt = a > lo
    eq = a == lo
    need = cap - jnp.sum(jnp.where(gt, 1.0, 0.0), axis=1, keepdims=True)
    eq_rank = _excl_prefix_count(jnp.where(eq, 1.0, 0.0).astype(BF16))
    sel = jnp.logical_or(gt, jnp.logical_and(eq, eq_rank < need))
    pos = _excl_prefix_count(jnp.where(sel, 1.0, 0.0).astype(BF16))
    posm = jnp.where(sel, pos, -1.0)
    pos_ref[:, 0, :] = posm
    digit_row = lax.broadcasted_iota(jnp.int32, (2 * SUBLANES, S), 0)
    tok = lax.broadcasted_iota(jnp.int32, (2 * SUBLANES, S), 1)
    digits = jnp.where(digit_row == 0, tok // TOK_TILE,
                       jnp.where(digit_row == 1, tok % TOK_TILE, 0)).astype(F32).astype(BF16)
    slot = lax.broadcasted_iota(jnp.int32, (cap, S), 0).astype(F32)
    for e in range(a.shape[0]):
        hit = jnp.where(slot == posm[e:e + 1, :], 1.0, 0.0).astype(BF16)
        idx_ref[e] = lax.dot_general(digits, hit, (((1,), (1,)), ((), ())),
                                     preferred_element_type=F32)[:SUBLANES]


def _topk(aff, B, S, cap):
    E = aff.shape[0]
    assert S <= 256 * TOK_TILE, "token index digits must be integers bf16 holds exactly"
    return pl.pallas_call(
        functools.partial(_topk_kernel, cap=cap),
        grid=(B,),
        in_specs=[pl.BlockSpec((E, None, 1, S), lambda b: (0, b, 0, 0))],
        out_specs=[pl.BlockSpec((E, None, 1, S), lambda b: (0, b, 0, 0)),
                   pl.BlockSpec((E, None, SUBLANES, cap), lambda b: (0, b, 0, 0))],
        out_shape=[jax.ShapeDtypeStruct((E, B, 1, S), F32),
                   jax.ShapeDtypeStruct((E, B, SUBLANES, cap), F32)],
        compiler_params=_cparams(("parallel",)),
        name="topk",
    )(aff)


def _moe_ffn_kernel(idx_ref, xg_ref, pos_ref, aff_ref, wg_ref, wu_ref, wd_ref, y_ref, gbuf_s, wg_s, wu_s, wd_s,
                    *, cap, n_exp):
    e = pl.program_id(0)
    b = pl.program_id(1)

    @pl.when(e < n_exp)
    def _():
        slot = e % 2
        rg = pl.multiple_of(b * wg_ref.shape[0], wg_ref.shape[0])
        rd = pl.multiple_of(b * wd_ref.shape[0], wd_ref.shape[0])
        wg_s[slot, pl.ds(rg, wg_ref.shape[0]), :] = wg_ref[...].astype(BF16)
        wu_s[slot, pl.ds(rg, wu_ref.shape[0]), :] = wu_ref[...].astype(BF16)
        wd_s[slot, pl.ds(rd, wd_ref.shape[0]), :] = wd_ref[...].astype(BF16)

    @pl.when(e > 0)
    def _():
        S = pos_ref.shape[1]
        slot = (e - 1) % 2
        for c in range(cap):
            row = pl.multiple_of(idx_ref[0, c] * TOK_TILE, TOK_TILE)
            gbuf_s[TOK_TILE * c:TOK_TILE * (c + 1), :] = xg_ref[pl.ds(row, TOK_TILE), :]
        perm = _tile_perm()
        xs = _row_major(gbuf_s, perm)
        sl = lax.broadcasted_iota(jnp.int32, (cap, S), 0).astype(F32)
        gate = jnp.sum(jnp.where(sl == pos_ref[...], aff_ref[...], 0.0), axis=1, keepdims=True)
        hg = jnp.dot(xs, wg_s[slot], preferred_element_type=F32)
        hu = jnp.dot(xs, wu_s[slot], preferred_element_type=F32)
        hid = (hg * _sigmoid(hg) * hu).astype(BF16)
        y = (jnp.dot(hid, wd_s[slot], preferred_element_type=F32) * gate).astype(BF16)
        y_ref[...] = _token_major(y, perm)


def _moe_ffn(idx, xg, pos4, aff4, wg, wu, wd, B, S, cap):
    E, D, F = wg.shape
    prev = lambda e: jnp.maximum(e - 1, 0)
    stage = lambda e, b: (jnp.minimum(e, E - 1), jnp.where(e < E, b, B - 1), 0)
    return pl.pallas_call(
        functools.partial(_moe_ffn_kernel, cap=cap, n_exp=E),
        grid=(E + 1, B),
        in_specs=[
            pl.BlockSpec((None, None, 1, cap), lambda e, b: (prev(e), b, 0, 0), memory_space=pltpu.SMEM),
            pl.BlockSpec((S * TOK_TILE, LANES), lambda e, b: (b, 0)),
            pl.BlockSpec((None, None, 1, S), lambda e, b: (prev(e), b, 0, 0)),
            pl.BlockSpec((None, None, 1, S), lambda e, b: (prev(e), b, 0, 0)),
            pl.BlockSpec((None, D // B, F), stage),
            pl.BlockSpec((None, D // B, F), stage),
            pl.BlockSpec((None, F // B, D), stage),
        ],
        out_specs=pl.BlockSpec((None, None, cap * TOK_TILE, LANES),
                               lambda e, b: (jnp.where(e > 0, b, 0), prev(e), 0, 0)),
        out_shape=jax.ShapeDtypeStruct((B, E, cap * TOK_TILE, LANES), BF16),
        scratch_shapes=[pltpu.VMEM((cap * TOK_TILE, LANES), BF16),
                        pltpu.VMEM((2, D, F), BF16), pltpu.VMEM((2, D, F), BF16), pltpu.VMEM((2, F, D), BF16)],
        compiler_params=_cparams(("arbitrary", "arbitrary")),
        name="moe_ffn",
    )(idx, xg, pos4, aff4, wg, wu, wd)


def _row_major_f32(g_ref, r0, n_tok, perm):
    n = TOK_TILE * COL_BLOCKS
    cols = [[] for _ in range(COL_BLOCKS)]
    perm3 = jnp.concatenate([perm, perm, perm], axis=1)
    for k in range(0, n_tok * TOK_TILE, 2 * n):
        m2 = jnp.concatenate([g_ref[pl.ds(r0 + k, n), :], g_ref[pl.ds(r0 + k + n, n), :]], axis=1)
        hi = m2.astype(BF16)
        r1 = m2 - hi.astype(F32)
        mid = r1.astype(BF16)
        lo = (r1 - mid.astype(F32)).astype(BF16)
        o2 = jnp.dot(perm3, jnp.concatenate([hi, mid, lo], axis=0),
                     preferred_element_type=F32)
        for j in range(COL_BLOCKS):
            blk = o2[TOK_TILE * j:TOK_TILE * (j + 1), :]
            cols[j] += [blk[:, :LANES], blk[:, LANES:]]
    return jnp.concatenate([jnp.concatenate(cj, axis=0) for cj in cols], axis=1)


def _moe_comb_kernel(idx_ref, y_ref, o_ref, *, cap):
    @pl.when(pl.program_id(1) == 0)
    def _():
        o_ref[...] = jnp.zeros(o_ref.shape, F32)

    for g in range(idx_ref.shape[0]):
        for c0 in range(0, cap, SCATTER_BATCH):
            rows = [pl.multiple_of(idx_ref[g, 0, c0 + u] * TOK_TILE, TOK_TILE) for u in range(SCATTER_BATCH)]
            new = [o_ref[pl.ds(rows[u], TOK_TILE), :]
                   + y_ref[g, TOK_TILE * (c0 + u):TOK_TILE * (c0 + u + 1), :].astype(F32)
                   for u in range(SCATTER_BATCH)]
            for u in range(SCATTER_BATCH):
                o_ref[pl.ds(rows[u], TOK_TILE), :] = new[u]


def _moe_comb(idx, y, B, S, cap, eg=4):
    E = y.shape[1]
    return pl.pallas_call(
        functools.partial(_moe_comb_kernel, cap=cap),
        grid=(B, E // eg),
        in_specs=[
            pl.BlockSpec((eg, None, 1, cap), lambda b, j: (j, b, 0, 0), memory_space=pltpu.SMEM),
            pl.BlockSpec((None, eg, cap * TOK_TILE, LANES), lambda b, j: (b, j, 0, 0)),
        ],
        out_specs=pl.BlockSpec((S * TOK_TILE, LANES), lambda b, j: (b, 0)),
        out_shape=jax.ShapeDtypeStruct((B * S * TOK_TILE, LANES), F32),
        compiler_params=_cparams(("parallel", "arbitrary")),
        name="moe_comb",
    )(idx, y)


def _final_kernel(x1_ref, moe_ref, p_ref, wpg_ref, bpg_ref, wpp_ref, g_ref, b_ref, o_ref):
    perm = _tile_perm()
    for r0 in range(0, x1_ref.shape[0], ROW_CHAIN):
        rows = slice(r0, r0 + ROW_CHAIN)
        x1 = x1_ref[rows, :]
        gate = _sigmoid(jnp.dot(x1.astype(BF16), wpg_ref[...], preferred_element_type=F32) + bpg_ref[...])
        plv = gate * jnp.dot(p_ref[rows, :].astype(BF16), wpp_ref[...], preferred_element_type=F32)
        moe = _row_major_f32(moe_ref, r0 * TOK_TILE, ROW_CHAIN, perm)
        o_ref[rows, :] = _layer_norm(ALPHA * x1 + moe + plv, g_ref[...], b_ref[...])


def _final(x1, moe_tm, p2, wpg, bpg, wpp, g, b, tm=512):
    T, D = x1.shape
    return pl.pallas_call(
        _final_kernel,
        grid=(T // tm,),
        in_specs=[
            pl.BlockSpec((tm, D), lambda i: (i, 0)),
            pl.BlockSpec((tm * TOK_TILE, LANES), lambda i: (i, 0)),
            pl.BlockSpec((tm, P_DIM), lambda i: (i, 0)),
            _resident((D, D), lambda i: (0, 0)),
            _resident((1, D), lambda i: (0, 0)),
            _resident((P_DIM, D), lambda i: (0, 0)),
            _resident((1, D), lambda i: (0, 0)),
            _resident((1, D), lambda i: (0, 0)),
        ],
        out_specs=pl.BlockSpec((tm, D), lambda i: (i, 0)),
        out_shape=jax.ShapeDtypeStruct((T, D), F32),
        compiler_params=_cparams(("parallel",)),
        name="final",
    )(x1, moe_tm, p2, wpg, bpg, wpp, g, b)


def _rope_tables(S):
    rows = S // GRID_W
    row_idx = jnp.broadcast_to(jnp.arange(rows, dtype=F32)[:, None], (rows, GRID_W)).reshape(-1)
    col_idx = jnp.broadcast_to(jnp.arange(GRID_W, dtype=F32)[None, :], (rows, GRID_W)).reshape(-1)
    inv_freq = ROPE_THETA ** (-jnp.arange(ROPE_FREQS, dtype=F32) / ROPE_FREQS)
    ar = row_idx[:, None] * inv_freq
    ac = col_idx[:, None] * inv_freq
    cos_t = jnp.concatenate([jnp.cos(ar), jnp.cos(ar), jnp.cos(ac), jnp.cos(ac)], axis=1)
    sin_t = jnp.concatenate([-jnp.sin(ar), jnp.sin(ar), -jnp.sin(ac), jnp.sin(ac)], axis=1)
    return cos_t, sin_t


def _pair_major(a):
    return a.reshape(2, HEAD_PAIRS, 2).transpose(1, 0, 2)


def _layer(x2, p2, B, S, w_in, conv_w, b_i, b_f, g_mlstm, g_q, g_k, w_out, ln1_g, ln1_b, w_router, w_gate, w_up,
           w_down, w_pl_proj, w_pl_gate, b_pl_gate, ln2_g, ln2_b):
    D = D_MODEL
    cap = CAPACITY_FACTOR * S // N_EXPERTS
    o_mg = 2 * MQ_COLS + 2 * MV_COLS
    w_main = jnp.concatenate([w_in[:, :o_mg], w_in[:, o_mg + MG_COLS:]], axis=1).astype(BF16)
    w_g = w_in[:, o_mg:o_mg + MG_COLS].reshape(D, 2, 2, HEAD_PAIRS, 2).transpose(0, 3, 2, 1, 4).reshape(D, MG_COLS)
    bias = jnp.stack([_pair_major(b_i), _pair_major(b_f)], axis=1).reshape(HEAD_PAIRS, GATES_PER_PAIR)

    proj, grow = _proj(x2, w_main, w_g.astype(BF16).T)
    h_m = _mlstm(proj, grow, conv_w, bias[:, :, None], g_mlstm.reshape(HEAD_PAIRS, 2, MLSTM_DV), B, S)
    cos_t, sin_t = _rope_tables(S)
    h_a = _attn(proj, cos_t, sin_t, g_q[None, :], g_k[None, :], B, S)
    w_o = w_out.astype(BF16)
    x1, xg, aff4 = _outproj(h_m, h_a, x2, w_o, w_o, ln1_g[None, :], ln1_b[None, :], w_router.T, S)
    pos4, idx_digits = _topk(aff4, B, S, cap)
    idx = (idx_digits[:, :, 0:1, :] * TOK_TILE + idx_digits[:, :, 1:2, :]).astype(jnp.int32)
    y = _moe_ffn(idx, xg, pos4, aff4, w_gate, w_up, w_down, B, S, cap)
    moe = _moe_comb(idx, y, B, S, cap)
    return _final(x1, moe, p2, w_pl_gate.astype(BF16), b_pl_gate[None, :], w_pl_proj.astype(BF16),
                  ln2_g[None, :], ln2_b[None, :])


def kernel(x, p, w_in, conv_w, b_igate, b_fgate, g_mlstm, g_q, g_k, w_out, ln1_g, ln1_b, w_router, w_gate, w_up,
           w_down, w_pl_proj, w_pl_gate, b_pl_gate, ln2_g, ln2_b):
    B, S, D = x.shape
    x2 = x.reshape(B * S, D)
    for i in range(DEPTH):
        x2 = _layer(x2, p[i].reshape(B * S, P_DIM), B, S, w_in[i], conv_w[i], b_igate[i], b_fgate[i], g_mlstm[i],
                    g_q[i], g_k[i], w_out[i], ln1_g[i], ln1_b[i], w_router[i], w_gate[i], w_up[i], w_down[i],
                    w_pl_proj[i], w_pl_gate[i], b_pl_gate[i], ln2_g[i], ln2_b[i])
    return x2.reshape(B, S, D)
```

```python
import functools

import jax
import jax.numpy as jnp
from jax import lax
from jax.experimental import pallas as pl
from jax.experimental.pallas import tpu as pltpu

F32 = jnp.float32
BF16 = jnp.bfloat16

D_MODEL = 2048
P_DIM = 256
GRID_W = 64
MLSTM_WIDTH = D_MODEL // 2
ATTN_WIDTH = D_MODEL - MLSTM_WIDTH
MLSTM_HEADS = 8
MLSTM_DV = MLSTM_WIDTH // MLSTM_HEADS
MLSTM_DQK = MLSTM_DV // 2
CONV_W = 5
ATTN_HEAD_DIM = 128
ATTN_Q_HEADS = ATTN_WIDTH // ATTN_HEAD_DIM
ATTN_KV_HEADS = 2
ATTN_GROUP = ATTN_Q_HEADS // ATTN_KV_HEADS
ROPE_FREQS = ATTN_HEAD_DIM // 4
ROPE_THETA = 10000.0
N_EXPERTS = 16
EXPERT_FF = D_MODEL // 2
CAPACITY_FACTOR = 2
NORM_EPS = 1e-6
DEPTH = 1
ALPHA = (2.0 * DEPTH) ** 0.25
LOG2E = 1.4426950408889634

MQ_COLS = MLSTM_HEADS * MLSTM_DQK
MV_COLS = MLSTM_WIDTH
MG_COLS = 2 * 2 * MLSTM_HEADS
AQ_COLS = ATTN_WIDTH
AKV_COLS = ATTN_KV_HEADS * ATTN_HEAD_DIM
PROJ_COLS = 2 * MQ_COLS + 2 * MV_COLS + AQ_COLS + 2 * AKV_COLS
OFF_MQ, OFF_MK, OFF_MV, OFF_MO = 0, MQ_COLS, 2 * MQ_COLS, 2 * MQ_COLS + MV_COLS
OFF_AQ = OFF_MO + MV_COLS
OFF_AK = OFF_AQ + AQ_COLS
OFF_AV = OFF_AK + AKV_COLS
PROJ_COL_STEP = 1536

HEAD_PAIRS = MLSTM_HEADS // 2
GATES_PER_PAIR = 8
CHAINS = 4
MLSTM_LC = 128
STATE_UNROLL = 8
OUT_UNROLL = 8
ATTN_CHAIN_ROWS = 256
ROW_CHAIN = 256
LANES = 128
SUBLANES = 8
TOK_TILE = 16
COL_BLOCKS = D_MODEL // LANES
SCATTER_BATCH = 16
V7X_VMEM_LIMIT = 56 * 1024 * 1024
TOPK_REFINE_STEPS = 8


def _cparams(sem, vmem=V7X_VMEM_LIMIT):
    return pltpu.CompilerParams(dimension_semantics=sem, vmem_limit_bytes=vmem)


def _resident(shape, index_map):
    return pl.BlockSpec(shape, index_map, pipeline_mode=pl.Buffered(1))


def _sigmoid(x):
    return 0.5 * jnp.tanh(0.5 * x) + 0.5


def _log_sigmoid(x):
    return jnp.minimum(x, 0.0) - jnp.log1p(jnp.exp(-jnp.abs(x)))


def _proj_kernel(x_ref, w_ref, wgt_ref, o_ref, gr_ref):
    xb = x_ref[...].astype(BF16)
    gr = lax.dot_general(wgt_ref[...], xb, (((1,), (1,)), ((), ())), preferred_element_type=F32)
    for p in range(HEAD_PAIRS):
        gr_ref[p] = gr[GATES_PER_PAIR * p:GATES_PER_PAIR * (p + 1), :]
    tn = PROJ_COL_STEP
    for n0 in range(0, w_ref.shape[1], tn):
        o_ref[:, n0:n0 + tn] = jnp.dot(xb, w_ref[:, n0:n0 + tn], preferred_element_type=F32).astype(o_ref.dtype)


def _proj(x2, w, wgt, tm=512):
    T, D = x2.shape
    N = w.shape[1]
    return pl.pallas_call(
        _proj_kernel,
        grid=(T // tm,),
        in_specs=[
            pl.BlockSpec((tm, D), lambda i: (i, 0)),
            _resident((D, N), lambda i: (0, 0)),
            _resident((MG_COLS, D), lambda i: (0, 0)),
        ],
        out_specs=[
            pl.BlockSpec((tm, N), lambda i: (i, 0)),
            pl.BlockSpec((HEAD_PAIRS, GATES_PER_PAIR, tm), lambda i: (0, 0, i)),
        ],
        out_shape=[
            jax.ShapeDtypeStruct((T, N), BF16),
            jax.ShapeDtypeStruct((HEAD_PAIRS, GATES_PER_PAIR, T), F32),
        ],
        compiler_params=_cparams(("parallel",)),
        name="proj",
    )(x2, w, wgt)


def _conv_silu(x, w, pad_s):
    S, C = x.shape
    half = CONV_W // 2
    halo = jnp.zeros((SUBLANES, C), F32)
    pad_s[0:SUBLANES, :] = halo
    pad_s[SUBLANES + S:2 * SUBLANES + S, :] = halo
    pad_s[SUBLANES:SUBLANES + S, :] = x
    acc = x * w[half:half + 1, :]
    for j in range(CONV_W):
        if j != half:
            acc = acc + pad_s[SUBLANES + j - half:SUBLANES + j - half + S, :] * w[j:j + 1, :]
    return acc * _sigmoid(acc)


def _mlstm_kernel(q_ref, k_ref, v_ref, o_ref, gr_ref, cwq_ref, cwk_ref, br_ref, gh_ref, out_ref,
                  q0_s, q1_s, kt_s, col_s, winter_s, wl_s, dec_s, sc_s, cst_s, call_s, pad_s, acol_s, rmat_s):
    S = q_ref.shape[0]
    L = MLSTM_LC
    NC = S // L
    DQ = MLSTM_DQK
    DV = MLSTM_DV
    hi = lax.Precision.HIGHEST
    neg = -jnp.inf

    g8 = gr_ref[...] + br_ref[...]
    li8 = g8 * LOG2E
    lf8 = pltpu.roll(_log_sigmoid(g8) * LOG2E, CHAINS, 0)
    row = lax.broadcasted_iota(jnp.int32, (GATES_PER_PAIR, L), 0)
    lane = lax.broadcasted_iota(jnp.int32, (GATES_PER_PAIR, L), 1)
    fwd = (row % CHAINS) < 2
    fwd1 = fwd[:, :1]
    si = lax.broadcasted_iota(jnp.int32, (L, 2 * L), 0)
    ti = lax.broadcasted_iota(jnp.int32, (L, 2 * L), 1)
    tri = jnp.where(ti < L, jnp.where(si <= ti, 1.0, 0.0), jnp.where(si >= ti - L, 1.0, 0.0))
    tot, mloc, b_l, cm_l, r_l = [], [], [], [], []
    for c in range(NC):
        pr = jnp.dot(lf8[:, c * L:(c + 1) * L], tri, precision=hi, preferred_element_type=F32)
        b_c = jnp.where(fwd, pr[:, :L], pr[:, L:])
        tot_c = pr[:, L - 1:L]
        li_c = li8[:, c * L:(c + 1) * L]
        r_c = li_c - b_c
        cm = r_c
        k = 1
        while k < L:
            pre = jnp.where(lane >= k, pltpu.roll(cm, k, 1), neg)
            suf = jnp.where(lane < L - k, pltpu.roll(cm, L - k, 1), neg)
            cm = jnp.maximum(cm, jnp.where(fwd, pre, suf))
            k *= 2
        g_c = tot_c - b_c + li_c
        mloc_c = jnp.max(g_c, axis=1, keepdims=True)
        wl_s[c] = jnp.exp2(g_c - mloc_c)
        r_l.append(r_c)
        tot.append(tot_c)
        mloc.append(mloc_c)
        b_l.append(b_c)
        cm_l.append(cm)

    def scan(order):
        m = jnp.zeros((GATES_PER_PAIR, 1), F32)
        m_in, dec, sc = [None] * NC, [None] * NC, [None] * NC
        for c in order:
            m_new = jnp.maximum(tot[c] + m, mloc[c])
            m_in[c] = m
            dec[c] = jnp.exp2(tot[c] + m - m_new)
            sc[c] = jnp.exp2(mloc[c] - m_new)
            m = m_new
        return m_in, dec, sc

    mf, df, sf = scan(range(NC))
    mb, db, sb = scan(range(NC - 1, -1, -1))
    def terms3(x):
        t0 = x.astype(BF16).astype(F32)
        rem = x - t0
        t1 = rem.astype(BF16).astype(F32)
        return t0, t1, (rem - t1).astype(BF16).astype(F32)

    blk_rows = 2 * SUBLANES
    rowi = lax.broadcasted_iota(jnp.int32, (blk_rows, L), 0)
    bro = lambda x, ci: jnp.broadcast_to(x[ci:ci + 1, :], (blk_rows, L))
    zblk = jnp.zeros((blk_rows, L), F32)
    pad = jnp.zeros((LANES - GATES_PER_PAIR, L), F32)
    for c in range(NC):
        m_in = jnp.where(fwd1, mf[c], mb[c])
        dec_s[c] = jnp.broadcast_to(jnp.where(fwd1, df[c], db[c]), (GATES_PER_PAIR, L))
        sc_s[c] = jnp.broadcast_to(jnp.where(fwd1, sf[c], sb[c]), (GATES_PER_PAIR, L))
        a_c = jnp.maximum(m_in, cm_l[c])
        winter_s[c] = jnp.exp2(m_in - a_c)
        emt_c = jnp.exp2(-(b_l[c] + a_c))
        col_s[c * L:(c + 1) * L, :] = jnp.concatenate([emt_c, pad], axis=0).T
        a0, a1, a2 = terms3(a_c)
        r0, r1, r2 = terms3(r_l[c])
        a_blks, r_cols = [], []
        for ci in range(CHAINS):
            a_blks.append(jnp.where(rowi == 0, bro(a0, ci), jnp.where(rowi == 1, bro(a1, ci), jnp.where(
                rowi == 2, bro(a2, ci), jnp.where(rowi < 6, 1.0, 0.0)))))
            r_blk = jnp.where(rowi < 3, -1.0, jnp.where(rowi == 3, bro(r0, ci), jnp.where(
                rowi == 4, bro(r1, ci), jnp.where(rowi == 5, bro(r2, ci), 0.0))))
            r_cols.append(jnp.concatenate([zblk] * ci + [r_blk] + [zblk] * (LANES // blk_rows - 1 - ci), axis=0))
        a_rows = jnp.concatenate(a_blks + [zblk] * (LANES // blk_rows - CHAINS), axis=0)
        acol_s[c * L:(c + 1) * L, :] = a_rows.T.astype(BF16)
        rmat_s[c] = jnp.concatenate(r_cols, axis=1).astype(BF16)

    qs = (_conv_silu(q_ref[...].astype(F32), cwq_ref[...], pad_s) * (DQ ** -0.5)).astype(BF16)
    q0_s[...] = qs[:, :DQ]
    q1_s[...] = qs[:, DQ:]
    kt = _conv_silu(k_ref[...].astype(F32), cwk_ref[...], pad_s).T
    for c in range(NC):
        kt_s[c] = kt[:, c * L:(c + 1) * L]

    ones_blk = jnp.ones((L, DV), BF16)

    def vext_of(sl, hh):
        return jnp.concatenate([v_ref[sl, DV * hh:DV * (hh + 1)], ones_blk], axis=1)

    cst_s[...] = jnp.zeros(cst_s.shape, F32)

    def state_step(c):
        for d in range(2):
            ch = c if d == 0 else NC - 1 - c
            sl = pl.ds(pl.multiple_of(ch * L, L), L)
            wl = wl_s[ch]
            ktc = kt_s[ch]
            decs = dec_s[ch]
            scs = sc_s[ch]
            for hh in range(2):
                ci = d * 2 + hh
                cx = cst_s[ci]
                call_s[ch, ci] = cx.astype(BF16)
                kw = (ktc[DQ * hh:DQ * (hh + 1), :] * wl[ci:ci + 1, :]).astype(BF16)
                cst_s[ci] = (decs[ci:ci + 1, 0:1] * cx
                             + scs[ci:ci + 1, 0:1] * jnp.dot(kw, vext_of(sl, hh), preferred_element_type=F32))

    def state_body(c, carry):
        for u in range(STATE_UNROLL):
            state_step(STATE_UNROLL * c + u)
        return carry

    lax.fori_loop(0, NC // STATE_UNROLL, state_body, 0)

    tt = lax.broadcasted_iota(jnp.int32, (L, L), 0)
    ss = lax.broadcasted_iota(jnp.int32, (L, L), 1)
    masks = (ss <= tt, ss >= tt)

    eye = tt == ss

    def chunk_out(ch, hh, sl, cols, winter, e_all):
        og = _sigmoid(o_ref[sl, DV * hh:DV * (hh + 1)].astype(F32))
        qc = (q0_s if hh == 0 else q1_s)[sl, :]
        kth = kt_s[ch, DQ * hh:DQ * (hh + 1), :].astype(BF16)
        qk = jnp.dot(qc, kth, preferred_element_type=F32)
        vext = vext_of(sl, hh)
        hsum = None
        for d in range(2):
            ci = d * 2 + hh
            w_intra = jnp.exp2(jnp.where(masks[d], e_all[:, L * ci:L * (ci + 1)], neg))
            qcx = jnp.dot(qc, call_s[ch, ci], preferred_element_type=F32).astype(BF16)
            w_diag = jnp.where(eye, winter[ci:ci + 1, :], 0.0).astype(BF16)
            nd = jnp.dot(jnp.concatenate([(qk * w_intra).astype(BF16), w_diag], axis=1),
                         jnp.concatenate([vext, qcx], axis=0),
                         preferred_element_type=F32)
            h = nd[:, :DV] / jnp.maximum(jnp.abs(nd[:, DV:]), cols[:, ci:ci + 1])
            hsum = h if hsum is None else hsum + h
        y = hsum * lax.rsqrt(jnp.mean(hsum * hsum, axis=1, keepdims=True) + NORM_EPS) * gh_ref[hh:hh + 1, :]
        out_ref[sl, DV * hh:DV * (hh + 1)] = (og * y).astype(out_ref.dtype)

    def out_chunk(c):
        sl = pl.ds(pl.multiple_of(c * L, L), L)
        e_all = jnp.dot(acol_s[sl, :], rmat_s[c], preferred_element_type=F32)
        cols = col_s[sl, :]
        winter = winter_s[c]
        chunk_out(c, 0, sl, cols, winter, e_all)
        chunk_out(c, 1, sl, cols, winter, e_all)

    def out_body(c, carry):
        for u in range(OUT_UNROLL):
            out_chunk(OUT_UNROLL * c + u)
        return carry

    lax.fori_loop(0, NC // OUT_UNROLL, out_body, 0)


def _mlstm(proj, grow, conv_w, bias_r, g_head, B, S):
    T = B * S
    NC = S // MLSTM_LC
    pw = 2 * MLSTM_DQK
    vw = 2 * MLSTM_DV
    return pl.pallas_call(
        _mlstm_kernel,
        grid=(B, HEAD_PAIRS),
        in_specs=[
            pl.BlockSpec((S, pw), lambda b, p: (b, OFF_MQ // pw + p)),
            pl.BlockSpec((S, pw), lambda b, p: (b, OFF_MK // pw + p)),
            pl.BlockSpec((S, vw), lambda b, p: (b, OFF_MV // vw + p)),
            pl.BlockSpec((S, vw), lambda b, p: (b, OFF_MO // vw + p)),
            pl.BlockSpec((None, GATES_PER_PAIR, S), lambda b, p: (p, 0, b)),
            pl.BlockSpec((CONV_W, pw), lambda b, p: (0, p)),
            pl.BlockSpec((CONV_W, pw), lambda b, p: (0, MQ_COLS // pw + p)),
            pl.BlockSpec((None, GATES_PER_PAIR, 1), lambda b, p: (p, 0, 0)),
            pl.BlockSpec((None, 2, MLSTM_DV), lambda b, p: (p, 0, 0)),
        ],
        out_specs=pl.BlockSpec((S, vw), lambda b, p: (b, p)),
        out_shape=jax.ShapeDtypeStruct((T, MLSTM_WIDTH), BF16),
        scratch_shapes=[
            pltpu.VMEM((S, MLSTM_DQK), BF16),
            pltpu.VMEM((S, MLSTM_DQK), BF16),
            pltpu.VMEM((NC, pw, MLSTM_LC), F32),
            pltpu.VMEM((S, LANES), F32),
            pltpu.VMEM((NC, GATES_PER_PAIR, MLSTM_LC), F32),
            pltpu.VMEM((NC, GATES_PER_PAIR, MLSTM_LC), F32),
            pltpu.VMEM((NC, GATES_PER_PAIR, MLSTM_LC), F32),
            pltpu.VMEM((NC, GATES_PER_PAIR, MLSTM_LC), F32),
            pltpu.VMEM((CHAINS, MLSTM_DQK, 2 * MLSTM_DV), F32),
            pltpu.VMEM((NC, CHAINS, MLSTM_DQK, 2 * MLSTM_DV), BF16),
            pltpu.VMEM((S + 2 * SUBLANES, pw), F32),
            pltpu.VMEM((S, LANES), BF16),
            pltpu.VMEM((NC, LANES, CHAINS * MLSTM_LC), BF16),
        ],
        compiler_params=_cparams(("parallel", "parallel")),
        name="mlstm",
    )(proj, proj, proj, proj, grow, conv_w, conv_w, bias_r, g_head)


def _norm_rope(x, g, cos, sin_signed):
    xn = x * lax.rsqrt(jnp.mean(x * x, axis=1, keepdims=True) + NORM_EPS) * g
    lane = lax.broadcasted_iota(jnp.int32, x.shape, 1)
    first_half = (lane % (2 * ROPE_FREQS)) < ROPE_FREQS
    partner = jnp.where(first_half,
                        pltpu.roll(xn, LANES - ROPE_FREQS, 1),
                        pltpu.roll(xn, ROPE_FREQS, 1))
    return xn * cos + partner * sin_signed


def _attn_kernel(q_ref, k_ref, v_ref, cq_ref, sq_ref, ck_ref, sk_ref, gq_ref, gk_ref, o_ref, kr_s, vx_s):
    d = ATTN_HEAD_DIM

    @pl.when(pl.program_id(2) == 0)
    def _():
        kr_s[...] = _norm_rope(k_ref[...].astype(F32), gk_ref[...], ck_ref[...], sk_ref[...]).astype(BF16)
        vx_s[...] = jnp.concatenate([v_ref[...], jnp.ones(v_ref.shape, BF16)], axis=1)

    gq = gq_ref[...]
    kr = kr_s[...]
    vx = vx_s[...]
    for r0 in range(0, q_ref.shape[0], ATTN_CHAIN_ROWS):
        rows = slice(r0, r0 + ATTN_CHAIN_ROWS)
        cq, sq = cq_ref[rows, :], sq_ref[rows, :]
        for g in range(ATTN_GROUP):
            cols = slice(d * g, d * (g + 1))
            qg = (_norm_rope(q_ref[rows, cols].astype(F32), gq, cq, sq) * (d ** -0.5 * LOG2E)).astype(BF16)
            s = lax.dot_general(qg, kr, (((1,), (1,)), ((), ())), preferred_element_type=F32)
            p = jnp.exp2(s - jnp.max(s, axis=1, keepdims=True))
            ov = jnp.dot(p.astype(BF16), vx, preferred_element_type=F32)
            o_ref[rows, cols] = (ov[:, :d] / ov[:, d:]).astype(o_ref.dtype)


def _attn(proj, cos_t, sin_t, g_q, g_k, B, S, tq=1024):
    T = B * S
    d = ATTN_HEAD_DIM
    gw = ATTN_GROUP * d
    nq = S // tq
    return pl.pallas_call(
        _attn_kernel,
        grid=(B, ATTN_KV_HEADS, nq),
        in_specs=[
            pl.BlockSpec((tq, gw), lambda b, kv, qi: (b * nq + qi, OFF_AQ // gw + kv)),
            pl.BlockSpec((S, d), lambda b, kv, qi: (b, OFF_AK // d + kv)),
            pl.BlockSpec((S, d), lambda b, kv, qi: (b, OFF_AV // d + kv)),
            pl.BlockSpec((tq, d), lambda b, kv, qi: (qi, 0)),
            pl.BlockSpec((tq, d), lambda b, kv, qi: (qi, 0)),
            _resident((S, d), lambda b, kv, qi: (0, 0)),
            _resident((S, d), lambda b, kv, qi: (0, 0)),
            _resident((1, d), lambda b, kv, qi: (0, 0)),
            _resident((1, d), lambda b, kv, qi: (0, 0)),
        ],
        out_specs=pl.BlockSpec((tq, gw), lambda b, kv, qi: (b * nq + qi, kv)),
        out_shape=jax.ShapeDtypeStruct((T, ATTN_WIDTH), BF16),
        scratch_shapes=[pltpu.VMEM((S, d), BF16), pltpu.VMEM((S, 2 * d), BF16)],
        compiler_params=_cparams(("parallel", "parallel", "arbitrary")),
        name="attn",
    )(proj, proj, proj, cos_t, sin_t, cos_t, sin_t, g_q, g_k)


def _layer_norm(y, g, b):
    mu = jnp.mean(y, axis=1, keepdims=True)
    yc = y - mu
    var = jnp.mean(yc * yc, axis=1, keepdims=True)
    return yc * lax.rsqrt(var + NORM_EPS) * g + b


def _split_bf16(x):
    hi = x.astype(BF16)
    return hi, (x - hi.astype(F32)).astype(BF16)


def _tile_perm():
    n = TOK_TILE * COL_BLOCKS
    a = lax.broadcasted_iota(jnp.int32, (n, n), 0)
    b = lax.broadcasted_iota(jnp.int32, (n, n), 1)
    t = TOK_TILE
    return jnp.where(a // t == b % t, jnp.where(a % t == b // t, 1.0, 0.0), 0.0).astype(BF16)


def _token_major(xh, perm):
    out = []
    for r0 in range(0, xh.shape[0], 2 * TOK_TILE):
        halves = [jnp.concatenate([xh[r0 + h * TOK_TILE:r0 + (h + 1) * TOK_TILE, LANES * j:LANES * (j + 1)]
                                   for j in range(COL_BLOCKS)], axis=0) for h in range(2)]
        o2 = jnp.dot(perm, jnp.concatenate(halves, axis=1), preferred_element_type=F32).astype(BF16)
        out += [o2[:, :LANES], o2[:, LANES:]]
    return jnp.concatenate(out, axis=0)


def _row_major(g_ref, perm):
    n = TOK_TILE * COL_BLOCKS
    cols = [[] for _ in range(COL_BLOCKS)]
    for r0 in range(0, g_ref.shape[0], 2 * n):
        m2 = jnp.concatenate([g_ref[r0:r0 + n, :], g_ref[r0 + n:r0 + 2 * n, :]], axis=1)
        o2 = jnp.dot(perm, m2, preferred_element_type=F32).astype(BF16)
        for j in range(COL_BLOCKS):
            blk = o2[TOK_TILE * j:TOK_TILE * (j + 1), :]
            cols[j] += [blk[:, :LANES], blk[:, LANES:]]
    return jnp.concatenate([jnp.concatenate(cj, axis=0) for cj in cols], axis=1)


def _outproj_kernel(hm_ref, ha_ref, x_ref, wt_ref, wb_ref, g_ref, b_ref, wr_ref, x1_ref, xg_ref, aff_ref):
    y = (ALPHA * x_ref[...]
         + jnp.dot(hm_ref[...], wt_ref[...], preferred_element_type=F32)
         + jnp.dot(ha_ref[...], wb_ref[...], preferred_element_type=F32))
    x1 = _layer_norm(y, g_ref[...], b_ref[...])
    x1_ref[...] = x1
    xh, xl = _split_bf16(x1)
    wh, wl = _split_bf16(wr_ref[...])
    xg_ref[...] = _token_major(xh, _tile_perm())
    nt = (((1,), (1,)), ((), ()))
    logits = (lax.dot_general(wh, xh, nt, preferred_element_type=F32)
              + lax.dot_general(wh, xl, nt, preferred_element_type=F32)
              + lax.dot_general(wl, xh, nt, preferred_element_type=F32))
    e = jnp.exp(logits - jnp.max(logits, axis=0, keepdims=True))
    aff_ref[:, 0, :] = e / jnp.sum(e, axis=0, keepdims=True)


def _outproj(hm, ha, x2, wt, wb, g, b, wr, S, tm=512):
    T, D = x2.shape
    nt = S // tm
    return pl.pallas_call(
        _outproj_kernel,
        grid=(T // tm,),
        in_specs=[
            pl.BlockSpec((tm, MLSTM_WIDTH), lambda i: (i, 0)),
            pl.BlockSpec((tm, ATTN_WIDTH), lambda i: (i, 0)),
            pl.BlockSpec((tm, D), lambda i: (i, 0)),
            _resident((MLSTM_WIDTH, D), lambda i: (0, 0)),
            _resident((ATTN_WIDTH, D), lambda i: (MLSTM_WIDTH // ATTN_WIDTH, 0)),
            _resident((1, D), lambda i: (0, 0)),
            _resident((1, D), lambda i: (0, 0)),
            _resident((N_EXPERTS, D), lambda i: (0, 0)),
        ],
        out_specs=[
            pl.BlockSpec((tm, D), lambda i: (i, 0)),
            pl.BlockSpec((tm * TOK_TILE, LANES), lambda i: (i, 0)),
            pl.BlockSpec((N_EXPERTS, None, 1, tm), lambda i: (0, i // nt, 0, i % nt)),
        ],
        out_shape=[
            jax.ShapeDtypeStruct((T, D), F32),
            jax.ShapeDtypeStruct((T * TOK_TILE, LANES), BF16),
            jax.ShapeDtypeStruct((N_EXPERTS, T // S, 1, S), F32),
        ],
        compiler_params=_cparams(("parallel",)),
        name="outproj",
    )(hm, ha, x2, wt, wb, g, b, wr)


def _excl_prefix_count(m):
    E, S = m.shape
    nb = S // LANES
    s_i = lax.broadcasted_iota(jnp.int32, (LANES, 2 * LANES), 0)
    t_i = lax.broadcasted_iota(jnp.int32, (LANES, 2 * LANES), 1)
    tri_ext = jnp.where(t_i >= LANES, 1.0, jnp.where(s_i < t_i, 1.0, 0.0)).astype(BF16)
    r_i = lax.broadcasted_iota(jnp.int32, (nb * E, nb * E), 0)
    c_i = lax.broadcasted_iota(jnp.int32, (nb * E, nb * E), 1)
    earlier = jnp.where(c_i // E < r_i // E, jnp.where(c_i % E == r_i % E, 1.0, 0.0), 0.0).astype(BF16)
    stacked = jnp.concatenate([m[:, LANES * j:LANES * (j + 1)] for j in range(nb)], axis=0)
    r = jnp.dot(stacked, tri_ext, preferred_element_type=F32)
    off = jnp.dot(earlier, r[:, LANES:].astype(BF16), preferred_element_type=F32)
    tot = r[:, :LANES] + off
    return jnp.concatenate([tot[E * j:E * (j + 1), :] for j in range(nb)], axis=1)


def _topk_kernel(aff_ref, pos_ref, idx_ref, *, cap):
    S = aff_ref.shape[2]
    a = aff_ref[:, 0, :]

    def count_ge(v):
        return jnp.sum(jnp.where(a >= v, 1.0, 0.0), axis=1, keepdims=True)

    def keep_if_enough(cand, thr):
        return jnp.where(count_ge(pltpu.bitcast(cand, F32)) >= cap, cand, thr)

    thr = keep_if_enough(jnp.full((a.shape[0], 1), 1 << 30, jnp.int32), jnp.zeros((a.shape[0], 1), jnp.int32))
    for bit in range(28, -1, -2):
        b1, b0 = 1 << (bit + 1), 1 << bit
        thr = keep_if_enough(thr | b1 | b0, keep_if_enough(thr | b1, keep_if_enough(thr | b0, thr)))
    lo = pltpu.bitcast(thr, F32)
    hi = pltpu.bitcast(thr + 1, F32)
    for _ in range(TOPK_REFINE_STEPS):
        mid = 0.5 * (lo + hi)
        ok = count_ge(mid) >= cap
        lo = jnp.where(ok, mid, lo)
        hi = jnp.where(ok, hi, mid)
    gt = a > lo
    eq = a == lo
    need = cap - jnp.sum(jnp.where(gt, 1.0, 0.0), axis=1, keepdims=True)
    eq_rank = _excl_prefix_count(jnp.where(eq, 1.0, 0.0).astype(BF16))
    sel = jnp.logical_or(gt, jnp.logical_and(eq, eq_rank < need))
    pos = _excl_prefix_count(jnp.where(sel, 1.0, 0.0).astype(BF16))
    posm = jnp.where(sel, pos, -1.0)
    pos_ref[:, 0, :] = posm
    digit_row = lax.broadcasted_iota(jnp.int32, (2 * SUBLANES, S), 0)
    tok = lax.broadcasted_iota(jnp.int32, (2 * SUBLANES, S), 1)
    digits = jnp.where(digit_row == 0, tok // TOK_TILE,
                       jnp.where(digit_row == 1, tok % TOK_TILE, 0)).astype(F32).astype(BF16)
    slot = lax.broadcasted_iota(jnp.int32, (cap, S), 0).astype(F32)
    for e in range(a.shape[0]):
        hit = jnp.where(slot == posm[e:e + 1, :], 1.0, 0.0).astype(BF16)
        idx_ref[e] = lax.dot_general(digits, hit, (((1,), (1,)), ((), ())),
                                     preferred_element_type=F32)[:SUBLANES]


def _topk(aff, B, S, cap):
    E = aff.shape[0]
    assert S <= 256 * TOK_TILE, "token index digits must be integers bf16 holds exactly"
    return pl.pallas_call(
        functools.partial(_topk_kernel, cap=cap),
        grid=(B,),
        in_specs=[pl.BlockSpec((E, None, 1, S), lambda b: (0, b, 0, 0))],
        out_specs=[pl.BlockSpec((E, None, 1, S), lambda b: (0, b, 0, 0)),
                   pl.BlockSpec((E, None, SUBLANES, cap), lambda b: (0, b, 0, 0))],
        out_shape=[jax.ShapeDtypeStruct((E, B, 1, S), F32),
                   jax.ShapeDtypeStruct((E, B, SUBLANES, cap), F32)],
        compiler_params=_cparams(("parallel",)),
        name="topk",
    )(aff)


def _moe_ffn_kernel(idx_ref, xg_ref, pos_ref, aff_ref, wg_ref, wu_ref, wd_ref, y_ref, gbuf_s, wg_s, wu_s, wd_s,
                    *, cap, n_exp):
    e = pl.program_id(0)
    b = pl.program_id(1)

    @pl.when(e < n_exp)
    def _():
        slot = e % 2
        rg = pl.multiple_of(b * wg_ref.shape[0], wg_ref.shape[0])
        rd = pl.multiple_of(b * wd_ref.shape[0], wd_ref.shape[0])
        wg_s[slot, pl.ds(rg, wg_ref.shape[0]), :] = wg_ref[...].astype(BF16)
        wu_s[slot, pl.ds(rg, wu_ref.shape[0]), :] = wu_ref[...].astype(BF16)
        wd_s[slot, pl.ds(rd, wd_ref.shape[0]), :] = wd_ref[...].astype(BF16)

    @pl.when(e > 0)
    def _():
        S = pos_ref.shape[1]
        slot = (e - 1) % 2
        for c in range(cap):
            row = pl.multiple_of(idx_ref[0, c] * TOK_TILE, TOK_TILE)
            gbuf_s[TOK_TILE * c:TOK_TILE * (c + 1), :] = xg_ref[pl.ds(row, TOK_TILE), :]
        perm = _tile_perm()
        xs = _row_major(gbuf_s, perm)
        sl = lax.broadcasted_iota(jnp.int32, (cap, S), 0).astype(F32)
        gate = jnp.sum(jnp.where(sl == pos_ref[...], aff_ref[...], 0.0), axis=1, keepdims=True)
        hg = jnp.dot(xs, wg_s[slot], preferred_element_type=F32)
        hu = jnp.dot(xs, wu_s[slot], preferred_element_type=F32)
        hid = (hg * _sigmoid(hg) * hu).astype(BF16)
        y = (jnp.dot(hid, wd_s[slot], preferred_element_type=F32) * gate).astype(BF16)
        y_ref[...] = y


def _moe_ffn(idx, xg, pos4, aff4, wg, wu, wd, B, S, cap):
    E, D, F = wg.shape
    prev = lambda e: jnp.maximum(e - 1, 0)
    stage = lambda e, b: (jnp.minimum(e, E - 1), jnp.where(e < E, b, B - 1), 0)
    return pl.pallas_call(
        functools.partial(_moe_ffn_kernel, cap=cap, n_exp=E),
        grid=(E + 1, B),
        in_specs=[
            pl.BlockSpec((None, None, 1, cap), lambda e, b: (prev(e), b, 0, 0), memory_space=pltpu.SMEM),
            pl.BlockSpec((S * TOK_TILE, LANES), lambda e, b: (b, 0)),
            pl.BlockSpec((None, None, 1, S), lambda e, b: (prev(e), b, 0, 0)),
            pl.BlockSpec((None, None, 1, S), lambda e, b: (prev(e), b, 0, 0)),
            pl.BlockSpec((None, D // B, F), stage),
            pl.BlockSpec((None, D // B, F), stage),
            pl.BlockSpec((None, F // B, D), stage),
        ],
        out_specs=pl.BlockSpec((None, None, cap, D), lambda e, b: (jnp.where(e > 0, b, 0), prev(e), 0, 0)),
        out_shape=jax.ShapeDtypeStruct((B, E, cap, D), BF16),
        scratch_shapes=[pltpu.VMEM((cap * TOK_TILE, LANES), BF16),
                        pltpu.VMEM((2, D, F), BF16), pltpu.VMEM((2, D, F), BF16), pltpu.VMEM((2, F, D), BF16)],
        compiler_params=_cparams(("arbitrary", "arbitrary")),
        name="moe_ffn",
    )(idx, xg, pos4, aff4, wg, wu, wd)


def _row_major_f32(g_ref, r0, n_tok, perm):
    n = TOK_TILE * COL_BLOCKS
    cols = [[] for _ in range(COL_BLOCKS)]
    perm3 = jnp.concatenate([perm, perm, perm], axis=1)
    for k in range(0, n_tok * TOK_TILE, 2 * n):
        m2 = jnp.concatenate([g_ref[pl.ds(r0 + k, n), :], g_ref[pl.ds(r0 + k + n, n), :]], axis=1)
        hi = m2.astype(BF16)
        r1 = m2 - hi.astype(F32)
        mid = r1.astype(BF16)
        lo = (r1 - mid.astype(F32)).astype(BF16)
        o2 = jnp.dot(perm3, jnp.concatenate([hi, mid, lo], axis=0),
                     preferred_element_type=F32)
        for j in range(COL_BLOCKS):
            blk = o2[TOK_TILE * j:TOK_TILE * (j + 1), :]
            cols[j] += [blk[:, :LANES], blk[:, LANES:]]
    return jnp.concatenate([jnp.concatenate(cj, axis=0) for cj in cols], axis=1)


def _moe_comb_kernel(idx_ref, y_ref, o_ref, acc_s, *, cap, n_grp):
    j = pl.program_id(1)

    @pl.when(j == 0)
    def _():
        acc_s[...] = jnp.zeros(acc_s.shape, F32)

    @pl.when(j < n_grp)
    def _():
        perm = _tile_perm()
        for g in range(idx_ref.shape[0]):
            y_tm = _token_major(y_ref[g], perm)
            for c0 in range(0, cap, SCATTER_BATCH):
                rows = [pl.multiple_of(idx_ref[g, 0, c0 + u] * TOK_TILE, TOK_TILE) for u in range(SCATTER_BATCH)]
                new = [acc_s[pl.ds(rows[u], TOK_TILE), :]
                       + y_tm[TOK_TILE * (c0 + u):TOK_TILE * (c0 + u + 1), :].astype(F32)
                       for u in range(SCATTER_BATCH)]
                for u in range(SCATTER_BATCH):
                    acc_s[pl.ds(rows[u], TOK_TILE), :] = new[u]

    @pl.when(j >= n_grp)
    def _():
        n_tok = o_ref.shape[0]
        r0 = pl.multiple_of((j - n_grp) * (n_tok * TOK_TILE), n_tok * TOK_TILE)
        o_ref[...] = _row_major_f32(acc_s, r0, n_tok, _tile_perm())


def _moe_comb(idx, y, B, S, cap, chunk=1024, eg=4):
    E = y.shape[1]
    D = D_MODEL
    nch = S // chunk
    ng = E // eg
    return pl.pallas_call(
        functools.partial(_moe_comb_kernel, cap=cap, n_grp=ng),
        grid=(B, ng + nch),
        in_specs=[
            pl.BlockSpec((eg, None, 1, cap), lambda b, j: (jnp.minimum(j, ng - 1), b, 0, 0),
                         memory_space=pltpu.SMEM),
            pl.BlockSpec((None, eg, cap, D), lambda b, j: (b, jnp.minimum(j, ng - 1), 0, 0)),
        ],
        out_specs=pl.BlockSpec((chunk, D), lambda b, j: (b * nch + jnp.maximum(j - ng, 0), 0)),
        out_shape=jax.ShapeDtypeStruct((B * S, D), F32),
        scratch_shapes=[pltpu.VMEM((S * TOK_TILE, LANES), F32)],
        compiler_params=_cparams(("arbitrary", "arbitrary")),
        name="moe_comb",
    )(idx, y)


def _final_kernel(x1_ref, moe_ref, p_ref, wpg_ref, bpg_ref, wpp_ref, g_ref, b_ref, o_ref):
    for r0 in range(0, x1_ref.shape[0], ROW_CHAIN):
        rows = slice(r0, r0 + ROW_CHAIN)
        x1 = x1_ref[rows, :]
        gate = _sigmoid(jnp.dot(x1.astype(BF16), wpg_ref[...], preferred_element_type=F32) + bpg_ref[...])
        plv = gate * jnp.dot(p_ref[rows, :].astype(BF16), wpp_ref[...], preferred_element_type=F32)
        o_ref[rows, :] = _layer_norm(ALPHA * x1 + moe_ref[rows, :] + plv, g_ref[...], b_ref[...])


def _final(x1, moe, p2, wpg, bpg, wpp, g, b, tm=512):
    T, D = x1.shape
    return pl.pallas_call(
        _final_kernel,
        grid=(T // tm,),
        in_specs=[
            pl.BlockSpec((tm, D), lambda i: (i, 0)),
            pl.BlockSpec((tm, D), lambda i: (i, 0)),
            pl.BlockSpec((tm, P_DIM), lambda i: (i, 0)),
            _resident((D, D), lambda i: (0, 0)),
            _resident((1, D), lambda i: (0, 0)),
            _resident((P_DIM, D), lambda i: (0, 0)),
            _resident((1, D), lambda i: (0, 0)),
            _resident((1, D), lambda i: (0, 0)),
        ],
        out_specs=pl.BlockSpec((tm, D), lambda i: (i, 0)),
        out_shape=jax.ShapeDtypeStruct((T, D), F32),
        compiler_params=_cparams(("parallel",)),
        name="final",
    )(x1, moe, p2, wpg, bpg, wpp, g, b)


def _rope_tables(S):
    rows = S // GRID_W
    row_idx = jnp.broadcast_to(jnp.arange(rows, dtype=F32)[:, None], (rows, GRID_W)).reshape(-1)
    col_idx = jnp.broadcast_to(jnp.arange(GRID_W, dtype=F32)[None, :], (rows, GRID_W)).reshape(-1)
    inv_freq = ROPE_THETA ** (-jnp.arange(ROPE_FREQS, dtype=F32) / ROPE_FREQS)
    ar = row_idx[:, None] * inv_freq
    ac = col_idx[:, None] * inv_freq
    cos_t = jnp.concatenate([jnp.cos(ar), jnp.cos(ar), jnp.cos(ac), jnp.cos(ac)], axis=1)
    sin_t = jnp.concatenate([-jnp.sin(ar), jnp.sin(ar), -jnp.sin(ac), jnp.sin(ac)], axis=1)
    return cos_t, sin_t


def _pair_major(a):
    return a.reshape(2, HEAD_PAIRS, 2).transpose(1, 0, 2)


def _layer(x2, p2, B, S, w_in, conv_w, b_i, b_f, g_mlstm, g_q, g_k, w_out, ln1_g, ln1_b, w_router, w_gate, w_up,
           w_down, w_pl_proj, w_pl_gate, b_pl_gate, ln2_g, ln2_b):
    D = D_MODEL
    cap = CAPACITY_FACTOR * S // N_EXPERTS
    o_mg = 2 * MQ_COLS + 2 * MV_COLS
    w_main = jnp.concatenate([w_in[:, :o_mg], w_in[:, o_mg + MG_COLS:]], axis=1).astype(BF16)
    w_g = w_in[:, o_mg:o_mg + MG_COLS].reshape(D, 2, 2, HEAD_PAIRS, 2).transpose(0, 3, 2, 1, 4).reshape(D, MG_COLS)
    bias = jnp.stack([_pair_major(b_i), _pair_major(b_f)], axis=1).reshape(HEAD_PAIRS, GATES_PER_PAIR)

    proj, grow = _proj(x2, w_main, w_g.astype(BF16).T)
    h_m = _mlstm(proj, grow, conv_w, bias[:, :, None], g_mlstm.reshape(HEAD_PAIRS, 2, MLSTM_DV), B, S)
    cos_t, sin_t = _rope_tables(S)
    h_a = _attn(proj, cos_t, sin_t, g_q[None, :], g_k[None, :], B, S)
    w_o = w_out.astype(BF16)
    x1, xg, aff4 = _outproj(h_m, h_a, x2, w_o, w_o, ln1_g[None, :], ln1_b[None, :], w_router.T, S)
    pos4, idx_digits = _topk(aff4, B, S, cap)
    idx = (idx_digits[:, :, 0:1, :] * TOK_TILE + idx_digits[:, :, 1:2, :]).astype(jnp.int32)
    y = _moe_ffn(idx, xg, pos4, aff4, w_gate, w_up, w_down, B, S, cap)
    moe = _moe_comb(idx, y, B, S, cap)
    return _final(x1, moe, p2, w_pl_gate.astype(BF16), b_pl_gate[None, :], w_pl_proj.astype(BF16),
                  ln2_g[None, :], ln2_b[None, :])


def kernel(x, p, w_in, conv_w, b_igate, b_fgate, g_mlstm, g_q, g_k, w_out, ln1_g, ln1_b, w_router, w_gate, w_up,
           w_down, w_pl_proj, w_pl_gate, b_pl_gate, ln2_g, ln2_b):
    B, S, D = x.shape
    x2 = x.reshape(B * S, D)
    for i in range(DEPTH):
        x2 = _layer(x2, p[i].reshape(B * S, P_DIM), B, S, w_in[i], conv_w[i], b_igate[i], b_fgate[i], g_mlstm[i],
                    g_q[i], g_k[i], w_out[i], ln1_g[i], ln1_b[i], w_router[i], w_gate[i], w_up[i], w_down[i],
                    w_pl_proj[i], w_pl_gate[i], b_pl_gate[i], ln2_g[i], ln2_b[i])
    return x2.reshape(B, S, D)
```

```python
import functools

import jax
import jax.numpy as jnp
import numpy as np
from jax import lax
from jax.experimental import pallas as pl
from jax.experimental.pallas import tpu as pltpu

F32 = jnp.float32
BF16 = jnp.bfloat16

D_MODEL = 2048
P_DIM = 256
GRID_W = 64
MLSTM_WIDTH = D_MODEL // 2
ATTN_WIDTH = D_MODEL - MLSTM_WIDTH
MLSTM_HEADS = 8
MLSTM_DV = MLSTM_WIDTH // MLSTM_HEADS
MLSTM_DQK = MLSTM_DV // 2
CONV_W = 5
ATTN_HEAD_DIM = 128
ATTN_Q_HEADS = ATTN_WIDTH // ATTN_HEAD_DIM
ATTN_KV_HEADS = 2
ATTN_GROUP = ATTN_Q_HEADS // ATTN_KV_HEADS
ROPE_FREQS = ATTN_HEAD_DIM // 4
ROPE_THETA = 10000.0
N_EXPERTS = 16
EXPERT_FF = D_MODEL // 2
CAPACITY_FACTOR = 2
NORM_EPS = 1e-6
DEPTH = 1
ALPHA = (2.0 * DEPTH) ** 0.25
LOG2E = 1.4426950408889634

MQ_COLS = MLSTM_HEADS * MLSTM_DQK
MV_COLS = MLSTM_WIDTH
MG_COLS = 2 * 2 * MLSTM_HEADS
AQ_COLS = ATTN_WIDTH
AKV_COLS = ATTN_KV_HEADS * ATTN_HEAD_DIM
PROJ_COLS = 2 * MQ_COLS + 2 * MV_COLS + AQ_COLS + 2 * AKV_COLS
OFF_MQ, OFF_MK, OFF_MV, OFF_MO = 0, MQ_COLS, 2 * MQ_COLS, 2 * MQ_COLS + MV_COLS
OFF_AQ = OFF_MO + MV_COLS
OFF_AK = OFF_AQ + AQ_COLS
OFF_AV = OFF_AK + AKV_COLS
PROJ_COL_STEP = 1536

HEAD_PAIRS = MLSTM_HEADS // 2
GATES_PER_PAIR = 8
CHAINS = 4
MLSTM_LC = 128
STATE_UNROLL = 8
OUT_UNROLL = 8
ATTN_CHAIN_ROWS = 256
ROW_CHAIN = 256
LANES = 128
SUBLANES = 8
TOK_TILE = 16
COL_BLOCKS = D_MODEL // LANES
SCATTER_BATCH = 16
V7X_VMEM_LIMIT = 56 * 1024 * 1024
TOPK_REFINE_STEPS = 8


def _cparams(sem, vmem=V7X_VMEM_LIMIT):
    return pltpu.CompilerParams(dimension_semantics=sem, vmem_limit_bytes=vmem)


def _resident(shape, index_map):
    return pl.BlockSpec(shape, index_map, pipeline_mode=pl.Buffered(1))


def _sigmoid(x):
    return 0.5 * jnp.tanh(0.5 * x) + 0.5


def _log_sigmoid(x):
    return jnp.minimum(x, 0.0) - jnp.log1p(jnp.exp(-jnp.abs(x)))


def _proj_kernel(x_ref, w_ref, wgt_ref, o_ref, gr_ref):
    xb = x_ref[...].astype(BF16)
    gr = lax.dot_general(wgt_ref[...], xb, (((1,), (1,)), ((), ())), preferred_element_type=F32)
    for p in range(HEAD_PAIRS):
        gr_ref[p] = gr[GATES_PER_PAIR * p:GATES_PER_PAIR * (p + 1), :]
    tn = PROJ_COL_STEP
    for n0 in range(0, w_ref.shape[1], tn):
        o_ref[:, n0:n0 + tn] = jnp.dot(xb, w_ref[:, n0:n0 + tn], preferred_element_type=F32).astype(o_ref.dtype)


def _proj(x2, w, wgt, tm=512):
    T, D = x2.shape
    N = w.shape[1]
    return pl.pallas_call(
        _proj_kernel,
        grid=(T // tm,),
        in_specs=[
            pl.BlockSpec((tm, D), lambda i: (i, 0)),
            _resident((D, N), lambda i: (0, 0)),
            _resident((MG_COLS, D), lambda i: (0, 0)),
        ],
        out_specs=[
            pl.BlockSpec((tm, N), lambda i: (i, 0)),
            pl.BlockSpec((HEAD_PAIRS, GATES_PER_PAIR, tm), lambda i: (0, 0, i)),
        ],
        out_shape=[
            jax.ShapeDtypeStruct((T, N), BF16),
            jax.ShapeDtypeStruct((HEAD_PAIRS, GATES_PER_PAIR, T), F32),
        ],
        compiler_params=_cparams(("parallel",)),
        name="proj",
    )(x2, w, wgt)


def _conv_silu(x, w, pad_s):
    S, C = x.shape
    half = CONV_W // 2
    halo = jnp.zeros((SUBLANES, C), F32)
    pad_s[0:SUBLANES, :] = halo
    pad_s[SUBLANES + S:2 * SUBLANES + S, :] = halo
    pad_s[SUBLANES:SUBLANES + S, :] = x
    acc = x * w[half:half + 1, :]
    for j in range(CONV_W):
        if j != half:
            acc = acc + pad_s[SUBLANES + j - half:SUBLANES + j - half + S, :] * w[j:j + 1, :]
    return acc * _sigmoid(acc)


def _mlstm_kernel(q_ref, k_ref, v_ref, o_ref, gr_ref, cwq_ref, cwk_ref, br_ref, gh_ref, out_ref,
                  q0_s, q1_s, kt_s, col_s, winter_s, wl_s, dec_s, sc_s, cst_s, call_s, pad_s, acol_s, rmat_s):
    S = q_ref.shape[0]
    L = MLSTM_LC
    NC = S // L
    DQ = MLSTM_DQK
    DV = MLSTM_DV
    hi = lax.Precision.HIGHEST
    neg = -jnp.inf

    g8 = gr_ref[...] + br_ref[...]
    li8 = g8 * LOG2E
    lf8 = pltpu.roll(_log_sigmoid(g8) * LOG2E, CHAINS, 0)
    row = lax.broadcasted_iota(jnp.int32, (GATES_PER_PAIR, L), 0)
    lane = lax.broadcasted_iota(jnp.int32, (GATES_PER_PAIR, L), 1)
    fwd = (row % CHAINS) < 2
    fwd1 = fwd[:, :1]
    si = lax.broadcasted_iota(jnp.int32, (L, 2 * L), 0)
    ti = lax.broadcasted_iota(jnp.int32, (L, 2 * L), 1)
    tri = jnp.where(ti < L, jnp.where(si <= ti, 1.0, 0.0), jnp.where(si >= ti - L, 1.0, 0.0))
    tot, mloc, b_l, cm_l, r_l = [], [], [], [], []
    for c in range(NC):
        pr = jnp.dot(lf8[:, c * L:(c + 1) * L], tri, precision=hi, preferred_element_type=F32)
        b_c = jnp.where(fwd, pr[:, :L], pr[:, L:])
        tot_c = pr[:, L - 1:L]
        li_c = li8[:, c * L:(c + 1) * L]
        r_c = li_c - b_c
        cm = r_c
        k = 1
        while k < L:
            pre = jnp.where(lane >= k, pltpu.roll(cm, k, 1), neg)
            suf = jnp.where(lane < L - k, pltpu.roll(cm, L - k, 1), neg)
            cm = jnp.maximum(cm, jnp.where(fwd, pre, suf))
            k *= 2
        g_c = tot_c - b_c + li_c
        mloc_c = jnp.max(g_c, axis=1, keepdims=True)
        wl_s[c] = jnp.exp2(g_c - mloc_c)
        r_l.append(r_c)
        tot.append(tot_c)
        mloc.append(mloc_c)
        b_l.append(b_c)
        cm_l.append(cm)

    def scan(order):
        m = jnp.zeros((GATES_PER_PAIR, 1), F32)
        m_in, dec, sc = [None] * NC, [None] * NC, [None] * NC
        for c in order:
            m_new = jnp.maximum(tot[c] + m, mloc[c])
            m_in[c] = m
            dec[c] = jnp.exp2(tot[c] + m - m_new)
            sc[c] = jnp.exp2(mloc[c] - m_new)
            m = m_new
        return m_in, dec, sc

    mf, df, sf = scan(range(NC))
    mb, db, sb = scan(range(NC - 1, -1, -1))
    def terms3(x):
        t0 = x.astype(BF16).astype(F32)
        rem = x - t0
        t1 = rem.astype(BF16).astype(F32)
        return t0, t1, (rem - t1).astype(BF16).astype(F32)

    blk_rows = 2 * SUBLANES
    rowi = lax.broadcasted_iota(jnp.int32, (blk_rows, L), 0)
    bro = lambda x, ci: jnp.broadcast_to(x[ci:ci + 1, :], (blk_rows, L))
    zblk = jnp.zeros((blk_rows, L), F32)
    pad = jnp.zeros((LANES - GATES_PER_PAIR, L), F32)
    for c in range(NC):
        m_in = jnp.where(fwd1, mf[c], mb[c])
        dec_s[c] = jnp.broadcast_to(jnp.where(fwd1, df[c], db[c]), (GATES_PER_PAIR, L))
        sc_s[c] = jnp.broadcast_to(jnp.where(fwd1, sf[c], sb[c]), (GATES_PER_PAIR, L))
        a_c = jnp.maximum(m_in, cm_l[c])
        winter_s[c] = jnp.exp2(m_in - a_c)
        emt_c = jnp.exp2(-(b_l[c] + a_c))
        col_s[c * L:(c + 1) * L, :] = jnp.concatenate([emt_c, pad], axis=0).T
        a0, a1, a2 = terms3(a_c)
        r0, r1, r2 = terms3(r_l[c])
        a_blks, r_cols = [], []
        for ci in range(CHAINS):
            a_blks.append(jnp.where(rowi == 0, bro(a0, ci), jnp.where(rowi == 1, bro(a1, ci), jnp.where(
                rowi == 2, bro(a2, ci), jnp.where(rowi < 6, 1.0, 0.0)))))
            r_blk = jnp.where(rowi < 3, -1.0, jnp.where(rowi == 3, bro(r0, ci), jnp.where(
                rowi == 4, bro(r1, ci), jnp.where(rowi == 5, bro(r2, ci), 0.0))))
            r_cols.append(jnp.concatenate([zblk] * ci + [r_blk] + [zblk] * (LANES // blk_rows - 1 - ci), axis=0))
        a_rows = jnp.concatenate(a_blks + [zblk] * (LANES // blk_rows - CHAINS), axis=0)
        acol_s[c * L:(c + 1) * L, :] = a_rows.T.astype(BF16)
        rmat_s[c] = jnp.concatenate(r_cols, axis=1).astype(BF16)

    qs = (_conv_silu(q_ref[...].astype(F32), cwq_ref[...], pad_s) * (DQ ** -0.5)).astype(BF16)
    q0_s[...] = qs[:, :DQ]
    q1_s[...] = qs[:, DQ:]
    kt = _conv_silu(k_ref[...].astype(F32), cwk_ref[...], pad_s).T
    for c in range(NC):
        kt_s[c] = kt[:, c * L:(c + 1) * L]

    ones_blk = jnp.ones((L, DV), BF16)

    def vext_of(sl, hh):
        return jnp.concatenate([v_ref[sl, DV * hh:DV * (hh + 1)], ones_blk], axis=1)

    cst_s[...] = jnp.zeros(cst_s.shape, F32)

    def state_step(c):
        for d in range(2):
            ch = c if d == 0 else NC - 1 - c
            sl = pl.ds(pl.multiple_of(ch * L, L), L)
            wl = wl_s[ch]
            ktc = kt_s[ch]
            decs = dec_s[ch]
            scs = sc_s[ch]
            for hh in range(2):
                ci = d * 2 + hh
                cx = cst_s[ci]
                call_s[ch, ci] = cx.astype(BF16)
                kw = (ktc[DQ * hh:DQ * (hh + 1), :] * wl[ci:ci + 1, :]).astype(BF16)
                cst_s[ci] = (decs[ci:ci + 1, 0:1] * cx
                             + scs[ci:ci + 1, 0:1] * jnp.dot(kw, vext_of(sl, hh), preferred_element_type=F32))

    def state_body(c, carry):
        for u in range(STATE_UNROLL):
            state_step(STATE_UNROLL * c + u)
        return carry

    lax.fori_loop(0, NC // STATE_UNROLL, state_body, 0)

    tt = lax.broadcasted_iota(jnp.int32, (L, L), 0)
    ss = lax.broadcasted_iota(jnp.int32, (L, L), 1)
    masks = (ss <= tt, ss >= tt)

    eye = tt == ss

    def chunk_out(ch, hh, sl, cols, winter, e_all):
        og = _sigmoid(o_ref[sl, DV * hh:DV * (hh + 1)].astype(F32))
        qc = (q0_s if hh == 0 else q1_s)[sl, :]
        kth = kt_s[ch, DQ * hh:DQ * (hh + 1), :].astype(BF16)
        qk = jnp.dot(qc, kth, preferred_element_type=F32)
        vext = vext_of(sl, hh)
        hsum = None
        for d in range(2):
            ci = d * 2 + hh
            w_intra = jnp.exp2(jnp.where(masks[d], e_all[:, L * ci:L * (ci + 1)], neg))
            qcx = jnp.dot(qc, call_s[ch, ci], preferred_element_type=F32).astype(BF16)
            w_diag = jnp.where(eye, winter[ci:ci + 1, :], 0.0).astype(BF16)
            nd = jnp.dot(jnp.concatenate([(qk * w_intra).astype(BF16), w_diag], axis=1),
                         jnp.concatenate([vext, qcx], axis=0),
                         preferred_element_type=F32)
            h = nd[:, :DV] / jnp.maximum(jnp.abs(nd[:, DV:]), cols[:, ci:ci + 1])
            hsum = h if hsum is None else hsum + h
        y = hsum * lax.rsqrt(jnp.mean(hsum * hsum, axis=1, keepdims=True) + NORM_EPS) * gh_ref[hh:hh + 1, :]
        out_ref[sl, DV * hh:DV * (hh + 1)] = (og * y).astype(out_ref.dtype)

    def out_chunk(c):
        sl = pl.ds(pl.multiple_of(c * L, L), L)
        e_all = jnp.dot(acol_s[sl, :], rmat_s[c], preferred_element_type=F32)
        cols = col_s[sl, :]
        winter = winter_s[c]
        chunk_out(c, 0, sl, cols, winter, e_all)
        chunk_out(c, 1, sl, cols, winter, e_all)

    def out_body(c, carry):
        for u in range(OUT_UNROLL):
            out_chunk(OUT_UNROLL * c + u)
        return carry

    lax.fori_loop(0, NC // OUT_UNROLL, out_body, 0)


def _mlstm(proj, grow, conv_w, bias_r, g_head, B, S):
    T = B * S
    NC = S // MLSTM_LC
    pw = 2 * MLSTM_DQK
    vw = 2 * MLSTM_DV
    return pl.pallas_call(
        _mlstm_kernel,
        grid=(B, HEAD_PAIRS),
        in_specs=[
            pl.BlockSpec((S, pw), lambda b, p: (b, OFF_MQ // pw + p)),
            pl.BlockSpec((S, pw), lambda b, p: (b, OFF_MK // pw + p)),
            pl.BlockSpec((S, vw), lambda b, p: (b, OFF_MV // vw + p)),
            pl.BlockSpec((S, vw), lambda b, p: (b, OFF_MO // vw + p)),
            pl.BlockSpec((None, GATES_PER_PAIR, S), lambda b, p: (p, 0, b)),
            pl.BlockSpec((CONV_W, pw), lambda b, p: (0, p)),
            pl.BlockSpec((CONV_W, pw), lambda b, p: (0, MQ_COLS // pw + p)),
            pl.BlockSpec((None, GATES_PER_PAIR, 1), lambda b, p: (p, 0, 0)),
            pl.BlockSpec((None, 2, MLSTM_DV), lambda b, p: (p, 0, 0)),
        ],
        out_specs=pl.BlockSpec((S, vw), lambda b, p: (b, p)),
        out_shape=jax.ShapeDtypeStruct((T, MLSTM_WIDTH), BF16),
        scratch_shapes=[
            pltpu.VMEM((S, MLSTM_DQK), BF16),
            pltpu.VMEM((S, MLSTM_DQK), BF16),
            pltpu.VMEM((NC, pw, MLSTM_LC), F32),
            pltpu.VMEM((S, LANES), F32),
            pltpu.VMEM((NC, GATES_PER_PAIR, MLSTM_LC), F32),
            pltpu.VMEM((NC, GATES_PER_PAIR, MLSTM_LC), F32),
            pltpu.VMEM((NC, GATES_PER_PAIR, MLSTM_LC), F32),
            pltpu.VMEM((NC, GATES_PER_PAIR, MLSTM_LC), F32),
            pltpu.VMEM((CHAINS, MLSTM_DQK, 2 * MLSTM_DV), F32),
            pltpu.VMEM((NC, CHAINS, MLSTM_DQK, 2 * MLSTM_DV), BF16),
            pltpu.VMEM((S + 2 * SUBLANES, pw), F32),
            pltpu.VMEM((S, LANES), BF16),
            pltpu.VMEM((NC, LANES, CHAINS * MLSTM_LC), BF16),
        ],
        compiler_params=_cparams(("parallel", "parallel")),
        name="mlstm",
    )(proj, proj, proj, proj, grow, conv_w, conv_w, bias_r, g_head)


def _norm_rope(x, g, cos, sin_signed):
    xn = x * lax.rsqrt(jnp.mean(x * x, axis=1, keepdims=True) + NORM_EPS) * g
    lane = lax.broadcasted_iota(jnp.int32, x.shape, 1)
    first_half = (lane % (2 * ROPE_FREQS)) < ROPE_FREQS
    partner = jnp.where(first_half,
                        pltpu.roll(xn, LANES - ROPE_FREQS, 1),
                        pltpu.roll(xn, ROPE_FREQS, 1))
    return xn * cos + partner * sin_signed


def _attn_kernel(q_ref, k_ref, v_ref, cq_ref, sq_ref, ck_ref, sk_ref, gq_ref, gk_ref, o_ref, kr_s, vx_s):
    d = ATTN_HEAD_DIM

    @pl.when(pl.program_id(2) == 0)
    def _():
        kr_s[...] = _norm_rope(k_ref[...].astype(F32), gk_ref[...], ck_ref[...], sk_ref[...]).astype(BF16)
        vx_s[...] = jnp.concatenate([v_ref[...], jnp.ones(v_ref.shape, BF16)], axis=1)

    gq = gq_ref[...]
    kr = kr_s[...]
    vx = vx_s[...]
    for r0 in range(0, q_ref.shape[0], ATTN_CHAIN_ROWS):
        rows = slice(r0, r0 + ATTN_CHAIN_ROWS)
        cq, sq = cq_ref[rows, :], sq_ref[rows, :]
        for g in range(ATTN_GROUP):
            cols = slice(d * g, d * (g + 1))
            qg = (_norm_rope(q_ref[rows, cols].astype(F32), gq, cq, sq) * (d ** -0.5 * LOG2E)).astype(BF16)
            s = lax.dot_general(qg, kr, (((1,), (1,)), ((), ())), preferred_element_type=F32)
            p = jnp.exp2(s - jnp.max(s, axis=1, keepdims=True))
            ov = jnp.dot(p.astype(BF16), vx, preferred_element_type=F32)
            o_ref[rows, cols] = (ov[:, :d] / ov[:, d:]).astype(o_ref.dtype)


def _attn(proj, cos_t, sin_t, g_q, g_k, B, S, tq=1024):
    T = B * S
    d = ATTN_HEAD_DIM
    gw = ATTN_GROUP * d
    nq = S // tq
    return pl.pallas_call(
        _attn_kernel,
        grid=(B, ATTN_KV_HEADS, nq),
        in_specs=[
            pl.BlockSpec((tq, gw), lambda b, kv, qi: (b * nq + qi, OFF_AQ // gw + kv)),
            pl.BlockSpec((S, d), lambda b, kv, qi: (b, OFF_AK // d + kv)),
            pl.BlockSpec((S, d), lambda b, kv, qi: (b, OFF_AV // d + kv)),
            pl.BlockSpec((tq, d), lambda b, kv, qi: (qi, 0)),
            pl.BlockSpec((tq, d), lambda b, kv, qi: (qi, 0)),
            _resident((S, d), lambda b, kv, qi: (0, 0)),
            _resident((S, d), lambda b, kv, qi: (0, 0)),
            _resident((1, d), lambda b, kv, qi: (0, 0)),
            _resident((1, d), lambda b, kv, qi: (0, 0)),
        ],
        out_specs=pl.BlockSpec((tq, gw), lambda b, kv, qi: (b * nq + qi, kv)),
        out_shape=jax.ShapeDtypeStruct((T, ATTN_WIDTH), BF16),
        scratch_shapes=[pltpu.VMEM((S, d), BF16), pltpu.VMEM((S, 2 * d), BF16)],
        compiler_params=_cparams(("parallel", "parallel", "arbitrary")),
        name="attn",
    )(proj, proj, proj, cos_t, sin_t, cos_t, sin_t, g_q, g_k)


def _layer_norm(y, g, b):
    mu = jnp.mean(y, axis=1, keepdims=True)
    yc = y - mu
    var = jnp.mean(yc * yc, axis=1, keepdims=True)
    return yc * lax.rsqrt(var + NORM_EPS) * g + b


def _split_bf16(x):
    hi = x.astype(BF16)
    return hi, (x - hi.astype(F32)).astype(BF16)


def _tile_perm():
    n = TOK_TILE * COL_BLOCKS
    a = lax.broadcasted_iota(jnp.int32, (n, n), 0)
    b = lax.broadcasted_iota(jnp.int32, (n, n), 1)
    t = TOK_TILE
    return jnp.where(a // t == b % t, jnp.where(a % t == b // t, 1.0, 0.0), 0.0).astype(BF16)


def _token_major(xh, perm):
    out = []
    for r0 in range(0, xh.shape[0], 2 * TOK_TILE):
        halves = [jnp.concatenate([xh[r0 + h * TOK_TILE:r0 + (h + 1) * TOK_TILE, LANES * j:LANES * (j + 1)]
                                   for j in range(COL_BLOCKS)], axis=0) for h in range(2)]
        o2 = jnp.dot(perm, jnp.concatenate(halves, axis=1), preferred_element_type=F32).astype(BF16)
        out += [o2[:, :LANES], o2[:, LANES:]]
    return jnp.concatenate(out, axis=0)


def _row_major(g_ref, perm):
    n = TOK_TILE * COL_BLOCKS
    cols = [[] for _ in range(COL_BLOCKS)]
    for r0 in range(0, g_ref.shape[0], 2 * n):
        m2 = jnp.concatenate([g_ref[r0:r0 + n, :], g_ref[r0 + n:r0 + 2 * n, :]], axis=1)
        o2 = jnp.dot(perm, m2, preferred_element_type=F32).astype(BF16)
        for j in range(COL_BLOCKS):
            blk = o2[TOK_TILE * j:TOK_TILE * (j + 1), :]
            cols[j] += [blk[:, :LANES], blk[:, LANES:]]
    return jnp.concatenate([jnp.concatenate(cj, axis=0) for cj in cols], axis=1)


def _outproj_kernel(hm_ref, ha_ref, x_ref, wt_ref, wb_ref, g_ref, b_ref, wr_ref, x1_ref, xg_ref, aff_ref):
    y = (ALPHA * x_ref[...]
         + jnp.dot(hm_ref[...], wt_ref[...], preferred_element_type=F32)
         + jnp.dot(ha_ref[...], wb_ref[...], preferred_element_type=F32))
    x1 = _layer_norm(y, g_ref[...], b_ref[...])
    x1_ref[...] = x1
    xh, xl = _split_bf16(x1)
    wh, wl = _split_bf16(wr_ref[...])
    xg_ref[...] = _token_major(xh, _tile_perm())
    nt = (((1,), (1,)), ((), ()))
    logits = (lax.dot_general(wh, xh, nt, preferred_element_type=F32)
              + lax.dot_general(wh, xl, nt, preferred_element_type=F32)
              + lax.dot_general(wl, xh, nt, preferred_element_type=F32))
    e = jnp.exp(logits - jnp.max(logits, axis=0, keepdims=True))
    aff_ref[:, 0, :] = e / jnp.sum(e, axis=0, keepdims=True)


def _outproj(hm, ha, x2, wt, wb, g, b, wr, S, tm=512):
    T, D = x2.shape
    nt = S // tm
    return pl.pallas_call(
        _outproj_kernel,
        grid=(T // tm,),
        in_specs=[
            pl.BlockSpec((tm, MLSTM_WIDTH), lambda i: (i, 0)),
            pl.BlockSpec((tm, ATTN_WIDTH), lambda i: (i, 0)),
            pl.BlockSpec((tm, D), lambda i: (i, 0)),
            _resident((MLSTM_WIDTH, D), lambda i: (0, 0)),
            _resident((ATTN_WIDTH, D), lambda i: (MLSTM_WIDTH // ATTN_WIDTH, 0)),
            _resident((1, D), lambda i: (0, 0)),
            _resident((1, D), lambda i: (0, 0)),
            _resident((N_EXPERTS, D), lambda i: (0, 0)),
        ],
        out_specs=[
            pl.BlockSpec((tm, D), lambda i: (i, 0)),
            pl.BlockSpec((tm * TOK_TILE, LANES), lambda i: (i, 0)),
            pl.BlockSpec((N_EXPERTS, None, 1, tm), lambda i: (0, i // nt, 0, i % nt)),
        ],
        out_shape=[
            jax.ShapeDtypeStruct((T, D), F32),
            jax.ShapeDtypeStruct((T * TOK_TILE, LANES), BF16),
            jax.ShapeDtypeStruct((N_EXPERTS, T // S, 1, S), F32),
        ],
        compiler_params=_cparams(("parallel",)),
        name="outproj",
    )(hm, ha, x2, wt, wb, g, b, wr)


def _excl_prefix_count(m):
    E, S = m.shape
    nb = S // LANES
    s_i = lax.broadcasted_iota(jnp.int32, (LANES, 2 * LANES), 0)
    t_i = lax.broadcasted_iota(jnp.int32, (LANES, 2 * LANES), 1)
    tri_ext = jnp.where(t_i >= LANES, 1.0, jnp.where(s_i < t_i, 1.0, 0.0)).astype(BF16)
    r_i = lax.broadcasted_iota(jnp.int32, (nb * E, nb * E), 0)
    c_i = lax.broadcasted_iota(jnp.int32, (nb * E, nb * E), 1)
    earlier = jnp.where(c_i // E < r_i // E, jnp.where(c_i % E == r_i % E, 1.0, 0.0), 0.0).astype(BF16)
    stacked = jnp.concatenate([m[:, LANES * j:LANES * (j + 1)] for j in range(nb)], axis=0)
    r = jnp.dot(stacked, tri_ext, preferred_element_type=F32)
    off = jnp.dot(earlier, r[:, LANES:].astype(BF16), preferred_element_type=F32)
    tot = r[:, :LANES] + off
    return jnp.concatenate([tot[E * j:E * (j + 1), :] for j in range(nb)], axis=1)


def _topk_kernel(aff_ref, pos_ref, idx_ref, *, cap):
    S = aff_ref.shape[2]
    a = aff_ref[:, 0, :]

    def count_ge(v):
        return jnp.sum(jnp.where(a >= v, 1.0, 0.0), axis=1, keepdims=True)

    def keep_if_enough(cand, thr):
        return jnp.where(count_ge(pltpu.bitcast(cand, F32)) >= cap, cand, thr)

    thr = keep_if_enough(jnp.full((a.shape[0], 1), 1 << 30, jnp.int32), jnp.zeros((a.shape[0], 1), jnp.int32))
    for bit in range(28, -1, -2):
        b1, b0 = 1 << (bit + 1), 1 << bit
        thr = keep_if_enough(thr | b1 | b0, keep_if_enough(thr | b1, keep_if_enough(thr | b0, thr)))
    lo = pltpu.bitcast(thr, F32)
    hi = pltpu.bitcast(thr + 1, F32)
    for _ in range(TOPK_REFINE_STEPS):
        mid = 0.5 * (lo + hi)
        ok = count_ge(mid) >= cap
        lo = jnp.where(ok, mid, lo)
        hi = jnp.where(ok, hi, mid)
    gt = a > lo
    eq = a == lo
    need = cap - jnp.sum(jnp.where(gt, 1.0, 0.0), axis=1, keepdims=True)
    eq_rank = _excl_prefix_count(jnp.where(eq, 1.0, 0.0).astype(BF16))
    sel = jnp.logical_or(gt, jnp.logical_and(eq, eq_rank < need))
    pos = _excl_prefix_count(jnp.where(sel, 1.0, 0.0).astype(BF16))
    posm = jnp.where(sel, pos, -1.0)
    pos_ref[:, 0, :] = posm
    digit_row = lax.broadcasted_iota(jnp.int32, (2 * SUBLANES, S), 0)
    tok = lax.broadcasted_iota(jnp.int32, (2 * SUBLANES, S), 1)
    digits = jnp.where(digit_row == 0, tok // TOK_TILE,
                       jnp.where(digit_row == 1, tok % TOK_TILE, 0)).astype(F32).astype(BF16)
    slot = lax.broadcasted_iota(jnp.int32, (cap, S), 0).astype(F32)
    for e in range(a.shape[0]):
        hit = jnp.where(slot == posm[e:e + 1, :], 1.0, 0.0).astype(BF16)
        idx_ref[e] = lax.dot_general(digits, hit, (((1,), (1,)), ((), ())),
                                     preferred_element_type=F32)[:SUBLANES]


def _topk(aff, B, S, cap):
    E = aff.shape[0]
    assert S <= 256 * TOK_TILE, "token index digits must be integers bf16 holds exactly"
    return pl.pallas_call(
        functools.partial(_topk_kernel, cap=cap),
        grid=(B,),
        in_specs=[pl.BlockSpec((E, None, 1, S), lambda b: (0, b, 0, 0))],
        out_specs=[pl.BlockSpec((E, None, 1, S), lambda b: (0, b, 0, 0)),
                   pl.BlockSpec((E, None, SUBLANES, cap), lambda b: (0, b, 0, 0))],
        out_shape=[jax.ShapeDtypeStruct((E, B, 1, S), F32),
                   jax.ShapeDtypeStruct((E, B, SUBLANES, cap), F32)],
        compiler_params=_cparams(("parallel",)),
        name="topk",
    )(aff)


def _moe_ffn_kernel(idx_ref, xg_ref, pos_ref, aff_ref, wg_ref, wu_ref, wd_ref, y_ref, gbuf_s, wg_s, wu_s, wd_s,
                    *, cap, n_exp):
    e = pl.program_id(0)
    b = pl.program_id(1)

    @pl.when(e < n_exp)
    def _():
        slot = e % 2
        rg = pl.multiple_of(b * wg_ref.shape[0], wg_ref.shape[0])
        rd = pl.multiple_of(b * wd_ref.shape[0], wd_ref.shape[0])
        wg_s[slot, pl.ds(rg, wg_ref.shape[0]), :] = wg_ref[...].astype(BF16)
        wu_s[slot, pl.ds(rg, wu_ref.shape[0]), :] = wu_ref[...].astype(BF16)
        wd_s[slot, pl.ds(rd, wd_ref.shape[0]), :] = wd_ref[...].astype(BF16)

    @pl.when(e > 0)
    def _():
        S = pos_ref.shape[1]
        slot = (e - 1) % 2
        for c in range(cap):
            row = pl.multiple_of(idx_ref[0, c] * TOK_TILE, TOK_TILE)
            gbuf_s[TOK_TILE * c:TOK_TILE * (c + 1), :] = xg_ref[pl.ds(row, TOK_TILE), :]
        perm = _tile_perm()
        xs = _row_major(gbuf_s, perm)
        sl = lax.broadcasted_iota(jnp.int32, (cap, S), 0).astype(F32)
        gate = jnp.sum(jnp.where(sl == pos_ref[...], aff_ref[...], 0.0), axis=1, keepdims=True)
        hg = jnp.dot(xs, wg_s[slot], preferred_element_type=F32)
        hu = jnp.dot(xs, wu_s[slot], preferred_element_type=F32)
        hid = (hg * _sigmoid(hg) * hu).astype(BF16)
        y = (jnp.dot(hid, wd_s[slot], preferred_element_type=F32) * gate).astype(BF16)
        y_ref[...] = y


def _moe_ffn(idx, xg, pos4, aff4, wg, wu, wd, B, S, cap):
    E, D, F = wg.shape
    prev = lambda e: jnp.maximum(e - 1, 0)
    stage = lambda e, b: (jnp.minimum(e, E - 1), jnp.where(e < E, b, B - 1), 0)
    return pl.pallas_call(
        functools.partial(_moe_ffn_kernel, cap=cap, n_exp=E),
        grid=(E + 1, B),
        in_specs=[
            pl.BlockSpec((None, None, 1, cap), lambda e, b: (prev(e), b, 0, 0), memory_space=pltpu.SMEM),
            pl.BlockSpec((S * TOK_TILE, LANES), lambda e, b: (b, 0)),
            pl.BlockSpec((None, None, 1, S), lambda e, b: (prev(e), b, 0, 0)),
            pl.BlockSpec((None, None, 1, S), lambda e, b: (prev(e), b, 0, 0)),
            pl.BlockSpec((None, D // B, F), stage),
            pl.BlockSpec((None, D // B, F), stage),
            pl.BlockSpec((None, F // B, D), stage),
        ],
        out_specs=pl.BlockSpec((None, None, cap, D), lambda e, b: (jnp.where(e > 0, b, 0), prev(e), 0, 0)),
        out_shape=jax.ShapeDtypeStruct((B, E, cap, D), BF16),
        scratch_shapes=[pltpu.VMEM((cap * TOK_TILE, LANES), BF16),
                        pltpu.VMEM((2, D, F), BF16), pltpu.VMEM((2, D, F), BF16), pltpu.VMEM((2, F, D), BF16)],
        compiler_params=_cparams(("arbitrary", "arbitrary")),
        name="moe_ffn",
    )(idx, xg, pos4, aff4, wg, wu, wd)


def _row_major_f32(g_ref, r0, n_tok, perm):
    n = TOK_TILE * COL_BLOCKS
    cols = [[] for _ in range(COL_BLOCKS)]
    perm3 = jnp.concatenate([perm, perm, perm], axis=1)
    for k in range(0, n_tok * TOK_TILE, 2 * n):
        m2 = jnp.concatenate([g_ref[pl.ds(r0 + k, n), :], g_ref[pl.ds(r0 + k + n, n), :]], axis=1)
        hi = m2.astype(BF16)
        r1 = m2 - hi.astype(F32)
        mid = r1.astype(BF16)
        lo = (r1 - mid.astype(F32)).astype(BF16)
        o2 = jnp.dot(perm3, jnp.concatenate([hi, mid, lo], axis=0),
                     preferred_element_type=F32)
        for j in range(COL_BLOCKS):
            blk = o2[TOK_TILE * j:TOK_TILE * (j + 1), :]
            cols[j] += [blk[:, :LANES], blk[:, LANES:]]
    return jnp.concatenate([jnp.concatenate(cj, axis=0) for cj in cols], axis=1)


def _moe_comb_kernel(idx_ref, y_ref, o_ref, acc_s, *, cap, n_grp):
    j = pl.program_id(1)

    @pl.when(j == 0)
    def _():
        acc_s[...] = jnp.zeros(acc_s.shape, F32)

    @pl.when(j < n_grp)
    def _():
        perm = _tile_perm()
        for g in range(idx_ref.shape[0]):
            y_tm = _token_major(y_ref[g], perm)
            for c0 in range(0, cap, SCATTER_BATCH):
                rows = [pl.multiple_of(idx_ref[g, 0, c0 + u] * TOK_TILE, TOK_TILE) for u in range(SCATTER_BATCH)]
                new = [acc_s[pl.ds(rows[u], TOK_TILE), :]
                       + y_tm[TOK_TILE * (c0 + u):TOK_TILE * (c0 + u + 1), :].astype(F32)
                       for u in range(SCATTER_BATCH)]
                for u in range(SCATTER_BATCH):
                    acc_s[pl.ds(rows[u], TOK_TILE), :] = new[u]

    @pl.when(j >= n_grp)
    def _():
        n_tok = o_ref.shape[0]
        r0 = pl.multiple_of((j - n_grp) * (n_tok * TOK_TILE), n_tok * TOK_TILE)
        o_ref[...] = _row_major_f32(acc_s, r0, n_tok, _tile_perm())


def _moe_comb(idx, y, B, S, cap, chunk=1024, eg=4):
    E = y.shape[1]
    D = D_MODEL
    nch = S // chunk
    ng = E // eg
    return pl.pallas_call(
        functools.partial(_moe_comb_kernel, cap=cap, n_grp=ng),
        grid=(B, ng + nch),
        in_specs=[
            pl.BlockSpec((eg, None, 1, cap), lambda b, j: (jnp.minimum(j, ng - 1), b, 0, 0),
                         memory_space=pltpu.SMEM),
            pl.BlockSpec((None, eg, cap, D), lambda b, j: (b, jnp.minimum(j, ng - 1), 0, 0)),
        ],
        out_specs=pl.BlockSpec((chunk, D), lambda b, j: (b * nch + jnp.maximum(j - ng, 0), 0)),
        out_shape=jax.ShapeDtypeStruct((B * S, D), F32),
        scratch_shapes=[pltpu.VMEM((S * TOK_TILE, LANES), F32)],
        compiler_params=_cparams(("arbitrary", "arbitrary")),
        name="moe_comb",
    )(idx, y)


def _final_kernel(x1_ref, moe_ref, p_ref, wpg_ref, bpg_ref, wpp_ref, g_ref, b_ref, o_ref):
    for r0 in range(0, x1_ref.shape[0], ROW_CHAIN):
        rows = slice(r0, r0 + ROW_CHAIN)
        x1 = x1_ref[rows, :]
        gate = _sigmoid(jnp.dot(x1.astype(BF16), wpg_ref[...], preferred_element_type=F32) + bpg_ref[...])
        plv = gate * jnp.dot(p_ref[rows, :].astype(BF16), wpp_ref[...], preferred_element_type=F32)
        o_ref[rows, :] = _layer_norm(ALPHA * x1 + moe_ref[rows, :] + plv, g_ref[...], b_ref[...])


def _final(x1, moe, p2, wpg, bpg, wpp, g, b, tm=512):
    T, D = x1.shape
    return pl.pallas_call(
        _final_kernel,
        grid=(T // tm,),
        in_specs=[
            pl.BlockSpec((tm, D), lambda i: (i, 0)),
            pl.BlockSpec((tm, D), lambda i: (i, 0)),
            pl.BlockSpec((tm, P_DIM), lambda i: (i, 0)),
            _resident((D, D), lambda i: (0, 0)),
            _resident((1, D), lambda i: (0, 0)),
            _resident((P_DIM, D), lambda i: (0, 0)),
            _resident((1, D), lambda i: (0, 0)),
            _resident((1, D), lambda i: (0, 0)),
        ],
        out_specs=pl.BlockSpec((tm, D), lambda i: (i, 0)),
        out_shape=jax.ShapeDtypeStruct((T, D), F32),
        compiler_params=_cparams(("parallel",)),
        name="final",
    )(x1, moe, p2, wpg, bpg, wpp, g, b)


def _rope_tables(S):
    rows = S // GRID_W
    row_idx = np.repeat(np.arange(rows, dtype=np.float64), GRID_W)
    col_idx = np.tile(np.arange(GRID_W, dtype=np.float64), rows)
    inv_freq = ROPE_THETA ** (-np.arange(ROPE_FREQS, dtype=np.float64) / ROPE_FREQS)
    ar = row_idx[:, None] * inv_freq
    ac = col_idx[:, None] * inv_freq
    cos_t = np.concatenate([np.cos(ar), np.cos(ar), np.cos(ac), np.cos(ac)], axis=1)
    sin_t = np.concatenate([-np.sin(ar), np.sin(ar), -np.sin(ac), np.sin(ac)], axis=1)
    return jnp.asarray(cos_t, F32), jnp.asarray(sin_t, F32)


def _pair_major(a):
    return a.reshape(2, HEAD_PAIRS, 2).transpose(1, 0, 2)


def _layer(x2, p2, B, S, w_in, conv_w, b_i, b_f, g_mlstm, g_q, g_k, w_out, ln1_g, ln1_b, w_router, w_gate, w_up,
           w_down, w_pl_proj, w_pl_gate, b_pl_gate, ln2_g, ln2_b):
    D = D_MODEL
    cap = CAPACITY_FACTOR * S // N_EXPERTS
    o_mg = 2 * MQ_COLS + 2 * MV_COLS
    w_main = jnp.concatenate([w_in[:, :o_mg], w_in[:, o_mg + MG_COLS:]], axis=1).astype(BF16)
    w_g = w_in[:, o_mg:o_mg + MG_COLS].reshape(D, 2, 2, HEAD_PAIRS, 2).transpose(0, 3, 2, 1, 4).reshape(D, MG_COLS)
    bias = jnp.stack([_pair_major(b_i), _pair_major(b_f)], axis=1).reshape(HEAD_PAIRS, GATES_PER_PAIR)

    proj, grow = _proj(x2, w_main, w_g.astype(BF16).T)
    h_m = _mlstm(proj, grow, conv_w, bias[:, :, None], g_mlstm.reshape(HEAD_PAIRS, 2, MLSTM_DV), B, S)
    cos_t, sin_t = _rope_tables(S)
    h_a = _attn(proj, cos_t, sin_t, g_q[None, :], g_k[None, :], B, S)
    w_o = w_out.astype(BF16)
    x1, xg, aff4 = _outproj(h_m, h_a, x2, w_o, w_o, ln1_g[None, :], ln1_b[None, :], w_router.T, S)
    pos4, idx_digits = _topk(aff4, B, S, cap)
    idx = (idx_digits[:, :, 0:1, :] * TOK_TILE + idx_digits[:, :, 1:2, :]).astype(jnp.int32)
    y = _moe_ffn(idx, xg, pos4, aff4, w_gate, w_up, w_down, B, S, cap)
    moe = _moe_comb(idx, y, B, S, cap)
    return _final(x1, moe, p2, w_pl_gate.astype(BF16), b_pl_gate[None, :], w_pl_proj.astype(BF16),
                  ln2_g[None, :], ln2_b[None, :])


def kernel(x, p, w_in, conv_w, b_igate, b_fgate, g_mlstm, g_q, g_k, w_out, ln1_g, ln1_b, w_router, w_gate, w_up,
           w_down, w_pl_proj, w_pl_gate, b_pl_gate, ln2_g, ln2_b):
    B, S, D = x.shape
    x2 = x.reshape(B * S, D)
    for i in range(DEPTH):
        x2 = _layer(x2, p[i].reshape(B * S, P_DIM), B, S, w_in[i], conv_w[i], b_igate[i], b_fgate[i], g_mlstm[i],
                    g_q[i], g_k[i], w_out[i], ln1_g[i], ln1_b[i], w_router[i], w_gate[i], w_up[i], w_down[i],
                    w_pl_proj[i], w_pl_gate[i], b_pl_gate[i], ln2_g[i], ln2_b[i])
    return x2.reshape(B, S, D)
```

```python
import functools

import jax
import jax.numpy as jnp
import numpy as np
from jax import lax
from jax.experimental import pallas as pl
from jax.experimental.pallas import tpu as pltpu

F32 = jnp.float32
BF16 = jnp.bfloat16

D_MODEL = 2048
P_DIM = 256
GRID_W = 64
MLSTM_WIDTH = D_MODEL // 2
ATTN_WIDTH = D_MODEL - MLSTM_WIDTH
MLSTM_HEADS = 8
MLSTM_DV = MLSTM_WIDTH // MLSTM_HEADS
MLSTM_DQK = MLSTM_DV // 2
CONV_W = 5
ATTN_HEAD_DIM = 128
ATTN_Q_HEADS = ATTN_WIDTH // ATTN_HEAD_DIM
ATTN_KV_HEADS = 2
ATTN_GROUP = ATTN_Q_HEADS // ATTN_KV_HEADS
ROPE_FREQS = ATTN_HEAD_DIM // 4
ROPE_THETA = 10000.0
N_EXPERTS = 16
EXPERT_FF = D_MODEL // 2
CAPACITY_FACTOR = 2
NORM_EPS = 1e-6
DEPTH = 1
ALPHA = (2.0 * DEPTH) ** 0.25
LOG2E = 1.4426950408889634

MQ_COLS = MLSTM_HEADS * MLSTM_DQK
MV_COLS = MLSTM_WIDTH
MG_COLS = 2 * 2 * MLSTM_HEADS
AQ_COLS = ATTN_WIDTH
AKV_COLS = ATTN_KV_HEADS * ATTN_HEAD_DIM
PROJ_COLS = 2 * MQ_COLS + 2 * MV_COLS + AQ_COLS + 2 * AKV_COLS
OFF_MQ, OFF_MK, OFF_MV, OFF_MO = 0, MQ_COLS, 2 * MQ_COLS, 2 * MQ_COLS + MV_COLS
OFF_AQ = OFF_MO + MV_COLS
OFF_AK = OFF_AQ + AQ_COLS
OFF_AV = OFF_AK + AKV_COLS
PROJ_COL_STEP = 1536

HEAD_PAIRS = MLSTM_HEADS // 2
GATES_PER_PAIR = 8
CHAINS = 4
MLSTM_LC = 128
STATE_UNROLL = 8
OUT_UNROLL = 8
ATTN_CHAIN_ROWS = 256
ROW_CHAIN = 256
LANES = 128
SUBLANES = 8
TOK_TILE = 16
COL_BLOCKS = D_MODEL // LANES
SCATTER_BATCH = 16
V7X_VMEM_LIMIT = 56 * 1024 * 1024
TOPK_REFINE_STEPS = 8


def _cparams(sem, vmem=V7X_VMEM_LIMIT):
    return pltpu.CompilerParams(dimension_semantics=sem, vmem_limit_bytes=vmem)


def _resident(shape, index_map):
    return pl.BlockSpec(shape, index_map, pipeline_mode=pl.Buffered(1))


def _sigmoid(x):
    return 0.5 * jnp.tanh(0.5 * x) + 0.5


def _log_sigmoid(x):
    return jnp.minimum(x, 0.0) - jnp.log1p(jnp.exp(-jnp.abs(x)))


def _proj_kernel(x_ref, w_ref, wgt_ref, o_ref, gr_ref):
    xb = x_ref[...].astype(BF16)
    gr = lax.dot_general(wgt_ref[...], xb, (((1,), (1,)), ((), ())), preferred_element_type=F32)
    for p in range(HEAD_PAIRS):
        gr_ref[p] = gr[GATES_PER_PAIR * p:GATES_PER_PAIR * (p + 1), :]
    tn = PROJ_COL_STEP
    for n0 in range(0, w_ref.shape[1], tn):
        o_ref[:, n0:n0 + tn] = jnp.dot(xb, w_ref[:, n0:n0 + tn], preferred_element_type=F32).astype(o_ref.dtype)


def _proj(x2, w, wgt, tm=512):
    T, D = x2.shape
    N = w.shape[1]
    return pl.pallas_call(
        _proj_kernel,
        grid=(T // tm,),
        in_specs=[
            pl.BlockSpec((tm, D), lambda i: (i, 0)),
            _resident((D, N), lambda i: (0, 0)),
            _resident((MG_COLS, D), lambda i: (0, 0)),
        ],
        out_specs=[
            pl.BlockSpec((tm, N), lambda i: (i, 0)),
            pl.BlockSpec((HEAD_PAIRS, GATES_PER_PAIR, tm), lambda i: (0, 0, i)),
        ],
        out_shape=[
            jax.ShapeDtypeStruct((T, N), BF16),
            jax.ShapeDtypeStruct((HEAD_PAIRS, GATES_PER_PAIR, T), F32),
        ],
        compiler_params=_cparams(("parallel",)),
        name="proj",
    )(x2, w, wgt)


def _conv_silu(x, w, pad_s):
    S, C = x.shape
    half = CONV_W // 2
    halo = jnp.zeros((SUBLANES, C), F32)
    pad_s[0:SUBLANES, :] = halo
    pad_s[SUBLANES + S:2 * SUBLANES + S, :] = halo
    pad_s[SUBLANES:SUBLANES + S, :] = x
    acc = x * w[half:half + 1, :]
    for j in range(CONV_W):
        if j != half:
            acc = acc + pad_s[SUBLANES + j - half:SUBLANES + j - half + S, :] * w[j:j + 1, :]
    return acc * _sigmoid(acc)


def _mlstm_kernel(q_ref, k_ref, v_ref, o_ref, gr_ref, cwq_ref, cwk_ref, br_ref, gh_ref, out_ref,
                  q0_s, q1_s, kt_s, col_s, winter_s, wl_s, dec_s, sc_s, cst_s, call_s, pad_s, acol_s, rmat_s):
    S = q_ref.shape[0]
    L = MLSTM_LC
    NC = S // L
    DQ = MLSTM_DQK
    DV = MLSTM_DV
    hi = lax.Precision.HIGHEST
    neg = -jnp.inf

    g8 = gr_ref[...] + br_ref[...]
    li8 = g8 * LOG2E
    lf8 = pltpu.roll(_log_sigmoid(g8) * LOG2E, CHAINS, 0)
    row = lax.broadcasted_iota(jnp.int32, (GATES_PER_PAIR, L), 0)
    lane = lax.broadcasted_iota(jnp.int32, (GATES_PER_PAIR, L), 1)
    fwd = (row % CHAINS) < 2
    fwd1 = fwd[:, :1]
    si = lax.broadcasted_iota(jnp.int32, (L, 2 * L), 0)
    ti = lax.broadcasted_iota(jnp.int32, (L, 2 * L), 1)
    tri = jnp.where(ti < L, jnp.where(si <= ti, 1.0, 0.0), jnp.where(si >= ti - L, 1.0, 0.0))
    tot, mloc, b_l, cm_l, r_l = [], [], [], [], []
    for c in range(NC):
        pr = jnp.dot(lf8[:, c * L:(c + 1) * L], tri, precision=hi, preferred_element_type=F32)
        b_c = jnp.where(fwd, pr[:, :L], pr[:, L:])
        tot_c = pr[:, L - 1:L]
        li_c = li8[:, c * L:(c + 1) * L]
        r_c = li_c - b_c
        cm = r_c
        k = 1
        while k < L:
            pre = jnp.where(lane >= k, pltpu.roll(cm, k, 1), neg)
            suf = jnp.where(lane < L - k, pltpu.roll(cm, L - k, 1), neg)
            cm = jnp.maximum(cm, jnp.where(fwd, pre, suf))
            k *= 2
        g_c = tot_c - b_c + li_c
        mloc_c = jnp.max(g_c, axis=1, keepdims=True)
        wl_s[c] = jnp.exp2(g_c - mloc_c)
        r_l.append(r_c)
        tot.append(tot_c)
        mloc.append(mloc_c)
        b_l.append(b_c)
        cm_l.append(cm)

    def scan(order):
        m = jnp.zeros((GATES_PER_PAIR, 1), F32)
        m_in, dec, sc = [None] * NC, [None] * NC, [None] * NC
        for c in order:
            m_new = jnp.maximum(tot[c] + m, mloc[c])
            m_in[c] = m
            dec[c] = jnp.exp2(tot[c] + m - m_new)
            sc[c] = jnp.exp2(mloc[c] - m_new)
            m = m_new
        return m_in, dec, sc

    mf, df, sf = scan(range(NC))
    mb, db, sb = scan(range(NC - 1, -1, -1))
    def terms3(x):
        t0 = x.astype(BF16).astype(F32)
        rem = x - t0
        t1 = rem.astype(BF16).astype(F32)
        return t0, t1, (rem - t1).astype(BF16).astype(F32)

    blk_rows = 2 * SUBLANES
    rowi = lax.broadcasted_iota(jnp.int32, (blk_rows, L), 0)
    bro = lambda x, ci: jnp.broadcast_to(x[ci:ci + 1, :], (blk_rows, L))
    zblk = jnp.zeros((blk_rows, L), F32)
    pad = jnp.zeros((LANES - GATES_PER_PAIR, L), F32)
    for c in range(NC):
        m_in = jnp.where(fwd1, mf[c], mb[c])
        dec_s[c] = jnp.broadcast_to(jnp.where(fwd1, df[c], db[c]), (GATES_PER_PAIR, L))
        sc_s[c] = jnp.broadcast_to(jnp.where(fwd1, sf[c], sb[c]), (GATES_PER_PAIR, L))
        a_c = jnp.maximum(m_in, cm_l[c])
        winter_s[c] = jnp.exp2(m_in - a_c)
        emt_c = jnp.exp2(-(b_l[c] + a_c))
        col_s[c * L:(c + 1) * L, :] = jnp.concatenate([emt_c, pad], axis=0).T
        a0, a1, a2 = terms3(a_c)
        r0, r1, r2 = terms3(r_l[c])
        a_blks, r_cols = [], []
        for ci in range(CHAINS):
            a_blks.append(jnp.where(rowi == 0, bro(a0, ci), jnp.where(rowi == 1, bro(a1, ci), jnp.where(
                rowi == 2, bro(a2, ci), jnp.where(rowi < 6, 1.0, 0.0)))))
            r_blk = jnp.where(rowi < 3, -1.0, jnp.where(rowi == 3, bro(r0, ci), jnp.where(
                rowi == 4, bro(r1, ci), jnp.where(rowi == 5, bro(r2, ci), 0.0))))
            r_cols.append(jnp.concatenate([zblk] * ci + [r_blk] + [zblk] * (LANES // blk_rows - 1 - ci), axis=0))
        a_rows = jnp.concatenate(a_blks + [zblk] * (LANES // blk_rows - CHAINS), axis=0)
        acol_s[c * L:(c + 1) * L, :] = a_rows.T.astype(BF16)
        rmat_s[c] = jnp.concatenate(r_cols, axis=1).astype(BF16)

    qs = (_conv_silu(q_ref[...].astype(F32), cwq_ref[...], pad_s) * (DQ ** -0.5)).astype(BF16)
    q0_s[...] = qs[:, :DQ]
    q1_s[...] = qs[:, DQ:]
    kt = _conv_silu(k_ref[...].astype(F32), cwk_ref[...], pad_s).T
    for c in range(NC):
        kt_s[c] = kt[:, c * L:(c + 1) * L]

    ones_blk = jnp.ones((L, DV), BF16)

    def vext_of(sl, hh):
        return jnp.concatenate([v_ref[sl, DV * hh:DV * (hh + 1)], ones_blk], axis=1)

    cst_s[...] = jnp.zeros(cst_s.shape, F32)

    def state_step(c):
        for d in range(2):
            ch = c if d == 0 else NC - 1 - c
            sl = pl.ds(pl.multiple_of(ch * L, L), L)
            wl = wl_s[ch]
            ktc = kt_s[ch]
            decs = dec_s[ch]
            scs = sc_s[ch]
            for hh in range(2):
                ci = d * 2 + hh
                cx = cst_s[ci]
                call_s[ch, ci] = cx.astype(BF16)
                kw = (ktc[DQ * hh:DQ * (hh + 1), :] * wl[ci:ci + 1, :]).astype(BF16)
                cst_s[ci] = (decs[ci:ci + 1, 0:1] * cx
                             + scs[ci:ci + 1, 0:1] * jnp.dot(kw, vext_of(sl, hh), preferred_element_type=F32))

    def state_body(c, carry):
        for u in range(STATE_UNROLL):
            state_step(STATE_UNROLL * c + u)
        return carry

    lax.fori_loop(0, NC // STATE_UNROLL, state_body, 0)

    tt = lax.broadcasted_iota(jnp.int32, (L, L), 0)
    ss = lax.broadcasted_iota(jnp.int32, (L, L), 1)
    masks = (ss <= tt, ss >= tt)

    eye = tt == ss

    def chunk_out(ch, hh, sl, cols, winter, e_all):
        og = _sigmoid(o_ref[sl, DV * hh:DV * (hh + 1)].astype(F32))
        qc = (q0_s if hh == 0 else q1_s)[sl, :]
        kth = kt_s[ch, DQ * hh:DQ * (hh + 1), :].astype(BF16)
        qk = jnp.dot(qc, kth, preferred_element_type=F32)
        vext = vext_of(sl, hh)
        hsum = None
        for d in range(2):
            ci = d * 2 + hh
            w_intra = jnp.exp2(jnp.where(masks[d], e_all[:, L * ci:L * (ci + 1)], neg))
            qcx = jnp.dot(qc, call_s[ch, ci], preferred_element_type=F32).astype(BF16)
            w_diag = jnp.where(eye, winter[ci:ci + 1, :], 0.0).astype(BF16)
            nd = jnp.dot(jnp.concatenate([(qk * w_intra).astype(BF16), w_diag], axis=1),
                         jnp.concatenate([vext, qcx], axis=0),
                         preferred_element_type=F32)
            h = nd[:, :DV] / jnp.maximum(jnp.abs(nd[:, DV:]), cols[:, ci:ci + 1])
            hsum = h if hsum is None else hsum + h
        y = hsum * lax.rsqrt(jnp.mean(hsum * hsum, axis=1, keepdims=True) + NORM_EPS) * gh_ref[hh:hh + 1, :]
        out_ref[sl, DV * hh:DV * (hh + 1)] = (og * y).astype(out_ref.dtype)

    def out_chunk(c):
        sl = pl.ds(pl.multiple_of(c * L, L), L)
        e_all = jnp.dot(acol_s[sl, :], rmat_s[c], preferred_element_type=F32)
        cols = col_s[sl, :]
        winter = winter_s[c]
        chunk_out(c, 0, sl, cols, winter, e_all)
        chunk_out(c, 1, sl, cols, winter, e_all)

    def out_body(c, carry):
        for u in range(OUT_UNROLL):
            out_chunk(OUT_UNROLL * c + u)
        return carry

    lax.fori_loop(0, NC // OUT_UNROLL, out_body, 0)


def _mlstm(proj, grow, conv_w, bias_r, g_head, B, S):
    T = B * S
    NC = S // MLSTM_LC
    pw = 2 * MLSTM_DQK
    vw = 2 * MLSTM_DV
    return pl.pallas_call(
        _mlstm_kernel,
        grid=(B, HEAD_PAIRS),
        in_specs=[
            pl.BlockSpec((S, pw), lambda b, p: (b, OFF_MQ // pw + p)),
            pl.BlockSpec((S, pw), lambda b, p: (b, OFF_MK // pw + p)),
            pl.BlockSpec((S, vw), lambda b, p: (b, OFF_MV // vw + p)),
            pl.BlockSpec((S, vw), lambda b, p: (b, OFF_MO // vw + p)),
            pl.BlockSpec((None, GATES_PER_PAIR, S), lambda b, p: (p, 0, b)),
            pl.BlockSpec((CONV_W, pw), lambda b, p: (0, p)),
            pl.BlockSpec((CONV_W, pw), lambda b, p: (0, MQ_COLS // pw + p)),
            pl.BlockSpec((None, GATES_PER_PAIR, 1), lambda b, p: (p, 0, 0)),
            pl.BlockSpec((None, 2, MLSTM_DV), lambda b, p: (p, 0, 0)),
        ],
        out_specs=pl.BlockSpec((S, vw), lambda b, p: (b, p)),
        out_shape=jax.ShapeDtypeStruct((T, MLSTM_WIDTH), BF16),
        scratch_shapes=[
            pltpu.VMEM((S, MLSTM_DQK), BF16),
            pltpu.VMEM((S, MLSTM_DQK), BF16),
            pltpu.VMEM((NC, pw, MLSTM_LC), F32),
            pltpu.VMEM((S, LANES), F32),
            pltpu.VMEM((NC, GATES_PER_PAIR, MLSTM_LC), F32),
            pltpu.VMEM((NC, GATES_PER_PAIR, MLSTM_LC), F32),
            pltpu.VMEM((NC, GATES_PER_PAIR, MLSTM_LC), F32),
            pltpu.VMEM((NC, GATES_PER_PAIR, MLSTM_LC), F32),
            pltpu.VMEM((CHAINS, MLSTM_DQK, 2 * MLSTM_DV), F32),
            pltpu.VMEM((NC, CHAINS, MLSTM_DQK, 2 * MLSTM_DV), BF16),
            pltpu.VMEM((S + 2 * SUBLANES, pw), F32),
            pltpu.VMEM((S, LANES), BF16),
            pltpu.VMEM((NC, LANES, CHAINS * MLSTM_LC), BF16),
        ],
        compiler_params=_cparams(("parallel", "parallel")),
        name="mlstm",
    )(proj, proj, proj, proj, grow, conv_w, conv_w, bias_r, g_head)


def _norm_rope(x, g, cos, sin_signed):
    xn = x * lax.rsqrt(jnp.mean(x * x, axis=1, keepdims=True) + NORM_EPS) * g
    lane = lax.broadcasted_iota(jnp.int32, x.shape, 1)
    first_half = (lane % (2 * ROPE_FREQS)) < ROPE_FREQS
    partner = jnp.where(first_half,
                        pltpu.roll(xn, LANES - ROPE_FREQS, 1),
                        pltpu.roll(xn, ROPE_FREQS, 1))
    return xn * cos + partner * sin_signed


def _attn_kernel(q_ref, k_ref, v_ref, cq_ref, sq_ref, ck_ref, sk_ref, gq_ref, gk_ref, o_ref, kr_s, vx_s):
    d = ATTN_HEAD_DIM

    @pl.when(pl.program_id(2) == 0)
    def _():
        kr_s[...] = _norm_rope(k_ref[...].astype(F32), gk_ref[...], ck_ref[...], sk_ref[...]).astype(BF16)
        vx_s[...] = jnp.concatenate([v_ref[...], jnp.ones(v_ref.shape, BF16)], axis=1)

    gq = gq_ref[...]
    kr = kr_s[...]
    vx = vx_s[...]
    for r0 in range(0, q_ref.shape[0], ATTN_CHAIN_ROWS):
        rows = slice(r0, r0 + ATTN_CHAIN_ROWS)
        cq, sq = cq_ref[rows, :], sq_ref[rows, :]
        for g in range(ATTN_GROUP):
            cols = slice(d * g, d * (g + 1))
            qg = (_norm_rope(q_ref[rows, cols].astype(F32), gq, cq, sq) * (d ** -0.5 * LOG2E)).astype(BF16)
            s = lax.dot_general(qg, kr, (((1,), (1,)), ((), ())), preferred_element_type=F32)
            p = jnp.exp2(s - jnp.max(s, axis=1, keepdims=True))
            ov = jnp.dot(p.astype(BF16), vx, preferred_element_type=F32)
            o_ref[rows, cols] = (ov[:, :d] / ov[:, d:]).astype(o_ref.dtype)


def _attn(proj, cos_t, sin_t, g_q, g_k, B, S, tq=1024):
    T = B * S
    d = ATTN_HEAD_DIM
    gw = ATTN_GROUP * d
    nq = S // tq
    return pl.pallas_call(
        _attn_kernel,
        grid=(B, ATTN_KV_HEADS, nq),
        in_specs=[
            pl.BlockSpec((tq, gw), lambda b, kv, qi: (b * nq + qi, OFF_AQ // gw + kv)),
            pl.BlockSpec((S, d), lambda b, kv, qi: (b, OFF_AK // d + kv)),
            pl.BlockSpec((S, d), lambda b, kv, qi: (b, OFF_AV // d + kv)),
            pl.BlockSpec((tq, d), lambda b, kv, qi: (qi, 0)),
            pl.BlockSpec((tq, d), lambda b, kv, qi: (qi, 0)),
            _resident((S, d), lambda b, kv, qi: (0, 0)),
            _resident((S, d), lambda b, kv, qi: (0, 0)),
            _resident((1, d), lambda b, kv, qi: (0, 0)),
            _resident((1, d), lambda b, kv, qi: (0, 0)),
        ],
        out_specs=pl.BlockSpec((tq, gw), lambda b, kv, qi: (b * nq + qi, kv)),
        out_shape=jax.ShapeDtypeStruct((T, ATTN_WIDTH), BF16),
        scratch_shapes=[pltpu.VMEM((S, d), BF16), pltpu.VMEM((S, 2 * d), BF16)],
        compiler_params=_cparams(("parallel", "parallel", "arbitrary")),
        name="attn",
    )(proj, proj, proj, cos_t, sin_t, cos_t, sin_t, g_q, g_k)


def _layer_norm(y, g, b):
    mu = jnp.mean(y, axis=1, keepdims=True)
    yc = y - mu
    var = jnp.mean(yc * yc, axis=1, keepdims=True)
    return yc * lax.rsqrt(var + NORM_EPS) * g + b


def _split_bf16(x):
    hi = x.astype(BF16)
    return hi, (x - hi.astype(F32)).astype(BF16)


def _tile_perm():
    n = TOK_TILE * COL_BLOCKS
    a = lax.broadcasted_iota(jnp.int32, (n, n), 0)
    b = lax.broadcasted_iota(jnp.int32, (n, n), 1)
    t = TOK_TILE
    return jnp.where(a // t == b % t, jnp.where(a % t == b // t, 1.0, 0.0), 0.0).astype(BF16)


def _token_major(xh, perm):
    out = []
    for r0 in range(0, xh.shape[0], 2 * TOK_TILE):
        halves = [jnp.concatenate([xh[r0 + h * TOK_TILE:r0 + (h + 1) * TOK_TILE, LANES * j:LANES * (j + 1)]
                                   for j in range(COL_BLOCKS)], axis=0) for h in range(2)]
        o2 = jnp.dot(perm, jnp.concatenate(halves, axis=1), preferred_element_type=F32).astype(BF16)
        out += [o2[:, :LANES], o2[:, LANES:]]
    return jnp.concatenate(out, axis=0)


def _row_major(g_ref, perm):
    n = TOK_TILE * COL_BLOCKS
    cols = [[] for _ in range(COL_BLOCKS)]
    for r0 in range(0, g_ref.shape[0], 2 * n):
        m2 = jnp.concatenate([g_ref[r0:r0 + n, :], g_ref[r0 + n:r0 + 2 * n, :]], axis=1)
        o2 = jnp.dot(perm, m2, preferred_element_type=F32).astype(BF16)
        for j in range(COL_BLOCKS):
            blk = o2[TOK_TILE * j:TOK_TILE * (j + 1), :]
            cols[j] += [blk[:, :LANES], blk[:, LANES:]]
    return jnp.concatenate([jnp.concatenate(cj, axis=0) for cj in cols], axis=1)


def _outproj_kernel(hm_ref, ha_ref, x_ref, wt_ref, wb_ref, g_ref, b_ref, wr_ref, x1_ref, xg_ref, aff_ref):
    y = (ALPHA * x_ref[...]
         + jnp.dot(hm_ref[...], wt_ref[...], preferred_element_type=F32)
         + jnp.dot(ha_ref[...], wb_ref[...], preferred_element_type=F32))
    x1 = _layer_norm(y, g_ref[...], b_ref[...])
    x1_ref[...] = x1
    xh, xl = _split_bf16(x1)
    wh, wl = _split_bf16(wr_ref[...])
    xg_ref[...] = _token_major(xh, _tile_perm())
    nt = (((1,), (1,)), ((), ()))
    logits = (lax.dot_general(wh, xh, nt, preferred_element_type=F32)
              + lax.dot_general(wh, xl, nt, preferred_element_type=F32)
              + lax.dot_general(wl, xh, nt, preferred_element_type=F32))
    e = jnp.exp(logits - jnp.max(logits, axis=0, keepdims=True))
    aff_ref[:, 0, :] = e / jnp.sum(e, axis=0, keepdims=True)


def _outproj(hm, ha, x2, wt, wb, g, b, wr, S, tm=512):
    T, D = x2.shape
    nt = S // tm
    return pl.pallas_call(
        _outproj_kernel,
        grid=(T // tm,),
        in_specs=[
            pl.BlockSpec((tm, MLSTM_WIDTH), lambda i: (i, 0)),
            pl.BlockSpec((tm, ATTN_WIDTH), lambda i: (i, 0)),
            pl.BlockSpec((tm, D), lambda i: (i, 0)),
            _resident((MLSTM_WIDTH, D), lambda i: (0, 0)),
            _resident((ATTN_WIDTH, D), lambda i: (MLSTM_WIDTH // ATTN_WIDTH, 0)),
            _resident((1, D), lambda i: (0, 0)),
            _resident((1, D), lambda i: (0, 0)),
            _resident((N_EXPERTS, D), lambda i: (0, 0)),
        ],
        out_specs=[
            pl.BlockSpec((tm, D), lambda i: (i, 0)),
            pl.BlockSpec((tm * TOK_TILE, LANES), lambda i: (i, 0)),
            pl.BlockSpec((N_EXPERTS, None, 1, tm), lambda i: (0, i // nt, 0, i % nt)),
        ],
        out_shape=[
            jax.ShapeDtypeStruct((T, D), F32),
            jax.ShapeDtypeStruct((T * TOK_TILE, LANES), BF16),
            jax.ShapeDtypeStruct((N_EXPERTS, T // S, 1, S), F32),
        ],
        compiler_params=_cparams(("parallel",)),
        name="outproj",
    )(hm, ha, x2, wt, wb, g, b, wr)


def _excl_prefix_count(m):
    E, S = m.shape
    nb = S // LANES
    s_i = lax.broadcasted_iota(jnp.int32, (LANES, 2 * LANES), 0)
    t_i = lax.broadcasted_iota(jnp.int32, (LANES, 2 * LANES), 1)
    tri_ext = jnp.where(t_i >= LANES, 1.0, jnp.where(s_i < t_i, 1.0, 0.0)).astype(BF16)
    r_i = lax.broadcasted_iota(jnp.int32, (nb * E, nb * E), 0)
    c_i = lax.broadcasted_iota(jnp.int32, (nb * E, nb * E), 1)
    earlier = jnp.where(c_i // E < r_i // E, jnp.where(c_i % E == r_i % E, 1.0, 0.0), 0.0).astype(BF16)
    stacked = jnp.concatenate([m[:, LANES * j:LANES * (j + 1)] for j in range(nb)], axis=0)
    r = jnp.dot(stacked, tri_ext, preferred_element_type=F32)
    off = jnp.dot(earlier, r[:, LANES:].astype(BF16), preferred_element_type=F32)
    tot = r[:, :LANES] + off
    return jnp.concatenate([tot[E * j:E * (j + 1), :] for j in range(nb)], axis=1)


def _topk_kernel(aff_ref, pos_ref, idx_ref, *, cap):
    S = aff_ref.shape[2]
    a = aff_ref[:, 0, :]

    def count_ge(v):
        return jnp.sum(jnp.where(a >= v, 1.0, 0.0), axis=1, keepdims=True)

    def keep_if_enough(cand, thr):
        return jnp.where(count_ge(pltpu.bitcast(cand, F32)) >= cap, cand, thr)

    thr = keep_if_enough(jnp.full((a.shape[0], 1), 1 << 30, jnp.int32), jnp.zeros((a.shape[0], 1), jnp.int32))
    for bit in range(28, -1, -2):
        b1, b0 = 1 << (bit + 1), 1 << bit
        thr = keep_if_enough(thr | b1 | b0, keep_if_enough(thr | b1, keep_if_enough(thr | b0, thr)))
    lo = pltpu.bitcast(thr, F32)
    hi = pltpu.bitcast(thr + 1, F32)
    for _ in range(TOPK_REFINE_STEPS):
        mid = 0.5 * (lo + hi)
        ok = count_ge(mid) >= cap
        lo = jnp.where(ok, mid, lo)
        hi = jnp.where(ok, hi, mid)
    gt = a > lo
    eq = a == lo
    need = cap - jnp.sum(jnp.where(gt, 1.0, 0.0), axis=1, keepdims=True)
    eq_rank = _excl_prefix_count(jnp.where(eq, 1.0, 0.0).astype(BF16))
    sel = jnp.logical_or(gt, jnp.logical_and(eq, eq_rank < need))
    pos = _excl_prefix_count(jnp.where(sel, 1.0, 0.0).astype(BF16))
    posm = jnp.where(sel, pos, -1.0)
    pos_ref[:, 0, :] = posm
    digit_row = lax.broadcasted_iota(jnp.int32, (2 * SUBLANES, S), 0)
    tok = lax.broadcasted_iota(jnp.int32, (2 * SUBLANES, S), 1)
    digits = jnp.where(digit_row == 0, tok // TOK_TILE,
                       jnp.where(digit_row == 1, tok % TOK_TILE, 0)).astype(F32).astype(BF16)
    slot = lax.broadcasted_iota(jnp.int32, (cap, S), 0).astype(F32)
    for e in range(a.shape[0]):
        hit = jnp.where(slot == posm[e:e + 1, :], 1.0, 0.0).astype(BF16)
        idx_ref[e] = lax.dot_general(digits, hit, (((1,), (1,)), ((), ())),
                                     preferred_element_type=F32)[:SUBLANES]


def _topk(aff, B, S, cap):
    E = aff.shape[0]
    assert S <= 256 * TOK_TILE, "token index digits must be integers bf16 holds exactly"
    return pl.pallas_call(
        functools.partial(_topk_kernel, cap=cap),
        grid=(B,),
        in_specs=[pl.BlockSpec((E, None, 1, S), lambda b: (0, b, 0, 0))],
        out_specs=[pl.BlockSpec((E, None, 1, S), lambda b: (0, b, 0, 0)),
                   pl.BlockSpec((E, None, SUBLANES, cap), lambda b: (0, b, 0, 0))],
        out_shape=[jax.ShapeDtypeStruct((E, B, 1, S), F32),
                   jax.ShapeDtypeStruct((E, B, SUBLANES, cap), F32)],
        compiler_params=_cparams(("parallel",)),
        name="topk",
    )(aff)


def _moe_ffn_kernel(idx_ref, xg_ref, pos_ref, aff_ref, wg_ref, wu_ref, wd_ref, y_ref, gbuf_s, wg_s, wu_s, wd_s,
                    *, cap, n_exp):
    e = pl.program_id(0)
    b = pl.program_id(1)

    @pl.when(e < n_exp)
    def _():
        slot = e % 2
        rg = pl.multiple_of(b * wg_ref.shape[0], wg_ref.shape[0])
        rd = pl.multiple_of(b * wd_ref.shape[0], wd_ref.shape[0])
        wg_s[slot, pl.ds(rg, wg_ref.shape[0]), :] = wg_ref[...].astype(BF16)
        wu_s[slot, pl.ds(rg, wu_ref.shape[0]), :] = wu_ref[...].astype(BF16)
        wd_s[slot, pl.ds(rd, wd_ref.shape[0]), :] = wd_ref[...].astype(BF16)

    @pl.when(e > 0)
    def _():
        S = pos_ref.shape[1]
        slot = (e - 1) % 2
        for c in range(cap):
            row = pl.multiple_of(idx_ref[0, c] * TOK_TILE, TOK_TILE)
            gbuf_s[TOK_TILE * c:TOK_TILE * (c + 1), :] = xg_ref[pl.ds(row, TOK_TILE), :]
        perm = _tile_perm()
        xs = _row_major(gbuf_s, perm)
        sl = lax.broadcasted_iota(jnp.int32, (cap, S), 0).astype(F32)
        gate = jnp.sum(jnp.where(sl == pos_ref[...], aff_ref[...], 0.0), axis=1, keepdims=True)
        hg = jnp.dot(xs, wg_s[slot], preferred_element_type=F32)
        hu = jnp.dot(xs, wu_s[slot], preferred_element_type=F32)
        hid = (hg * _sigmoid(hg) * hu).astype(BF16)
        y = (jnp.dot(hid, wd_s[slot], preferred_element_type=F32) * gate).astype(BF16)
        y_ref[...] = y


def _moe_ffn(idx, xg, pos4, aff4, wg, wu, wd, B, S, cap):
    E, D, F = wg.shape
    prev = lambda e: jnp.maximum(e - 1, 0)
    stage = lambda e, b: (jnp.minimum(e, E - 1), jnp.where(e < E, b, B - 1), 0)
    return pl.pallas_call(
        functools.partial(_moe_ffn_kernel, cap=cap, n_exp=E),
        grid=(E + 1, B),
        in_specs=[
            pl.BlockSpec((None, None, 1, cap), lambda e, b: (prev(e), b, 0, 0), memory_space=pltpu.SMEM),
            pl.BlockSpec((S * TOK_TILE, LANES), lambda e, b: (b, 0)),
            pl.BlockSpec((None, None, 1, S), lambda e, b: (prev(e), b, 0, 0)),
            pl.BlockSpec((None, None, 1, S), lambda e, b: (prev(e), b, 0, 0)),
            pl.BlockSpec((None, D // B, F), stage),
            pl.BlockSpec((None, D // B, F), stage),
            pl.BlockSpec((None, F // B, D), stage),
        ],
        out_specs=pl.BlockSpec((None, None, cap, D), lambda e, b: (jnp.where(e > 0, b, 0), prev(e), 0, 0)),
        out_shape=jax.ShapeDtypeStruct((B, E, cap, D), BF16),
        scratch_shapes=[pltpu.VMEM((cap * TOK_TILE, LANES), BF16),
                        pltpu.VMEM((2, D, F), BF16), pltpu.VMEM((2, D, F), BF16), pltpu.VMEM((2, F, D), BF16)],
        compiler_params=_cparams(("arbitrary", "arbitrary")),
        name="moe_ffn",
    )(idx, xg, pos4, aff4, wg, wu, wd)


def _row_major_f32(g_ref, r0, n_tok, perm):
    n = TOK_TILE * COL_BLOCKS
    cols = [[] for _ in range(COL_BLOCKS)]
    perm3 = jnp.concatenate([perm, perm, perm], axis=1)
    for k in range(0, n_tok * TOK_TILE, 2 * n):
        m2 = jnp.concatenate([g_ref[pl.ds(r0 + k, n), :], g_ref[pl.ds(r0 + k + n, n), :]], axis=1)
        hi = m2.astype(BF16)
        r1 = m2 - hi.astype(F32)
        mid = r1.astype(BF16)
        lo = (r1 - mid.astype(F32)).astype(BF16)
        o2 = jnp.dot(perm3, jnp.concatenate([hi, mid, lo], axis=0),
                     preferred_element_type=F32)
        for j in range(COL_BLOCKS):
            blk = o2[TOK_TILE * j:TOK_TILE * (j + 1), :]
            cols[j] += [blk[:, :LANES], blk[:, LANES:]]
    return jnp.concatenate([jnp.concatenate(cj, axis=0) for cj in cols], axis=1)


def _moe_comb_kernel(idx_ref, y_ref, o_ref, acc_s, *, cap, n_grp):
    j = pl.program_id(1)

    @pl.when(j == 0)
    def _():
        acc_s[...] = jnp.zeros(acc_s.shape, F32)

    @pl.when(j < n_grp)
    def _():
        perm = _tile_perm()
        for g in range(idx_ref.shape[0]):
            y_tm = _token_major(y_ref[g], perm)
            for c0 in range(0, cap, SCATTER_BATCH):
                rows = [pl.multiple_of(idx_ref[g, 0, c0 + u] * TOK_TILE, TOK_TILE) for u in range(SCATTER_BATCH)]
                new = [acc_s[pl.ds(rows[u], TOK_TILE), :]
                       + y_tm[TOK_TILE * (c0 + u):TOK_TILE * (c0 + u + 1), :].astype(F32)
                       for u in range(SCATTER_BATCH)]
                for u in range(SCATTER_BATCH):
                    acc_s[pl.ds(rows[u], TOK_TILE), :] = new[u]

    @pl.when(j >= n_grp)
    def _():
        n_tok = o_ref.shape[0]
        r0 = pl.multiple_of((j - n_grp) * (n_tok * TOK_TILE), n_tok * TOK_TILE)
        o_ref[...] = _row_major_f32(acc_s, r0, n_tok, _tile_perm())


def _moe_comb(idx, y, B, S, cap, chunk=1024, eg=8):
    E = y.shape[1]
    D = D_MODEL
    nch = S // chunk
    ng = E // eg
    return pl.pallas_call(
        functools.partial(_moe_comb_kernel, cap=cap, n_grp=ng),
        grid=(B, ng + nch),
        in_specs=[
            pl.BlockSpec((eg, None, 1, cap), lambda b, j: (jnp.minimum(j, ng - 1), b, 0, 0),
                         memory_space=pltpu.SMEM),
            pl.BlockSpec((None, eg, cap, D), lambda b, j: (b, jnp.minimum(j, ng - 1), 0, 0)),
        ],
        out_specs=pl.BlockSpec((chunk, D), lambda b, j: (b * nch + jnp.maximum(j - ng, 0), 0)),
        out_shape=jax.ShapeDtypeStruct((B * S, D), F32),
        scratch_shapes=[pltpu.VMEM((S * TOK_TILE, LANES), F32)],
        compiler_params=_cparams(("arbitrary", "arbitrary")),
        name="moe_comb",
    )(idx, y)


def _final_kernel(x1_ref, moe_ref, p_ref, wpg_ref, bpg_ref, wpp_ref, g_ref, b_ref, o_ref):
    for r0 in range(0, x1_ref.shape[0], ROW_CHAIN):
        rows = slice(r0, r0 + ROW_CHAIN)
        x1 = x1_ref[rows, :]
        gate = _sigmoid(jnp.dot(x1.astype(BF16), wpg_ref[...], preferred_element_type=F32) + bpg_ref[...])
        plv = gate * jnp.dot(p_ref[rows, :].astype(BF16), wpp_ref[...], preferred_element_type=F32)
        o_ref[rows, :] = _layer_norm(ALPHA * x1 + moe_ref[rows, :] + plv, g_ref[...], b_ref[...])


def _final(x1, moe, p2, wpg, bpg, wpp, g, b, tm=512):
    T, D = x1.shape
    return pl.pallas_call(
        _final_kernel,
        grid=(T // tm,),
        in_specs=[
            pl.BlockSpec((tm, D), lambda i: (i, 0)),
            pl.BlockSpec((tm, D), lambda i: (i, 0)),
            pl.BlockSpec((tm, P_DIM), lambda i: (i, 0)),
            _resident((D, D), lambda i: (0, 0)),
            _resident((1, D), lambda i: (0, 0)),
            _resident((P_DIM, D), lambda i: (0, 0)),
            _resident((1, D), lambda i: (0, 0)),
            _resident((1, D), lambda i: (0, 0)),
        ],
        out_specs=pl.BlockSpec((tm, D), lambda i: (i, 0)),
        out_shape=jax.ShapeDtypeStruct((T, D), F32),
        compiler_params=_cparams(("parallel",)),
        name="final",
    )(x1, moe, p2, wpg, bpg, wpp, g, b)


def _rope_tables(S):
    rows = S // GRID_W
    row_idx = np.repeat(np.arange(rows, dtype=np.float64), GRID_W)
    col_idx = np.tile(np.arange(GRID_W, dtype=np.float64), rows)
    inv_freq = ROPE_THETA ** (-np.arange(ROPE_FREQS, dtype=np.float64) / ROPE_FREQS)
    ar = row_idx[:, None] * inv_freq
    ac = col_idx[:, None] * inv_freq
    cos_t = np.concatenate([np.cos(ar), np.cos(ar), np.cos(ac), np.cos(ac)], axis=1)
    sin_t = np.concatenate([-np.sin(ar), np.sin(ar), -np.sin(ac), np.sin(ac)], axis=1)
    return jnp.asarray(cos_t, F32), jnp.asarray(sin_t, F32)


def _pair_major(a):
    return a.reshape(2, HEAD_PAIRS, 2).transpose(1, 0, 2)


def _layer(x2, p2, B, S, w_in, conv_w, b_i, b_f, g_mlstm, g_q, g_k, w_out, ln1_g, ln1_b, w_router, w_gate, w_up,
           w_down, w_pl_proj, w_pl_gate, b_pl_gate, ln2_g, ln2_b):
    D = D_MODEL
    cap = CAPACITY_FACTOR * S // N_EXPERTS
    o_mg = 2 * MQ_COLS + 2 * MV_COLS
    w_main = jnp.concatenate([w_in[:, :o_mg], w_in[:, o_mg + MG_COLS:]], axis=1).astype(BF16)
    w_g = w_in[:, o_mg:o_mg + MG_COLS].reshape(D, 2, 2, HEAD_PAIRS, 2).transpose(0, 3, 2, 1, 4).reshape(D, MG_COLS)
    bias = jnp.stack([_pair_major(b_i), _pair_major(b_f)], axis=1).reshape(HEAD_PAIRS, GATES_PER_PAIR)

    proj, grow = _proj(x2, w_main, w_g.astype(BF16).T)
    h_m = _mlstm(proj, grow, conv_w, bias[:, :, None], g_mlstm.reshape(HEAD_PAIRS, 2, MLSTM_DV), B, S)
    cos_t, sin_t = _rope_tables(S)
    h_a = _attn(proj, cos_t, sin_t, g_q[None, :], g_k[None, :], B, S)
    w_o = w_out.astype(BF16)
    x1, xg, aff4 = _outproj(h_m, h_a, x2, w_o, w_o, ln1_g[None, :], ln1_b[None, :], w_router.T, S)
    pos4, idx_digits = _topk(aff4, B, S, cap)
    idx = (idx_digits[:, :, 0:1, :] * TOK_TILE + idx_digits[:, :, 1:2, :]).astype(jnp.int32)
    y = _moe_ffn(idx, xg, pos4, aff4, w_gate, w_up, w_down, B, S, cap)
    moe = _moe_comb(idx, y, B, S, cap)
    return _final(x1, moe, p2, w_pl_gate.astype(BF16), b_pl_gate[None, :], w_pl_proj.astype(BF16),
                  ln2_g[None, :], ln2_b[None, :])


def kernel(x, p, w_in, conv_w, b_igate, b_fgate, g_mlstm, g_q, g_k, w_out, ln1_g, ln1_b, w_router, w_gate, w_up,
           w_down, w_pl_proj, w_pl_gate, b_pl_gate, ln2_g, ln2_b):
    B, S, D = x.shape
    x2 = x.reshape(B * S, D)
    for i in range(DEPTH):
        x2 = _layer(x2, p[i].reshape(B * S, P_DIM), B, S, w_in[i], conv_w[i], b_igate[i], b_fgate[i], g_mlstm[i],
                    g_q[i], g_k[i], w_out[i], ln1_g[i], ln1_b[i], w_router[i], w_gate[i], w_up[i], w_down[i],
                    w_pl_proj[i], w_pl_gate[i], b_pl_gate[i], ln2_g[i], ln2_b[i])
    return x2.reshape(B, S, D)
```

```python
import functools

import jax
import jax.numpy as jnp
import numpy as np
from jax import lax
from jax.experimental import pallas as pl
from jax.experimental.pallas import tpu as pltpu

F32 = jnp.float32
BF16 = jnp.bfloat16

D_MODEL = 2048
P_DIM = 256
GRID_W = 64
MLSTM_WIDTH = D_MODEL // 2
ATTN_WIDTH = D_MODEL - MLSTM_WIDTH
MLSTM_HEADS = 8
MLSTM_DV = MLSTM_WIDTH // MLSTM_HEADS
MLSTM_DQK = MLSTM_DV // 2
CONV_W = 5
ATTN_HEAD_DIM = 128
ATTN_Q_HEADS = ATTN_WIDTH // ATTN_HEAD_DIM
ATTN_KV_HEADS = 2
ATTN_GROUP = ATTN_Q_HEADS // ATTN_KV_HEADS
ROPE_FREQS = ATTN_HEAD_DIM // 4
ROPE_THETA = 10000.0
N_EXPERTS = 16
EXPERT_FF = D_MODEL // 2
CAPACITY_FACTOR = 2
NORM_EPS = 1e-6
DEPTH = 1
ALPHA = (2.0 * DEPTH) ** 0.25
LOG2E = 1.4426950408889634

MQ_COLS = MLSTM_HEADS * MLSTM_DQK
MV_COLS = MLSTM_WIDTH
MG_COLS = 2 * 2 * MLSTM_HEADS
AQ_COLS = ATTN_WIDTH
AKV_COLS = ATTN_KV_HEADS * ATTN_HEAD_DIM
PROJ_COLS = 2 * MQ_COLS + 2 * MV_COLS + AQ_COLS + 2 * AKV_COLS
OFF_MQ, OFF_MK, OFF_MV, OFF_MO = 0, MQ_COLS, 2 * MQ_COLS, 2 * MQ_COLS + MV_COLS
OFF_AQ = OFF_MO + MV_COLS
OFF_AK = OFF_AQ + AQ_COLS
OFF_AV = OFF_AK + AKV_COLS
PROJ_COL_STEP = 1536

HEAD_PAIRS = MLSTM_HEADS // 2
GATES_PER_PAIR = 8
CHAINS = 4
MLSTM_LC = 128
STATE_UNROLL = 8
OUT_UNROLL = 8
ATTN_CHAIN_ROWS = 256
ROW_CHAIN = 256
LANES = 128
SUBLANES = 8
TOK_TILE = 16
COL_BLOCKS = D_MODEL // LANES
SCATTER_BATCH = 16
V7X_VMEM_LIMIT = 56 * 1024 * 1024
TOPK_REFINE_STEPS = 8


def _cparams(sem, vmem=V7X_VMEM_LIMIT):
    return pltpu.CompilerParams(dimension_semantics=sem, vmem_limit_bytes=vmem)


def _resident(shape, index_map):
    return pl.BlockSpec(shape, index_map, pipeline_mode=pl.Buffered(1))


def _streamed(shape, index_map):
    return pl.BlockSpec(shape, index_map, pipeline_mode=pl.Buffered(3))


def _sigmoid(x):
    return 0.5 * jnp.tanh(0.5 * x) + 0.5


def _log_sigmoid(x):
    return jnp.minimum(x, 0.0) - jnp.log1p(jnp.exp(-jnp.abs(x)))


def _proj_kernel(x_ref, w_ref, wgt_ref, o_ref, gr_ref):
    xb = x_ref[...].astype(BF16)
    gr = lax.dot_general(wgt_ref[...], xb, (((1,), (1,)), ((), ())), preferred_element_type=F32)
    for p in range(HEAD_PAIRS):
        gr_ref[p] = gr[GATES_PER_PAIR * p:GATES_PER_PAIR * (p + 1), :]
    tn = PROJ_COL_STEP
    for n0 in range(0, w_ref.shape[1], tn):
        o_ref[:, n0:n0 + tn] = jnp.dot(xb, w_ref[:, n0:n0 + tn], preferred_element_type=F32).astype(o_ref.dtype)


def _proj(x2, w, wgt, tm=512):
    T, D = x2.shape
    N = w.shape[1]
    return pl.pallas_call(
        _proj_kernel,
        grid=(T // tm,),
        in_specs=[
            pl.BlockSpec((tm, D), lambda i: (i, 0)),
            _resident((D, N), lambda i: (0, 0)),
            _resident((MG_COLS, D), lambda i: (0, 0)),
        ],
        out_specs=[
            pl.BlockSpec((tm, N), lambda i: (i, 0)),
            pl.BlockSpec((HEAD_PAIRS, GATES_PER_PAIR, tm), lambda i: (0, 0, i)),
        ],
        out_shape=[
            jax.ShapeDtypeStruct((T, N), BF16),
            jax.ShapeDtypeStruct((HEAD_PAIRS, GATES_PER_PAIR, T), F32),
        ],
        compiler_params=_cparams(("parallel",)),
        name="proj",
    )(x2, w, wgt)


def _conv_silu(x, w, pad_s):
    S, C = x.shape
    half = CONV_W // 2
    halo = jnp.zeros((SUBLANES, C), F32)
    pad_s[0:SUBLANES, :] = halo
    pad_s[SUBLANES + S:2 * SUBLANES + S, :] = halo
    pad_s[SUBLANES:SUBLANES + S, :] = x
    acc = x * w[half:half + 1, :]
    for j in range(CONV_W):
        if j != half:
            acc = acc + pad_s[SUBLANES + j - half:SUBLANES + j - half + S, :] * w[j:j + 1, :]
    return acc * _sigmoid(acc)


def _mlstm_kernel(q_ref, k_ref, v_ref, o_ref, gr_ref, cwq_ref, cwk_ref, br_ref, gh_ref, out_ref,
                  q0_s, q1_s, kt_s, col_s, winter_s, wl_s, dec_s, sc_s, cst_s, call_s, pad_s, acol_s, rmat_s):
    S = q_ref.shape[0]
    L = MLSTM_LC
    NC = S // L
    DQ = MLSTM_DQK
    DV = MLSTM_DV
    hi = lax.Precision.HIGHEST
    neg = -jnp.inf

    g8 = gr_ref[...] + br_ref[...]
    li8 = g8 * LOG2E
    lf8 = pltpu.roll(_log_sigmoid(g8) * LOG2E, CHAINS, 0)
    row = lax.broadcasted_iota(jnp.int32, (GATES_PER_PAIR, L), 0)
    lane = lax.broadcasted_iota(jnp.int32, (GATES_PER_PAIR, L), 1)
    fwd = (row % CHAINS) < 2
    fwd1 = fwd[:, :1]
    si = lax.broadcasted_iota(jnp.int32, (L, 2 * L), 0)
    ti = lax.broadcasted_iota(jnp.int32, (L, 2 * L), 1)
    tri = jnp.where(ti < L, jnp.where(si <= ti, 1.0, 0.0), jnp.where(si >= ti - L, 1.0, 0.0))
    tot, mloc, b_l, cm_l, r_l = [], [], [], [], []
    for c in range(NC):
        pr = jnp.dot(lf8[:, c * L:(c + 1) * L], tri, precision=hi, preferred_element_type=F32)
        b_c = jnp.where(fwd, pr[:, :L], pr[:, L:])
        tot_c = pr[:, L - 1:L]
        li_c = li8[:, c * L:(c + 1) * L]
        r_c = li_c - b_c
        cm = r_c
        k = 1
        while k < L:
            pre = jnp.where(lane >= k, pltpu.roll(cm, k, 1), neg)
            suf = jnp.where(lane < L - k, pltpu.roll(cm, L - k, 1), neg)
            cm = jnp.maximum(cm, jnp.where(fwd, pre, suf))
            k *= 2
        g_c = tot_c - b_c + li_c
        mloc_c = jnp.max(g_c, axis=1, keepdims=True)
        wl_s[c] = jnp.exp2(g_c - mloc_c)
        r_l.append(r_c)
        tot.append(tot_c)
        mloc.append(mloc_c)
        b_l.append(b_c)
        cm_l.append(cm)

    def scan(order):
        m = jnp.zeros((GATES_PER_PAIR, 1), F32)
        m_in, dec, sc = [None] * NC, [None] * NC, [None] * NC
        for c in order:
            m_new = jnp.maximum(tot[c] + m, mloc[c])
            m_in[c] = m
            dec[c] = jnp.exp2(tot[c] + m - m_new)
            sc[c] = jnp.exp2(mloc[c] - m_new)
            m = m_new
        return m_in, dec, sc

    mf, df, sf = scan(range(NC))
    mb, db, sb = scan(range(NC - 1, -1, -1))
    def terms3(x):
        t0 = x.astype(BF16).astype(F32)
        rem = x - t0
        t1 = rem.astype(BF16).astype(F32)
        return t0, t1, (rem - t1).astype(BF16).astype(F32)

    blk_rows = 2 * SUBLANES
    rowi = lax.broadcasted_iota(jnp.int32, (blk_rows, L), 0)
    bro = lambda x, ci: jnp.broadcast_to(x[ci:ci + 1, :], (blk_rows, L))
    zblk = jnp.zeros((blk_rows, L), F32)
    pad = jnp.zeros((LANES - GATES_PER_PAIR, L), F32)
    for c in range(NC):
        m_in = jnp.where(fwd1, mf[c], mb[c])
        dec_s[c] = jnp.broadcast_to(jnp.where(fwd1, df[c], db[c]), (GATES_PER_PAIR, L))
        sc_s[c] = jnp.broadcast_to(jnp.where(fwd1, sf[c], sb[c]), (GATES_PER_PAIR, L))
        a_c = jnp.maximum(m_in, cm_l[c])
        winter_s[c] = jnp.exp2(m_in - a_c)
        emt_c = jnp.exp2(-(b_l[c] + a_c))
        col_s[c * L:(c + 1) * L, :] = jnp.concatenate([emt_c, pad], axis=0).T
        a0, a1, a2 = terms3(a_c)
        r0, r1, r2 = terms3(r_l[c])
        a_blks, r_cols = [], []
        for ci in range(CHAINS):
            a_blks.append(jnp.where(rowi == 0, bro(a0, ci), jnp.where(rowi == 1, bro(a1, ci), jnp.where(
                rowi == 2, bro(a2, ci), jnp.where(rowi < 6, 1.0, 0.0)))))
            r_blk = jnp.where(rowi < 3, -1.0, jnp.where(rowi == 3, bro(r0, ci), jnp.where(
                rowi == 4, bro(r1, ci), jnp.where(rowi == 5, bro(r2, ci), 0.0))))
            r_cols.append(jnp.concatenate([zblk] * ci + [r_blk] + [zblk] * (LANES // blk_rows - 1 - ci), axis=0))
        a_rows = jnp.concatenate(a_blks + [zblk] * (LANES // blk_rows - CHAINS), axis=0)
        acol_s[c * L:(c + 1) * L, :] = a_rows.T.astype(BF16)
        rmat_s[c] = jnp.concatenate(r_cols, axis=1).astype(BF16)

    qs = (_conv_silu(q_ref[...].astype(F32), cwq_ref[...], pad_s) * (DQ ** -0.5)).astype(BF16)
    q0_s[...] = qs[:, :DQ]
    q1_s[...] = qs[:, DQ:]
    kt = _conv_silu(k_ref[...].astype(F32), cwk_ref[...], pad_s).T
    for c in range(NC):
        kt_s[c] = kt[:, c * L:(c + 1) * L]

    ones_blk = jnp.ones((L, DV), BF16)

    def vext_of(sl, hh):
        return jnp.concatenate([v_ref[sl, DV * hh:DV * (hh + 1)], ones_blk], axis=1)

    cst_s[...] = jnp.zeros(cst_s.shape, F32)

    def state_step(c):
        for d in range(2):
            ch = c if d == 0 else NC - 1 - c
            sl = pl.ds(pl.multiple_of(ch * L, L), L)
            wl = wl_s[ch]
            ktc = kt_s[ch]
            decs = dec_s[ch]
            scs = sc_s[ch]
            for hh in range(2):
                ci = d * 2 + hh
                cx = cst_s[ci]
                call_s[ch, ci] = cx.astype(BF16)
                kw = (ktc[DQ * hh:DQ * (hh + 1), :] * wl[ci:ci + 1, :]).astype(BF16)
                cst_s[ci] = (decs[ci:ci + 1, 0:1] * cx
                             + scs[ci:ci + 1, 0:1] * jnp.dot(kw, vext_of(sl, hh), preferred_element_type=F32))

    def state_body(c, carry):
        for u in range(STATE_UNROLL):
            state_step(STATE_UNROLL * c + u)
        return carry

    lax.fori_loop(0, NC // STATE_UNROLL, state_body, 0)

    tt = lax.broadcasted_iota(jnp.int32, (L, L), 0)
    ss = lax.broadcasted_iota(jnp.int32, (L, L), 1)
    masks = (ss <= tt, ss >= tt)

    eye = tt == ss

    def chunk_out(ch, hh, sl, cols, winter, e_all):
        og = _sigmoid(o_ref[sl, DV * hh:DV * (hh + 1)].astype(F32))
        qc = (q0_s if hh == 0 else q1_s)[sl, :]
        kth = kt_s[ch, DQ * hh:DQ * (hh + 1), :].astype(BF16)
        qk = jnp.dot(qc, kth, preferred_element_type=F32)
        vext = vext_of(sl, hh)
        hsum = None
        for d in range(2):
            ci = d * 2 + hh
            w_intra = jnp.exp2(jnp.where(masks[d], e_all[:, L * ci:L * (ci + 1)], neg))
            qcx = jnp.dot(qc, call_s[ch, ci], preferred_element_type=F32).astype(BF16)
            w_diag = jnp.where(eye, winter[ci:ci + 1, :], 0.0).astype(BF16)
            nd = jnp.dot(jnp.concatenate([(qk * w_intra).astype(BF16), w_diag], axis=1),
                         jnp.concatenate([vext, qcx], axis=0),
                         preferred_element_type=F32)
            h = nd[:, :DV] / jnp.maximum(jnp.abs(nd[:, DV:]), cols[:, ci:ci + 1])
            hsum = h if hsum is None else hsum + h
        y = hsum * lax.rsqrt(jnp.mean(hsum * hsum, axis=1, keepdims=True) + NORM_EPS) * gh_ref[hh:hh + 1, :]
        out_ref[sl, DV * hh:DV * (hh + 1)] = (og * y).astype(out_ref.dtype)

    def out_chunk(c):
        sl = pl.ds(pl.multiple_of(c * L, L), L)
        e_all = jnp.dot(acol_s[sl, :], rmat_s[c], preferred_element_type=F32)
        cols = col_s[sl, :]
        winter = winter_s[c]
        chunk_out(c, 0, sl, cols, winter, e_all)
        chunk_out(c, 1, sl, cols, winter, e_all)

    def out_body(c, carry):
        for u in range(OUT_UNROLL):
            out_chunk(OUT_UNROLL * c + u)
        return carry

    lax.fori_loop(0, NC // OUT_UNROLL, out_body, 0)


def _mlstm(proj, grow, conv_w, bias_r, g_head, B, S):
    T = B * S
    NC = S // MLSTM_LC
    pw = 2 * MLSTM_DQK
    vw = 2 * MLSTM_DV
    return pl.pallas_call(
        _mlstm_kernel,
        grid=(B, HEAD_PAIRS),
        in_specs=[
            pl.BlockSpec((S, pw), lambda b, p: (b, OFF_MQ // pw + p)),
            pl.BlockSpec((S, pw), lambda b, p: (b, OFF_MK // pw + p)),
            pl.BlockSpec((S, vw), lambda b, p: (b, OFF_MV // vw + p)),
            pl.BlockSpec((S, vw), lambda b, p: (b, OFF_MO // vw + p)),
            pl.BlockSpec((None, GATES_PER_PAIR, S), lambda b, p: (p, 0, b)),
            pl.BlockSpec((CONV_W, pw), lambda b, p: (0, p)),
            pl.BlockSpec((CONV_W, pw), lambda b, p: (0, MQ_COLS // pw + p)),
            pl.BlockSpec((None, GATES_PER_PAIR, 1), lambda b, p: (p, 0, 0)),
            pl.BlockSpec((None, 2, MLSTM_DV), lambda b, p: (p, 0, 0)),
        ],
        out_specs=pl.BlockSpec((S, vw), lambda b, p: (b, p)),
        out_shape=jax.ShapeDtypeStruct((T, MLSTM_WIDTH), BF16),
        scratch_shapes=[
            pltpu.VMEM((S, MLSTM_DQK), BF16),
            pltpu.VMEM((S, MLSTM_DQK), BF16),
            pltpu.VMEM((NC, pw, MLSTM_LC), F32),
            pltpu.VMEM((S, LANES), F32),
            pltpu.VMEM((NC, GATES_PER_PAIR, MLSTM_LC), F32),
            pltpu.VMEM((NC, GATES_PER_PAIR, MLSTM_LC), F32),
            pltpu.VMEM((NC, GATES_PER_PAIR, MLSTM_LC), F32),
            pltpu.VMEM((NC, GATES_PER_PAIR, MLSTM_LC), F32),
            pltpu.VMEM((CHAINS, MLSTM_DQK, 2 * MLSTM_DV), F32),
            pltpu.VMEM((NC, CHAINS, MLSTM_DQK, 2 * MLSTM_DV), BF16),
            pltpu.VMEM((S + 2 * SUBLANES, pw), F32),
            pltpu.VMEM((S, LANES), BF16),
            pltpu.VMEM((NC, LANES, CHAINS * MLSTM_LC), BF16),
        ],
        compiler_params=_cparams(("parallel", "parallel")),
        name="mlstm",
    )(proj, proj, proj, proj, grow, conv_w, conv_w, bias_r, g_head)


def _norm_rope(x, g, cos, sin_signed):
    xn = x * lax.rsqrt(jnp.mean(x * x, axis=1, keepdims=True) + NORM_EPS) * g
    lane = lax.broadcasted_iota(jnp.int32, x.shape, 1)
    first_half = (lane % (2 * ROPE_FREQS)) < ROPE_FREQS
    partner = jnp.where(first_half,
                        pltpu.roll(xn, LANES - ROPE_FREQS, 1),
                        pltpu.roll(xn, ROPE_FREQS, 1))
    return xn * cos + partner * sin_signed


def _attn_kernel(q_ref, k_ref, v_ref, cq_ref, sq_ref, ck_ref, sk_ref, gq_ref, gk_ref, o_ref, kr_s, vx_s):
    d = ATTN_HEAD_DIM

    @pl.when(pl.program_id(2) == 0)
    def _():
        kr_s[...] = _norm_rope(k_ref[...].astype(F32), gk_ref[...], ck_ref[...], sk_ref[...]).astype(BF16)
        vx_s[...] = jnp.concatenate([v_ref[...], jnp.ones(v_ref.shape, BF16)], axis=1)

    gq = gq_ref[...]
    kr = kr_s[...]
    vx = vx_s[...]
    for r0 in range(0, q_ref.shape[0], ATTN_CHAIN_ROWS):
        rows = slice(r0, r0 + ATTN_CHAIN_ROWS)
        cq, sq = cq_ref[rows, :], sq_ref[rows, :]
        for g in range(ATTN_GROUP):
            cols = slice(d * g, d * (g + 1))
            qg = (_norm_rope(q_ref[rows, cols].astype(F32), gq, cq, sq) * (d ** -0.5 * LOG2E)).astype(BF16)
            s = lax.dot_general(qg, kr, (((1,), (1,)), ((), ())), preferred_element_type=F32)
            p = jnp.exp2(s - jnp.max(s, axis=1, keepdims=True))
            ov = jnp.dot(p.astype(BF16), vx, preferred_element_type=F32)
            o_ref[rows, cols] = (ov[:, :d] / ov[:, d:]).astype(o_ref.dtype)


def _attn(proj, cos_t, sin_t, g_q, g_k, B, S, tq=1024):
    T = B * S
    d = ATTN_HEAD_DIM
    gw = ATTN_GROUP * d
    nq = S // tq
    return pl.pallas_call(
        _attn_kernel,
        grid=(B, ATTN_KV_HEADS, nq),
        in_specs=[
            pl.BlockSpec((tq, gw), lambda b, kv, qi: (b * nq + qi, OFF_AQ // gw + kv)),
            pl.BlockSpec((S, d), lambda b, kv, qi: (b, OFF_AK // d + kv)),
            pl.BlockSpec((S, d), lambda b, kv, qi: (b, OFF_AV // d + kv)),
            pl.BlockSpec((tq, d), lambda b, kv, qi: (qi, 0)),
            pl.BlockSpec((tq, d), lambda b, kv, qi: (qi, 0)),
            _resident((S, d), lambda b, kv, qi: (0, 0)),
            _resident((S, d), lambda b, kv, qi: (0, 0)),
            _resident((1, d), lambda b, kv, qi: (0, 0)),
            _resident((1, d), lambda b, kv, qi: (0, 0)),
        ],
        out_specs=pl.BlockSpec((tq, gw), lambda b, kv, qi: (b * nq + qi, kv)),
        out_shape=jax.ShapeDtypeStruct((T, ATTN_WIDTH), BF16),
        scratch_shapes=[pltpu.VMEM((S, d), BF16), pltpu.VMEM((S, 2 * d), BF16)],
        compiler_params=_cparams(("parallel", "parallel", "arbitrary")),
        name="attn",
    )(proj, proj, proj, cos_t, sin_t, cos_t, sin_t, g_q, g_k)


def _layer_norm(y, g, b):
    mu = jnp.mean(y, axis=1, keepdims=True)
    yc = y - mu
    var = jnp.mean(yc * yc, axis=1, keepdims=True)
    return yc * lax.rsqrt(var + NORM_EPS) * g + b


def _split_bf16(x):
    hi = x.astype(BF16)
    return hi, (x - hi.astype(F32)).astype(BF16)


def _tile_perm():
    n = TOK_TILE * COL_BLOCKS
    a = lax.broadcasted_iota(jnp.int32, (n, n), 0)
    b = lax.broadcasted_iota(jnp.int32, (n, n), 1)
    t = TOK_TILE
    return jnp.where(a // t == b % t, jnp.where(a % t == b // t, 1.0, 0.0), 0.0).astype(BF16)


def _token_major(xh, perm):
    out = []
    for r0 in range(0, xh.shape[0], 2 * TOK_TILE):
        halves = [jnp.concatenate([xh[r0 + h * TOK_TILE:r0 + (h + 1) * TOK_TILE, LANES * j:LANES * (j + 1)]
                                   for j in range(COL_BLOCKS)], axis=0) for h in range(2)]
        o2 = jnp.dot(perm, jnp.concatenate(halves, axis=1), preferred_element_type=F32).astype(BF16)
        out += [o2[:, :LANES], o2[:, LANES:]]
    return jnp.concatenate(out, axis=0)


def _row_major(g_ref, perm):
    n = TOK_TILE * COL_BLOCKS
    cols = [[] for _ in range(COL_BLOCKS)]
    for r0 in range(0, g_ref.shape[0], 2 * n):
        m2 = jnp.concatenate([g_ref[r0:r0 + n, :], g_ref[r0 + n:r0 + 2 * n, :]], axis=1)
        o2 = jnp.dot(perm, m2, preferred_element_type=F32).astype(BF16)
        for j in range(COL_BLOCKS):
            blk = o2[TOK_TILE * j:TOK_TILE * (j + 1), :]
            cols[j] += [blk[:, :LANES], blk[:, LANES:]]
    return jnp.concatenate([jnp.concatenate(cj, axis=0) for cj in cols], axis=1)


def _outproj_kernel(hm_ref, ha_ref, x_ref, wt_ref, wb_ref, g_ref, b_ref, wr_ref, x1_ref, xg_ref, aff_ref):
    y = (ALPHA * x_ref[...]
         + jnp.dot(hm_ref[...], wt_ref[...], preferred_element_type=F32)
         + jnp.dot(ha_ref[...], wb_ref[...], preferred_element_type=F32))
    x1 = _layer_norm(y, g_ref[...], b_ref[...])
    x1_ref[...] = x1
    xh, xl = _split_bf16(x1)
    wh, wl = _split_bf16(wr_ref[...])
    xg_ref[...] = _token_major(xh, _tile_perm())
    nt = (((1,), (1,)), ((), ()))
    logits = (lax.dot_general(wh, xh, nt, preferred_element_type=F32)
              + lax.dot_general(wh, xl, nt, preferred_element_type=F32)
              + lax.dot_general(wl, xh, nt, preferred_element_type=F32))
    e = jnp.exp(logits - jnp.max(logits, axis=0, keepdims=True))
    aff_ref[:, 0, :] = e / jnp.sum(e, axis=0, keepdims=True)


def _outproj(hm, ha, x2, wt, wb, g, b, wr, S, tm=512):
    T, D = x2.shape
    nt = S // tm
    return pl.pallas_call(
        _outproj_kernel,
        grid=(T // tm,),
        in_specs=[
            pl.BlockSpec((tm, MLSTM_WIDTH), lambda i: (i, 0)),
            pl.BlockSpec((tm, ATTN_WIDTH), lambda i: (i, 0)),
            pl.BlockSpec((tm, D), lambda i: (i, 0)),
            _resident((MLSTM_WIDTH, D), lambda i: (0, 0)),
            _resident((ATTN_WIDTH, D), lambda i: (MLSTM_WIDTH // ATTN_WIDTH, 0)),
            _resident((1, D), lambda i: (0, 0)),
            _resident((1, D), lambda i: (0, 0)),
            _resident((N_EXPERTS, D), lambda i: (0, 0)),
        ],
        out_specs=[
            pl.BlockSpec((tm, D), lambda i: (i, 0)),
            pl.BlockSpec((tm * TOK_TILE, LANES), lambda i: (i, 0)),
            pl.BlockSpec((N_EXPERTS, None, 1, tm), lambda i: (0, i // nt, 0, i % nt)),
        ],
        out_shape=[
            jax.ShapeDtypeStruct((T, D), F32),
            jax.ShapeDtypeStruct((T * TOK_TILE, LANES), BF16),
            jax.ShapeDtypeStruct((N_EXPERTS, T // S, 1, S), F32),
        ],
        compiler_params=_cparams(("parallel",)),
        name="outproj",
    )(hm, ha, x2, wt, wb, g, b, wr)


def _excl_prefix_count(m):
    E, S = m.shape
    nb = S // LANES
    s_i = lax.broadcasted_iota(jnp.int32, (LANES, 2 * LANES), 0)
    t_i = lax.broadcasted_iota(jnp.int32, (LANES, 2 * LANES), 1)
    tri_ext = jnp.where(t_i >= LANES, 1.0, jnp.where(s_i < t_i, 1.0, 0.0)).astype(BF16)
    r_i = lax.broadcasted_iota(jnp.int32, (nb * E, nb * E), 0)
    c_i = lax.broadcasted_iota(jnp.int32, (nb * E, nb * E), 1)
    earlier = jnp.where(c_i // E < r_i // E, jnp.where(c_i % E == r_i % E, 1.0, 0.0), 0.0).astype(BF16)
    stacked = jnp.concatenate([m[:, LANES * j:LANES * (j + 1)] for j in range(nb)], axis=0)
    r = jnp.dot(stacked, tri_ext, preferred_element_type=F32)
    off = jnp.dot(earlier, r[:, LANES:].astype(BF16), preferred_element_type=F32)
    tot = r[:, :LANES] + off
    return jnp.concatenate([tot[E * j:E * (j + 1), :] for j in range(nb)], axis=1)


def _topk_kernel(aff_ref, pos_ref, idx_ref, *, cap):
    S = aff_ref.shape[2]
    a = aff_ref[:, 0, :]

    def count_ge(v):
        return jnp.sum(jnp.where(a >= v, 1.0, 0.0), axis=1, keepdims=True)

    def keep_if_enough(cand, thr):
        return jnp.where(count_ge(pltpu.bitcast(cand, F32)) >= cap, cand, thr)

    thr = keep_if_enough(jnp.full((a.shape[0], 1), 1 << 30, jnp.int32), jnp.zeros((a.shape[0], 1), jnp.int32))
    for bit in range(28, -1, -2):
        b1, b0 = 1 << (bit + 1), 1 << bit
        thr = keep_if_enough(thr | b1 | b0, keep_if_enough(thr | b1, keep_if_enough(thr | b0, thr)))
    lo = pltpu.bitcast(thr, F32)
    hi = pltpu.bitcast(thr + 1, F32)
    for _ in range(TOPK_REFINE_STEPS):
        mid = 0.5 * (lo + hi)
        ok = count_ge(mid) >= cap
        lo = jnp.where(ok, mid, lo)
        hi = jnp.where(ok, hi, mid)
    gt = a > lo
    eq = a == lo
    need = cap - jnp.sum(jnp.where(gt, 1.0, 0.0), axis=1, keepdims=True)
    eq_rank = _excl_prefix_count(jnp.where(eq, 1.0, 0.0).astype(BF16))
    sel = jnp.logical_or(gt, jnp.logical_and(eq, eq_rank < need))
    pos = _excl_prefix_count(jnp.where(sel, 1.0, 0.0).astype(BF16))
    posm = jnp.where(sel, pos, -1.0)
    pos_ref[:, 0, :] = posm
    digit_row = lax.broadcasted_iota(jnp.int32, (2 * SUBLANES, S), 0)
    tok = lax.broadcasted_iota(jnp.int32, (2 * SUBLANES, S), 1)
    digits = jnp.where(digit_row == 0, tok // TOK_TILE,
                       jnp.where(digit_row == 1, tok % TOK_TILE, 0)).astype(F32).astype(BF16)
    slot = lax.broadcasted_iota(jnp.int32, (cap, S), 0).astype(F32)
    for e in range(a.shape[0]):
        hit = jnp.where(slot == posm[e:e + 1, :], 1.0, 0.0).astype(BF16)
        idx_ref[e] = lax.dot_general(digits, hit, (((1,), (1,)), ((), ())),
                                     preferred_element_type=F32)[:SUBLANES]


def _topk(aff, B, S, cap):
    E = aff.shape[0]
    assert S <= 256 * TOK_TILE, "token index digits must be integers bf16 holds exactly"
    return pl.pallas_call(
        functools.partial(_topk_kernel, cap=cap),
        grid=(B,),
        in_specs=[pl.BlockSpec((E, None, 1, S), lambda b: (0, b, 0, 0))],
        out_specs=[pl.BlockSpec((E, None, 1, S), lambda b: (0, b, 0, 0)),
                   pl.BlockSpec((E, None, SUBLANES, cap), lambda b: (0, b, 0, 0))],
        out_shape=[jax.ShapeDtypeStruct((E, B, 1, S), F32),
                   jax.ShapeDtypeStruct((E, B, SUBLANES, cap), F32)],
        compiler_params=_cparams(("parallel",)),
        name="topk",
    )(aff)


def _moe_ffn_kernel(idx_ref, xg_ref, pos_ref, aff_ref, wg_ref, wu_ref, wd_ref, y_ref, gbuf_s, wg_s, wu_s, wd_s,
                    *, cap, n_exp):
    e = pl.program_id(0)
    b = pl.program_id(1)

    @pl.when(e < n_exp)
    def _():
        slot = e % 2
        rg = pl.multiple_of(b * wg_ref.shape[0], wg_ref.shape[0])
        rd = pl.multiple_of(b * wd_ref.shape[0], wd_ref.shape[0])
        wg_s[slot, pl.ds(rg, wg_ref.shape[0]), :] = wg_ref[...].astype(BF16)
        wu_s[slot, pl.ds(rg, wu_ref.shape[0]), :] = wu_ref[...].astype(BF16)
        wd_s[slot, pl.ds(rd, wd_ref.shape[0]), :] = wd_ref[...].astype(BF16)

    @pl.when(e > 0)
    def _():
        S = pos_ref.shape[1]
        slot = (e - 1) % 2
        for c in range(cap):
            row = pl.multiple_of(idx_ref[0, c] * TOK_TILE, TOK_TILE)
            gbuf_s[TOK_TILE * c:TOK_TILE * (c + 1), :] = xg_ref[pl.ds(row, TOK_TILE), :]
        perm = _tile_perm()
        xs = _row_major(gbuf_s, perm)
        sl = lax.broadcasted_iota(jnp.int32, (cap, S), 0).astype(F32)
        gate = jnp.sum(jnp.where(sl == pos_ref[...], aff_ref[...], 0.0), axis=1, keepdims=True)
        hg = jnp.dot(xs, wg_s[slot], preferred_element_type=F32)
        hu = jnp.dot(xs, wu_s[slot], preferred_element_type=F32)
        hid = (hg * _sigmoid(hg) * hu).astype(BF16)
        y = (jnp.dot(hid, wd_s[slot], preferred_element_type=F32) * gate).astype(BF16)
        y_ref[...] = y


def _moe_ffn(idx, xg, pos4, aff4, wg, wu, wd, B, S, cap):
    E, D, F = wg.shape
    prev = lambda e: jnp.maximum(e - 1, 0)
    stage = lambda e, b: (jnp.minimum(e, E - 1), jnp.where(e < E, b, B - 1), 0)
    return pl.pallas_call(
        functools.partial(_moe_ffn_kernel, cap=cap, n_exp=E),
        grid=(E + 1, B),
        in_specs=[
            pl.BlockSpec((None, None, 1, cap), lambda e, b: (prev(e), b, 0, 0), memory_space=pltpu.SMEM),
            pl.BlockSpec((S * TOK_TILE, LANES), lambda e, b: (b, 0)),
            pl.BlockSpec((None, None, 1, S), lambda e, b: (prev(e), b, 0, 0)),
            pl.BlockSpec((None, None, 1, S), lambda e, b: (prev(e), b, 0, 0)),
            pl.BlockSpec((None, D // B, F), stage),
            pl.BlockSpec((None, D // B, F), stage),
            pl.BlockSpec((None, F // B, D), stage),
        ],
        out_specs=pl.BlockSpec((None, None, cap, D), lambda e, b: (jnp.where(e > 0, b, 0), prev(e), 0, 0)),
        out_shape=jax.ShapeDtypeStruct((B, E, cap, D), BF16),
        scratch_shapes=[pltpu.VMEM((cap * TOK_TILE, LANES), BF16),
                        pltpu.VMEM((2, D, F), BF16), pltpu.VMEM((2, D, F), BF16), pltpu.VMEM((2, F, D), BF16)],
        compiler_params=_cparams(("arbitrary", "arbitrary")),
        name="moe_ffn",
    )(idx, xg, pos4, aff4, wg, wu, wd)


def _row_major_f32(g_ref, r0, n_tok, perm):
    n = TOK_TILE * COL_BLOCKS
    cols = [[] for _ in range(COL_BLOCKS)]
    perm3 = jnp.concatenate([perm, perm, perm], axis=1)
    for k in range(0, n_tok * TOK_TILE, 2 * n):
        m2 = jnp.concatenate([g_ref[pl.ds(r0 + k, n), :], g_ref[pl.ds(r0 + k + n, n), :]], axis=1)
        hi = m2.astype(BF16)
        r1 = m2 - hi.astype(F32)
        mid = r1.astype(BF16)
        lo = (r1 - mid.astype(F32)).astype(BF16)
        o2 = jnp.dot(perm3, jnp.concatenate([hi, mid, lo], axis=0),
                     preferred_element_type=F32)
        for j in range(COL_BLOCKS):
            blk = o2[TOK_TILE * j:TOK_TILE * (j + 1), :]
            cols[j] += [blk[:, :LANES], blk[:, LANES:]]
    return jnp.concatenate([jnp.concatenate(cj, axis=0) for cj in cols], axis=1)


def _moe_comb_kernel(idx_ref, y_ref, o_ref, acc_s, *, cap, n_grp):
    j = pl.program_id(1)

    @pl.when(j == 0)
    def _():
        acc_s[...] = jnp.zeros(acc_s.shape, F32)

    @pl.when(j < n_grp)
    def _():
        perm = _tile_perm()
        for g in range(idx_ref.shape[0]):
            y_tm = _token_major(y_ref[g], perm)
            for c0 in range(0, cap, SCATTER_BATCH):
                rows = [pl.multiple_of(idx_ref[g, 0, c0 + u] * TOK_TILE, TOK_TILE) for u in range(SCATTER_BATCH)]
                new = [acc_s[pl.ds(rows[u], TOK_TILE), :]
                       + y_tm[TOK_TILE * (c0 + u):TOK_TILE * (c0 + u + 1), :].astype(F32)
                       for u in range(SCATTER_BATCH)]
                for u in range(SCATTER_BATCH):
                    acc_s[pl.ds(rows[u], TOK_TILE), :] = new[u]

    @pl.when(j >= n_grp)
    def _():
        n_tok = o_ref.shape[0]
        r0 = pl.multiple_of((j - n_grp) * (n_tok * TOK_TILE), n_tok * TOK_TILE)
        o_ref[...] = _row_major_f32(acc_s, r0, n_tok, _tile_perm())


def _moe_comb(idx, y, B, S, cap, chunk=1024, eg=8):
    E = y.shape[1]
    D = D_MODEL
    nch = S // chunk
    ng = E // eg
    return pl.pallas_call(
        functools.partial(_moe_comb_kernel, cap=cap, n_grp=ng),
        grid=(B, ng + nch),
        in_specs=[
            pl.BlockSpec((eg, None, 1, cap), lambda b, j: (jnp.minimum(j, ng - 1), b, 0, 0),
                         memory_space=pltpu.SMEM),
            pl.BlockSpec((None, eg, cap, D), lambda b, j: (b, jnp.minimum(j, ng - 1), 0, 0)),
        ],
        out_specs=pl.BlockSpec((chunk, D), lambda b, j: (b * nch + jnp.maximum(j - ng, 0), 0)),
        out_shape=jax.ShapeDtypeStruct((B * S, D), F32),
        scratch_shapes=[pltpu.VMEM((S * TOK_TILE, LANES), F32)],
        compiler_params=_cparams(("arbitrary", "arbitrary")),
        name="moe_comb",
    )(idx, y)


def _final_kernel(x1_ref, moe_ref, p_ref, wpg_ref, bpg_ref, wpp_ref, g_ref, b_ref, o_ref):
    for r0 in range(0, x1_ref.shape[0], ROW_CHAIN):
        rows = slice(r0, r0 + ROW_CHAIN)
        x1 = x1_ref[rows, :]
        gate = _sigmoid(jnp.dot(x1.astype(BF16), wpg_ref[...], preferred_element_type=F32) + bpg_ref[...])
        plv = gate * jnp.dot(p_ref[rows, :].astype(BF16), wpp_ref[...], preferred_element_type=F32)
        o_ref[rows, :] = _layer_norm(ALPHA * x1 + moe_ref[rows, :] + plv, g_ref[...], b_ref[...])


def _final(x1, moe, p2, wpg, bpg, wpp, g, b, tm=512):
    T, D = x1.shape
    tile = lambda i: (i, 0)

    def outer(x1_hbm, moe_hbm, p_hbm, wpg_ref, bpg_ref, wpp_ref, g_ref, b_ref, o_hbm):
        def body(x1_ref, moe_ref, p_ref, o_ref):
            _final_kernel(x1_ref, moe_ref, p_ref, wpg_ref, bpg_ref, wpp_ref, g_ref, b_ref, o_ref)

        pltpu.emit_pipeline(
            body,
            grid=(T // tm,),
            in_specs=[_streamed((tm, D), tile), _streamed((tm, D), tile), pl.BlockSpec((tm, P_DIM), tile)],
            out_specs=[pl.BlockSpec((tm, D), tile)],
        )(x1_hbm, moe_hbm, p_hbm, o_hbm)

    hbm = pl.BlockSpec(memory_space=pl.ANY)
    vmem = pl.BlockSpec(memory_space=pltpu.VMEM)
    return pl.pallas_call(
        outer,
        in_specs=[hbm, hbm, hbm, vmem, vmem, vmem, vmem, vmem],
        out_specs=hbm,
        out_shape=jax.ShapeDtypeStruct((T, D), F32),
        compiler_params=pltpu.CompilerParams(vmem_limit_bytes=V7X_VMEM_LIMIT),
        name="final",
    )(x1, moe, p2, wpg, bpg, wpp, g, b)


def _rope_tables(S):
    rows = S // GRID_W
    row_idx = np.repeat(np.arange(rows, dtype=np.float64), GRID_W)
    col_idx = np.tile(np.arange(GRID_W, dtype=np.float64), rows)
    inv_freq = ROPE_THETA ** (-np.arange(ROPE_FREQS, dtype=np.float64) / ROPE_FREQS)
    ar = row_idx[:, None] * inv_freq
    ac = col_idx[:, None] * inv_freq
    cos_t = np.concatenate([np.cos(ar), np.cos(ar), np.cos(ac), np.cos(ac)], axis=1)
    sin_t = np.concatenate([-np.sin(ar), np.sin(ar), -np.sin(ac), np.sin(ac)], axis=1)
    return jnp.asarray(cos_t, F32), jnp.asarray(sin_t, F32)


def _pair_major(a):
    return a.reshape(2, HEAD_PAIRS, 2).transpose(1, 0, 2)


def _layer(x2, p2, B, S, w_in, conv_w, b_i, b_f, g_mlstm, g_q, g_k, w_out, ln1_g, ln1_b, w_router, w_gate, w_up,
           w_down, w_pl_proj, w_pl_gate, b_pl_gate, ln2_g, ln2_b):
    D = D_MODEL
    cap = CAPACITY_FACTOR * S // N_EXPERTS
    o_mg = 2 * MQ_COLS + 2 * MV_COLS
    w_main = jnp.concatenate([w_in[:, :o_mg], w_in[:, o_mg + MG_COLS:]], axis=1).astype(BF16)
    w_g = w_in[:, o_mg:o_mg + MG_COLS].reshape(D, 2, 2, HEAD_PAIRS, 2).transpose(0, 3, 2, 1, 4).reshape(D, MG_COLS)
    bias = jnp.stack([_pair_major(b_i), _pair_major(b_f)], axis=1).reshape(HEAD_PAIRS, GATES_PER_PAIR)

    proj, grow = _proj(x2, w_main, w_g.astype(BF16).T)
    h_m = _mlstm(proj, grow, conv_w, bias[:, :, None], g_mlstm.reshape(HEAD_PAIRS, 2, MLSTM_DV), B, S)
    cos_t, sin_t = _rope_tables(S)
    h_a = _attn(proj, cos_t, sin_t, g_q[None, :], g_k[None, :], B, S)
    w_o = w_out.astype(BF16)
    x1, xg, aff4 = _outproj(h_m, h_a, x2, w_o, w_o, ln1_g[None, :], ln1_b[None, :], w_router.T, S)
    pos4, idx_digits = _topk(aff4, B, S, cap)
    idx = (idx_digits[:, :, 0:1, :] * TOK_TILE + idx_digits[:, :, 1:2, :]).astype(jnp.int32)
    y = _moe_ffn(idx, xg, pos4, aff4, w_gate, w_up, w_down, B, S, cap)
    moe = _moe_comb(idx, y, B, S, cap)
    return _final(x1, moe, p2, w_pl_gate.astype(BF16), b_pl_gate[None, :], w_pl_proj.astype(BF16),
                  ln2_g[None, :], ln2_b[None, :])


def kernel(x, p, w_in, conv_w, b_igate, b_fgate, g_mlstm, g_q, g_k, w_out, ln1_g, ln1_b, w_router, w_gate, w_up,
           w_down, w_pl_proj, w_pl_gate, b_pl_gate, ln2_g, ln2_b):
    B, S, D = x.shape
    x2 = x.reshape(B * S, D)
    for i in range(DEPTH):
        x2 = _layer(x2, p[i].reshape(B * S, P_DIM), B, S, w_in[i], conv_w[i], b_igate[i], b_fgate[i], g_mlstm[i],
                    g_q[i], g_k[i], w_out[i], ln1_g[i], ln1_b[i], w_router[i], w_gate[i], w_up[i], w_down[i],
                    w_pl_proj[i], w_pl_gate[i], b_pl_gate[i], ln2_g[i], ln2_b[i])
    return x2.reshape(B, S, D)
```
